```python
import math
import jax
import jax.numpy as jnp
from jax import lax
import numpy as np

D_MODEL = 1024
BATCH = 2
SEQ = 16384
DEPTH = 4

CTX_LEN = 256
GRID_W = 64
EPS = 1e-6

SSD_HEAD_DIM = 64
SSD_INNER = D_MODEL
SSD_HEADS = SSD_INNER // SSD_HEAD_DIM
SSD_GROUPS = 2
SSD_STATE = 128
SSD_CONV = 3
SSD_CHUNK = 128
SSD_XBC = SSD_INNER + 2 * SSD_GROUPS * SSD_STATE

HY_WIDTH = D_MODEL
HY_ORDER = 2
HY_SHORT = 3
HY_BANDS = 16
HY_EMB = 1 + 2 * HY_BANDS
HY_FILTER_HIDDEN = 64
HY_SHORT_DECAY_PCT = 0.3
HY_LONG_DECAY_PCT = 1.5
HY_DECAY_TARGET = 1e-2
HY_FILTER_OUT_STD = 0.003

DA_V_DIM = 128
DA_HEADS = D_MODEL // DA_V_DIM
DA_HEAD_DIM = DA_V_DIM // 2
ROPE_AXIS_DIM = DA_HEAD_DIM // 2
ROPE_THETA = 10000.0
Q_BLOCK = 128

N_BRANCH = 3
FFN_RAW = -(-8 * D_MODEL // 3)
FFN_HIDDEN = -(-FFN_RAW // 256) * 256

DT_W = 2 * SSD_HEADS
HY_COLS = (HY_ORDER + 1) * HY_WIDTH
QK_W = DA_HEADS * 2 * DA_HEAD_DIM
V_W = DA_HEADS * DA_V_DIM
GATE_W = N_BRANCH * D_MODEL
IN_COLS = SSD_INNER + SSD_XBC + DT_W + HY_COLS + 2 * QK_W + V_W + GATE_W

kernel_name = 'hybrid_ssd_hyena_diffattn_dit_trunk'


def rmsnorm(x, g):
    x32 = x.astype(jnp.float32)
    y = x32 * lax.rsqrt(jnp.mean(x32 * x32, axis=-1, keepdims=True) + EPS)
    return (y * g.astype(jnp.float32)).astype(x.dtype)


def modulate(xn, shift, scale):
    return xn * (1.0 + scale) + shift


def dwconv_centered(x, w, b):
    k_w = w.shape[0]
    half = k_w // 2
    length = x.shape[1]
    xp = jnp.pad(x, ((0, 0), (half, half), (0, 0)))
    y = b + xp[:, 0:length] * w[0]
    for i in range(1, k_w):
        y = y + xp[:, i:i + length] * w[i]
    return y


def split_proj(p):
    widths = (SSD_INNER, SSD_XBC, DT_W, HY_COLS, QK_W, QK_W, V_W)
    idx = [int(s) for s in np.cumsum(widths)]
    return jnp.split(p, idx, axis=-1)


def ssd_prep(xbc, dt_raw, conv_w, conv_b, dt_bias):
    xbc = jax.nn.silu(dwconv_centered(xbc, conv_w, conv_b))
    bsz, length, _ = xbc.shape
    xs, bm, cm = jnp.split(xbc, [SSD_INNER, SSD_INNER + SSD_GROUPS * SSD_STATE], axis=-1)
    xs = xs.reshape(bsz, length, SSD_HEADS, SSD_HEAD_DIM)
    bm = bm.reshape(bsz, length, SSD_GROUPS, SSD_STATE)
    cm = cm.reshape(bsz, length, SSD_GROUPS, SSD_STATE)
    dt = jax.nn.softplus(dt_raw.astype(jnp.float32).reshape(bsz, length, 2, SSD_HEADS)
                         + dt_bias.astype(jnp.float32))
    return xs, bm, cm, dt


def ssd_chunked(x, dt, a_neg, b_in, c_in, h0, need_y):
    f32 = jnp.float32
    bsz, length, n_heads, p_dim = x.shape
    n_groups, n_state = b_in.shape[2], b_in.shape[3]
    hpg = n_heads // n_groups
    n_chunks = length // SSD_CHUNK
    q = SSD_CHUNK
    xc = x.astype(f32).reshape(bsz, n_chunks, q, n_groups, hpg, p_dim)
    dtc = dt.astype(f32).reshape(bsz, n_chunks, q, n_groups, hpg)
    bc = b_in.astype(f32).reshape(bsz, n_chunks, q, n_groups, n_state)
    cc = c_in.astype(f32).reshape(bsz, n_chunks, q, n_groups, n_state)
    a_cum = jnp.cumsum(dtc * a_neg.reshape(n_groups, hpg), axis=2)
    xdt = xc * dtc[..., None]
    decay_to_end = jnp.exp(a_cum[:, :, -1:] - a_cum)
    states = jnp.einsum('bcsgn,bcsgj,bcsgjp->bcgjpn', bc, decay_to_end, xdt)
    chunk_decay = jnp.exp(a_cum[:, :, -1])

    def step(h, inp):
        st, dc = inp
        return h * dc[..., None, None] + st, h

    h_init = h0.astype(f32).reshape(bsz, n_groups, hpg, p_dim, n_state)
    h_last, h_prev = lax.scan(step, h_init, (jnp.moveaxis(states, 1, 0), jnp.moveaxis(chunk_decay, 1, 0)))
    h_last = h_last.reshape(bsz, n_heads, p_dim, n_state)
    if not need_y:
        return None, h_last
    h_prev = jnp.moveaxis(h_prev, 0, 1)
    lower = jnp.tril(jnp.ones((q, q), bool))[:, :, None, None]
    seg = a_cum[:, :, :, None] - a_cum[:, :, None, :]
    decay_qs = jnp.where(lower, jnp.exp(jnp.where(lower, seg, 0.0)), 0.0)
    cb = jnp.einsum('bcqgn,bcsgn->bcqsg', cc, bc)
    y_diag = jnp.einsum('bcqsg,bcqsgj,bcsgjp->bcqgjp', cb, decay_qs, xdt)
    y_off = jnp.einsum('bcqgn,bcgjpn,bcqgj->bcqgjp', cc, h_prev, jnp.exp(a_cum))
    return (y_diag + y_off).reshape(bsz, length, n_heads, p_dim), h_last


def ssd_bidir(xs, bm, cm, dt, a_neg, h0_fwd, h0_bwd, need_y):
    flip = lambda t: jnp.flip(t, axis=1)
    y_f, h_f = ssd_chunked(xs, dt[:, :, 0], a_neg[0], bm, cm, h0_fwd, need_y)
    y_b, h_b = ssd_chunked(flip(xs), flip(dt[:, :, 1]), a_neg[1], flip(bm), flip(cm), h0_bwd, need_y)
    y = y_f + flip(y_b) if need_y else None
    return y, h_f, h_b


def ssd_output(y, xs, z, d_skip, norm_w):
    bsz, length = z.shape[:2]
    y = y + d_skip.astype(jnp.float32)[:, None] * xs.astype(jnp.float32)
    y = y.reshape(bsz, length, SSD_INNER) * jax.nn.silu(z.astype(jnp.float32))
    y = y.reshape(bsz, length, SSD_GROUPS, SSD_INNER // SSD_GROUPS)
    y = y * lax.rsqrt(jnp.mean(y * y, axis=-1, keepdims=True) + EPS)
    return (y.reshape(bsz, length, SSD_INNER) * norm_w.astype(jnp.float32)).astype(z.dtype)


def hyena_filters(length, w1, b1, w2, b2, w3, freq):
    f32 = jnp.float32
    t = jnp.linspace(0.0, 1.0, length, dtype=f32)[:, None]
    phase = 2.0 * math.pi * jnp.arange(length, dtype=f32)[:, None] / length
    bands = jnp.linspace(1e-4, HY_BANDS - 1, HY_BANDS, dtype=f32)[None, :]
    feats = jnp.concatenate([t, jnp.cos(phase * bands), -jnp.sin(phase * bands)], axis=-1)
    hdn = jnp.sin(freq * (feats @ w1 + b1))
    hdn = jnp.sin(freq * (hdn @ w2 + b2))
    filt = (hdn @ w3).reshape(length, 2 * HY_ORDER, HY_WIDTH)
    max_decay = math.log(HY_DECAY_TARGET) / HY_SHORT_DECAY_PCT
    min_decay = math.log(HY_DECAY_TARGET) / HY_LONG_DECAY_PCT
    deltas = jnp.abs(jnp.linspace(min_decay, max_decay, HY_WIDTH, dtype=f32))
    window = jnp.exp(-t * deltas[None, :])
    return filt.astype(f32) * window[:, None, :]


def long_conv_bidir(u, h_fwd, h_bwd, skip):
    length = u.shape[1]
    taps = jnp.concatenate([h_fwd, jnp.zeros_like(h_fwd[:1]), h_bwd[:0:-1]], axis=0)
    taps_f = jnp.fft.rfft(taps.astype(jnp.float32), axis=0)
    u_f = jnp.fft.rfft(u, n=2 * length, axis=1)
    y = jnp.fft.irfft(u_f * taps_f, n=2 * length, axis=1)[:, :length]
    return y + u * skip


def hyena_mixer(proj, conv_w, conv_b, filt, skip):
    u = dwconv_centered(proj, conv_w, conv_b).astype(jnp.float32)
    v, x1, x2 = jnp.split(u, HY_ORDER + 1, axis=-1)
    z = x1 * long_conv_bidir(v, filt[:, 0], filt[:, 1], skip[0])
    y = x2 * long_conv_bidir(z, filt[:, 2], filt[:, 3], skip[1])
    return y.astype(proj.dtype)


def rope_2d_tables(length):
    rows = length // GRID_W
    row = jnp.repeat(jnp.arange(rows, dtype=jnp.float32), GRID_W)
    col = (jnp.arange(length) % GRID_W).astype(jnp.float32)
    inv = ROPE_THETA ** (-jnp.arange(0, ROPE_AXIS_DIM, 2, dtype=jnp.float32) / ROPE_AXIS_DIM)
    ang = jnp.stack([row[:, None] * inv, col[:, None] * inv], axis=1)
    return jnp.cos(ang), jnp.sin(ang)


def apply_rope_2d(t, cos, sin):
    bsz, length, n_h, n_m, d = t.shape
    tt = t.reshape(bsz, length, n_h, n_m, 2, ROPE_AXIS_DIM)
    half = ROPE_AXIS_DIM // 2
    t1, t2 = tt[..., :half], tt[..., half:]
    cs = cos[None, :, None, None]
    sn = sin[None, :, None, None]
    out = jnp.concatenate([t1 * cs - t2 * sn, t2 * cs + t1 * sn], axis=-1)
    return out.reshape(bsz, length, n_h, n_m, d).astype(t.dtype)


def diff_attn_dense(q, k, v, lam):
    s = jnp.einsum('bqhmd,bkhmd->bhmqk', q, k).astype(jnp.float32) * (DA_HEAD_DIM ** -0.5)
    p = jax.nn.softmax(s, axis=-1)
    a = p[:, :, 0] - lam * p[:, :, 1]
    return jnp.einsum('bhqk,bkhe->bqhe', a.astype(v.dtype), v)


def diff_attn_blocked(q, k_all, v_all, lam):
    bsz, length = q.shape[:2]
    n_blk = length // Q_BLOCK
    qb = q.reshape(bsz, n_blk, Q_BLOCK, DA_HEADS, 2, DA_HEAD_DIM).swapaxes(0, 1)
    out = lax.map(lambda qblk: diff_attn_dense(qblk, k_all, v_all, lam), qb)
    return out.swapaxes(0, 1).reshape(bsz, length, DA_HEADS, DA_V_DIM)


def diff_attn_post(o, subln_w, lam_init):
    bsz, length = o.shape[:2]
    return (rmsnorm(o, subln_w) * (1.0 - lam_init)).reshape(bsz, length, V_W)


def merge_branches(ya, yb, yc, gate_pre, w_branch, w_out):
    ys = jnp.stack([ya, yb, yc], axis=2)
    proj = jnp.einsum('blkw,kwd->blkd', ys, w_branch)
    g = jax.nn.sigmoid(gate_pre.reshape(gate_pre.shape[:-1] + (N_BRANCH, D_MODEL)))
    return jnp.sum(g * proj, axis=2) @ w_out


def swiglu(h, w13, w2):
    a1, a3 = jnp.split(h @ w13, 2, axis=-1)
    return (jax.nn.silu(a1) * a3) @ w2


def setup_inputs(seed: int = 0) -> dict:
    key = jax.random.key(seed)
    keys = jax.random.split(key, 40)
    f32 = jnp.float32

    def nrm(i, shape, std):
        return std * jax.random.normal(keys[i], shape, f32)

    u_dt = jax.random.uniform(keys[10], (DEPTH, 2, SSD_HEADS), f32)
    dt0 = jnp.exp(u_dt * (math.log(0.1) - math.log(1e-3)) + math.log(1e-3))
    dt_bias = dt0 + jnp.log(-jnp.expm1(-dt0))
    a_log = jnp.log(jax.random.uniform(keys[11], (DEPTH, 2, SSD_HEADS), f32, 1.0, 16.0))
    return {
        'x': nrm(0, (BATCH, SEQ, D_MODEL), 1.0),
        'c': nrm(1, (BATCH, D_MODEL), 1.0),
        'ctx': nrm(2, (BATCH, CTX_LEN, D_MODEL), 1.0),
        'c_ctx': nrm(3, (D_MODEL,), 1.0),
        'ada_w': nrm(4, (DEPTH, D_MODEL, 6 * D_MODEL), 0.5 * D_MODEL ** -0.5),
        'ada_b': nrm(5, (DEPTH, 6 * D_MODEL), 0.01),
        'norm1_g': 1.0 + nrm(6, (DEPTH, D_MODEL), 0.01),
        'norm2_g': 1.0 + nrm(7, (DEPTH, D_MODEL), 0.01),
        'w_in': nrm(8, (DEPTH, D_MODEL, IN_COLS), D_MODEL ** -0.5),
        'ssd_conv_w': nrm(9, (DEPTH, SSD_CONV, SSD_XBC), SSD_CONV ** -0.5),
        'ssd_conv_b': nrm(12, (DEPTH, SSD_XBC), 0.01),
        'ssd_a_log': a_log,
        'ssd_dt_bias': dt_bias,
        'ssd_d': 1.0 + nrm(13, (DEPTH, SSD_HEADS), 0.01),
        'ssd_norm_w': 1.0 + nrm(14, (DEPTH, SSD_INNER), 0.01),
        'hy_conv_w': nrm(15, (DEPTH, HY_SHORT, HY_COLS), HY_SHORT ** -0.5),
        'hy_conv_b': nrm(16, (DEPTH, HY_COLS), 0.01),
        'hy_w1': nrm(17, (DEPTH, HY_EMB, HY_FILTER_HIDDEN), HY_EMB ** -0.5),
        'hy_b1': nrm(18, (DEPTH, HY_FILTER_HIDDEN), 0.1),
        'hy_w2': nrm(19, (DEPTH, HY_FILTER_HIDDEN, HY_FILTER_HIDDEN), HY_FILTER_HIDDEN ** -0.5),
        'hy_b2': nrm(20, (DEPTH, HY_FILTER_HIDDEN), 0.1),
        'hy_w3': nrm(21, (DEPTH, HY_FILTER_HIDDEN, 2 * HY_ORDER * HY_WIDTH), HY_FILTER_OUT_STD),
        'hy_freq': 1.0 + nrm(22, (DEPTH, HY_FILTER_HIDDEN), 0.01),
        'hy_bias': nrm(23, (DEPTH, HY_ORDER, HY_WIDTH), 0.5),
        'da_lambda': nrm(24, (DEPTH, 4, DA_HEAD_DIM), 0.1),
        'da_subln_w': 1.0 + nrm(25, (DEPTH, DA_V_DIM), 0.01),
        'w_branch': nrm(26, (DEPTH, N_BRANCH, D_MODEL, D_MODEL), D_MODEL ** -0.5),
        'w_out': nrm(27, (DEPTH, D_MODEL, D_MODEL), D_MODEL ** -0.5),
        'ffn_w13': nrm(28, (DEPTH, D_MODEL, 2 * FFN_HIDDEN), D_MODEL ** -0.5),
        'ffn_w2': nrm(29, (DEPTH, FFN_HIDDEN, D_MODEL), FFN_HIDDEN ** -0.5),
        'final_g': 1.0 + nrm(30, (D_MODEL,), 0.01),
    }


def reference(x, c, ctx, c_ctx, ada_w, ada_b, norm1_g, norm2_g, w_in, ssd_conv_w, ssd_conv_b,
              ssd_a_log, ssd_dt_bias, ssd_d, ssd_norm_w, hy_conv_w, hy_conv_b, hy_w1, hy_b1, hy_w2,
              hy_b2, hy_w3, hy_freq, hy_bias, da_lambda, da_subln_w, w_branch, w_out, ffn_w13,
              ffn_w2, final_g):
    f32 = jnp.float32
    bsz, length = x.shape[0], x.shape[1]
    ctx_len = ctx.shape[1]
    rope_cos, rope_sin = rope_2d_tables(length)
    silu_c = jax.nn.silu(c)
    silu_cc = jax.nn.silu(c_ctx)
    for layer in range(DEPTH):
        last = layer == DEPTH - 1
        lam_init = 0.8 - 0.6 * math.exp(-0.3 * layer)
        mod = (silu_c @ ada_w[layer] + ada_b[layer]).reshape(bsz, 1, 6, D_MODEL)
        mod_c = (silu_cc @ ada_w[layer] + ada_b[layer]).reshape(6, D_MODEL)

        h = modulate(rmsnorm(x, norm1_g[layer]), mod[:, :, 0], mod[:, :, 1])
        hc = modulate(rmsnorm(ctx, norm1_g[layer]), mod_c[0], mod_c[1])
        z_l, xbc_l, dt_l, hy_l, q_l, k_l, v_l, g_l = split_proj(h @ w_in[layer])
        z_c, xbc_c, dt_c, hy_c, q_c, k_c, v_c, g_c = split_proj(hc @ w_in[layer])

        a_neg = -jnp.exp(ssd_a_log[layer].astype(f32))
        xs_c, bm_c, cm_c, dtv_c = ssd_prep(xbc_c, dt_c, ssd_conv_w[layer], ssd_conv_b[layer], ssd_dt_bias[layer])
        h_zero = jnp.zeros((bsz, SSD_HEADS, SSD_HEAD_DIM, SSD_STATE), f32)
        ys_c, st_fwd, st_bwd = ssd_bidir(xs_c, bm_c, cm_c, dtv_c, a_neg, h_zero, h_zero, not last)
        xs_l, bm_l, cm_l, dtv_l = ssd_prep(xbc_l, dt_l, ssd_conv_w[layer], ssd_conv_b[layer], ssd_dt_bias[layer])
        ys_l, _, _ = ssd_bidir(xs_l, bm_l, cm_l, dtv_l, a_neg, st_fwd, st_bwd, True)
        ya_l = ssd_output(ys_l, xs_l, z_l, ssd_d[layer], ssd_norm_w[layer])

        filt_params = (hy_w1[layer], hy_b1[layer], hy_w2[layer], hy_b2[layer], hy_w3[layer], hy_freq[layer])
        yb_l = hyena_mixer(hy_l, hy_conv_w[layer], hy_conv_b[layer],
                           hyena_filters(length, *filt_params), hy_bias[layer])

        lam_vec = da_lambda[layer].astype(f32)
        lam = (jnp.exp(jnp.sum(lam_vec[0] * lam_vec[1])) - jnp.exp(jnp.sum(lam_vec[2] * lam_vec[3]))
               + lam_init)
        qh_l = apply_rope_2d(q_l.reshape(bsz, length, DA_HEADS, 2, DA_HEAD_DIM), rope_cos, rope_sin)
        kh_l = apply_rope_2d(k_l.reshape(bsz, length, DA_HEADS, 2, DA_HEAD_DIM), rope_cos, rope_sin)
        kh_c = k_c.reshape(bsz, ctx_len, DA_HEADS, 2, DA_HEAD_DIM)
        vh_c = v_c.reshape(bsz, ctx_len, DA_HEADS, DA_V_DIM)
        k_all = jnp.concatenate([kh_c, kh_l], axis=1)
        v_all = jnp.concatenate([vh_c, v_l.reshape(bsz, length, DA_HEADS, DA_V_DIM)], axis=1)
        yc_l = diff_attn_post(diff_attn_blocked(qh_l, k_all, v_all, lam), da_subln_w[layer], lam_init)

        x = x + mod[:, :, 2] * merge_branches(ya_l, yb_l, yc_l, g_l, w_branch[layer], w_out[layer])

        h2 = modulate(rmsnorm(x, norm2_g[layer]), mod[:, :, 3], mod[:, :, 4])
        x = x + mod[:, :, 5] * swiglu(h2, ffn_w13[layer], ffn_w2[layer])

        if not last:
            ya_c = ssd_output(ys_c, xs_c, z_c, ssd_d[layer], ssd_norm_w[layer])
            yb_c = hyena_mixer(hy_c, hy_conv_w[layer], hy_conv_b[layer],
                               hyena_filters(ctx_len, *filt_params), hy_bias[layer])
            qh_c = q_c.reshape(bsz, ctx_len, DA_HEADS, 2, DA_HEAD_DIM)
            yc_c = diff_attn_post(diff_attn_dense(qh_c, kh_c, vh_c, lam), da_subln_w[layer], lam_init)
            ctx = ctx + mod_c[2] * merge_branches(ya_c, yb_c, yc_c, g_c, w_branch[layer], w_out[layer])
            h2c = modulate(rmsnorm(ctx, norm2_g[layer]), mod_c[3], mod_c[4])
            ctx = ctx + mod_c[5] * swiglu(h2c, ffn_w13[layer], ffn_w2[layer])
    return rmsnorm(x, final_g)
```

```python
import functools
import math

import jax
import jax.numpy as jnp
from jax import lax
from jax.experimental import pallas as pl
from jax.experimental.pallas import tpu as pltpu

F32 = jnp.float32
BF16 = jnp.bfloat16
EPS = 1e-6

SSD_HEADS = 16
SSD_HEAD_DIM = 64
SSD_GROUPS = 2
SSD_STATE = 128
SSD_CHUNK = 128
HY_ORDER = 2
HY_BANDS = 16
HY_SHORT_DECAY_PCT = 0.3
HY_LONG_DECAY_PCT = 1.5
HY_DECAY_TARGET = 1e-2
DA_HEADS = 8
DA_V_DIM = 128
DA_HEAD_DIM = 64
ROPE_AXIS_DIM = 32
ROPE_THETA = 10000.0
GRID_W = 64
N_BRANCH = 3

LANES = 128
SUBLANES = 8
FFT_N2 = 128
VMEM_LIMIT = 56 * 1024 * 1024


def _cparams(*sem):
    return pltpu.CompilerParams(dimension_semantics=sem, vmem_limit_bytes=VMEM_LIMIT)


def _dot(a, b):
    return jnp.dot(a, b, preferred_element_type=F32)


def _dot_nt(a, b):
    return lax.dot_general(a, b, (((1,), (1,)), ((), ())), preferred_element_type=F32)


def _split3(x):
    hi = x.astype(BF16)
    r1 = x - hi.astype(F32)
    mid = r1.astype(BF16)
    lo = (r1 - mid.astype(F32)).astype(BF16)
    return hi, mid, lo


def _silu(x):
    return x * jax.nn.sigmoid(x)


def _softplus(x):
    return jnp.maximum(x, 0.0) + jnp.log1p(jnp.exp(-jnp.abs(x)))


def _pick_tile(n, pref):
    t = min(pref, n)
    while n % t:
        t //= 2
    return t


def _matmul_kernel(a_ref, w_ref, b_ref, o_ref, *, silu_in):
    a = a_ref[...]
    if silu_in:
        a = _silu(a.astype(F32))
    r = _dot(a.astype(BF16), w_ref[...]) + b_ref[...]
    o_ref[...] = r.astype(o_ref.dtype)


def matmul(a, w, bias=None, out_dtype=F32, silu_in=False, tm=1024, tn=512):
    m, k = a.shape
    n = w.shape[1]
    tm = _pick_tile(m, tm)
    tn = _pick_tile(n, tn)
    if bias is None:
        bias = jnp.zeros((1, n), F32)
    return pl.pallas_call(
        functools.partial(_matmul_kernel, silu_in=silu_in),
        grid=(m // tm, n // tn),
        in_specs=[pl.BlockSpec((tm, k), lambda i, j: (i, 0)),
                  pl.BlockSpec((k, tn), lambda i, j: (0, j)),
                  pl.BlockSpec((1, tn), lambda i, j: (0, j))],
        out_specs=pl.BlockSpec((tm, tn), lambda i, j: (i, j)),
        out_shape=jax.ShapeDtypeStruct((m, n), out_dtype),
        compiler_params=_cparams("parallel", "parallel"),
        name="matmul",
    )(a, w, bias)


def _rms(x, g):
    return x * lax.rsqrt(jnp.mean(x * x, axis=-1, keepdims=True) + EPS) * g


def _norm_mod_kernel(x_ref, g_ref, mod_ref, o_ref, *, row0):
    x = x_ref[...]
    y = _rms(x, g_ref[...])
    shift = mod_ref[row0:row0 + 1, :]
    scale = mod_ref[row0 + 1:row0 + 2, :]
    o_ref[...] = (y * (1.0 + scale) + shift).astype(o_ref.dtype)


def norm_mod(x, g, mods, row0, tr=512):
    b, t, d = x.shape
    tr = _pick_tile(t, tr)
    per_batch = mods.shape[0] > 1
    return pl.pallas_call(
        functools.partial(_norm_mod_kernel, row0=row0),
        grid=(b, t // tr),
        in_specs=[pl.BlockSpec((None, tr, d), lambda bi, i: (bi, i, 0)),
                  pl.BlockSpec((1, d), lambda bi, i: (0, 0)),
                  pl.BlockSpec((None, 8, d), (lambda bi, i: (bi, 0, 0)) if per_batch else (lambda bi, i: (0, 0, 0)))],
        out_specs=pl.BlockSpec((None, tr, d), lambda bi, i: (bi, i, 0)),
        out_shape=jax.ShapeDtypeStruct((b, t, d), BF16),
        compiler_params=_cparams("parallel", "parallel"),
        name="norm_mod",
    )(x, g.reshape(1, d), mods)


def _final_norm_kernel(x_ref, g_ref, o_ref):
    o_ref[...] = _rms(x_ref[...], g_ref[...])


def final_norm(x, g, tr=512):
    b, t, d = x.shape
    tr = _pick_tile(t, tr)
    return pl.pallas_call(
        _final_norm_kernel,
        grid=(b, t // tr),
        in_specs=[pl.BlockSpec((None, tr, d), lambda bi, i: (bi, i, 0)),
                  pl.BlockSpec((1, d), lambda bi, i: (0, 0))],
        out_specs=pl.BlockSpec((None, tr, d), lambda bi, i: (bi, i, 0)),
        out_shape=jax.ShapeDtypeStruct((b, t, d), F32),
        compiler_params=_cparams("parallel", "parallel"),
        name="final_norm",
    )(x, g.reshape(1, d))


def _dwconv_kernel(x_ref, xp_ref, xn_ref, w_ref, b_ref, o_ref, *, act, nblk):
    i = pl.program_id(1)
    x = x_ref[...]
    tr = x.shape[0]
    prev_row = jnp.where(i == 0, 0.0, xp_ref[SUBLANES - 1:SUBLANES, :])
    next_row = jnp.where(i == nblk - 1, 0.0, xn_ref[0:1, :])
    rows = lax.broadcasted_iota(jnp.int32, x.shape, 0)
    x_m1 = jnp.where(rows == 0, prev_row, pltpu.roll(x, 1, 0))
    x_p1 = jnp.where(rows == tr - 1, next_row, pltpu.roll(x, tr - 1, 0))
    y = b_ref[...] + x_m1 * w_ref[0:1, :] + x * w_ref[1:2, :] + x_p1 * w_ref[2:3, :]
    if act:
        y = _silu(y)
    o_ref[...] = y


def dwconv3(x, w, bias, act, tr=256, tc=512):
    b, t, c = x.shape
    tr = _pick_tile(t, tr)
    tc = _pick_tile(c, tc)
    nblk = t // tr
    r8 = tr // SUBLANES
    last8 = t // SUBLANES - 1
    return pl.pallas_call(
        functools.partial(_dwconv_kernel, act=act, nblk=nblk),
        grid=(b, nblk, c // tc),
        in_specs=[pl.BlockSpec((None, tr, tc), lambda bi, i, j: (bi, i, j)),
                  pl.BlockSpec((None, SUBLANES, tc), lambda bi, i, j: (bi, jnp.maximum(i * r8 - 1, 0), j)),
                  pl.BlockSpec((None, SUBLANES, tc), lambda bi, i, j: (bi, jnp.minimum((i + 1) * r8, last8), j)),
                  pl.BlockSpec((3, tc), lambda bi, i, j: (0, j)),
                  pl.BlockSpec((1, tc), lambda bi, i, j: (0, j))],
        out_specs=pl.BlockSpec((None, tr, tc), lambda bi, i, j: (bi, i, j)),
        out_shape=jax.ShapeDtypeStruct((b, t, c), F32),
        compiler_params=_cparams("parallel", "parallel", "parallel"),
        name="dwconv3",
    )(x, x, x, w, bias.reshape(1, c))


def _ssd_kernel(x_ref, b_ref, c_ref, dt_ref, bias_ref, aneg_ref, e_ref, h0_ref,
                y_ref, hout_ref, st_ref, *, direction, nchunks):
    ci = pl.program_id(1)

    @pl.when(ci == 0)
    def _():
        st_ref[...] = h0_ref[...]

    q = x_ref.shape[0]
    hd = SSD_HEAD_DIM
    gw = (SSD_HEADS // SSD_GROUPS) * hd
    x = x_ref[...]
    dtv = _softplus(dt_ref[...] + bias_ref[...])
    adt = dtv * aneg_ref[...]
    row = lax.broadcasted_iota(jnp.int32, (q, q), 0)
    col = lax.broadcasted_iota(jnp.int32, (q, q), 1)
    mask = (col >= row) if direction else (col <= row)
    tri = mask.astype(BF16)
    acum = sum(_dot(tri, p) for p in _split3(adt))
    acum_t = sum(_dot_nt(p, tri) for p in _split3(adt.T))
    total = jnp.sum(adt, axis=0, keepdims=True)
    eac = jnp.exp(acum)
    dte = jnp.exp(total - acum)
    cd = jnp.broadcast_to(jnp.exp(total), (SUBLANES, LANES))
    stack = jnp.concatenate([dtv, eac, dte, cd], axis=0)
    e01 = e_ref[...]
    hi = stack.astype(BF16)
    lo = (stack - hi.astype(F32)).astype(BF16)
    ex = _dot(hi, e01) + _dot(lo, e01)
    dt_e, eac_e, dte_e, cd_e = ex[:q], ex[q:2 * q], ex[2 * q:3 * q], ex[3 * q:3 * q + 1]
    xdt = x * dt_e
    lane = lax.broadcasted_iota(jnp.int32, (q, LANES), 1)
    first = lane < hd
    for g in range(SSD_GROUPS):
        gs = slice(g * gw, (g + 1) * gw)
        c_g = c_ref[:, g * SSD_STATE:(g + 1) * SSD_STATE].astype(BF16)
        b_g32 = b_ref[:, g * SSD_STATE:(g + 1) * SSD_STATE]
        s_g = st_ref[:, gs]
        y_off = _dot(c_g, s_g.astype(BF16)) * eac_e[:, gs]
        cb = _dot_nt(c_g, b_g32.astype(BF16))
        for jp in range(gw // LANES):
            h_a = g * (SSD_HEADS // SSD_GROUPS) + 2 * jp
            ls = slice(h_a * hd, h_a * hd + LANES)
            xpair = xdt[:, ls]
            acc = y_off[:, jp * LANES:(jp + 1) * LANES]
            for k in range(2):
                cix = direction * SSD_HEADS + h_a + k
                seg = acum[:, cix:cix + 1] - acum_t[cix:cix + 1, :]
                dec = jnp.where(mask, jnp.exp(jnp.where(mask, seg, 0.0)), 0.0)
                w = (cb * dec).astype(BF16)
                xk = jnp.where(first if k == 0 else jnp.logical_not(first), xpair, 0.0).astype(BF16)
                acc = acc + _dot(w, xk)
            y_ref[:, ls] = acc
        xdte = (xdt[:, gs] * dte_e[:, gs]).astype(BF16)
        st_ref[:, gs] = s_g * cd_e[:, gs] + _dot(b_g32.T.astype(BF16), xdte)

    @pl.when(ci == nchunks - 1)
    def _():
        hout_ref[...] = st_ref[...]


def ssd_scan(xbc, dt_raw, dt_bias_row, aneg_row, expand, h0, direction):
    b, t, _ = xbc.shape
    q = SSD_CHUNK
    nchunks = t // q
    inner = SSD_HEADS * SSD_HEAD_DIM
    bn = SSD_GROUPS * SSD_STATE
    cmap = (lambda c: nchunks - 1 - c) if direction else (lambda c: c)
    y, hout = pl.pallas_call(
        functools.partial(_ssd_kernel, direction=direction, nchunks=nchunks),
        grid=(b, nchunks),
        in_specs=[pl.BlockSpec((None, q, inner), lambda bi, c: (bi, cmap(c), 0)),
                  pl.BlockSpec((None, q, bn), lambda bi, c: (bi, cmap(c), inner // bn)),
                  pl.BlockSpec((None, q, bn), lambda bi, c: (bi, cmap(c), inner // bn + 1)),
                  pl.BlockSpec((None, q, LANES), lambda bi, c: (bi, cmap(c), 0)),
                  pl.BlockSpec((1, LANES), lambda bi, c: (0, 0)),
                  pl.BlockSpec((1, LANES), lambda bi, c: (0, 0)),
                  pl.BlockSpec((LANES, inner), lambda bi, c: (0, 0)),
                  pl.BlockSpec((None, SSD_STATE, inner), lambda bi, c: (bi, 0, 0))],
        out_specs=[pl.BlockSpec((None, q, inner), lambda bi, c: (bi, cmap(c), 0)),
                   pl.BlockSpec((None, SSD_STATE, inner), lambda bi, c: (bi, 0, 0))],
        out_shape=[jax.ShapeDtypeStruct((b, t, inner), F32),
                   jax.ShapeDtypeStruct((b, SSD_STATE, inner), F32)],
        scratch_shapes=[pltpu.VMEM((SSD_STATE, inner), F32)],
        compiler_params=_cparams("parallel", "arbitrary"),
        name="ssd_scan_bwd" if direction else "ssd_scan_fwd",
    )(xbc, xbc, xbc, dt_raw, dt_bias_row, aneg_row, expand, h0)
    return y, hout


def _ssd_out_kernel(yf_ref, yb_ref, x_ref, z_ref, d_ref, w_ref, o_ref):
    y = yf_ref[...] + yb_ref[...] + d_ref[...] * x_ref[...]
    y = y * _silu(z_ref[...])
    gw = y.shape[1] // SSD_GROUPS
    for g in range(SSD_GROUPS):
        gs = slice(g * gw, (g + 1) * gw)
        yg = y[:, gs]
        yg = yg * lax.rsqrt(jnp.mean(yg * yg, axis=-1, keepdims=True) + EPS)
        o_ref[:, gs] = (yg * w_ref[:, gs]).astype(o_ref.dtype)


def ssd_out(yf, yb, xbc, z, d_row, w_row, tr=256):
    b, t, inner = yf.shape
    tr = _pick_tile(t, tr)
    blk = pl.BlockSpec((None, tr, inner), lambda bi, i: (bi, i, 0))
    vec = pl.BlockSpec((1, inner), lambda bi, i: (0, 0))
    return pl.pallas_call(
        _ssd_out_kernel,
        grid=(b, t // tr),
        in_specs=[blk, blk, blk, blk, vec, vec],
        out_specs=blk,
        out_shape=jax.ShapeDtypeStruct((b, t, inner), BF16),
        compiler_params=_cparams("parallel", "parallel"),
        name="ssd_out",
    )(yf, yb, xbc, z, d_row, w_row)


def _taps_kernel(f_ref, tv_ref, w1_ref, b1_ref, w2_ref, b2_ref, w3_ref, fr_ref, dl_ref, o_ref):
    hp = lax.Precision.HIGHEST
    fr = fr_ref[...]
    h = jnp.sin(fr * (jnp.dot(f_ref[...], w1_ref[...], precision=hp, preferred_element_type=F32) + b1_ref[...]))
    h = jnp.sin(fr * (jnp.dot(h, w2_ref[...], precision=hp, preferred_element_type=F32) + b2_ref[...]))
    filt = jnp.dot(h, w3_ref[...], precision=hp, preferred_element_type=F32)
    c = dl_ref.shape[1]
    win = jnp.exp(-tv_ref[:, 0:1] * dl_ref[...]) * tv_ref[:, 1:2]
    o_ref[0] = filt[:, :c] * win
    o_ref[1] = filt[:, c:] * win


def hyena_taps(length, w1, b1, w2, b2, w3, freq):
    hid = w1.shape[1]
    c = w3.shape[1] // (2 * HY_ORDER)
    n = 2 * length
    t = jnp.linspace(0.0, 1.0, length, dtype=F32)[:, None]
    phase = 2.0 * math.pi * jnp.arange(length, dtype=F32)[:, None] / length
    bands = jnp.linspace(1e-4, HY_BANDS - 1, HY_BANDS, dtype=F32)[None, :]
    feats = jnp.concatenate([t, jnp.cos(phase * bands), -jnp.sin(phase * bands)], axis=-1)
    nf = feats.shape[1]
    rev = jnp.concatenate([jnp.zeros((1,), jnp.int32), jnp.arange(length - 1, 0, -1, dtype=jnp.int32)])
    feats2 = jnp.concatenate([feats, feats[rev]], axis=0)
    feats2 = jnp.pad(feats2, ((0, 0), (0, LANES - nf)))
    valid = jnp.ones((n,), F32).at[length].set(0.0)
    tv = jnp.stack([jnp.concatenate([t[:, 0], t[rev, 0]]), valid], axis=1)
    max_decay = math.log(HY_DECAY_TARGET) / HY_SHORT_DECAY_PCT
    min_decay = math.log(HY_DECAY_TARGET) / HY_LONG_DECAY_PCT
    deltas = jnp.abs(jnp.linspace(min_decay, max_decay, c, dtype=F32))[None, :]
    w1p = jnp.pad(w1, ((0, LANES - nf), (0, 0)))
    w3r = w3.reshape(hid, HY_ORDER, 2, c).transpose(2, 0, 1, 3).reshape(2, hid, HY_ORDER * c)
    tr = _pick_tile(length, 256)
    nhalf = length // tr
    return pl.pallas_call(
        _taps_kernel,
        grid=(n // tr,),
        in_specs=[pl.BlockSpec((tr, LANES), lambda i: (i, 0)),
                  pl.BlockSpec((tr, 2), lambda i: (i, 0)),
                  pl.BlockSpec((LANES, hid), lambda i: (0, 0)),
                  pl.BlockSpec((1, hid), lambda i: (0, 0)),
                  pl.BlockSpec((hid, hid), lambda i: (0, 0)),
                  pl.BlockSpec((1, hid), lambda i: (0, 0)),
                  pl.BlockSpec((None, hid, HY_ORDER * c), lambda i: (i // nhalf, 0, 0)),
                  pl.BlockSpec((1, hid), lambda i: (0, 0)),
                  pl.BlockSpec((1, c), lambda i: (0, 0))],
        out_specs=pl.BlockSpec((HY_ORDER, tr, c), lambda i: (0, i, 0)),
        out_shape=jax.ShapeDtypeStruct((HY_ORDER, n, c), F32),
        compiler_params=_cparams("parallel"),
        name="hyena_taps",
    )(feats2, tv, w1p, b1.reshape(1, hid), w2, b2.reshape(1, hid), w3r, freq.reshape(1, hid), deltas)


def _cs(num, den):
    ang = (2.0 * math.pi / den) * num.astype(F32)
    return jnp.cos(ang), jnp.sin(ang)


def fft_tables(n1, n2):
    n = n1 * n2
    h = n1 // 2
    k1 = jnp.arange(n1, dtype=jnp.int32)
    c1, s1 = _cs((k1[:, None] * k1[None, :]) % n1, n1)
    g_u = jnp.block([[c1[:, :h], s1[:, :h]], [-s1[:, :h], c1[:, :h]]])
    g_t = jnp.concatenate([c1, -s1], axis=0)
    g_i = jnp.block([[c1[:h, :], -s1[:h, :]], [s1[:h, :], c1[:h, :]]])
    s = jnp.arange(n2, dtype=jnp.int32)
    num = (n1 * (s[:, None] * s[None, :]))[None] + (k1[:, None, None] * s[None, None, :])
    cb, sb = _cs(num % n, n)
    m_f = jnp.concatenate([jnp.concatenate([cb, sb], axis=2),
                           jnp.concatenate([-sb, cb], axis=2)], axis=1)
    cbt, sbt = jnp.swapaxes(cb, 1, 2) / n, jnp.swapaxes(sb, 1, 2) / n
    m_i = jnp.concatenate([jnp.concatenate([cbt, -sbt], axis=2),
                           jnp.concatenate([sbt, cbt], axis=2)], axis=1)
    return dict(g_u=g_u.astype(BF16), g_t=g_t.astype(BF16), g_i=g_i.astype(BF16),
                m_f=m_f.astype(BF16), m_i=m_i.astype(BF16))


def _fft_a_kernel(g_ref, xa_ref, xb_ref, o_ref):
    h = xa_ref.shape[0]
    n1 = o_ref.shape[0]
    ga = g_ref[:, :h]
    gb = g_ref[:, h:]
    for s in range(xa_ref.shape[1]):
        r = _dot(ga, xa_ref[:, s, :].astype(BF16)) + _dot(gb, xb_ref[:, s, :].astype(BF16))
        o_ref[:, 0, s, :] = r[:n1]
        o_ref[:, 1, s, :] = r[n1:]


def fft_stage_a(g, x4, sel_a, sel_b, col_off, c, tc=256):
    n1 = g.shape[0] // 2
    h = n1 // 2
    n2 = x4.shape[2]
    ts = SUBLANES
    coff = col_off // tc
    return pl.pallas_call(
        _fft_a_kernel,
        grid=(c // tc, n2 // ts),
        in_specs=[pl.BlockSpec((2 * n1, n1), lambda j, s: (0, 0)),
                  pl.BlockSpec((None, h, ts, tc), lambda j, s: (sel_a[0], sel_a[1], s, j + coff)),
                  pl.BlockSpec((None, h, ts, tc), lambda j, s: (sel_b[0], sel_b[1], s, j + coff))],
        out_specs=pl.BlockSpec((n1, 2, ts, tc), lambda j, s: (0, 0, s, j)),
        out_shape=jax.ShapeDtypeStruct((n1, 2, n2, c), F32),
        compiler_params=_cparams("parallel", "parallel"),
        name="fft_stage_a",
    )(g, x4, x4)


def _fft_bh_kernel(m_ref, a_ref, o_ref):
    n2 = a_ref.shape[1]
    a = a_ref[...].reshape(2 * n2, a_ref.shape[2]).astype(BF16)
    o_ref[...] = _dot(m_ref[...], a).reshape(o_ref.shape)


def fft_stage_b_spectrum(m_f, a):
    n1, _, n2, c = a.shape
    return pl.pallas_call(
        _fft_bh_kernel,
        grid=(n1,),
        in_specs=[pl.BlockSpec((None, 2 * n2, 2 * n2), lambda k: (k, 0, 0)),
                  pl.BlockSpec((None, 2, n2, c), lambda k: (k, 0, 0, 0))],
        out_specs=pl.BlockSpec((None, 2, n2, c), lambda k: (k, 0, 0, 0)),
        out_shape=jax.ShapeDtypeStruct(a.shape, F32),
        compiler_params=_cparams("parallel"),
        name="fft_stage_b_spectrum",
    )(m_f, a)


def _fft_b_kernel(mf_ref, mi_ref, a_ref, h_ref, o_ref):
    n2 = a_ref.shape[1]
    a = a_ref[...].reshape(2 * n2, a_ref.shape[2]).astype(BF16)
    x = _dot(mf_ref[...], a)
    xr, xi = x[:n2], x[n2:]
    hr, hi = h_ref[0], h_ref[1]
    y = jnp.concatenate([xr * hr - xi * hi, xr * hi + xi * hr], axis=0).astype(BF16)
    o_ref[...] = _dot(mi_ref[...], y).reshape(o_ref.shape)


def fft_stage_b(m_f, m_i, a, hspec):
    n1, _, n2, c = a.shape
    blk = pl.BlockSpec((None, 2, n2, c), lambda k: (k, 0, 0, 0))
    mat = pl.BlockSpec((None, 2 * n2, 2 * n2), lambda k: (k, 0, 0))
    return pl.pallas_call(
        _fft_b_kernel,
        grid=(n1,),
        in_specs=[mat, mat, blk, blk],
        out_specs=blk,
        out_shape=jax.ShapeDtypeStruct(a.shape, F32),
        compiler_params=_cparams("parallel"),
        name="fft_stage_b",
    )(m_f, m_i, a, hspec)


def _fft_ai_kernel(g_ref, b_ref, u_ref, x_ref, skip_ref, o_ref):
    n1 = b_ref.shape[0]
    h = n1 // 2
    ga = g_ref[:, :n1]
    gb = g_ref[:, n1:]
    skip = skip_ref[...]
    for s in range(b_ref.shape[2]):
        y = _dot(ga, b_ref[:, 0, s, :].astype(BF16)) + _dot(gb, b_ref[:, 1, s, :].astype(BF16))
        for bi in range(2):
            yb = y[bi * h:(bi + 1) * h]
            o_ref[bi, :, s, :] = (x_ref[bi, :, s, :] * (yb + u_ref[bi, :, s, :] * skip)).astype(o_ref.dtype)


def fft_stage_a_inv(g_i, bm, u4, u_off, x4, x_off, skip_row, out_dtype, tc=256):
    n1, _, n2, c = bm.shape
    h = n1 // 2
    ts = SUBLANES
    uo, xo = u_off // tc, x_off // tc
    return pl.pallas_call(
        _fft_ai_kernel,
        grid=(c // tc, n2 // ts),
        in_specs=[pl.BlockSpec((n1, 2 * n1), lambda j, s: (0, 0)),
                  pl.BlockSpec((n1, 2, ts, tc), lambda j, s: (0, 0, s, j)),
                  pl.BlockSpec((2, h, ts, tc), lambda j, s: (0, 0, s, j + uo)),
                  pl.BlockSpec((2, h, ts, tc), lambda j, s: (0, 0, s, j + xo)),
                  pl.BlockSpec((1, tc), lambda j, s: (0, j))],
        out_specs=pl.BlockSpec((2, h, ts, tc), lambda j, s: (0, 0, s, j)),
        out_shape=jax.ShapeDtypeStruct((2, h, n2, c), out_dtype),
        compiler_params=_cparams("parallel", "parallel"),
        name="fft_stage_a_inv",
    )(g_i, bm, u4, x4, skip_row)


def hyena_long(u, taps, skip, tabs):
    b, length, c3 = u.shape
    assert b == 2, "the two batch rows are packed as one complex sequence"
    c = c3 // 3
    n2 = FFT_N2
    n1 = 2 * length // n2
    h = n1 // 2
    u4 = u.reshape(b, h, n2, c3)
    taps4 = taps.reshape(HY_ORDER * 2, h, n2, c)
    zin, zoff = u4, 0
    out = None
    for conv in range(HY_ORDER):
        hspec = fft_stage_b_spectrum(
            tabs["m_f"], fft_stage_a(tabs["g_t"], taps4, (2 * conv, 0), (2 * conv + 1, 0), 0, c))
        a = fft_stage_a(tabs["g_u"], zin, (0, 0), (1, 0), zoff, c)
        bm = fft_stage_b(tabs["m_f"], tabs["m_i"], a, hspec)
        last = conv == HY_ORDER - 1
        out = fft_stage_a_inv(tabs["g_i"], bm, zin, zoff, u4, (conv + 1) * c,
                              skip[conv].reshape(1, c), BF16 if last else F32)
        zin, zoff = out, 0
    return out.reshape(b, length, c)


def _hyena_ctx_kernel(u_ref, x1_ref, x2_ref, taps_ref, fc_ref, fs_ref, skip_ref, o_ref):
    length = u_ref.shape[1]
    n = 2 * length
    fc = fc_ref[...]
    fs = fs_ref[...]
    fcl, fsl = fc[:, :length], fs[:, :length]
    fct, fst = fc[:length, :], fs[:length, :]
    zr = u_ref[0]
    zi = u_ref[1]
    gates = (x1_ref, x2_ref)
    for conv in range(HY_ORDER):
        tp = taps_ref[conv].astype(BF16)
        hr, hi = _dot(fc, tp), -_dot(fs, tp)
        zrb, zib = zr.astype(BF16), zi.astype(BF16)
        xr = _dot(fcl, zrb) + _dot(fsl, zib)
        xi = _dot(fcl, zib) - _dot(fsl, zrb)
        yr = (xr * hr - xi * hi).astype(BF16)
        yi = (xr * hi + xi * hr).astype(BF16)
        cr = (_dot(fct, yr) - _dot(fst, yi)) * (1.0 / n)
        ci = (_dot(fct, yi) + _dot(fst, yr)) * (1.0 / n)
        sk = skip_ref[conv:conv + 1, :]
        zr = gates[conv][0] * (cr + zr * sk)
        zi = gates[conv][1] * (ci + zi * sk)
    o_ref[0] = zr.astype(o_ref.dtype)
    o_ref[1] = zi.astype(o_ref.dtype)


def hyena_ctx(u, taps, skip, tc=256):
    b, length, c3 = u.shape
    assert b == 2
    c = c3 // 3
    n = 2 * length
    k = jnp.arange(n, dtype=jnp.int32)
    fc, fs = _cs((k[:, None] * k[None, :]) % n, n)
    nb = c // tc
    ub = lambda off: pl.BlockSpec((2, length, tc), lambda j: (0, 0, j + off * nb))
    mat = pl.BlockSpec((n, n), lambda j: (0, 0))
    return pl.pallas_call(
        _hyena_ctx_kernel,
        grid=(nb,),
        in_specs=[ub(0), ub(1), ub(2),
                  pl.BlockSpec((HY_ORDER, n, tc), lambda j: (0, 0, j)),
                  mat, mat,
                  pl.BlockSpec((HY_ORDER, tc), lambda j: (0, j))],
        out_specs=pl.BlockSpec((2, length, tc), lambda j: (0, 0, j)),
        out_shape=jax.ShapeDtypeStruct((b, length, c), BF16),
        compiler_params=_cparams("parallel"),
        name="hyena_ctx",
    )(u, u, u, taps, fc.astype(BF16), fs.astype(BF16), skip)


def rope_tables(length):
    rows = length // GRID_W
    row = jnp.repeat(jnp.arange(rows, dtype=F32), GRID_W)
    col = (jnp.arange(length) % GRID_W).astype(F32)
    inv = ROPE_THETA ** (-jnp.arange(0, ROPE_AXIS_DIM, 2, dtype=F32) / ROPE_AXIS_DIM)
    ang = jnp.stack([row[:, None] * inv, col[:, None] * inv], axis=1)
    cos, sin = jnp.cos(ang), jnp.sin(ang)
    zero = jnp.zeros_like(sin)
    per_map = lambda lo, hi: jnp.concatenate([lo, hi], axis=-1).reshape(length, 2 * ROPE_AXIS_DIM)
    reps = LANES // (2 * ROPE_AXIS_DIM)
    c_t = jnp.tile(per_map(cos, cos), (1, reps))
    sm_t = jnp.tile(per_map(-sin, zero), (1, reps))
    sp_t = jnp.tile(per_map(zero, sin), (1, reps))
    return c_t, sm_t, sp_t


def _rope_kernel(x_ref, c_ref, sm_ref, sp_ref, o_ref, *, scale):
    half = ROPE_AXIS_DIM // 2
    c, sm, sp = c_ref[...], sm_ref[...], sp_ref[...]
    for g in range(x_ref.shape[1] // LANES):
        ls = slice(g * LANES, (g + 1) * LANES)
        x = x_ref[:, ls]
        y = x * c + pltpu.roll(x, LANES - half, 1) * sm + pltpu.roll(x, half, 1) * sp
        o_ref[:, ls] = (y * scale).astype(o_ref.dtype)


def rope(x, tables, scale, tr=512):
    b, t, d = x.shape
    tr = _pick_tile(t, tr)
    tab = pl.BlockSpec((tr, LANES), lambda bi, i: (i, 0))
    return pl.pallas_call(
        functools.partial(_rope_kernel, scale=scale),
        grid=(b, t // tr),
        in_specs=[pl.BlockSpec((None, tr, d), lambda bi, i: (bi, i, 0)), tab, tab, tab],
        out_specs=pl.BlockSpec((None, tr, d), lambda bi, i: (bi, i, 0)),
        out_shape=jax.ShapeDtypeStruct((b, t, d), BF16),
        compiler_params=_cparams("parallel", "parallel"),
        name="rope",
    )(x, *tables)


def _scale_cast_kernel(x_ref, o_ref, *, scale):
    o_ref[...] = (x_ref[...] * scale).astype(o_ref.dtype)


def scale_cast(x, scale, tr=256):
    b, t, d = x.shape
    tr = _pick_tile(t, tr)
    blk = pl.BlockSpec((None, tr, d), lambda bi, i: (bi, i, 0))
    return pl.pallas_call(
        functools.partial(_scale_cast_kernel, scale=scale),
        grid=(b, t // tr), in_specs=[blk], out_specs=blk,
        out_shape=jax.ShapeDtypeStruct((b, t, d), BF16),
        compiler_params=_cparams("parallel", "parallel"),
        name="scale_cast",
    )(x)


def _attn_kernel(*refs, lam_init, nk, with_ctx):
    if with_ctx:
        q_ref, k_ref, v_ref, kc_ref, vc_ref, lam_ref, w_ref, o_ref, q2_ref, m_ref, l_ref, acc_ref = refs
    else:
        q_ref, k_ref, v_ref, lam_ref, w_ref, o_ref, q2_ref, m_ref, l_ref, acc_ref = refs
    ki = pl.program_id(3)
    tq = q_ref.shape[0]

    def update(k, v):
        s = _dot_nt(q2_ref[...], k)
        m_prev = m_ref[...]
        m_new = jnp.maximum(m_prev, jnp.max(s, axis=1, keepdims=True))
        alpha = jnp.exp(m_prev - m_new)
        p = jnp.exp(s - m_new)
        l_ref[...] = alpha * l_ref[...] + jnp.sum(p, axis=1, keepdims=True)
        acc_ref[...] = alpha * acc_ref[...] + _dot(p.astype(BF16), v)
        m_ref[...] = m_new

    @pl.when(ki == 0)
    def _():
        q = q_ref[...]
        lane = lax.broadcasted_iota(jnp.int32, q.shape, 1)
        zero = jnp.zeros_like(q)
        q2_ref[:tq, :] = jnp.where(lane < DA_HEAD_DIM, q, zero)
        q2_ref[tq:, :] = jnp.where(lane >= DA_HEAD_DIM, q, zero)
        m_ref[...] = jnp.full(m_ref.shape, -jnp.inf, F32)
        l_ref[...] = jnp.zeros(l_ref.shape, F32)
        acc_ref[...] = jnp.zeros(acc_ref.shape, F32)
        if with_ctx:
            update(kc_ref[...], vc_ref[...])

    update(k_ref[...], v_ref[...])

    @pl.when(ki == nk - 1)
    def _():
        lv = lam_ref[...]
        e1 = jnp.exp(jnp.sum(lv[0:1, :] * lv[1:2, :], axis=1, keepdims=True))
        e2 = jnp.exp(jnp.sum(lv[2:3, :] * lv[3:4, :], axis=1, keepdims=True))
        lam = e1 - e2 + lam_init
        o = acc_ref[...] / l_ref[...]
        o = o[:tq] - lam * o[tq:]
        o_ref[...] = (_rms(o, w_ref[...]) * (1.0 - lam_init)).astype(o_ref.dtype)


def diff_attention(q, k, v, k_ctx, v_ctx, lam_vec, subln_w, lam_init, tq=512, tk=1024):
    b, t, d = q.shape
    nh = d // DA_V_DIM
    tq = _pick_tile(t, tq)
    tk = _pick_tile(k.shape[1], tk)
    nk = k.shape[1] // tk
    with_ctx = k_ctx is not None
    qspec = pl.BlockSpec((None, tq, DA_V_DIM), lambda bi, h, i, j: (bi, i, h))
    kspec = pl.BlockSpec((None, tk, DA_V_DIM), lambda bi, h, i, j: (bi, j, h))
    in_specs = [qspec, kspec, kspec]
    args = [q, k, v]
    if with_ctx:
        cspec = pl.BlockSpec((None, k_ctx.shape[1], DA_V_DIM), lambda bi, h, i, j: (bi, 0, h))
        in_specs += [cspec, cspec]
        args += [k_ctx, v_ctx]
    in_specs += [pl.BlockSpec((4, DA_HEAD_DIM), lambda bi, h, i, j: (0, 0)),
                 pl.BlockSpec((1, DA_V_DIM), lambda bi, h, i, j: (0, 0))]
    args += [lam_vec, subln_w.reshape(1, DA_V_DIM)]
    return pl.pallas_call(
        functools.partial(_attn_kernel, lam_init=lam_init, nk=nk, with_ctx=with_ctx),
        grid=(b, nh, t // tq, nk),
        in_specs=in_specs,
        out_specs=qspec,
        out_shape=jax.ShapeDtypeStruct((b, t, d), BF16),
        scratch_shapes=[pltpu.VMEM((2 * tq, DA_V_DIM), BF16),
                        pltpu.VMEM((2 * tq, 1), F32),
                        pltpu.VMEM((2 * tq, 1), F32),
                        pltpu.VMEM((2 * tq, DA_V_DIM), F32)],
        compiler_params=_cparams("parallel", "parallel", "parallel", "arbitrary"),
        name="diff_attention",
    )(*args)


def _merge_kernel(ya_ref, yb_ref, yc_ref, g_ref, wb_ref, wo_ref, x_ref, mod_ref, o_ref):
    d = x_ref.shape[1]
    ys = (ya_ref, yb_ref, yc_ref)
    s = None
    for k in range(N_BRANCH):
        t = jax.nn.sigmoid(g_ref[:, k * d:(k + 1) * d]) * _dot(ys[k][...], wb_ref[k])
        s = t if s is None else s + t
    r = _dot(s.astype(BF16), wo_ref[...])
    o_ref[...] = x_ref[...] + mod_ref[2:3, :] * r


def merge(ya, yb, yc, gates, wb, wo, x, mods, tr=256):
    b, t, d = x.shape
    tr = _pick_tile(t, tr)
    per_batch = mods.shape[0] > 1
    blk = pl.BlockSpec((None, tr, d), lambda bi, i: (bi, i, 0))
    return pl.pallas_call(
        _merge_kernel,
        grid=(b, t // tr),
        in_specs=[blk, blk, blk,
                  pl.BlockSpec((None, tr, N_BRANCH * d), lambda bi, i: (bi, i, 0)),
                  pl.BlockSpec((N_BRANCH, d, d), lambda bi, i: (0, 0, 0)),
                  pl.BlockSpec((d, d), lambda bi, i: (0, 0)),
                  blk,
                  pl.BlockSpec((None, 8, d), (lambda bi, i: (bi, 0, 0)) if per_batch else (lambda bi, i: (0, 0, 0)))],
        out_specs=blk,
        out_shape=jax.ShapeDtypeStruct((b, t, d), F32),
        compiler_params=_cparams("parallel", "parallel"),
        name="merge",
    )(ya, yb, yc, gates, wb, wo, x, mods)


def _ffn_kernel(x_ref, g_ref, mod_ref, w13_ref, w2_ref, o_ref, *, nchunk):
    x = x_ref[...]
    h = (_rms(x, g_ref[...]) * (1.0 + mod_ref[4:5, :]) + mod_ref[3:4, :]).astype(BF16)
    f = w2_ref.shape[0]
    fc = f // nchunk
    acc = None
    for c in range(nchunk):
        a1 = _dot(h, w13_ref[:, c * fc:(c + 1) * fc])
        a3 = _dot(h, w13_ref[:, f + c * fc:f + (c + 1) * fc])
        t = _dot((_silu(a1) * a3).astype(BF16), w2_ref[c * fc:(c + 1) * fc, :])
        acc = t if acc is None else acc + t
    o_ref[...] = x + mod_ref[5:6, :] * acc


def ffn(x, g, mods, w13, w2, tr=256):
    b, t, d = x.shape
    f = w2.shape[0]
    tr = _pick_tile(t, tr)
    nchunk = 2 if (f // 2) % LANES == 0 else 1
    per_batch = mods.shape[0] > 1
    blk = pl.BlockSpec((None, tr, d), lambda bi, i: (bi, i, 0))
    return pl.pallas_call(
        functools.partial(_ffn_kernel, nchunk=nchunk),
        grid=(b, t // tr),
        in_specs=[blk,
                  pl.BlockSpec((1, d), lambda bi, i: (0, 0)),
                  pl.BlockSpec((None, 8, d), (lambda bi, i: (bi, 0, 0)) if per_batch else (lambda bi, i: (0, 0, 0))),
                  pl.BlockSpec((d, 2 * f), lambda bi, i: (0, 0)),
                  pl.BlockSpec((f, d), lambda bi, i: (0, 0))],
        out_specs=blk,
        out_shape=jax.ShapeDtypeStruct((b, t, d), F32),
        compiler_params=_cparams("parallel", "parallel"),
        name="ffn",
    )(x, g.reshape(1, d), mods, w13, w2)


def kernel(x, c, ctx, c_ctx, ada_w, ada_b, norm1_g, norm2_g, w_in, ssd_conv_w, ssd_conv_b,
           ssd_a_log, ssd_dt_bias, ssd_d, ssd_norm_w, hy_conv_w, hy_conv_b, hy_w1, hy_b1, hy_w2,
           hy_b2, hy_w3, hy_freq, hy_bias, da_lambda, da_subln_w, w_branch, w_out, ffn_w13,
           ffn_w2, final_g):
    bsz, length, d = x.shape
    ctx_len = ctx.shape[1]
    depth = w_in.shape[0]
    inner = SSD_HEADS * SSD_HEAD_DIM
    xbc_w = inner + 2 * SSD_GROUPS * SSD_STATE
    dt_w = 2 * SSD_HEADS
    hy_w = (HY_ORDER + 1) * d
    widths = (inner, xbc_w, dt_w, hy_w, d, d, d, N_BRANCH * d)
    offs = [0]
    for wd in widths:
        offs.append(offs[-1] + wd)

    cond = jnp.concatenate([c, c_ctx[None, :], jnp.zeros((8 - bsz - 1, d), F32)], axis=0)

    rope_tabs = rope_tables(length)
    tabs = fft_tables(2 * length // FFT_N2, FFT_N2)
    expand = (jnp.arange(LANES)[:, None] % SSD_HEADS
              == jnp.arange(inner)[None, :] // SSD_HEAD_DIM)
    expand_dir = [(expand & ((jnp.arange(LANES)[:, None] // SSD_HEADS) == dr)).astype(BF16) for dr in range(2)]
    qscale = DA_HEAD_DIM ** -0.5

    x_l, x_c = x, ctx
    for layer in range(depth):
        last = layer == depth - 1
        lam_init = 0.8 - 0.6 * math.exp(-0.3 * layer)
        mod = matmul(cond, ada_w[layer].astype(BF16), ada_b[layer].reshape(1, -1), silu_in=True)
        mod = jnp.pad(mod.reshape(8, 6, d), ((0, 0), (0, 2), (0, 0)))
        mods_l, mods_c = mod[:bsz], mod[bsz:bsz + 1]

        w_l = w_in[layer].astype(BF16)
        w_parts = [w_l[:, offs[i]:offs[i + 1]] for i in range(len(widths))]
        w_parts[2] = jnp.pad(w_parts[2], ((0, 0), (0, LANES - dt_w)))
        a_neg = -jnp.exp(ssd_a_log[layer].astype(F32)).reshape(1, dt_w)
        aneg_row = jnp.pad(a_neg, ((0, 0), (0, LANES - dt_w)))
        bias_row = jnp.pad(ssd_dt_bias[layer].astype(F32).reshape(1, dt_w), ((0, 0), (0, LANES - dt_w)))
        d_row = jnp.repeat(ssd_d[layer].astype(F32), SSD_HEAD_DIM).reshape(1, inner)
        nw_row = ssd_norm_w[layer].astype(F32).reshape(1, inner)
        wb = w_branch[layer].astype(BF16)
        wo = w_out[layer].astype(BF16)
        w13 = ffn_w13[layer].astype(BF16)
        w2 = ffn_w2[layer].astype(BF16)

        def project(stream, mods):
            bb, tt, _ = stream.shape
            h = norm_mod(stream, norm1_g[layer], mods, 0).reshape(bb * tt, d)
            outs = [matmul(h, wp, out_dtype=(BF16 if i == 6 else F32)).reshape(bb, tt, -1)
                    for i, wp in enumerate(w_parts)]
            return outs

        z_c, xbc_c, dt_c, hy_c, q_c, k_c, v_c, g_c = project(x_c, mods_c)
        z_l, xbc_l, dt_l, hy_l, q_l, k_l, v_l, g_l = project(x_l, mods_l)

        xa_c = dwconv3(xbc_c, ssd_conv_w[layer], ssd_conv_b[layer], True)
        xa_l = dwconv3(xbc_l, ssd_conv_w[layer], ssd_conv_b[layer], True)
        h_zero = jnp.zeros((bsz, SSD_STATE, inner), F32)
        ys_c, ys_l = [], []
        for dr in range(2):
            y_c, st = ssd_scan(xa_c, dt_c, bias_row, aneg_row, expand_dir[dr], h_zero, dr)
            y_l, _ = ssd_scan(xa_l, dt_l, bias_row, aneg_row, expand_dir[dr], st, dr)
            ys_c.append(y_c)
            ys_l.append(y_l)
        ya_l = ssd_out(ys_l[0], ys_l[1], xa_l, z_l, d_row, nw_row)

        filt = (hy_w1[layer], hy_b1[layer], hy_w2[layer], hy_b2[layer], hy_w3[layer], hy_freq[layer])
        u_l = dwconv3(hy_l, hy_conv_w[layer], hy_conv_b[layer], False)
        yb_l = hyena_long(u_l, hyena_taps(length, *filt), hy_bias[layer], tabs)

        qr_l = rope(q_l, rope_tabs, qscale)
        kr_l = rope(k_l, rope_tabs, 1.0)
        kb_c = scale_cast(k_c, 1.0)
        yc_l = diff_attention(qr_l, kr_l, v_l, kb_c, v_c, da_lambda[layer], da_subln_w[layer], lam_init)

        x_l_new = merge(ya_l, yb_l, yc_l, g_l, wb, wo, x_l, mods_l)
        x_l = ffn(x_l_new, norm2_g[layer], mods_l, w13, w2)

        if not last:
            ya_c = ssd_out(ys_c[0], ys_c[1], xa_c, z_c, d_row, nw_row)
            u_c = dwconv3(hy_c, hy_conv_w[layer], hy_conv_b[layer], False)
            yb_c = hyena_ctx(u_c, hyena_taps(ctx_len, *filt), hy_bias[layer])
            qb_c = scale_cast(q_c, qscale)
            yc_c = diff_attention(qb_c, kb_c, v_c, None, None, da_lambda[layer], da_subln_w[layer], lam_init)
            x_c_new = merge(ya_c, yb_c, yc_c, g_c, wb, wo, x_c, mods_c)
            x_c = ffn(x_c_new, norm2_g[layer], mods_c, w13, w2)
    return final_norm(x_l, final_g)
```

```python
import functools
import math

import jax
import jax.numpy as jnp
from jax import lax
from jax.experimental import pallas as pl
from jax.experimental.pallas import tpu as pltpu

F32 = jnp.float32
BF16 = jnp.bfloat16
EPS = 1e-6

SSD_HEADS = 16
SSD_HEAD_DIM = 64
SSD_GROUPS = 2
SSD_STATE = 128
SSD_CHUNK = 128
HY_ORDER = 2
HY_BANDS = 16
HY_SHORT_DECAY_PCT = 0.3
HY_LONG_DECAY_PCT = 1.5
HY_DECAY_TARGET = 1e-2
DA_HEADS = 8
DA_V_DIM = 128
DA_HEAD_DIM = 64
ROPE_AXIS_DIM = 32
ROPE_THETA = 10000.0
GRID_W = 64
N_BRANCH = 3

LANES = 128
SUBLANES = 8
FFT_N2 = 128
VMEM_LIMIT = 56 * 1024 * 1024


def _cparams(*sem):
    return pltpu.CompilerParams(dimension_semantics=sem, vmem_limit_bytes=VMEM_LIMIT)


def _dot(a, b):
    return jnp.dot(a, b, preferred_element_type=F32)


def _dot_nt(a, b):
    return lax.dot_general(a, b, (((1,), (1,)), ((), ())), preferred_element_type=F32)


def _split3(x):
    hi = x.astype(BF16)
    r1 = x - hi.astype(F32)
    mid = r1.astype(BF16)
    lo = (r1 - mid.astype(F32)).astype(BF16)
    return hi, mid, lo


def _silu(x):
    return x * jax.nn.sigmoid(x)


def _softplus(x):
    return jnp.maximum(x, 0.0) + jnp.log1p(jnp.exp(-jnp.abs(x)))


def _pick_tile(n, pref):
    t = min(pref, n)
    while n % t:
        t //= 2
    return t


def _matmul_kernel(a_ref, w_ref, b_ref, o_ref, *, silu_in):
    a = a_ref[...]
    if silu_in:
        a = _silu(a.astype(F32))
    r = _dot(a.astype(BF16), w_ref[...]) + b_ref[...]
    o_ref[...] = r.astype(o_ref.dtype)


def matmul(a, w, bias=None, out_dtype=F32, silu_in=False, tm=1024, tn=512):
    m, k = a.shape
    n = w.shape[1]
    tm = _pick_tile(m, tm)
    tn = _pick_tile(n, tn)
    if bias is None:
        bias = jnp.zeros((1, n), F32)
    return pl.pallas_call(
        functools.partial(_matmul_kernel, silu_in=silu_in),
        grid=(m // tm, n // tn),
        in_specs=[pl.BlockSpec((tm, k), lambda i, j: (i, 0)),
                  pl.BlockSpec((k, tn), lambda i, j: (0, j)),
                  pl.BlockSpec((1, tn), lambda i, j: (0, j))],
        out_specs=pl.BlockSpec((tm, tn), lambda i, j: (i, j)),
        out_shape=jax.ShapeDtypeStruct((m, n), out_dtype),
        compiler_params=_cparams("parallel", "parallel"),
        name="matmul",
    )(a, w, bias)


def _rms(x, g):
    return x * lax.rsqrt(jnp.mean(x * x, axis=-1, keepdims=True) + EPS) * g


def _norm_mod_kernel(x_ref, g_ref, mod_ref, o_ref, *, row0):
    x = x_ref[...]
    y = _rms(x, g_ref[...])
    shift = mod_ref[row0:row0 + 1, :]
    scale = mod_ref[row0 + 1:row0 + 2, :]
    o_ref[...] = (y * (1.0 + scale) + shift).astype(o_ref.dtype)


def norm_mod(x, g, mods, row0, tr=512):
    b, t, d = x.shape
    tr = _pick_tile(t, tr)
    per_batch = mods.shape[0] > 1
    return pl.pallas_call(
        functools.partial(_norm_mod_kernel, row0=row0),
        grid=(b, t // tr),
        in_specs=[pl.BlockSpec((None, tr, d), lambda bi, i: (bi, i, 0)),
                  pl.BlockSpec((1, d), lambda bi, i: (0, 0)),
                  pl.BlockSpec((None, 8, d), (lambda bi, i: (bi, 0, 0)) if per_batch else (lambda bi, i: (0, 0, 0)))],
        out_specs=pl.BlockSpec((None, tr, d), lambda bi, i: (bi, i, 0)),
        out_shape=jax.ShapeDtypeStruct((b, t, d), BF16),
        compiler_params=_cparams("parallel", "parallel"),
        name="norm_mod",
    )(x, g.reshape(1, d), mods)


def _final_norm_kernel(x_ref, g_ref, o_ref):
    o_ref[...] = _rms(x_ref[...], g_ref[...])


def final_norm(x, g, tr=512):
    b, t, d = x.shape
    tr = _pick_tile(t, tr)
    return pl.pallas_call(
        _final_norm_kernel,
        grid=(b, t // tr),
        in_specs=[pl.BlockSpec((None, tr, d), lambda bi, i: (bi, i, 0)),
                  pl.BlockSpec((1, d), lambda bi, i: (0, 0))],
        out_specs=pl.BlockSpec((None, tr, d), lambda bi, i: (bi, i, 0)),
        out_shape=jax.ShapeDtypeStruct((b, t, d), F32),
        compiler_params=_cparams("parallel", "parallel"),
        name="final_norm",
    )(x, g.reshape(1, d))


def _dwconv_kernel(x_ref, xp_ref, xn_ref, w_ref, b_ref, o_ref, *, act, nblk):
    i = pl.program_id(1)
    x = x_ref[...]
    tr = x.shape[0]
    prev_row = jnp.where(i == 0, 0.0, xp_ref[SUBLANES - 1:SUBLANES, :])
    next_row = jnp.where(i == nblk - 1, 0.0, xn_ref[0:1, :])
    rows = lax.broadcasted_iota(jnp.int32, x.shape, 0)
    x_m1 = jnp.where(rows == 0, prev_row, pltpu.roll(x, 1, 0))
    x_p1 = jnp.where(rows == tr - 1, next_row, pltpu.roll(x, tr - 1, 0))
    y = b_ref[...] + x_m1 * w_ref[0:1, :] + x * w_ref[1:2, :] + x_p1 * w_ref[2:3, :]
    if act:
        y = _silu(y)
    o_ref[...] = y


def dwconv3(x, w, bias, act, tr=256, tc=512):
    b, t, c = x.shape
    tr = _pick_tile(t, tr)
    tc = _pick_tile(c, tc)
    nblk = t // tr
    r8 = tr // SUBLANES
    last8 = t // SUBLANES - 1
    return pl.pallas_call(
        functools.partial(_dwconv_kernel, act=act, nblk=nblk),
        grid=(b, nblk, c // tc),
        in_specs=[pl.BlockSpec((None, tr, tc), lambda bi, i, j: (bi, i, j)),
                  pl.BlockSpec((None, SUBLANES, tc), lambda bi, i, j: (bi, jnp.maximum(i * r8 - 1, 0), j)),
                  pl.BlockSpec((None, SUBLANES, tc), lambda bi, i, j: (bi, jnp.minimum((i + 1) * r8, last8), j)),
                  pl.BlockSpec((3, tc), lambda bi, i, j: (0, j)),
                  pl.BlockSpec((1, tc), lambda bi, i, j: (0, j))],
        out_specs=pl.BlockSpec((None, tr, tc), lambda bi, i, j: (bi, i, j)),
        out_shape=jax.ShapeDtypeStruct((b, t, c), F32),
        compiler_params=_cparams("parallel", "parallel", "parallel"),
        name="dwconv3",
    )(x, x, x, w, bias.reshape(1, c))


def _ssd_kernel(x_ref, b_ref, c_ref, dt_ref, bias_ref, aneg_ref, e_ref, h0_ref,
                y_ref, hout_ref, st_ref, *, direction, nchunks):
    ci = pl.program_id(1)

    @pl.when(ci == 0)
    def _():
        st_ref[...] = h0_ref[...]

    q = x_ref.shape[0]
    hd = SSD_HEAD_DIM
    gw = (SSD_HEADS // SSD_GROUPS) * hd
    x = x_ref[...]
    dtv = _softplus(dt_ref[...] + bias_ref[...])
    adt = dtv * aneg_ref[...]
    row = lax.broadcasted_iota(jnp.int32, (q, q), 0)
    col = lax.broadcasted_iota(jnp.int32, (q, q), 1)
    mask = (col >= row) if direction else (col <= row)
    tri = mask.astype(BF16)
    acum = sum(_dot(tri, p) for p in _split3(adt))
    acum_t = sum(_dot_nt(p, tri) for p in _split3(adt.T))
    total = jnp.sum(adt, axis=0, keepdims=True)
    eac = jnp.exp(acum)
    dte = jnp.exp(total - acum)
    cd = jnp.broadcast_to(jnp.exp(total), (SUBLANES, LANES))
    stack = jnp.concatenate([dtv, eac, dte, cd], axis=0)
    e01 = e_ref[...]
    hi = stack.astype(BF16)
    lo = (stack - hi.astype(F32)).astype(BF16)
    ex = _dot(hi, e01) + _dot(lo, e01)
    dt_e, eac_e, dte_e, cd_e = ex[:q], ex[q:2 * q], ex[2 * q:3 * q], ex[3 * q:3 * q + 1]
    xdt = x * dt_e
    lane = lax.broadcasted_iota(jnp.int32, (q, LANES), 1)
    first = lane < hd
    for g in range(SSD_GROUPS):
        gs = slice(g * gw, (g + 1) * gw)
        c_g = c_ref[:, g * SSD_STATE:(g + 1) * SSD_STATE].astype(BF16)
        b_g32 = b_ref[:, g * SSD_STATE:(g + 1) * SSD_STATE]
        s_g = st_ref[:, gs]
        y_off = _dot(c_g, s_g.astype(BF16)) * eac_e[:, gs]
        cb = _dot_nt(c_g, b_g32.astype(BF16))
        for jp in range(gw // LANES):
            h_a = g * (SSD_HEADS // SSD_GROUPS) + 2 * jp
            ls = slice(h_a * hd, h_a * hd + LANES)
            xpair = xdt[:, ls]
            acc = y_off[:, jp * LANES:(jp + 1) * LANES]
            for k in range(2):
                cix = direction * SSD_HEADS + h_a + k
                seg = acum[:, cix:cix + 1] - acum_t[cix:cix + 1, :]
                dec = jnp.where(mask, jnp.exp(jnp.where(mask, seg, 0.0)), 0.0)
                w = (cb * dec).astype(BF16)
                xk = jnp.where(first if k == 0 else jnp.logical_not(first), xpair, 0.0).astype(BF16)
                acc = acc + _dot(w, xk)
            y_ref[:, ls] = acc
        xdte = (xdt[:, gs] * dte_e[:, gs]).astype(BF16)
        st_ref[:, gs] = s_g * cd_e[:, gs] + _dot(b_g32.T.astype(BF16), xdte)

    @pl.when(ci == nchunks - 1)
    def _():
        hout_ref[...] = st_ref[...]


def ssd_scan(xbc, dt_raw, dt_bias_row, aneg_row, expand, h0, direction):
    b, t, _ = xbc.shape
    q = SSD_CHUNK
    nchunks = t // q
    inner = SSD_HEADS * SSD_HEAD_DIM
    bn = SSD_GROUPS * SSD_STATE
    cmap = (lambda c: nchunks - 1 - c) if direction else (lambda c: c)
    y, hout = pl.pallas_call(
        functools.partial(_ssd_kernel, direction=direction, nchunks=nchunks),
        grid=(b, nchunks),
        in_specs=[pl.BlockSpec((None, q, inner), lambda bi, c: (bi, cmap(c), 0)),
                  pl.BlockSpec((None, q, bn), lambda bi, c: (bi, cmap(c), inner // bn)),
                  pl.BlockSpec((None, q, bn), lambda bi, c: (bi, cmap(c), inner // bn + 1)),
                  pl.BlockSpec((None, q, LANES), lambda bi, c: (bi, cmap(c), 0)),
                  pl.BlockSpec((1, LANES), lambda bi, c: (0, 0)),
                  pl.BlockSpec((1, LANES), lambda bi, c: (0, 0)),
                  pl.BlockSpec((LANES, inner), lambda bi, c: (0, 0)),
                  pl.BlockSpec((None, SSD_STATE, inner), lambda bi, c: (bi, 0, 0))],
        out_specs=[pl.BlockSpec((None, q, inner), lambda bi, c: (bi, cmap(c), 0)),
                   pl.BlockSpec((None, SSD_STATE, inner), lambda bi, c: (bi, 0, 0))],
        out_shape=[jax.ShapeDtypeStruct((b, t, inner), F32),
                   jax.ShapeDtypeStruct((b, SSD_STATE, inner), F32)],
        scratch_shapes=[pltpu.VMEM((SSD_STATE, inner), F32)],
        compiler_params=_cparams("parallel", "arbitrary"),
        name="ssd_scan_bwd" if direction else "ssd_scan_fwd",
    )(xbc, xbc, xbc, dt_raw, dt_bias_row, aneg_row, expand, h0)
    return y, hout


def _ssd_out_kernel(yf_ref, yb_ref, x_ref, z_ref, d_ref, w_ref, o_ref):
    y = yf_ref[...] + yb_ref[...] + d_ref[...] * x_ref[...]
    y = y * _silu(z_ref[...])
    gw = y.shape[1] // SSD_GROUPS
    for g in range(SSD_GROUPS):
        gs = slice(g * gw, (g + 1) * gw)
        yg = y[:, gs]
        yg = yg * lax.rsqrt(jnp.mean(yg * yg, axis=-1, keepdims=True) + EPS)
        o_ref[:, gs] = (yg * w_ref[:, gs]).astype(o_ref.dtype)


def ssd_out(yf, yb, xbc, z, d_row, w_row, tr=256):
    b, t, inner = yf.shape
    tr = _pick_tile(t, tr)
    blk = pl.BlockSpec((None, tr, inner), lambda bi, i: (bi, i, 0))
    vec = pl.BlockSpec((1, inner), lambda bi, i: (0, 0))
    return pl.pallas_call(
        _ssd_out_kernel,
        grid=(b, t // tr),
        in_specs=[blk, blk, blk, blk, vec, vec],
        out_specs=blk,
        out_shape=jax.ShapeDtypeStruct((b, t, inner), BF16),
        compiler_params=_cparams("parallel", "parallel"),
        name="ssd_out",
    )(yf, yb, xbc, z, d_row, w_row)


def _taps_kernel(f_ref, tv_ref, w1_ref, b1_ref, w2_ref, b2_ref, w3_ref, fr_ref, dl_ref, o_ref):
    hp = lax.Precision.HIGHEST
    fr = fr_ref[...]
    h = jnp.sin(fr * (jnp.dot(f_ref[...], w1_ref[...], precision=hp, preferred_element_type=F32) + b1_ref[...]))
    h = jnp.sin(fr * (jnp.dot(h, w2_ref[...], precision=hp, preferred_element_type=F32) + b2_ref[...]))
    filt = jnp.dot(h, w3_ref[...], precision=hp, preferred_element_type=F32)
    c = dl_ref.shape[1]
    win = jnp.exp(-tv_ref[:, 0:1] * dl_ref[...]) * tv_ref[:, 1:2]
    o_ref[0] = filt[:, :c] * win
    o_ref[1] = filt[:, c:] * win


def hyena_taps(length, w1, b1, w2, b2, w3, freq):
    hid = w1.shape[1]
    c = w3.shape[1] // (2 * HY_ORDER)
    n = 2 * length
    t = jnp.linspace(0.0, 1.0, length, dtype=F32)[:, None]
    phase = 2.0 * math.pi * jnp.arange(length, dtype=F32)[:, None] / length
    bands = jnp.linspace(1e-4, HY_BANDS - 1, HY_BANDS, dtype=F32)[None, :]
    feats = jnp.concatenate([t, jnp.cos(phase * bands), -jnp.sin(phase * bands)], axis=-1)
    nf = feats.shape[1]
    rev = jnp.concatenate([jnp.zeros((1,), jnp.int32), jnp.arange(length - 1, 0, -1, dtype=jnp.int32)])
    feats2 = jnp.concatenate([feats, feats[rev]], axis=0)
    feats2 = jnp.pad(feats2, ((0, 0), (0, LANES - nf)))
    valid = jnp.ones((n,), F32).at[length].set(0.0)
    tv = jnp.stack([jnp.concatenate([t[:, 0], t[rev, 0]]), valid], axis=1)
    max_decay = math.log(HY_DECAY_TARGET) / HY_SHORT_DECAY_PCT
    min_decay = math.log(HY_DECAY_TARGET) / HY_LONG_DECAY_PCT
    deltas = jnp.abs(jnp.linspace(min_decay, max_decay, c, dtype=F32))[None, :]
    w1p = jnp.pad(w1, ((0, LANES - nf), (0, 0)))
    w3r = w3.reshape(hid, HY_ORDER, 2, c).transpose(2, 0, 1, 3).reshape(2, hid, HY_ORDER * c)
    tr = _pick_tile(length, 256)
    nhalf = length // tr
    return pl.pallas_call(
        _taps_kernel,
        grid=(n // tr,),
        in_specs=[pl.BlockSpec((tr, LANES), lambda i: (i, 0)),
                  pl.BlockSpec((tr, 2), lambda i: (i, 0)),
                  pl.BlockSpec((LANES, hid), lambda i: (0, 0)),
                  pl.BlockSpec((1, hid), lambda i: (0, 0)),
                  pl.BlockSpec((hid, hid), lambda i: (0, 0)),
                  pl.BlockSpec((1, hid), lambda i: (0, 0)),
                  pl.BlockSpec((None, hid, HY_ORDER * c), lambda i: (i // nhalf, 0, 0)),
                  pl.BlockSpec((1, hid), lambda i: (0, 0)),
                  pl.BlockSpec((1, c), lambda i: (0, 0))],
        out_specs=pl.BlockSpec((HY_ORDER, tr, c), lambda i: (0, i, 0)),
        out_shape=jax.ShapeDtypeStruct((HY_ORDER, n, c), F32),
        compiler_params=_cparams("parallel"),
        name="hyena_taps",
    )(feats2, tv, w1p, b1.reshape(1, hid), w2, b2.reshape(1, hid), w3r, freq.reshape(1, hid), deltas)


def _cs(num, den):
    ang = (2.0 * math.pi / den) * num.astype(F32)
    return jnp.cos(ang), jnp.sin(ang)


def fft_tables(n1, n2):
    n = n1 * n2
    h = n1 // 2
    k1 = jnp.arange(n1, dtype=jnp.int32)
    c1, s1 = _cs((k1[:, None] * k1[None, :]) % n1, n1)
    g_u = jnp.block([[c1[:, :h], s1[:, :h]], [-s1[:, :h], c1[:, :h]]])
    g_t = jnp.concatenate([c1, -s1], axis=0)
    g_i = jnp.block([[c1[:h, :], -s1[:h, :]], [s1[:h, :], c1[:h, :]]])
    s = jnp.arange(n2, dtype=jnp.int32)
    num = (n1 * (s[:, None] * s[None, :]))[None] + (k1[:, None, None] * s[None, None, :])
    cb, sb = _cs(num % n, n)
    m_f = jnp.concatenate([jnp.concatenate([cb, sb], axis=2),
                           jnp.concatenate([-sb, cb], axis=2)], axis=1)
    cbt, sbt = jnp.swapaxes(cb, 1, 2) / n, jnp.swapaxes(sb, 1, 2) / n
    m_i = jnp.concatenate([jnp.concatenate([cbt, -sbt], axis=2),
                           jnp.concatenate([sbt, cbt], axis=2)], axis=1)
    return dict(g_u=g_u.astype(BF16), g_t=g_t.astype(BF16), g_i=g_i.astype(BF16),
                m_f=m_f.astype(BF16), m_i=m_i.astype(BF16))


def _fft_a_kernel(g_ref, xa_ref, xb_ref, o_ref):
    h = xa_ref.shape[0]
    n1 = o_ref.shape[0]
    ga = g_ref[:, :h]
    gb = g_ref[:, h:]
    for s in range(xa_ref.shape[1]):
        r = _dot(ga, xa_ref[:, s, :].astype(BF16)) + _dot(gb, xb_ref[:, s, :].astype(BF16))
        o_ref[:, 0, s, :] = r[:n1]
        o_ref[:, 1, s, :] = r[n1:]


def fft_stage_a(g, x4, sel_a, sel_b, col_off, c, tc=256):
    n1 = g.shape[0] // 2
    h = n1 // 2
    n2 = x4.shape[2]
    ts = SUBLANES
    coff = col_off // tc
    return pl.pallas_call(
        _fft_a_kernel,
        grid=(c // tc, n2 // ts),
        in_specs=[pl.BlockSpec((2 * n1, n1), lambda j, s: (0, 0)),
                  pl.BlockSpec((None, h, ts, tc), lambda j, s: (sel_a[0], sel_a[1], s, j + coff)),
                  pl.BlockSpec((None, h, ts, tc), lambda j, s: (sel_b[0], sel_b[1], s, j + coff))],
        out_specs=pl.BlockSpec((n1, 2, ts, tc), lambda j, s: (0, 0, s, j)),
        out_shape=jax.ShapeDtypeStruct((n1, 2, n2, c), F32),
        compiler_params=_cparams("parallel", "parallel"),
        name="fft_stage_a",
    )(g, x4, x4)


def _fft_bh_kernel(m_ref, a_ref, o_ref):
    n2 = a_ref.shape[1]
    a = a_ref[...].reshape(2 * n2, a_ref.shape[2]).astype(BF16)
    o_ref[...] = _dot(m_ref[...], a).reshape(o_ref.shape)


def fft_stage_b_spectrum(m_f, a):
    n1, _, n2, c = a.shape
    return pl.pallas_call(
        _fft_bh_kernel,
        grid=(n1,),
        in_specs=[pl.BlockSpec((None, 2 * n2, 2 * n2), lambda k: (k, 0, 0)),
                  pl.BlockSpec((None, 2, n2, c), lambda k: (k, 0, 0, 0))],
        out_specs=pl.BlockSpec((None, 2, n2, c), lambda k: (k, 0, 0, 0)),
        out_shape=jax.ShapeDtypeStruct(a.shape, F32),
        compiler_params=_cparams("parallel"),
        name="fft_stage_b_spectrum",
    )(m_f, a)


def _fft_b_kernel(mf_ref, mi_ref, a_ref, h_ref, o_ref):
    n2 = a_ref.shape[1]
    a = a_ref[...].reshape(2 * n2, a_ref.shape[2]).astype(BF16)
    x = _dot(mf_ref[...], a)
    xr, xi = x[:n2], x[n2:]
    hr, hi = h_ref[0], h_ref[1]
    y = jnp.concatenate([xr * hr - xi * hi, xr * hi + xi * hr], axis=0).astype(BF16)
    o_ref[...] = _dot(mi_ref[...], y).reshape(o_ref.shape)


def fft_stage_b(m_f, m_i, a, hspec):
    n1, _, n2, c = a.shape
    blk = pl.BlockSpec((None, 2, n2, c), lambda k: (k, 0, 0, 0))
    mat = pl.BlockSpec((None, 2 * n2, 2 * n2), lambda k: (k, 0, 0))
    return pl.pallas_call(
        _fft_b_kernel,
        grid=(n1,),
        in_specs=[mat, mat, blk, blk],
        out_specs=blk,
        out_shape=jax.ShapeDtypeStruct(a.shape, F32),
        compiler_params=_cparams("parallel"),
        name="fft_stage_b",
    )(m_f, m_i, a, hspec)


def _fft_ai_kernel(g_ref, b_ref, u_ref, x_ref, skip_ref, o_ref):
    n1 = b_ref.shape[0]
    h = n1 // 2
    ga = g_ref[:, :n1]
    gb = g_ref[:, n1:]
    skip = skip_ref[...]
    for s in range(b_ref.shape[2]):
        y = _dot(ga, b_ref[:, 0, s, :].astype(BF16)) + _dot(gb, b_ref[:, 1, s, :].astype(BF16))
        for bi in range(2):
            yb = y[bi * h:(bi + 1) * h]
            o_ref[bi, :, s, :] = (x_ref[bi, :, s, :] * (yb + u_ref[bi, :, s, :] * skip)).astype(o_ref.dtype)


def fft_stage_a_inv(g_i, bm, u4, u_off, x4, x_off, skip_row, out_dtype, tc=256):
    n1, _, n2, c = bm.shape
    h = n1 // 2
    ts = SUBLANES
    uo, xo = u_off // tc, x_off // tc
    return pl.pallas_call(
        _fft_ai_kernel,
        grid=(c // tc, n2 // ts),
        in_specs=[pl.BlockSpec((n1, 2 * n1), lambda j, s: (0, 0)),
                  pl.BlockSpec((n1, 2, ts, tc), lambda j, s: (0, 0, s, j)),
                  pl.BlockSpec((2, h, ts, tc), lambda j, s: (0, 0, s, j + uo)),
                  pl.BlockSpec((2, h, ts, tc), lambda j, s: (0, 0, s, j + xo)),
                  pl.BlockSpec((1, tc), lambda j, s: (0, j))],
        out_specs=pl.BlockSpec((2, h, ts, tc), lambda j, s: (0, 0, s, j)),
        out_shape=jax.ShapeDtypeStruct((2, h, n2, c), out_dtype),
        compiler_params=_cparams("parallel", "parallel"),
        name="fft_stage_a_inv",
    )(g_i, bm, u4, x4, skip_row)


def hyena_long(u, taps, skip, tabs):
    b, length, c3 = u.shape
    assert b == 2, "the two batch rows are packed as one complex sequence"
    c = c3 // 3
    n2 = FFT_N2
    n1 = 2 * length // n2
    h = n1 // 2
    u4 = u.reshape(b, h, n2, c3)
    taps4 = taps.reshape(HY_ORDER * 2, h, n2, c)
    zin, zoff = u4, 0
    out = None
    for conv in range(HY_ORDER):
        hspec = fft_stage_b_spectrum(
            tabs["m_f"], fft_stage_a(tabs["g_t"], taps4, (2 * conv, 0), (2 * conv + 1, 0), 0, c))
        a = fft_stage_a(tabs["g_u"], zin, (0, 0), (1, 0), zoff, c)
        bm = fft_stage_b(tabs["m_f"], tabs["m_i"], a, hspec)
        last = conv == HY_ORDER - 1
        out = fft_stage_a_inv(tabs["g_i"], bm, zin, zoff, u4, (conv + 1) * c,
                              skip[conv].reshape(1, c), BF16 if last else F32)
        zin, zoff = out, 0
    return out.reshape(b, length, c)


def _hyena_ctx_kernel(u_ref, x1_ref, x2_ref, taps_ref, fc_ref, fs_ref, skip_ref, o_ref):
    length = u_ref.shape[1]
    n = 2 * length
    fc = fc_ref[...]
    fs = fs_ref[...]
    fcl, fsl = fc[:, :length], fs[:, :length]
    fct, fst = fc[:length, :], fs[:length, :]
    zr = u_ref[0]
    zi = u_ref[1]
    gates = (x1_ref, x2_ref)
    for conv in range(HY_ORDER):
        tp = taps_ref[conv].astype(BF16)
        hr, hi = _dot(fc, tp), -_dot(fs, tp)
        zrb, zib = zr.astype(BF16), zi.astype(BF16)
        xr = _dot(fcl, zrb) + _dot(fsl, zib)
        xi = _dot(fcl, zib) - _dot(fsl, zrb)
        yr = (xr * hr - xi * hi).astype(BF16)
        yi = (xr * hi + xi * hr).astype(BF16)
        cr = (_dot(fct, yr) - _dot(fst, yi)) * (1.0 / n)
        ci = (_dot(fct, yi) + _dot(fst, yr)) * (1.0 / n)
        sk = skip_ref[conv:conv + 1, :]
        zr = gates[conv][0] * (cr + zr * sk)
        zi = gates[conv][1] * (ci + zi * sk)
    o_ref[0] = zr.astype(o_ref.dtype)
    o_ref[1] = zi.astype(o_ref.dtype)


def hyena_ctx(u, taps, skip, tc=256):
    b, length, c3 = u.shape
    assert b == 2
    c = c3 // 3
    n = 2 * length
    k = jnp.arange(n, dtype=jnp.int32)
    fc, fs = _cs((k[:, None] * k[None, :]) % n, n)
    nb = c // tc
    ub = lambda off: pl.BlockSpec((2, length, tc), lambda j: (0, 0, j + off * nb))
    mat = pl.BlockSpec((n, n), lambda j: (0, 0))
    return pl.pallas_call(
        _hyena_ctx_kernel,
        grid=(nb,),
        in_specs=[ub(0), ub(1), ub(2),
                  pl.BlockSpec((HY_ORDER, n, tc), lambda j: (0, 0, j)),
                  mat, mat,
                  pl.BlockSpec((HY_ORDER, tc), lambda j: (0, j))],
        out_specs=pl.BlockSpec((2, length, tc), lambda j: (0, 0, j)),
        out_shape=jax.ShapeDtypeStruct((b, length, c), BF16),
        compiler_params=_cparams("parallel"),
        name="hyena_ctx",
    )(u, u, u, taps, fc.astype(BF16), fs.astype(BF16), skip)


def rope_tables(length):
    rows = length // GRID_W
    row = jnp.repeat(jnp.arange(rows, dtype=F32), GRID_W)
    col = (jnp.arange(length) % GRID_W).astype(F32)
    inv = ROPE_THETA ** (-jnp.arange(0, ROPE_AXIS_DIM, 2, dtype=F32) / ROPE_AXIS_DIM)
    ang = jnp.stack([row[:, None] * inv, col[:, None] * inv], axis=1)
    cos, sin = jnp.cos(ang), jnp.sin(ang)
    zero = jnp.zeros_like(sin)
    per_map = lambda lo, hi: jnp.concatenate([lo, hi], axis=-1).reshape(length, 2 * ROPE_AXIS_DIM)
    reps = LANES // (2 * ROPE_AXIS_DIM)
    c_t = jnp.tile(per_map(cos, cos), (1, reps))
    sm_t = jnp.tile(per_map(-sin, zero), (1, reps))
    sp_t = jnp.tile(per_map(zero, sin), (1, reps))
    return c_t, sm_t, sp_t


def _rope_kernel(x_ref, c_ref, sm_ref, sp_ref, o_ref, *n_refs, scale):
    half = ROPE_AXIS_DIM // 2
    c, sm, sp = c_ref[...], sm_ref[...], sp_ref[...]
    for g in range(x_ref.shape[1] // LANES):
        ls = slice(g * LANES, (g + 1) * LANES)
        x = x_ref[:, ls]
        y = (x * c + pltpu.roll(x, LANES - half, 1) * sm + pltpu.roll(x, half, 1) * sp) * scale
        o_ref[:, ls] = y.astype(o_ref.dtype)
        if n_refs:
            n2 = jnp.max(jnp.sum(y * y, axis=1, keepdims=True), axis=0, keepdims=True)
            n_refs[0][g:g + 1, :] = jnp.broadcast_to(n2, (1, LANES))


def rope(x, tables, scale, with_norms=False, tr=512):
    b, t, d = x.shape
    tr = _pick_tile(t, tr)
    nh = d // LANES
    tab = pl.BlockSpec((tr, LANES), lambda bi, i: (i, 0))
    out_specs = [pl.BlockSpec((None, tr, d), lambda bi, i: (bi, i, 0))]
    out_shape = [jax.ShapeDtypeStruct((b, t, d), BF16)]
    if with_norms:
        out_specs.append(pl.BlockSpec((None, None, nh, LANES), lambda bi, i: (bi, i, 0, 0)))
        out_shape.append(jax.ShapeDtypeStruct((b, t // tr, nh, LANES), F32))
    outs = pl.pallas_call(
        functools.partial(_rope_kernel, scale=scale),
        grid=(b, t // tr),
        in_specs=[pl.BlockSpec((None, tr, d), lambda bi, i: (bi, i, 0)), tab, tab, tab],
        out_specs=out_specs,
        out_shape=out_shape,
        compiler_params=_cparams("parallel", "parallel"),
        name="rope",
    )(x, *tables)
    if with_norms:
        return outs[0], jnp.swapaxes(outs[1], 1, 2)
    return outs[0]


def _scale_cast_kernel(x_ref, o_ref, *, scale):
    o_ref[...] = (x_ref[...] * scale).astype(o_ref.dtype)


def scale_cast(x, scale, tr=256):
    b, t, d = x.shape
    tr = _pick_tile(t, tr)
    blk = pl.BlockSpec((None, tr, d), lambda bi, i: (bi, i, 0))
    return pl.pallas_call(
        functools.partial(_scale_cast_kernel, scale=scale),
        grid=(b, t // tr), in_specs=[blk], out_specs=blk,
        out_shape=jax.ShapeDtypeStruct((b, t, d), BF16),
        compiler_params=_cparams("parallel", "parallel"),
        name="scale_cast",
    )(x)


ATTN_GUARD_LOG2 = 80.0
ATTN_NORM_SLACK = 1.02


def _attn_kernel(*refs, lam_init, nk, with_ctx, nsub):
    if with_ctx:
        (q_ref, k_ref, v_ref, kc_ref, vc_ref, kn_ref, lam_ref, w_ref, o_ref,
         q2_ref, m_ref, acc_ref, p_scr, flag_ref) = refs
    else:
        q_ref, k_ref, v_ref, lam_ref, w_ref, o_ref, q2_ref, m_ref, acc_ref, p_scr, flag_ref = refs
    ki = pl.program_id(3)
    tq = q_ref.shape[0]

    def with_ones(v):
        lane = lax.broadcasted_iota(jnp.int32, v.shape, 1)
        return jnp.concatenate([v, (lane == 0).astype(BF16)], axis=1)

    def exact_update(k, v, p_buf):
        s = _dot_nt(q2_ref[...], k)
        m_prev = m_ref[...]
        m_new = jnp.maximum(m_prev, jnp.max(s, axis=1, keepdims=True))
        alpha = jnp.exp2(m_prev - m_new)
        p_buf[...] = jnp.exp2(s - m_new).astype(BF16)
        acc_ref[...] = alpha * acc_ref[...] + _dot(p_buf[...], with_ones(v))
        m_ref[...] = m_new

    def fast_update(k, v, p_buf):
        s = _dot_nt(q2_ref[...], k)
        p_buf[...] = jnp.exp2(s - m_ref[...]).astype(BF16)
        acc_ref[...] += _dot(p_buf[...], with_ones(v))

    @pl.when(ki == 0)
    def _():
        q = q_ref[...]
        lane = lax.broadcasted_iota(jnp.int32, q.shape, 1)
        zero = jnp.zeros_like(q)
        q2_ref[:tq, :] = jnp.where(lane < DA_HEAD_DIM, q, zero)
        q2_ref[tq:, :] = jnp.where(lane >= DA_HEAD_DIM, q, zero)
        m_ref[...] = jnp.full(m_ref.shape, -jnp.inf, F32)
        acc_ref[...] = jnp.zeros(acc_ref.shape, F32)
        flag_ref[0] = 0
        if with_ctx:
            tc = kc_ref.shape[0]
            exact_update(kc_ref[...], vc_ref[...], p_scr.at[0, :, pl.ds(0, tc)])
            q2 = q2_ref[...].astype(F32)
            qn2 = jnp.sum(q2 * q2, axis=1, keepdims=True)
            kn2 = jnp.max(kn_ref[...])
            gap = jnp.max(jnp.sqrt(qn2 * kn2) * ATTN_NORM_SLACK - m_ref[...])
            flag_ref[0] = (gap <= ATTN_GUARD_LOG2).astype(jnp.int32)

    tks = k_ref.shape[0] // nsub
    fast = flag_ref[0] == 1

    @pl.when(fast)
    def _():
        for sub in range(nsub):
            ks = slice(sub * tks, (sub + 1) * tks)
            fast_update(k_ref[ks, :], v_ref[ks, :], p_scr.at[sub])

    @pl.when(jnp.logical_not(fast))
    def _():
        for sub in range(nsub):
            ks = slice(sub * tks, (sub + 1) * tks)
            exact_update(k_ref[ks, :], v_ref[ks, :], p_scr.at[sub])

    @pl.when(ki == nk - 1)
    def _():
        lv = lam_ref[...]
        e1 = jnp.exp(jnp.sum(lv[0:1, :] * lv[1:2, :], axis=1, keepdims=True))
        e2 = jnp.exp(jnp.sum(lv[2:3, :] * lv[3:4, :], axis=1, keepdims=True))
        lam = e1 - e2 + lam_init
        acc = acc_ref[...]
        o = acc[:, :DA_V_DIM] / acc[:, DA_V_DIM:DA_V_DIM + 1]
        o = o[:tq] - lam * o[tq:]
        o_ref[...] = (_rms(o, w_ref[...]) * (1.0 - lam_init)).astype(o_ref.dtype)


def diff_attention(q, k, v, k_ctx, v_ctx, knorm, lam_vec, subln_w, lam_init, tq=512, tk=1024, tks=1024):
    b, t, d = q.shape
    nh = d // DA_V_DIM
    tq = _pick_tile(t, tq)
    tk = _pick_tile(k.shape[1], tk)
    tks = _pick_tile(tk, tks)
    nsub = tk // tks
    nk = k.shape[1] // tk
    with_ctx = k_ctx is not None
    assert not with_ctx or k_ctx.shape[1] <= tks
    qspec = pl.BlockSpec((None, tq, DA_V_DIM), lambda bi, h, i, j: (bi, i, h))
    kspec = pl.BlockSpec((None, tk, DA_V_DIM), lambda bi, h, i, j: (bi, j, h))
    in_specs = [qspec, kspec, kspec]
    args = [q, k, v]
    if with_ctx:
        cspec = pl.BlockSpec((None, k_ctx.shape[1], DA_V_DIM), lambda bi, h, i, j: (bi, 0, h))
        nspec = pl.BlockSpec((None, None) + knorm.shape[2:], lambda bi, h, i, j: (bi, h, 0, 0))
        in_specs += [cspec, cspec, nspec]
        args += [k_ctx, v_ctx, knorm]
    in_specs += [pl.BlockSpec((4, DA_HEAD_DIM), lambda bi, h, i, j: (0, 0)),
                 pl.BlockSpec((1, DA_V_DIM), lambda bi, h, i, j: (0, 0))]
    args += [lam_vec, subln_w.reshape(1, DA_V_DIM)]
    return pl.pallas_call(
        functools.partial(_attn_kernel, lam_init=lam_init, nk=nk, with_ctx=with_ctx, nsub=nsub),
        grid=(b, nh, t // tq, nk),
        in_specs=in_specs,
        out_specs=qspec,
        out_shape=jax.ShapeDtypeStruct((b, t, d), BF16),
        scratch_shapes=[pltpu.VMEM((2 * tq, DA_V_DIM), BF16),
                        pltpu.VMEM((2 * tq, 1), F32),
                        pltpu.VMEM((2 * tq, 2 * DA_V_DIM), F32),
                        pltpu.VMEM((nsub, 2 * tq, tks), BF16),
                        pltpu.SMEM((1,), jnp.int32)],
        compiler_params=_cparams("parallel", "parallel", "parallel", "arbitrary"),
        name="diff_attention",
    )(*args)


def _merge_kernel(ya_ref, yb_ref, yc_ref, g_ref, wb_ref, wo_ref, x_ref, mod_ref, o_ref):
    d = x_ref.shape[1]
    ys = (ya_ref, yb_ref, yc_ref)
    s = None
    for k in range(N_BRANCH):
        t = jax.nn.sigmoid(g_ref[:, k * d:(k + 1) * d]) * _dot(ys[k][...], wb_ref[k])
        s = t if s is None else s + t
    r = _dot(s.astype(BF16), wo_ref[...])
    o_ref[...] = x_ref[...] + mod_ref[2:3, :] * r


def merge(ya, yb, yc, gates, wb, wo, x, mods, tr=256):
    b, t, d = x.shape
    tr = _pick_tile(t, tr)
    per_batch = mods.shape[0] > 1
    blk = pl.BlockSpec((None, tr, d), lambda bi, i: (bi, i, 0))
    return pl.pallas_call(
        _merge_kernel,
        grid=(b, t // tr),
        in_specs=[blk, blk, blk,
                  pl.BlockSpec((None, tr, N_BRANCH * d), lambda bi, i: (bi, i, 0)),
                  pl.BlockSpec((N_BRANCH, d, d), lambda bi, i: (0, 0, 0)),
                  pl.BlockSpec((d, d), lambda bi, i: (0, 0)),
                  blk,
                  pl.BlockSpec((None, 8, d), (lambda bi, i: (bi, 0, 0)) if per_batch else (lambda bi, i: (0, 0, 0)))],
        out_specs=blk,
        out_shape=jax.ShapeDtypeStruct((b, t, d), F32),
        compiler_params=_cparams("parallel", "parallel"),
        name="merge",
    )(ya, yb, yc, gates, wb, wo, x, mods)


def _ffn_kernel(x_ref, g_ref, mod_ref, w13_ref, w2_ref, o_ref, *, nchunk):
    x = x_ref[...]
    h = (_rms(x, g_ref[...]) * (1.0 + mod_ref[4:5, :]) + mod_ref[3:4, :]).astype(BF16)
    f = w2_ref.shape[0]
    fc = f // nchunk
    acc = None
    for c in range(nchunk):
        a1 = _dot(h, w13_ref[:, c * fc:(c + 1) * fc])
        a3 = _dot(h, w13_ref[:, f + c * fc:f + (c + 1) * fc])
        t = _dot((_silu(a1) * a3).astype(BF16), w2_ref[c * fc:(c + 1) * fc, :])
        acc = t if acc is None else acc + t
    o_ref[...] = x + mod_ref[5:6, :] * acc


def ffn(x, g, mods, w13, w2, tr=256):
    b, t, d = x.shape
    f = w2.shape[0]
    tr = _pick_tile(t, tr)
    nchunk = 2 if (f // 2) % LANES == 0 else 1
    per_batch = mods.shape[0] > 1
    blk = pl.BlockSpec((None, tr, d), lambda bi, i: (bi, i, 0))
    return pl.pallas_call(
        functools.partial(_ffn_kernel, nchunk=nchunk),
        grid=(b, t // tr),
        in_specs=[blk,
                  pl.BlockSpec((1, d), lambda bi, i: (0, 0)),
                  pl.BlockSpec((None, 8, d), (lambda bi, i: (bi, 0, 0)) if per_batch else (lambda bi, i: (0, 0, 0))),
                  pl.BlockSpec((d, 2 * f), lambda bi, i: (0, 0)),
                  pl.BlockSpec((f, d), lambda bi, i: (0, 0))],
        out_specs=blk,
        out_shape=jax.ShapeDtypeStruct((b, t, d), F32),
        compiler_params=_cparams("parallel", "parallel"),
        name="ffn",
    )(x, g.reshape(1, d), mods, w13, w2)


def kernel(x, c, ctx, c_ctx, ada_w, ada_b, norm1_g, norm2_g, w_in, ssd_conv_w, ssd_conv_b,
           ssd_a_log, ssd_dt_bias, ssd_d, ssd_norm_w, hy_conv_w, hy_conv_b, hy_w1, hy_b1, hy_w2,
           hy_b2, hy_w3, hy_freq, hy_bias, da_lambda, da_subln_w, w_branch, w_out, ffn_w13,
           ffn_w2, final_g):
    bsz, length, d = x.shape
    ctx_len = ctx.shape[1]
    depth = w_in.shape[0]
    inner = SSD_HEADS * SSD_HEAD_DIM
    xbc_w = inner + 2 * SSD_GROUPS * SSD_STATE
    dt_w = 2 * SSD_HEADS
    hy_w = (HY_ORDER + 1) * d
    widths = (inner, xbc_w, dt_w, hy_w, d, d, d, N_BRANCH * d)
    offs = [0]
    for wd in widths:
        offs.append(offs[-1] + wd)

    cond = jnp.concatenate([c, c_ctx[None, :], jnp.zeros((8 - bsz - 1, d), F32)], axis=0)

    rope_tabs = rope_tables(length)
    tabs = fft_tables(2 * length // FFT_N2, FFT_N2)
    expand = (jnp.arange(LANES)[:, None] % SSD_HEADS
              == jnp.arange(inner)[None, :] // SSD_HEAD_DIM)
    expand_dir = [(expand & ((jnp.arange(LANES)[:, None] // SSD_HEADS) == dr)).astype(BF16) for dr in range(2)]
    qscale = DA_HEAD_DIM ** -0.5 * math.log2(math.e)

    x_l, x_c = x, ctx
    for layer in range(depth):
        last = layer == depth - 1
        lam_init = 0.8 - 0.6 * math.exp(-0.3 * layer)
        mod = matmul(cond, ada_w[layer].astype(BF16), ada_b[layer].reshape(1, -1), silu_in=True)
        mod = jnp.pad(mod.reshape(8, 6, d), ((0, 0), (0, 2), (0, 0)))
        mods_l, mods_c = mod[:bsz], mod[bsz:bsz + 1]

        w_l = w_in[layer].astype(BF16)
        w_parts = [w_l[:, offs[i]:offs[i + 1]] for i in range(len(widths))]
        w_parts[2] = jnp.pad(w_parts[2], ((0, 0), (0, LANES - dt_w)))
        a_neg = -jnp.exp(ssd_a_log[layer].astype(F32)).reshape(1, dt_w)
        aneg_row = jnp.pad(a_neg, ((0, 0), (0, LANES - dt_w)))
        bias_row = jnp.pad(ssd_dt_bias[layer].astype(F32).reshape(1, dt_w), ((0, 0), (0, LANES - dt_w)))
        d_row = jnp.repeat(ssd_d[layer].astype(F32), SSD_HEAD_DIM).reshape(1, inner)
        nw_row = ssd_norm_w[layer].astype(F32).reshape(1, inner)
        wb = w_branch[layer].astype(BF16)
        wo = w_out[layer].astype(BF16)
        w13 = ffn_w13[layer].astype(BF16)
        w2 = ffn_w2[layer].astype(BF16)

        def project(stream, mods):
            bb, tt, _ = stream.shape
            h = norm_mod(stream, norm1_g[layer], mods, 0).reshape(bb * tt, d)
            outs = [matmul(h, wp, out_dtype=(BF16 if i == 6 else F32)).reshape(bb, tt, -1)
                    for i, wp in enumerate(w_parts)]
            return outs

        z_c, xbc_c, dt_c, hy_c, q_c, k_c, v_c, g_c = project(x_c, mods_c)
        z_l, xbc_l, dt_l, hy_l, q_l, k_l, v_l, g_l = project(x_l, mods_l)

        xa_c = dwconv3(xbc_c, ssd_conv_w[layer], ssd_conv_b[layer], True)
        xa_l = dwconv3(xbc_l, ssd_conv_w[layer], ssd_conv_b[layer], True)
        h_zero = jnp.zeros((bsz, SSD_STATE, inner), F32)
        ys_c, ys_l = [], []
        for dr in range(2):
            y_c, st = ssd_scan(xa_c, dt_c, bias_row, aneg_row, expand_dir[dr], h_zero, dr)
            y_l, _ = ssd_scan(xa_l, dt_l, bias_row, aneg_row, expand_dir[dr], st, dr)
            ys_c.append(y_c)
            ys_l.append(y_l)
        ya_l = ssd_out(ys_l[0], ys_l[1], xa_l, z_l, d_row, nw_row)

        filt = (hy_w1[layer], hy_b1[layer], hy_w2[layer], hy_b2[layer], hy_w3[layer], hy_freq[layer])
        u_l = dwconv3(hy_l, hy_conv_w[layer], hy_conv_b[layer], False)
        yb_l = hyena_long(u_l, hyena_taps(length, *filt), hy_bias[layer], tabs)

        qr_l = rope(q_l, rope_tabs, qscale)
        kr_l, kn_l = rope(k_l, rope_tabs, 1.0, with_norms=True)
        kb_c = scale_cast(k_c, 1.0)
        yc_l = diff_attention(qr_l, kr_l, v_l, kb_c, v_c, kn_l, da_lambda[layer], da_subln_w[layer], lam_init)

        x_l_new = merge(ya_l, yb_l, yc_l, g_l, wb, wo, x_l, mods_l)
        x_l = ffn(x_l_new, norm2_g[layer], mods_l, w13, w2)

        if not last:
            ya_c = ssd_out(ys_c[0], ys_c[1], xa_c, z_c, d_row, nw_row)
            u_c = dwconv3(hy_c, hy_conv_w[layer], hy_conv_b[layer], False)
            yb_c = hyena_ctx(u_c, hyena_taps(ctx_len, *filt), hy_bias[layer])
            qb_c = scale_cast(q_c, qscale)
            yc_c = diff_attention(qb_c, kb_c, v_c, None, None, None, da_lambda[layer], da_subln_w[layer], lam_init)
            x_c_new = merge(ya_c, yb_c, yc_c, g_c, wb, wo, x_c, mods_c)
            x_c = ffn(x_c_new, norm2_g[layer], mods_c, w13, w2)
    return final_norm(x_l, final_g)
```

```python
import functools
import math

import jax
import jax.numpy as jnp
from jax import lax
from jax.experimental import pallas as pl
from jax.experimental.pallas import tpu as pltpu

F32 = jnp.float32
BF16 = jnp.bfloat16
EPS = 1e-6

SSD_HEADS = 16
SSD_HEAD_DIM = 64
SSD_GROUPS = 2
SSD_STATE = 128
SSD_CHUNK = 128
HY_ORDER = 2
HY_BANDS = 16
HY_SHORT_DECAY_PCT = 0.3
HY_LONG_DECAY_PCT = 1.5
HY_DECAY_TARGET = 1e-2
DA_HEADS = 8
DA_V_DIM = 128
DA_HEAD_DIM = 64
ROPE_AXIS_DIM = 32
ROPE_THETA = 10000.0
GRID_W = 64
N_BRANCH = 3

LANES = 128
SUBLANES = 8
FFT_N2 = 128
VMEM_LIMIT = 56 * 1024 * 1024


def _cparams(*sem):
    return pltpu.CompilerParams(dimension_semantics=sem, vmem_limit_bytes=VMEM_LIMIT)


def _dot(a, b):
    return jnp.dot(a, b, preferred_element_type=F32)


def _dot_nt(a, b):
    return lax.dot_general(a, b, (((1,), (1,)), ((), ())), preferred_element_type=F32)


def _split3(x):
    hi = x.astype(BF16)
    r1 = x - hi.astype(F32)
    mid = r1.astype(BF16)
    lo = (r1 - mid.astype(F32)).astype(BF16)
    return hi, mid, lo


def _silu(x):
    return x * jax.nn.sigmoid(x)


def _softplus(x):
    return jnp.maximum(x, 0.0) + jnp.log1p(jnp.exp(-jnp.abs(x)))


def _pick_tile(n, pref):
    t = min(pref, n)
    while n % t:
        t //= 2
    return t


def _matmul_kernel(a_ref, w_ref, b_ref, o_ref, *, silu_in):
    a = a_ref[...]
    if silu_in:
        a = _silu(a.astype(F32))
    r = _dot(a.astype(BF16), w_ref[...]) + b_ref[...]
    o_ref[...] = r.astype(o_ref.dtype)


def matmul(a, w, bias=None, out_dtype=F32, silu_in=False, tm=1024, tn=512):
    m, k = a.shape
    n = w.shape[1]
    tm = _pick_tile(m, tm)
    tn = _pick_tile(n, tn)
    if bias is None:
        bias = jnp.zeros((1, n), F32)
    return pl.pallas_call(
        functools.partial(_matmul_kernel, silu_in=silu_in),
        grid=(m // tm, n // tn),
        in_specs=[pl.BlockSpec((tm, k), lambda i, j: (i, 0)),
                  pl.BlockSpec((k, tn), lambda i, j: (0, j)),
                  pl.BlockSpec((1, tn), lambda i, j: (0, j))],
        out_specs=pl.BlockSpec((tm, tn), lambda i, j: (i, j)),
        out_shape=jax.ShapeDtypeStruct((m, n), out_dtype),
        compiler_params=_cparams("parallel", "parallel"),
        name="matmul",
    )(a, w, bias)


def _rms(x, g):
    return x * lax.rsqrt(jnp.mean(x * x, axis=-1, keepdims=True) + EPS) * g


def _norm_mod_kernel(x_ref, g_ref, mod_ref, o_ref, *, row0):
    x = x_ref[...]
    y = _rms(x, g_ref[...])
    shift = mod_ref[row0:row0 + 1, :]
    scale = mod_ref[row0 + 1:row0 + 2, :]
    o_ref[...] = (y * (1.0 + scale) + shift).astype(o_ref.dtype)


def norm_mod(x, g, mods, row0, tr=512):
    b, t, d = x.shape
    tr = _pick_tile(t, tr)
    per_batch = mods.shape[0] > 1
    return pl.pallas_call(
        functools.partial(_norm_mod_kernel, row0=row0),
        grid=(b, t // tr),
        in_specs=[pl.BlockSpec((None, tr, d), lambda bi, i: (bi, i, 0)),
                  pl.BlockSpec((1, d), lambda bi, i: (0, 0)),
                  pl.BlockSpec((None, 8, d), (lambda bi, i: (bi, 0, 0)) if per_batch else (lambda bi, i: (0, 0, 0)))],
        out_specs=pl.BlockSpec((None, tr, d), lambda bi, i: (bi, i, 0)),
        out_shape=jax.ShapeDtypeStruct((b, t, d), BF16),
        compiler_params=_cparams("parallel", "parallel"),
        name="norm_mod",
    )(x, g.reshape(1, d), mods)


def _final_norm_kernel(x_ref, g_ref, o_ref):
    o_ref[...] = _rms(x_ref[...], g_ref[...])


def final_norm(x, g, tr=512):
    b, t, d = x.shape
    tr = _pick_tile(t, tr)
    return pl.pallas_call(
        _final_norm_kernel,
        grid=(b, t // tr),
        in_specs=[pl.BlockSpec((None, tr, d), lambda bi, i: (bi, i, 0)),
                  pl.BlockSpec((1, d), lambda bi, i: (0, 0))],
        out_specs=pl.BlockSpec((None, tr, d), lambda bi, i: (bi, i, 0)),
        out_shape=jax.ShapeDtypeStruct((b, t, d), F32),
        compiler_params=_cparams("parallel", "parallel"),
        name="final_norm",
    )(x, g.reshape(1, d))


def _dwconv_kernel(x_ref, xp_ref, xn_ref, w_ref, b_ref, o_ref, *, act, nblk):
    i = pl.program_id(1)
    x = x_ref[...].astype(F32)
    tr = x.shape[0]
    halo = xp_ref.shape[0]
    prev_row = jnp.where(i == 0, 0.0, xp_ref[halo - 1:halo, :].astype(F32))
    next_row = jnp.where(i == nblk - 1, 0.0, xn_ref[0:1, :].astype(F32))
    rows = lax.broadcasted_iota(jnp.int32, x.shape, 0)
    x_m1 = jnp.where(rows == 0, prev_row, pltpu.roll(x, 1, 0))
    x_p1 = jnp.where(rows == tr - 1, next_row, pltpu.roll(x, tr - 1, 0))
    y = b_ref[...] + x_m1 * w_ref[0:1, :] + x * w_ref[1:2, :] + x_p1 * w_ref[2:3, :]
    if act:
        y = _silu(y)
    o_ref[...] = y


def dwconv3(x, w, bias, act, tr=512, tc=1536):
    b, t, c = x.shape
    tr = _pick_tile(t, tr)
    tc = min(tc, c)
    assert c % tc == 0
    nblk = t // tr
    halo = SUBLANES * (4 // x.dtype.itemsize)
    rh = tr // halo
    lasth = t // halo - 1
    return pl.pallas_call(
        functools.partial(_dwconv_kernel, act=act, nblk=nblk),
        grid=(b, nblk, c // tc),
        in_specs=[pl.BlockSpec((None, tr, tc), lambda bi, i, j: (bi, i, j)),
                  pl.BlockSpec((None, halo, tc), lambda bi, i, j: (bi, jnp.maximum(i * rh - 1, 0), j)),
                  pl.BlockSpec((None, halo, tc), lambda bi, i, j: (bi, jnp.minimum((i + 1) * rh, lasth), j)),
                  pl.BlockSpec((3, tc), lambda bi, i, j: (0, j)),
                  pl.BlockSpec((1, tc), lambda bi, i, j: (0, j))],
        out_specs=pl.BlockSpec((None, tr, tc), lambda bi, i, j: (bi, i, j)),
        out_shape=jax.ShapeDtypeStruct((b, t, c), F32),
        compiler_params=_cparams("parallel", "parallel", "parallel"),
        name="dwconv3",
    )(x, x, x, w, bias.reshape(1, c))


def _ssd_kernel(x_ref, b_ref, c_ref, dt_ref, bias_ref, aneg_ref, e_ref, h0_ref,
                y_ref, hout_ref, st_ref, *, direction, nchunks):
    ci = pl.program_id(1)

    @pl.when(ci == 0)
    def _():
        st_ref[...] = h0_ref[...]

    q = x_ref.shape[0]
    hd = SSD_HEAD_DIM
    gw = (SSD_HEADS // SSD_GROUPS) * hd
    x = x_ref[...]
    dtv = _softplus(dt_ref[...] + bias_ref[...])
    adt = dtv * aneg_ref[...]
    row = lax.broadcasted_iota(jnp.int32, (q, q), 0)
    col = lax.broadcasted_iota(jnp.int32, (q, q), 1)
    mask = (col >= row) if direction else (col <= row)
    tri = mask.astype(BF16)
    acum = sum(_dot(tri, p) for p in _split3(adt))
    acum_t = sum(_dot_nt(p, tri) for p in _split3(adt.T))
    total = jnp.sum(adt, axis=0, keepdims=True)
    eac = jnp.exp(acum)
    dte = jnp.exp(total - acum)
    cd = jnp.broadcast_to(jnp.exp(total), (SUBLANES, LANES))
    stack = jnp.concatenate([dtv, eac, dte, cd], axis=0)
    e01 = e_ref[...]
    hi = stack.astype(BF16)
    lo = (stack - hi.astype(F32)).astype(BF16)
    ex = _dot(hi, e01) + _dot(lo, e01)
    dt_e, eac_e, dte_e, cd_e = ex[:q], ex[q:2 * q], ex[2 * q:3 * q], ex[3 * q:3 * q + 1]
    xdt = x * dt_e
    lane = lax.broadcasted_iota(jnp.int32, (q, LANES), 1)
    first = lane < hd
    for g in range(SSD_GROUPS):
        gs = slice(g * gw, (g + 1) * gw)
        c_g = c_ref[:, g * SSD_STATE:(g + 1) * SSD_STATE].astype(BF16)
        b_g32 = b_ref[:, g * SSD_STATE:(g + 1) * SSD_STATE]
        s_g = st_ref[:, gs]
        y_off = _dot(c_g, s_g.astype(BF16)) * eac_e[:, gs]
        cb = _dot_nt(c_g, b_g32.astype(BF16))
        for jp in range(gw // LANES):
            h_a = g * (SSD_HEADS // SSD_GROUPS) + 2 * jp
            ls = slice(h_a * hd, h_a * hd + LANES)
            xpair = xdt[:, ls]
            acc = y_off[:, jp * LANES:(jp + 1) * LANES]
            for k in range(2):
                cix = direction * SSD_HEADS + h_a + k
                seg = acum[:, cix:cix + 1] - acum_t[cix:cix + 1, :]
                dec = jnp.where(mask, jnp.exp(jnp.where(mask, seg, 0.0)), 0.0)
                w = (cb * dec).astype(BF16)
                xk = jnp.where(first if k == 0 else jnp.logical_not(first), xpair, 0.0).astype(BF16)
                acc = acc + _dot(w, xk)
            y_ref[:, ls] = acc
        xdte = (xdt[:, gs] * dte_e[:, gs]).astype(BF16)
        st_ref[:, gs] = s_g * cd_e[:, gs] + _dot(b_g32.T.astype(BF16), xdte)

    @pl.when(ci == nchunks - 1)
    def _():
        hout_ref[...] = st_ref[...]


def ssd_scan(xbc, dt_raw, dt_bias_row, aneg_row, expand, h0, direction):
    b, t, _ = xbc.shape
    q = SSD_CHUNK
    nchunks = t // q
    inner = SSD_HEADS * SSD_HEAD_DIM
    bn = SSD_GROUPS * SSD_STATE
    cmap = (lambda c: nchunks - 1 - c) if direction else (lambda c: c)
    y, hout = pl.pallas_call(
        functools.partial(_ssd_kernel, direction=direction, nchunks=nchunks),
        grid=(b, nchunks),
        in_specs=[pl.BlockSpec((None, q, inner), lambda bi, c: (bi, cmap(c), 0)),
                  pl.BlockSpec((None, q, bn), lambda bi, c: (bi, cmap(c), inner // bn)),
                  pl.BlockSpec((None, q, bn), lambda bi, c: (bi, cmap(c), inner // bn + 1)),
                  pl.BlockSpec((None, q, LANES), lambda bi, c: (bi, cmap(c), 0)),
                  pl.BlockSpec((1, LANES), lambda bi, c: (0, 0)),
                  pl.BlockSpec((1, LANES), lambda bi, c: (0, 0)),
                  pl.BlockSpec((LANES, inner), lambda bi, c: (0, 0)),
                  pl.BlockSpec((None, SSD_STATE, inner), lambda bi, c: (bi, 0, 0))],
        out_specs=[pl.BlockSpec((None, q, inner), lambda bi, c: (bi, cmap(c), 0)),
                   pl.BlockSpec((None, SSD_STATE, inner), lambda bi, c: (bi, 0, 0))],
        out_shape=[jax.ShapeDtypeStruct((b, t, inner), F32),
                   jax.ShapeDtypeStruct((b, SSD_STATE, inner), F32)],
        scratch_shapes=[pltpu.VMEM((SSD_STATE, inner), F32)],
        compiler_params=_cparams("parallel", "arbitrary"),
        name="ssd_scan_bwd" if direction else "ssd_scan_fwd",
    )(xbc, xbc, xbc, dt_raw, dt_bias_row, aneg_row, expand, h0)
    return y, hout


def _ssd_out_kernel(yf_ref, yb_ref, x_ref, z_ref, d_ref, w_ref, o_ref):
    y = yf_ref[...] + yb_ref[...] + d_ref[...] * x_ref[...]
    y = y * _silu(z_ref[...].astype(F32))
    gw = y.shape[1] // SSD_GROUPS
    for g in range(SSD_GROUPS):
        gs = slice(g * gw, (g + 1) * gw)
        yg = y[:, gs]
        yg = yg * lax.rsqrt(jnp.mean(yg * yg, axis=-1, keepdims=True) + EPS)
        o_ref[:, gs] = (yg * w_ref[:, gs]).astype(o_ref.dtype)


def ssd_out(yf, yb, xbc, z, d_row, w_row, tr=256):
    b, t, inner = yf.shape
    tr = _pick_tile(t, tr)
    blk = pl.BlockSpec((None, tr, inner), lambda bi, i: (bi, i, 0))
    vec = pl.BlockSpec((1, inner), lambda bi, i: (0, 0))
    return pl.pallas_call(
        _ssd_out_kernel,
        grid=(b, t // tr),
        in_specs=[blk, blk, blk, blk, vec, vec],
        out_specs=blk,
        out_shape=jax.ShapeDtypeStruct((b, t, inner), BF16),
        compiler_params=_cparams("parallel", "parallel"),
        name="ssd_out",
    )(yf, yb, xbc, z, d_row, w_row)


def _taps_kernel(f_ref, tv_ref, w1_ref, b1_ref, w2_ref, b2_ref, w3_ref, fr_ref, dl_ref, o_ref):
    hp = lax.Precision.HIGHEST
    fr = fr_ref[...]
    h = jnp.sin(fr * (jnp.dot(f_ref[...], w1_ref[...], precision=hp, preferred_element_type=F32) + b1_ref[...]))
    h = jnp.sin(fr * (jnp.dot(h, w2_ref[...], precision=hp, preferred_element_type=F32) + b2_ref[...]))
    filt = jnp.dot(h, w3_ref[...], precision=hp, preferred_element_type=F32)
    c = dl_ref.shape[1]
    win = jnp.exp(-tv_ref[:, 0:1] * dl_ref[...]) * tv_ref[:, 1:2]
    o_ref[0] = filt[:, :c] * win
    o_ref[1] = filt[:, c:] * win


def hyena_taps(length, w1, b1, w2, b2, w3, freq):
    hid = w1.shape[1]
    c = w3.shape[1] // (2 * HY_ORDER)
    n = 2 * length
    t = jnp.linspace(0.0, 1.0, length, dtype=F32)[:, None]
    phase = 2.0 * math.pi * jnp.arange(length, dtype=F32)[:, None] / length
    bands = jnp.linspace(1e-4, HY_BANDS - 1, HY_BANDS, dtype=F32)[None, :]
    feats = jnp.concatenate([t, jnp.cos(phase * bands), -jnp.sin(phase * bands)], axis=-1)
    nf = feats.shape[1]
    rev = jnp.concatenate([jnp.zeros((1,), jnp.int32), jnp.arange(length - 1, 0, -1, dtype=jnp.int32)])
    feats2 = jnp.concatenate([feats, feats[rev]], axis=0)
    feats2 = jnp.pad(feats2, ((0, 0), (0, LANES - nf)))
    valid = jnp.ones((n,), F32).at[length].set(0.0)
    tv = jnp.stack([jnp.concatenate([t[:, 0], t[rev, 0]]), valid], axis=1)
    max_decay = math.log(HY_DECAY_TARGET) / HY_SHORT_DECAY_PCT
    min_decay = math.log(HY_DECAY_TARGET) / HY_LONG_DECAY_PCT
    deltas = jnp.abs(jnp.linspace(min_decay, max_decay, c, dtype=F32))[None, :]
    w1p = jnp.pad(w1, ((0, LANES - nf), (0, 0)))
    w3r = w3.reshape(hid, HY_ORDER, 2, c).transpose(2, 0, 1, 3).reshape(2, hid, HY_ORDER * c)
    tr = _pick_tile(length, 1024)
    nhalf = length // tr
    return pl.pallas_call(
        _taps_kernel,
        grid=(n // tr,),
        in_specs=[pl.BlockSpec((tr, LANES), lambda i: (i, 0)),
                  pl.BlockSpec((tr, 2), lambda i: (i, 0)),
                  pl.BlockSpec((LANES, hid), lambda i: (0, 0)),
                  pl.BlockSpec((1, hid), lambda i: (0, 0)),
                  pl.BlockSpec((hid, hid), lambda i: (0, 0)),
                  pl.BlockSpec((1, hid), lambda i: (0, 0)),
                  pl.BlockSpec((None, hid, HY_ORDER * c), lambda i: (i // nhalf, 0, 0)),
                  pl.BlockSpec((1, hid), lambda i: (0, 0)),
                  pl.BlockSpec((1, c), lambda i: (0, 0))],
        out_specs=pl.BlockSpec((HY_ORDER, tr, c), lambda i: (0, i, 0)),
        out_shape=jax.ShapeDtypeStruct((HY_ORDER, n, c), F32),
        compiler_params=_cparams("parallel"),
        name="hyena_taps",
    )(feats2, tv, w1p, b1.reshape(1, hid), w2, b2.reshape(1, hid), w3r, freq.reshape(1, hid), deltas)


def _cs(num, den):
    ang = (2.0 * math.pi / den) * num.astype(F32)
    return jnp.cos(ang), jnp.sin(ang)


def fft_tables(n1, n2):
    n = n1 * n2
    h = n1 // 2
    k1 = jnp.arange(n1, dtype=jnp.int32)
    c1, s1 = _cs((k1[:, None] * k1[None, :]) % n1, n1)
    g_u = jnp.block([[c1[:, :h], s1[:, :h]], [-s1[:, :h], c1[:, :h]]])
    g_t = jnp.concatenate([c1, -s1], axis=0)
    g_i = jnp.block([[c1[:h, :], -s1[:h, :]], [s1[:h, :], c1[:h, :]]])
    s = jnp.arange(n2, dtype=jnp.int32)
    num = (n1 * (s[:, None] * s[None, :]))[None] + (k1[:, None, None] * s[None, None, :])
    cb, sb = _cs(num % n, n)
    m_f = jnp.concatenate([jnp.concatenate([cb, sb], axis=2),
                           jnp.concatenate([-sb, cb], axis=2)], axis=1)
    cbt, sbt = jnp.swapaxes(cb, 1, 2) / n, jnp.swapaxes(sb, 1, 2) / n
    m_i = jnp.concatenate([jnp.concatenate([cbt, -sbt], axis=2),
                           jnp.concatenate([sbt, cbt], axis=2)], axis=1)
    return dict(g_u=g_u.astype(BF16), g_t=g_t.astype(BF16), g_i=g_i.astype(BF16),
                m_f=m_f.astype(BF16), m_i=m_i.astype(BF16))


def _fft_a_kernel(g_ref, xa_ref, xb_ref, o_ref):
    h = xa_ref.shape[0]
    n1 = o_ref.shape[0]
    ga = g_ref[:, :h]
    gb = g_ref[:, h:]
    for s in range(xa_ref.shape[1]):
        r = _dot(ga, xa_ref[:, s, :].astype(BF16)) + _dot(gb, xb_ref[:, s, :].astype(BF16))
        o_ref[:, 0, s, :] = r[:n1]
        o_ref[:, 1, s, :] = r[n1:]


def fft_stage_a(g, x4, sel_a, sel_b, col_off, c, tc=256):
    n1 = g.shape[0] // 2
    h = n1 // 2
    n2 = x4.shape[2]
    ts = SUBLANES
    coff = col_off // tc
    return pl.pallas_call(
        _fft_a_kernel,
        grid=(c // tc, n2 // ts),
        in_specs=[pl.BlockSpec((2 * n1, n1), lambda j, s: (0, 0)),
                  pl.BlockSpec((None, h, ts, tc), lambda j, s: (sel_a[0], sel_a[1], s, j + coff)),
                  pl.BlockSpec((None, h, ts, tc), lambda j, s: (sel_b[0], sel_b[1], s, j + coff))],
        out_specs=pl.BlockSpec((n1, 2, ts, tc), lambda j, s: (0, 0, s, j)),
        out_shape=jax.ShapeDtypeStruct((n1, 2, n2, c), F32),
        compiler_params=_cparams("parallel", "parallel"),
        name="fft_stage_a",
    )(g, x4, x4)


def _fft_bh_kernel(m_ref, a_ref, o_ref):
    n2 = a_ref.shape[1]
    a = a_ref[...].reshape(2 * n2, a_ref.shape[2]).astype(BF16)
    o_ref[...] = _dot(m_ref[...], a).reshape(o_ref.shape)


def fft_stage_b_spectrum(m_f, a):
    n1, _, n2, c = a.shape
    return pl.pallas_call(
        _fft_bh_kernel,
        grid=(n1,),
        in_specs=[pl.BlockSpec((None, 2 * n2, 2 * n2), lambda k: (k, 0, 0)),
                  pl.BlockSpec((None, 2, n2, c), lambda k: (k, 0, 0, 0))],
        out_specs=pl.BlockSpec((None, 2, n2, c), lambda k: (k, 0, 0, 0)),
        out_shape=jax.ShapeDtypeStruct(a.shape, F32),
        compiler_params=_cparams("parallel"),
        name="fft_stage_b_spectrum",
    )(m_f, a)


def _fft_b_kernel(mf_ref, mi_ref, a_ref, h_ref, o_ref):
    n2 = a_ref.shape[1]
    a = a_ref[...].reshape(2 * n2, a_ref.shape[2]).astype(BF16)
    x = _dot(mf_ref[...], a)
    xr, xi = x[:n2], x[n2:]
    hr, hi = h_ref[0], h_ref[1]
    y = jnp.concatenate([xr * hr - xi * hi, xr * hi + xi * hr], axis=0).astype(BF16)
    o_ref[...] = _dot(mi_ref[...], y).reshape(o_ref.shape)


def fft_stage_b(m_f, m_i, a, hspec):
    n1, _, n2, c = a.shape
    blk = pl.BlockSpec((None, 2, n2, c), lambda k: (k, 0, 0, 0))
    mat = pl.BlockSpec((None, 2 * n2, 2 * n2), lambda k: (k, 0, 0))
    return pl.pallas_call(
        _fft_b_kernel,
        grid=(n1,),
        in_specs=[mat, mat, blk, blk],
        out_specs=blk,
        out_shape=jax.ShapeDtypeStruct(a.shape, F32),
        compiler_params=_cparams("parallel"),
        name="fft_stage_b",
    )(m_f, m_i, a, hspec)


def _fft_ai_kernel(g_ref, b_ref, u_ref, x_ref, skip_ref, o_ref):
    n1 = b_ref.shape[0]
    h = n1 // 2
    ga = g_ref[:, :n1]
    gb = g_ref[:, n1:]
    skip = skip_ref[...]
    for s in range(b_ref.shape[2]):
        y = _dot(ga, b_ref[:, 0, s, :].astype(BF16)) + _dot(gb, b_ref[:, 1, s, :].astype(BF16))
        for bi in range(2):
            yb = y[bi * h:(bi + 1) * h]
            o_ref[bi, :, s, :] = (x_ref[bi, :, s, :] * (yb + u_ref[bi, :, s, :] * skip)).astype(o_ref.dtype)


def fft_stage_a_inv(g_i, bm, u4, u_off, x4, x_off, skip_row, out_dtype, tc=256):
    n1, _, n2, c = bm.shape
    h = n1 // 2
    ts = SUBLANES
    uo, xo = u_off // tc, x_off // tc
    return pl.pallas_call(
        _fft_ai_kernel,
        grid=(c // tc, n2 // ts),
        in_specs=[pl.BlockSpec((n1, 2 * n1), lambda j, s: (0, 0)),
                  pl.BlockSpec((n1, 2, ts, tc), lambda j, s: (0, 0, s, j)),
                  pl.BlockSpec((2, h, ts, tc), lambda j, s: (0, 0, s, j + uo)),
                  pl.BlockSpec((2, h, ts, tc), lambda j, s: (0, 0, s, j + xo)),
                  pl.BlockSpec((1, tc), lambda j, s: (0, j))],
        out_specs=pl.BlockSpec((2, h, ts, tc), lambda j, s: (0, 0, s, j)),
        out_shape=jax.ShapeDtypeStruct((2, h, n2, c), out_dtype),
        compiler_params=_cparams("parallel", "parallel"),
        name="fft_stage_a_inv",
    )(g_i, bm, u4, x4, skip_row)


def hyena_long(u, taps, skip, tabs):
    b, length, c3 = u.shape
    assert b == 2, "the two batch rows are packed as one complex sequence"
    c = c3 // 3
    n2 = FFT_N2
    n1 = 2 * length // n2
    h = n1 // 2
    u4 = u.reshape(b, h, n2, c3)
    taps4 = taps.reshape(HY_ORDER * 2, h, n2, c)
    zin, zoff = u4, 0
    out = None
    for conv in range(HY_ORDER):
        hspec = fft_stage_b_spectrum(
            tabs["m_f"], fft_stage_a(tabs["g_t"], taps4, (2 * conv, 0), (2 * conv + 1, 0), 0, c))
        a = fft_stage_a(tabs["g_u"], zin, (0, 0), (1, 0), zoff, c)
        bm = fft_stage_b(tabs["m_f"], tabs["m_i"], a, hspec)
        last = conv == HY_ORDER - 1
        out = fft_stage_a_inv(tabs["g_i"], bm, zin, zoff, u4, (conv + 1) * c,
                              skip[conv].reshape(1, c), BF16 if last else F32)
        zin, zoff = out, 0
    return out.reshape(b, length, c)


def _hyena_ctx_kernel(u_ref, x1_ref, x2_ref, taps_ref, fc_ref, fs_ref, skip_ref, o_ref):
    length = u_ref.shape[1]
    n = 2 * length
    fc = fc_ref[...]
    fs = fs_ref[...]
    fcl, fsl = fc[:, :length], fs[:, :length]
    fct, fst = fc[:length, :], fs[:length, :]
    zr = u_ref[0]
    zi = u_ref[1]
    gates = (x1_ref, x2_ref)
    for conv in range(HY_ORDER):
        tp = taps_ref[conv].astype(BF16)
        hr, hi = _dot(fc, tp), -_dot(fs, tp)
        zrb, zib = zr.astype(BF16), zi.astype(BF16)
        xr = _dot(fcl, zrb) + _dot(fsl, zib)
        xi = _dot(fcl, zib) - _dot(fsl, zrb)
        yr = (xr * hr - xi * hi).astype(BF16)
        yi = (xr * hi + xi * hr).astype(BF16)
        cr = (_dot(fct, yr) - _dot(fst, yi)) * (1.0 / n)
        ci = (_dot(fct, yi) + _dot(fst, yr)) * (1.0 / n)
        sk = skip_ref[conv:conv + 1, :]
        zr = gates[conv][0] * (cr + zr * sk)
        zi = gates[conv][1] * (ci + zi * sk)
    o_ref[0] = zr.astype(o_ref.dtype)
    o_ref[1] = zi.astype(o_ref.dtype)


def hyena_ctx(u, taps, skip, tc=256):
    b, length, c3 = u.shape
    assert b == 2
    c = c3 // 3
    n = 2 * length
    k = jnp.arange(n, dtype=jnp.int32)
    fc, fs = _cs((k[:, None] * k[None, :]) % n, n)
    nb = c // tc
    ub = lambda off: pl.BlockSpec((2, length, tc), lambda j: (0, 0, j + off * nb))
    mat = pl.BlockSpec((n, n), lambda j: (0, 0))
    return pl.pallas_call(
        _hyena_ctx_kernel,
        grid=(nb,),
        in_specs=[ub(0), ub(1), ub(2),
                  pl.BlockSpec((HY_ORDER, n, tc), lambda j: (0, 0, j)),
                  mat, mat,
                  pl.BlockSpec((HY_ORDER, tc), lambda j: (0, j))],
        out_specs=pl.BlockSpec((2, length, tc), lambda j: (0, 0, j)),
        out_shape=jax.ShapeDtypeStruct((b, length, c), BF16),
        compiler_params=_cparams("parallel"),
        name="hyena_ctx",
    )(u, u, u, taps, fc.astype(BF16), fs.astype(BF16), skip)


def rope_tables(length):
    rows = length // GRID_W
    row = jnp.repeat(jnp.arange(rows, dtype=F32), GRID_W)
    col = (jnp.arange(length) % GRID_W).astype(F32)
    inv = ROPE_THETA ** (-jnp.arange(0, ROPE_AXIS_DIM, 2, dtype=F32) / ROPE_AXIS_DIM)
    ang = jnp.stack([row[:, None] * inv, col[:, None] * inv], axis=1)
    cos, sin = jnp.cos(ang), jnp.sin(ang)
    zero = jnp.zeros_like(sin)
    per_map = lambda lo, hi: jnp.concatenate([lo, hi], axis=-1).reshape(length, 2 * ROPE_AXIS_DIM)
    reps = LANES // (2 * ROPE_AXIS_DIM)
    c_t = jnp.tile(per_map(cos, cos), (1, reps))
    sm_t = jnp.tile(per_map(-sin, zero), (1, reps))
    sp_t = jnp.tile(per_map(zero, sin), (1, reps))
    return c_t, sm_t, sp_t


def _rope_kernel(x_ref, c_ref, sm_ref, sp_ref, o_ref, *n_refs, scale):
    half = ROPE_AXIS_DIM // 2
    c, sm, sp = c_ref[...], sm_ref[...], sp_ref[...]
    for g in range(x_ref.shape[1] // LANES):
        ls = slice(g * LANES, (g + 1) * LANES)
        x = x_ref[:, ls].astype(F32)
        y = (x * c + pltpu.roll(x, LANES - half, 1) * sm + pltpu.roll(x, half, 1) * sp) * scale
        o_ref[:, ls] = y.astype(o_ref.dtype)
        if n_refs:
            n2 = jnp.max(jnp.sum(y * y, axis=1, keepdims=True), axis=0, keepdims=True)
            n_refs[0][g:g + 1, :] = jnp.broadcast_to(n2, (1, LANES))


def rope(x, tables, scale, with_norms=False, tr=512):
    b, t, d = x.shape
    tr = _pick_tile(t, tr)
    nh = d // LANES
    tab = pl.BlockSpec((tr, LANES), lambda bi, i: (i, 0))
    out_specs = [pl.BlockSpec((None, tr, d), lambda bi, i: (bi, i, 0))]
    out_shape = [jax.ShapeDtypeStruct((b, t, d), BF16)]
    if with_norms:
        out_specs.append(pl.BlockSpec((None, None, nh, LANES), lambda bi, i: (bi, i, 0, 0)))
        out_shape.append(jax.ShapeDtypeStruct((b, t // tr, nh, LANES), F32))
    outs = pl.pallas_call(
        functools.partial(_rope_kernel, scale=scale),
        grid=(b, t // tr),
        in_specs=[pl.BlockSpec((None, tr, d), lambda bi, i: (bi, i, 0)), tab, tab, tab],
        out_specs=out_specs,
        out_shape=out_shape,
        compiler_params=_cparams("parallel", "parallel"),
        name="rope",
    )(x, *tables)
    if with_norms:
        return outs[0], jnp.swapaxes(outs[1], 1, 2)
    return outs[0]


def _scale_cast_kernel(x_ref, o_ref, *, scale):
    o_ref[...] = (x_ref[...].astype(F32) * scale).astype(o_ref.dtype)


def scale_cast(x, scale, tr=256):
    b, t, d = x.shape
    tr = _pick_tile(t, tr)
    blk = pl.BlockSpec((None, tr, d), lambda bi, i: (bi, i, 0))
    return pl.pallas_call(
        functools.partial(_scale_cast_kernel, scale=scale),
        grid=(b, t // tr), in_specs=[blk], out_specs=blk,
        out_shape=jax.ShapeDtypeStruct((b, t, d), BF16),
        compiler_params=_cparams("parallel", "parallel"),
        name="scale_cast",
    )(x)


ATTN_GUARD_LOG2 = 80.0
ATTN_NORM_SLACK = 1.02


def _attn_kernel(*refs, lam_init, nk, with_ctx, nsub):
    if with_ctx:
        (q_ref, k_ref, v_ref, kc_ref, vc_ref, kn_ref, lam_ref, w_ref, o_ref,
         q2_ref, m_ref, acc_ref, p_scr, flag_ref) = refs
    else:
        q_ref, k_ref, v_ref, lam_ref, w_ref, o_ref, q2_ref, m_ref, acc_ref, p_scr, flag_ref = refs
    ki = pl.program_id(3)
    tq = q_ref.shape[0]

    def with_ones(v):
        lane = lax.broadcasted_iota(jnp.int32, v.shape, 1)
        return jnp.concatenate([v, (lane == 0).astype(BF16)], axis=1)

    def exact_update(k, v, p_buf):
        s = _dot_nt(q2_ref[...], k)
        m_prev = m_ref[...]
        m_new = jnp.maximum(m_prev, jnp.max(s, axis=1, keepdims=True))
        alpha = jnp.exp2(m_prev - m_new)
        p_buf[...] = jnp.exp2(s - m_new).astype(BF16)
        acc_ref[...] = alpha * acc_ref[...] + _dot(p_buf[...], with_ones(v))
        m_ref[...] = m_new

    def fast_update(k, v, p_buf):
        s = _dot_nt(q2_ref[...], k)
        p_buf[...] = jnp.exp2(s - m_ref[...]).astype(BF16)
        acc_ref[...] += _dot(p_buf[...], with_ones(v))

    @pl.when(ki == 0)
    def _():
        q = q_ref[...]
        lane = lax.broadcasted_iota(jnp.int32, q.shape, 1)
        zero = jnp.zeros_like(q)
        q2_ref[:tq, :] = jnp.where(lane < DA_HEAD_DIM, q, zero)
        q2_ref[tq:, :] = jnp.where(lane >= DA_HEAD_DIM, q, zero)
        m_ref[...] = jnp.full(m_ref.shape, -jnp.inf, F32)
        acc_ref[...] = jnp.zeros(acc_ref.shape, F32)
        flag_ref[0] = 0
        if with_ctx:
            tc = kc_ref.shape[0]
            exact_update(kc_ref[...], vc_ref[...], p_scr.at[0, :, pl.ds(0, tc)])
            q2 = q2_ref[...].astype(F32)
            qn2 = jnp.sum(q2 * q2, axis=1, keepdims=True)
            kn2 = jnp.max(kn_ref[...])
            gap = jnp.max(jnp.sqrt(qn2 * kn2) * ATTN_NORM_SLACK - m_ref[...])
            flag_ref[0] = (gap <= ATTN_GUARD_LOG2).astype(jnp.int32)

    tks = k_ref.shape[0] // nsub
    fast = flag_ref[0] == 1

    @pl.when(fast)
    def _():
        for sub in range(nsub):
            ks = slice(sub * tks, (sub + 1) * tks)
            fast_update(k_ref[ks, :], v_ref[ks, :], p_scr.at[sub])

    @pl.when(jnp.logical_not(fast))
    def _():
        for sub in range(nsub):
            ks = slice(sub * tks, (sub + 1) * tks)
            exact_update(k_ref[ks, :], v_ref[ks, :], p_scr.at[sub])

    @pl.when(ki == nk - 1)
    def _():
        lv = lam_ref[...]
        e1 = jnp.exp(jnp.sum(lv[0:1, :] * lv[1:2, :], axis=1, keepdims=True))
        e2 = jnp.exp(jnp.sum(lv[2:3, :] * lv[3:4, :], axis=1, keepdims=True))
        lam = e1 - e2 + lam_init
        acc = acc_ref[...]
        o = acc[:, :DA_V_DIM] / acc[:, DA_V_DIM:DA_V_DIM + 1]
        o = o[:tq] - lam * o[tq:]
        o_ref[...] = (_rms(o, w_ref[...]) * (1.0 - lam_init)).astype(o_ref.dtype)


def diff_attention(q, k, v, k_ctx, v_ctx, knorm, lam_vec, subln_w, lam_init, tq=1024, tk=2048):
    b, t, d = q.shape
    nh = d // DA_V_DIM
    tq = _pick_tile(t, tq)
    tk = _pick_tile(k.shape[1], tk)
    nsub = 1
    tks = tk
    nk = k.shape[1] // tk
    with_ctx = k_ctx is not None
    assert not with_ctx or k_ctx.shape[1] <= tks
    qspec = pl.BlockSpec((None, tq, DA_V_DIM), lambda bi, h, i, j: (bi, i, h))
    kspec = pl.BlockSpec((None, tk, DA_V_DIM), lambda bi, h, i, j: (bi, j, h))
    in_specs = [qspec, kspec, kspec]
    args = [q, k, v]
    if with_ctx:
        cspec = pl.BlockSpec((None, k_ctx.shape[1], DA_V_DIM), lambda bi, h, i, j: (bi, 0, h))
        nspec = pl.BlockSpec((None, None) + knorm.shape[2:], lambda bi, h, i, j: (bi, h, 0, 0))
        in_specs += [cspec, cspec, nspec]
        args += [k_ctx, v_ctx, knorm]
    in_specs += [pl.BlockSpec((4, DA_HEAD_DIM), lambda bi, h, i, j: (0, 0)),
                 pl.BlockSpec((1, DA_V_DIM), lambda bi, h, i, j: (0, 0))]
    args += [lam_vec, subln_w.reshape(1, DA_V_DIM)]
    return pl.pallas_call(
        functools.partial(_attn_kernel, lam_init=lam_init, nk=nk, with_ctx=with_ctx, nsub=nsub),
        grid=(b, nh, t // tq, nk),
        in_specs=in_specs,
        out_specs=qspec,
        out_shape=jax.ShapeDtypeStruct((b, t, d), BF16),
        scratch_shapes=[pltpu.VMEM((2 * tq, DA_V_DIM), BF16),
                        pltpu.VMEM((2 * tq, 1), F32),
                        pltpu.VMEM((2 * tq, 2 * DA_V_DIM), F32),
                        pltpu.VMEM((nsub, 2 * tq, tks), BF16),
                        pltpu.SMEM((1,), jnp.int32)],
        compiler_params=_cparams("parallel", "parallel", "parallel", "arbitrary"),
        name="diff_attention",
    )(*args)


def _merge_kernel(ya_ref, yb_ref, yc_ref, g_ref, wb_ref, wo_ref, x_ref, mod_ref, o_ref):
    d = x_ref.shape[1]
    ys = (ya_ref, yb_ref, yc_ref)
    s = None
    for k in range(N_BRANCH):
        t = jax.nn.sigmoid(g_ref[:, k * d:(k + 1) * d].astype(F32)) * _dot(ys[k][...], wb_ref[k])
        s = t if s is None else s + t
    r = _dot(s.astype(BF16), wo_ref[...])
    o_ref[...] = x_ref[...] + mod_ref[2:3, :] * r


def merge(ya, yb, yc, gates, wb, wo, x, mods, tr=256):
    b, t, d = x.shape
    tr = _pick_tile(t, tr)
    per_batch = mods.shape[0] > 1
    blk = pl.BlockSpec((None, tr, d), lambda bi, i: (bi, i, 0))
    return pl.pallas_call(
        _merge_kernel,
        grid=(b, t // tr),
        in_specs=[blk, blk, blk,
                  pl.BlockSpec((None, tr, N_BRANCH * d), lambda bi, i: (bi, i, 0)),
                  pl.BlockSpec((N_BRANCH, d, d), lambda bi, i: (0, 0, 0)),
                  pl.BlockSpec((d, d), lambda bi, i: (0, 0)),
                  blk,
                  pl.BlockSpec((None, 8, d), (lambda bi, i: (bi, 0, 0)) if per_batch else (lambda bi, i: (0, 0, 0)))],
        out_specs=blk,
        out_shape=jax.ShapeDtypeStruct((b, t, d), F32),
        compiler_params=_cparams("parallel", "parallel"),
        name="merge",
    )(ya, yb, yc, gates, wb, wo, x, mods)


def _ffn_kernel(x_ref, g_ref, mod_ref, w13_ref, w2_ref, o_ref, *, nchunk):
    x = x_ref[...]
    h = (_rms(x, g_ref[...]) * (1.0 + mod_ref[4:5, :]) + mod_ref[3:4, :]).astype(BF16)
    f = w2_ref.shape[0]
    fc = f // nchunk
    acc = None
    for c in range(nchunk):
        a1 = _dot(h, w13_ref[:, c * fc:(c + 1) * fc])
        a3 = _dot(h, w13_ref[:, f + c * fc:f + (c + 1) * fc])
        t = _dot((_silu(a1) * a3).astype(BF16), w2_ref[c * fc:(c + 1) * fc, :])
        acc = t if acc is None else acc + t
    o_ref[...] = x + mod_ref[5:6, :] * acc


def ffn(x, g, mods, w13, w2, tr=256):
    b, t, d = x.shape
    f = w2.shape[0]
    tr = _pick_tile(t, tr)
    nchunk = 2 if (f // 2) % LANES == 0 else 1
    per_batch = mods.shape[0] > 1
    blk = pl.BlockSpec((None, tr, d), lambda bi, i: (bi, i, 0))
    return pl.pallas_call(
        functools.partial(_ffn_kernel, nchunk=nchunk),
        grid=(b, t // tr),
        in_specs=[blk,
                  pl.BlockSpec((1, d), lambda bi, i: (0, 0)),
                  pl.BlockSpec((None, 8, d), (lambda bi, i: (bi, 0, 0)) if per_batch else (lambda bi, i: (0, 0, 0))),
                  pl.BlockSpec((d, 2 * f), lambda bi, i: (0, 0)),
                  pl.BlockSpec((f, d), lambda bi, i: (0, 0))],
        out_specs=blk,
        out_shape=jax.ShapeDtypeStruct((b, t, d), F32),
        compiler_params=_cparams("parallel", "parallel"),
        name="ffn",
    )(x, g.reshape(1, d), mods, w13, w2)


def kernel(x, c, ctx, c_ctx, ada_w, ada_b, norm1_g, norm2_g, w_in, ssd_conv_w, ssd_conv_b,
           ssd_a_log, ssd_dt_bias, ssd_d, ssd_norm_w, hy_conv_w, hy_conv_b, hy_w1, hy_b1, hy_w2,
           hy_b2, hy_w3, hy_freq, hy_bias, da_lambda, da_subln_w, w_branch, w_out, ffn_w13,
           ffn_w2, final_g):
    bsz, length, d = x.shape
    ctx_len = ctx.shape[1]
    depth = w_in.shape[0]
    inner = SSD_HEADS * SSD_HEAD_DIM
    xbc_w = inner + 2 * SSD_GROUPS * SSD_STATE
    dt_w = 2 * SSD_HEADS
    hy_w = (HY_ORDER + 1) * d
    widths = (inner, xbc_w, dt_w, hy_w, d, d, d, N_BRANCH * d)
    offs = [0]
    for wd in widths:
        offs.append(offs[-1] + wd)

    cond = jnp.concatenate([c, c_ctx[None, :], jnp.zeros((8 - bsz - 1, d), F32)], axis=0)

    rope_tabs = rope_tables(length)
    tabs = fft_tables(2 * length // FFT_N2, FFT_N2)
    expand = (jnp.arange(LANES)[:, None] % SSD_HEADS
              == jnp.arange(inner)[None, :] // SSD_HEAD_DIM)
    expand_dir = [(expand & ((jnp.arange(LANES)[:, None] // SSD_HEADS) == dr)).astype(BF16) for dr in range(2)]
    qscale = DA_HEAD_DIM ** -0.5 * math.log2(math.e)

    x_l, x_c = x, ctx
    for layer in range(depth):
        last = layer == depth - 1
        lam_init = 0.8 - 0.6 * math.exp(-0.3 * layer)
        mod = matmul(cond, ada_w[layer].astype(BF16), ada_b[layer].reshape(1, -1), silu_in=True)
        mod = jnp.pad(mod.reshape(8, 6, d), ((0, 0), (0, 2), (0, 0)))
        mods_l, mods_c = mod[:bsz], mod[bsz:bsz + 1]

        w_l = w_in[layer].astype(BF16)
        w_parts = [w_l[:, offs[i]:offs[i + 1]] for i in range(len(widths))]
        w_parts[2] = jnp.pad(w_parts[2], ((0, 0), (0, LANES - dt_w)))
        a_neg = -jnp.exp(ssd_a_log[layer].astype(F32)).reshape(1, dt_w)
        aneg_row = jnp.pad(a_neg, ((0, 0), (0, LANES - dt_w)))
        bias_row = jnp.pad(ssd_dt_bias[layer].astype(F32).reshape(1, dt_w), ((0, 0), (0, LANES - dt_w)))
        d_row = jnp.repeat(ssd_d[layer].astype(F32), SSD_HEAD_DIM).reshape(1, inner)
        nw_row = ssd_norm_w[layer].astype(F32).reshape(1, inner)
        wb = w_branch[layer].astype(BF16)
        wo = w_out[layer].astype(BF16)
        w13 = ffn_w13[layer].astype(BF16)
        w2 = ffn_w2[layer].astype(BF16)

        def project(stream, mods):
            bb, tt, _ = stream.shape
            h = norm_mod(stream, norm1_g[layer], mods, 0).reshape(bb * tt, d)
            outs = [matmul(h, wp, out_dtype=(F32 if i == 2 else BF16)).reshape(bb, tt, -1)
                    for i, wp in enumerate(w_parts)]
            return outs

        z_c, xbc_c, dt_c, hy_c, q_c, k_c, v_c, g_c = project(x_c, mods_c)
        z_l, xbc_l, dt_l, hy_l, q_l, k_l, v_l, g_l = project(x_l, mods_l)

        xa_c = dwconv3(xbc_c, ssd_conv_w[layer], ssd_conv_b[layer], True)
        xa_l = dwconv3(xbc_l, ssd_conv_w[layer], ssd_conv_b[layer], True)
        h_zero = jnp.zeros((bsz, SSD_STATE, inner), F32)
        ys_c, ys_l = [], []
        for dr in range(2):
            y_c, st = ssd_scan(xa_c, dt_c, bias_row, aneg_row, expand_dir[dr], h_zero, dr)
            y_l, _ = ssd_scan(xa_l, dt_l, bias_row, aneg_row, expand_dir[dr], st, dr)
            ys_c.append(y_c)
            ys_l.append(y_l)
        ya_l = ssd_out(ys_l[0], ys_l[1], xa_l, z_l, d_row, nw_row)

        filt = (hy_w1[layer], hy_b1[layer], hy_w2[layer], hy_b2[layer], hy_w3[layer], hy_freq[layer])
        u_l = dwconv3(hy_l, hy_conv_w[layer], hy_conv_b[layer], False)
        yb_l = hyena_long(u_l, hyena_taps(length, *filt), hy_bias[layer], tabs)

        qr_l = rope(q_l, rope_tabs, qscale)
        kr_l, kn_l = rope(k_l, rope_tabs, 1.0, with_norms=True)
        kb_c = scale_cast(k_c, 1.0)
        yc_l = diff_attention(qr_l, kr_l, v_l, kb_c, v_c, kn_l, da_lambda[layer], da_subln_w[layer], lam_init)

        x_l_new = merge(ya_l, yb_l, yc_l, g_l, wb, wo, x_l, mods_l)
        x_l = ffn(x_l_new, norm2_g[layer], mods_l, w13, w2)

        if not last:
            ya_c = ssd_out(ys_c[0], ys_c[1], xa_c, z_c, d_row, nw_row)
            u_c = dwconv3(hy_c, hy_conv_w[layer], hy_conv_b[layer], False)
            yb_c = hyena_ctx(u_c, hyena_taps(ctx_len, *filt), hy_bias[layer])
            qb_c = scale_cast(q_c, qscale)
            yc_c = diff_attention(qb_c, kb_c, v_c, None, None, None, da_lambda[layer], da_subln_w[layer], lam_init)
            x_c_new = merge(ya_c, yb_c, yc_c, g_c, wb, wo, x_c, mods_c)
            x_c = ffn(x_c_new, norm2_g[layer], mods_c, w13, w2)
    return final_norm(x_l, final_g)
```

```python
import functools
import math

import jax
import jax.numpy as jnp
from jax import lax
from jax.experimental import pallas as pl
from jax.experimental.pallas import tpu as pltpu

F32 = jnp.float32
BF16 = jnp.bfloat16
EPS = 1e-6

SSD_HEADS = 16
SSD_HEAD_DIM = 64
SSD_GROUPS = 2
SSD_STATE = 128
SSD_CHUNK = 128
HY_ORDER = 2
HY_BANDS = 16
HY_SHORT_DECAY_PCT = 0.3
HY_LONG_DECAY_PCT = 1.5
HY_DECAY_TARGET = 1e-2
DA_HEADS = 8
DA_V_DIM = 128
DA_HEAD_DIM = 64
ROPE_AXIS_DIM = 32
ROPE_THETA = 10000.0
GRID_W = 64
N_BRANCH = 3

LANES = 128
SUBLANES = 8
FFT_N2 = 128
VMEM_LIMIT = 56 * 1024 * 1024


def _cparams(*sem):
    return pltpu.CompilerParams(dimension_semantics=sem, vmem_limit_bytes=VMEM_LIMIT)


def _dot(a, b):
    return jnp.dot(a, b, preferred_element_type=F32)


def _dot_nt(a, b):
    return lax.dot_general(a, b, (((1,), (1,)), ((), ())), preferred_element_type=F32)


def _split3(x):
    hi = x.astype(BF16)
    r1 = x - hi.astype(F32)
    mid = r1.astype(BF16)
    lo = (r1 - mid.astype(F32)).astype(BF16)
    return hi, mid, lo


def _silu(x):
    return x * jax.nn.sigmoid(x)


def _softplus(x):
    return jnp.maximum(x, 0.0) + jnp.log1p(jnp.exp(-jnp.abs(x)))


def _pick_tile(n, pref):
    t = min(pref, n)
    while n % t:
        t //= 2
    return t


def _matmul_kernel(a_ref, w_ref, b_ref, o_ref, *, silu_in):
    a = a_ref[...]
    if silu_in:
        a = _silu(a.astype(F32))
    r = _dot(a.astype(BF16), w_ref[...]) + b_ref[...]
    o_ref[...] = r.astype(o_ref.dtype)


def matmul(a, w, bias=None, out_dtype=F32, silu_in=False, tm=1024, tn=512):
    m, k = a.shape
    n = w.shape[1]
    tm = _pick_tile(m, tm)
    tn = _pick_tile(n, tn)
    if bias is None:
        bias = jnp.zeros((1, n), F32)
    return pl.pallas_call(
        functools.partial(_matmul_kernel, silu_in=silu_in),
        grid=(m // tm, n // tn),
        in_specs=[pl.BlockSpec((tm, k), lambda i, j: (i, 0)),
                  pl.BlockSpec((k, tn), lambda i, j: (0, j)),
                  pl.BlockSpec((1, tn), lambda i, j: (0, j))],
        out_specs=pl.BlockSpec((tm, tn), lambda i, j: (i, j)),
        out_shape=jax.ShapeDtypeStruct((m, n), out_dtype),
        compiler_params=_cparams("parallel", "parallel"),
        name="matmul",
    )(a, w, bias)


def _rms(x, g):
    return x * lax.rsqrt(jnp.mean(x * x, axis=-1, keepdims=True) + EPS) * g


def _norm_mod_kernel(x_ref, g_ref, mod_ref, o_ref, *, row0):
    x = x_ref[...]
    y = _rms(x, g_ref[...])
    shift = mod_ref[row0:row0 + 1, :]
    scale = mod_ref[row0 + 1:row0 + 2, :]
    o_ref[...] = (y * (1.0 + scale) + shift).astype(o_ref.dtype)


def norm_mod(x, g, mods, row0, tr=512):
    b, t, d = x.shape
    tr = _pick_tile(t, tr)
    per_batch = mods.shape[0] > 1
    return pl.pallas_call(
        functools.partial(_norm_mod_kernel, row0=row0),
        grid=(b, t // tr),
        in_specs=[pl.BlockSpec((None, tr, d), lambda bi, i: (bi, i, 0)),
                  pl.BlockSpec((1, d), lambda bi, i: (0, 0)),
                  pl.BlockSpec((None, 8, d), (lambda bi, i: (bi, 0, 0)) if per_batch else (lambda bi, i: (0, 0, 0)))],
        out_specs=pl.BlockSpec((None, tr, d), lambda bi, i: (bi, i, 0)),
        out_shape=jax.ShapeDtypeStruct((b, t, d), BF16),
        compiler_params=_cparams("parallel", "parallel"),
        name="norm_mod",
    )(x, g.reshape(1, d), mods)


def _final_norm_kernel(x_ref, g_ref, o_ref):
    o_ref[...] = _rms(x_ref[...], g_ref[...])


def final_norm(x, g, tr=512):
    b, t, d = x.shape
    tr = _pick_tile(t, tr)
    return pl.pallas_call(
        _final_norm_kernel,
        grid=(b, t // tr),
        in_specs=[pl.BlockSpec((None, tr, d), lambda bi, i: (bi, i, 0)),
                  pl.BlockSpec((1, d), lambda bi, i: (0, 0))],
        out_specs=pl.BlockSpec((None, tr, d), lambda bi, i: (bi, i, 0)),
        out_shape=jax.ShapeDtypeStruct((b, t, d), F32),
        compiler_params=_cparams("parallel", "parallel"),
        name="final_norm",
    )(x, g.reshape(1, d))


def _dwconv_kernel(x_ref, xp_ref, xn_ref, w_ref, b_ref, o_ref, *, act, nblk):
    i = pl.program_id(1)
    x = x_ref[...].astype(F32)
    tr = x.shape[0]
    halo = xp_ref.shape[0]
    prev_row = jnp.where(i == 0, 0.0, xp_ref[halo - 1:halo, :].astype(F32))
    next_row = jnp.where(i == nblk - 1, 0.0, xn_ref[0:1, :].astype(F32))
    rows = lax.broadcasted_iota(jnp.int32, x.shape, 0)
    x_m1 = jnp.where(rows == 0, prev_row, pltpu.roll(x, 1, 0))
    x_p1 = jnp.where(rows == tr - 1, next_row, pltpu.roll(x, tr - 1, 0))
    y = b_ref[...] + x_m1 * w_ref[0:1, :] + x * w_ref[1:2, :] + x_p1 * w_ref[2:3, :]
    if act:
        y = _silu(y)
    o_ref[...] = y


def dwconv3(x, w, bias, act, tr=512, tc=1536):
    b, t, c = x.shape
    tr = _pick_tile(t, tr)
    tc = min(tc, c)
    assert c % tc == 0
    nblk = t // tr
    halo = SUBLANES * (4 // x.dtype.itemsize)
    rh = tr // halo
    lasth = t // halo - 1
    return pl.pallas_call(
        functools.partial(_dwconv_kernel, act=act, nblk=nblk),
        grid=(b, nblk, c // tc),
        in_specs=[pl.BlockSpec((None, tr, tc), lambda bi, i, j: (bi, i, j)),
                  pl.BlockSpec((None, halo, tc), lambda bi, i, j: (bi, jnp.maximum(i * rh - 1, 0), j)),
                  pl.BlockSpec((None, halo, tc), lambda bi, i, j: (bi, jnp.minimum((i + 1) * rh, lasth), j)),
                  pl.BlockSpec((3, tc), lambda bi, i, j: (0, j)),
                  pl.BlockSpec((1, tc), lambda bi, i, j: (0, j))],
        out_specs=pl.BlockSpec((None, tr, tc), lambda bi, i, j: (bi, i, j)),
        out_shape=jax.ShapeDtypeStruct((b, t, c), F32),
        compiler_params=_cparams("parallel", "parallel", "parallel"),
        name="dwconv3",
    )(x, x, x, w, bias.reshape(1, c))


def _ssd_kernel(x_ref, b_ref, c_ref, dt_ref, bias_ref, aneg_ref, e_ref, h0_ref,
                y_ref, hout_ref, st_ref, *, direction, nchunks):
    ci = pl.program_id(1)

    @pl.when(ci == 0)
    def _():
        st_ref[...] = h0_ref[...]

    q = x_ref.shape[0]
    hd = SSD_HEAD_DIM
    gw = (SSD_HEADS // SSD_GROUPS) * hd
    x = x_ref[...]
    dtv = _softplus(dt_ref[...] + bias_ref[...])
    adt = dtv * aneg_ref[...]
    row = lax.broadcasted_iota(jnp.int32, (q, q), 0)
    col = lax.broadcasted_iota(jnp.int32, (q, q), 1)
    mask = (col >= row) if direction else (col <= row)
    tri = mask.astype(BF16)
    acum = sum(_dot(tri, p) for p in _split3(adt))
    acum_t = sum(_dot_nt(p, tri) for p in _split3(adt.T))
    total = jnp.sum(adt, axis=0, keepdims=True)
    eac = jnp.exp(acum)
    dte = jnp.exp(total - acum)
    cd = jnp.broadcast_to(jnp.exp(total), (SUBLANES, LANES))
    stack = jnp.concatenate([dtv, eac, dte, cd], axis=0)
    e01 = e_ref[...]
    hi = stack.astype(BF16)
    lo = (stack - hi.astype(F32)).astype(BF16)
    ex = _dot(hi, e01) + _dot(lo, e01)
    dt_e, eac_e, dte_e, cd_e = ex[:q], ex[q:2 * q], ex[2 * q:3 * q], ex[3 * q:3 * q + 1]
    xdt = x * dt_e
    lane = lax.broadcasted_iota(jnp.int32, (q, LANES), 1)
    first = lane < hd
    for g in range(SSD_GROUPS):
        gs = slice(g * gw, (g + 1) * gw)
        c_g = c_ref[:, g * SSD_STATE:(g + 1) * SSD_STATE].astype(BF16)
        b_g32 = b_ref[:, g * SSD_STATE:(g + 1) * SSD_STATE]
        s_g = st_ref[:, gs]
        y_off = _dot(c_g, s_g.astype(BF16)) * eac_e[:, gs]
        cb = _dot_nt(c_g, b_g32.astype(BF16))
        for jp in range(gw // LANES):
            h_a = g * (SSD_HEADS // SSD_GROUPS) + 2 * jp
            ls = slice(h_a * hd, h_a * hd + LANES)
            xpair = xdt[:, ls]
            acc = y_off[:, jp * LANES:(jp + 1) * LANES]
            for k in range(2):
                cix = direction * SSD_HEADS + h_a + k
                seg = acum[:, cix:cix + 1] - acum_t[cix:cix + 1, :]
                dec = jnp.where(mask, jnp.exp(jnp.where(mask, seg, 0.0)), 0.0)
                w = (cb * dec).astype(BF16)
                xk = jnp.where(first if k == 0 else jnp.logical_not(first), xpair, 0.0).astype(BF16)
                acc = acc + _dot(w, xk)
            y_ref[:, ls] = acc
        xdte = (xdt[:, gs] * dte_e[:, gs]).astype(BF16)
        st_ref[:, gs] = s_g * cd_e[:, gs] + _dot(b_g32.T.astype(BF16), xdte)

    @pl.when(ci == nchunks - 1)
    def _():
        hout_ref[...] = st_ref[...]


def ssd_scan(xbc, dt_raw, dt_bias_row, aneg_row, expand, h0, direction):
    b, t, _ = xbc.shape
    q = SSD_CHUNK
    nchunks = t // q
    inner = SSD_HEADS * SSD_HEAD_DIM
    bn = SSD_GROUPS * SSD_STATE
    cmap = (lambda c: nchunks - 1 - c) if direction else (lambda c: c)
    y, hout = pl.pallas_call(
        functools.partial(_ssd_kernel, direction=direction, nchunks=nchunks),
        grid=(b, nchunks),
        in_specs=[pl.BlockSpec((None, q, inner), lambda bi, c: (bi, cmap(c), 0)),
                  pl.BlockSpec((None, q, bn), lambda bi, c: (bi, cmap(c), inner // bn)),
                  pl.BlockSpec((None, q, bn), lambda bi, c: (bi, cmap(c), inner // bn + 1)),
                  pl.BlockSpec((None, q, LANES), lambda bi, c: (bi, cmap(c), 0)),
                  pl.BlockSpec((1, LANES), lambda bi, c: (0, 0)),
                  pl.BlockSpec((1, LANES), lambda bi, c: (0, 0)),
                  pl.BlockSpec((LANES, inner), lambda bi, c: (0, 0)),
                  pl.BlockSpec((None, SSD_STATE, inner), lambda bi, c: (bi, 0, 0))],
        out_specs=[pl.BlockSpec((None, q, inner), lambda bi, c: (bi, cmap(c), 0)),
                   pl.BlockSpec((None, SSD_STATE, inner), lambda bi, c: (bi, 0, 0))],
        out_shape=[jax.ShapeDtypeStruct((b, t, inner), F32),
                   jax.ShapeDtypeStruct((b, SSD_STATE, inner), F32)],
        scratch_shapes=[pltpu.VMEM((SSD_STATE, inner), F32)],
        compiler_params=_cparams("parallel", "arbitrary"),
        name="ssd_scan_bwd" if direction else "ssd_scan_fwd",
    )(xbc, xbc, xbc, dt_raw, dt_bias_row, aneg_row, expand, h0)
    return y, hout


def _ssd_out_kernel(yf_ref, yb_ref, x_ref, z_ref, d_ref, w_ref, o_ref):
    y = yf_ref[...] + yb_ref[...] + d_ref[...] * x_ref[...]
    y = y * _silu(z_ref[...].astype(F32))
    gw = y.shape[1] // SSD_GROUPS
    for g in range(SSD_GROUPS):
        gs = slice(g * gw, (g + 1) * gw)
        yg = y[:, gs]
        yg = yg * lax.rsqrt(jnp.mean(yg * yg, axis=-1, keepdims=True) + EPS)
        o_ref[:, gs] = (yg * w_ref[:, gs]).astype(o_ref.dtype)


def ssd_out(yf, yb, xbc, z, d_row, w_row, tr=256):
    b, t, inner = yf.shape
    tr = _pick_tile(t, tr)
    blk = pl.BlockSpec((None, tr, inner), lambda bi, i: (bi, i, 0))
    vec = pl.BlockSpec((1, inner), lambda bi, i: (0, 0))
    return pl.pallas_call(
        _ssd_out_kernel,
        grid=(b, t // tr),
        in_specs=[blk, blk, blk, blk, vec, vec],
        out_specs=blk,
        out_shape=jax.ShapeDtypeStruct((b, t, inner), BF16),
        compiler_params=_cparams("parallel", "parallel"),
        name="ssd_out",
    )(yf, yb, xbc, z, d_row, w_row)


def _taps_kernel(f_ref, tv_ref, w1_ref, b1_ref, w2_ref, b2_ref, w3_ref, fr_ref, dl_ref, o_ref):
    hp = lax.Precision.HIGHEST
    fr = fr_ref[...]
    h = jnp.sin(fr * (jnp.dot(f_ref[...], w1_ref[...], precision=hp, preferred_element_type=F32) + b1_ref[...]))
    h = jnp.sin(fr * (jnp.dot(h, w2_ref[...], precision=hp, preferred_element_type=F32) + b2_ref[...]))
    filt = jnp.dot(h, w3_ref[...], precision=hp, preferred_element_type=F32)
    c = dl_ref.shape[1]
    win = jnp.exp(-tv_ref[:, 0:1] * dl_ref[...]) * tv_ref[:, 1:2]
    o_ref[0] = filt[:, :c] * win
    o_ref[1] = filt[:, c:] * win


def hyena_taps(length, w1, b1, w2, b2, w3, freq):
    hid = w1.shape[1]
    c = w3.shape[1] // (2 * HY_ORDER)
    n = 2 * length
    t = jnp.linspace(0.0, 1.0, length, dtype=F32)[:, None]
    phase = 2.0 * math.pi * jnp.arange(length, dtype=F32)[:, None] / length
    bands = jnp.linspace(1e-4, HY_BANDS - 1, HY_BANDS, dtype=F32)[None, :]
    feats = jnp.concatenate([t, jnp.cos(phase * bands), -jnp.sin(phase * bands)], axis=-1)
    nf = feats.shape[1]
    rev = jnp.concatenate([jnp.zeros((1,), jnp.int32), jnp.arange(length - 1, 0, -1, dtype=jnp.int32)])
    feats2 = jnp.concatenate([feats, feats[rev]], axis=0)
    feats2 = jnp.pad(feats2, ((0, 0), (0, LANES - nf)))
    valid = jnp.ones((n,), F32).at[length].set(0.0)
    tv = jnp.stack([jnp.concatenate([t[:, 0], t[rev, 0]]), valid], axis=1)
    max_decay = math.log(HY_DECAY_TARGET) / HY_SHORT_DECAY_PCT
    min_decay = math.log(HY_DECAY_TARGET) / HY_LONG_DECAY_PCT
    deltas = jnp.abs(jnp.linspace(min_decay, max_decay, c, dtype=F32))[None, :]
    w1p = jnp.pad(w1, ((0, LANES - nf), (0, 0)))
    w3r = w3.reshape(hid, HY_ORDER, 2, c).transpose(2, 0, 1, 3).reshape(2, hid, HY_ORDER * c)
    tr = _pick_tile(length, 1024)
    nhalf = length // tr
    return pl.pallas_call(
        _taps_kernel,
        grid=(n // tr,),
        in_specs=[pl.BlockSpec((tr, LANES), lambda i: (i, 0)),
                  pl.BlockSpec((tr, 2), lambda i: (i, 0)),
                  pl.BlockSpec((LANES, hid), lambda i: (0, 0)),
                  pl.BlockSpec((1, hid), lambda i: (0, 0)),
                  pl.BlockSpec((hid, hid), lambda i: (0, 0)),
                  pl.BlockSpec((1, hid), lambda i: (0, 0)),
                  pl.BlockSpec((None, hid, HY_ORDER * c), lambda i: (i // nhalf, 0, 0)),
                  pl.BlockSpec((1, hid), lambda i: (0, 0)),
                  pl.BlockSpec((1, c), lambda i: (0, 0))],
        out_specs=pl.BlockSpec((HY_ORDER, tr, c), lambda i: (0, i, 0)),
        out_shape=jax.ShapeDtypeStruct((HY_ORDER, n, c), F32),
        compiler_params=_cparams("parallel"),
        name="hyena_taps",
    )(feats2, tv, w1p, b1.reshape(1, hid), w2, b2.reshape(1, hid), w3r, freq.reshape(1, hid), deltas)


def _cs(num, den):
    ang = (2.0 * math.pi / den) * num.astype(F32)
    return jnp.cos(ang), jnp.sin(ang)


def fft_tables(n1, n2):
    n = n1 * n2
    h = n1 // 2
    k1 = jnp.arange(n1, dtype=jnp.int32)
    c1, s1 = _cs((k1[:, None] * k1[None, :]) % n1, n1)
    g_u = jnp.block([[c1[:, :h], s1[:, :h]], [-s1[:, :h], c1[:, :h]]])
    g_t = jnp.concatenate([c1, -s1], axis=0)
    g_i = jnp.block([[c1[:h, :], -s1[:h, :]], [s1[:h, :], c1[:h, :]]])
    s = jnp.arange(n2, dtype=jnp.int32)
    num = (n1 * (s[:, None] * s[None, :]))[None] + (k1[:, None, None] * s[None, None, :])
    cb, sb = _cs(num % n, n)
    m_f = jnp.concatenate([jnp.concatenate([cb, sb], axis=2),
                           jnp.concatenate([-sb, cb], axis=2)], axis=1)
    cbt, sbt = jnp.swapaxes(cb, 1, 2) / n, jnp.swapaxes(sb, 1, 2) / n
    m_i = jnp.concatenate([jnp.concatenate([cbt, -sbt], axis=2),
                           jnp.concatenate([sbt, cbt], axis=2)], axis=1)
    return dict(g_u=g_u.astype(BF16), g_t=g_t.astype(BF16), g_i=g_i.astype(BF16),
                m_f=m_f.astype(BF16), m_i=m_i.astype(BF16))


FFT_KGROUP = SUBLANES


def _rows_get(ref, start, n, stride):
    return ref.reshape(math.prod(ref.shape[:-1]), ref.shape[-1])[pl.ds(start, n, stride=stride), :]


def _rows_set(ref, start, n, stride, val):
    ref.reshape(math.prod(ref.shape[:-1]), ref.shape[-1])[pl.ds(start, n, stride=stride), :] = val


def _fft_a_kernel(g_ref, xa_ref, xb_ref, o_ref, *, s_major_in):
    h = g_ref.shape[1] // 2
    n1 = o_ref.shape[2]
    ts = o_ref.shape[1]
    ga = g_ref[:, :h]
    gb = g_ref[:, h:]
    for s in range(ts):
        xa = xa_ref[s] if s_major_in else _rows_get(xa_ref, s, h, ts)
        xb = xb_ref[s] if s_major_in else _rows_get(xb_ref, s, h, ts)
        r = _dot(ga, xa.astype(BF16)) + _dot(gb, xb.astype(BF16))
        o_ref[0, s] = r[:n1]
        o_ref[1, s] = r[n1:]


def fft_stage_a(g, x4, sel_a, sel_b, col_off, c, s_major_in=False, tc=LANES):
    n1 = g.shape[0] // 2
    h = n1 // 2
    n2 = x4.shape[1] if s_major_in else x4.shape[2]
    ts = SUBLANES
    coff = col_off // tc
    if s_major_in:
        spec = lambda sel: pl.BlockSpec((None, ts, h, tc), lambda j, s: (sel[0], s, sel[1], j + coff))
    else:
        spec = lambda sel: pl.BlockSpec((None, h, ts, tc), lambda j, s: (sel[0], sel[1], s, j + coff))
    return pl.pallas_call(
        functools.partial(_fft_a_kernel, s_major_in=s_major_in),
        grid=(c // tc, n2 // ts),
        in_specs=[pl.BlockSpec((2 * n1, n1), lambda j, s: (0, 0)), spec(sel_a), spec(sel_b)],
        out_specs=pl.BlockSpec((2, ts, n1, tc), lambda j, s: (0, s, 0, j)),
        out_shape=jax.ShapeDtypeStruct((2, n2, n1, c), F32),
        compiler_params=_cparams("parallel", "parallel"),
        name="fft_stage_a",
    )(g, x4, x4)


def _fft_load_k(a_ref, j):
    _, n2, kb, _ = a_ref.shape
    return _rows_get(a_ref, j, 2 * n2, kb).astype(BF16)


def _fft_bh_kernel(m_ref, a_ref, o_ref):
    for j in range(a_ref.shape[2]):
        o_ref[j] = _dot(m_ref[j], _fft_load_k(a_ref, j)).reshape(o_ref.shape[1:])


def _fft_b_kernel(mf_ref, mi_ref, a_ref, h_ref, o_ref):
    n2 = a_ref.shape[1]
    for j in range(a_ref.shape[2]):
        x = _dot(mf_ref[j], _fft_load_k(a_ref, j))
        xr, xi = x[:n2], x[n2:]
        hr, hi = h_ref[j, 0], h_ref[j, 1]
        y = jnp.concatenate([xr * hr - xi * hi, xr * hi + xi * hr], axis=0).astype(BF16)
        r = _dot(mi_ref[j], y)
        o_ref[0, j] = r[:n2]
        o_ref[1, j] = r[n2:]


def fft_stage_b(m_f, m_i, a, hspec, tc=LANES):
    _, n2, n1, c = a.shape
    kb = FFT_KGROUP
    ablk = pl.BlockSpec((2, n2, kb, tc), lambda k, j: (0, 0, k, j))
    hblk = pl.BlockSpec((kb, 2, n2, tc), lambda k, j: (k, 0, 0, j))
    mat = pl.BlockSpec((kb, 2 * n2, 2 * n2), lambda k, j: (k, 0, 0))
    if hspec is None:
        kern, in_specs, args, name = _fft_bh_kernel, [mat, ablk], (m_f, a), "fft_stage_b_spectrum"
        out_spec, out_shape = hblk, (n1, 2, n2, c)
    else:
        kern, in_specs, args, name = _fft_b_kernel, [mat, mat, ablk, hblk], (m_f, m_i, a, hspec), "fft_stage_b"
        out_spec, out_shape = pl.BlockSpec((2, kb, n2, tc), lambda k, j: (0, k, 0, j)), (2, n1, n2, c)
    return pl.pallas_call(
        kern,
        grid=(n1 // kb, c // tc),
        in_specs=in_specs,
        out_specs=out_spec,
        out_shape=jax.ShapeDtypeStruct(out_shape, F32),
        compiler_params=_cparams("parallel", "parallel"),
        name=name,
    )(*args)


def _fft_ai_kernel(g_ref, b_ref, u_ref, x_ref, skip_ref, o_ref, *, u_s_major, out_s_major):
    _, n1, ts, _ = b_ref.shape
    h = n1 // 2
    ga = g_ref[:, :n1]
    gb = g_ref[:, n1:]
    skip = skip_ref[...]
    for s in range(ts):
        y = (_dot(ga, _rows_get(b_ref, s, n1, ts).astype(BF16))
             + _dot(gb, _rows_get(b_ref, n1 * ts + s, n1, ts).astype(BF16)))
        for bi in range(2):
            start = bi * h * ts + s
            u = u_ref[bi, s] if u_s_major else _rows_get(u_ref, start, h, ts)
            val = _rows_get(x_ref, start, h, ts) * (y[bi * h:(bi + 1) * h] + u * skip)
            if out_s_major:
                o_ref[bi, s] = val
            else:
                _rows_set(o_ref, start, h, ts, val)


def fft_stage_a_inv(g_i, bm, u4, u_off, u_s_major, x4, x_off, skip_row, out_s_major, tc=LANES):
    _, n1, n2, c = bm.shape
    h = n1 // 2
    ts = SUBLANES
    uo, xo = u_off // tc, x_off // tc
    nat = lambda off: pl.BlockSpec((2, h, ts, tc), lambda j, s: (0, 0, s, j + off))
    smj = lambda off: pl.BlockSpec((2, ts, h, tc), lambda j, s: (0, s, 0, j + off))
    return pl.pallas_call(
        functools.partial(_fft_ai_kernel, u_s_major=u_s_major, out_s_major=out_s_major),
        grid=(c // tc, n2 // ts),
        in_specs=[pl.BlockSpec((n1, 2 * n1), lambda j, s: (0, 0)),
                  pl.BlockSpec((2, n1, ts, tc), lambda j, s: (0, 0, s, j)),
                  smj(uo) if u_s_major else nat(uo),
                  nat(xo),
                  pl.BlockSpec((1, tc), lambda j, s: (0, j))],
        out_specs=smj(0) if out_s_major else nat(0),
        out_shape=jax.ShapeDtypeStruct((2, n2, h, c) if out_s_major else (2, h, n2, c), F32),
        compiler_params=_cparams("parallel", "parallel"),
        name="fft_stage_a_inv",
    )(g_i, bm, u4, x4, skip_row)


def hyena_long(u, taps, skip, tabs):
    b, length, c3 = u.shape
    assert b == 2, "the two batch rows are packed as one complex sequence"
    c = c3 // 3
    n2 = FFT_N2
    n1 = 2 * length // n2
    h = n1 // 2
    u4 = u.reshape(b, h, n2, c3)
    taps4 = taps.reshape(HY_ORDER * 2, h, n2, c)
    zin, zoff, z_s_major = u4, 0, False
    out = None
    for conv in range(HY_ORDER):
        last = conv == HY_ORDER - 1
        hspec = fft_stage_b(tabs["m_f"], None,
                            fft_stage_a(tabs["g_t"], taps4, (2 * conv, 0), (2 * conv + 1, 0), 0, c), None)
        a = fft_stage_a(tabs["g_u"], zin, (0, 0), (1, 0), zoff, c, s_major_in=z_s_major)
        bm = fft_stage_b(tabs["m_f"], tabs["m_i"], a, hspec)
        out = fft_stage_a_inv(tabs["g_i"], bm, zin, zoff, z_s_major, u4, (conv + 1) * c,
                              skip[conv].reshape(1, c), not last)
        zin, zoff, z_s_major = out, 0, True
    return out.reshape(b, length, c)


def _hyena_ctx_kernel(u_ref, x1_ref, x2_ref, taps_ref, fc_ref, fs_ref, skip_ref, o_ref):
    length = u_ref.shape[1]
    n = 2 * length
    fc = fc_ref[...]
    fs = fs_ref[...]
    fcl, fsl = fc[:, :length], fs[:, :length]
    fct, fst = fc[:length, :], fs[:length, :]
    zr = u_ref[0]
    zi = u_ref[1]
    gates = (x1_ref, x2_ref)
    for conv in range(HY_ORDER):
        tp = taps_ref[conv].astype(BF16)
        hr, hi = _dot(fc, tp), -_dot(fs, tp)
        zrb, zib = zr.astype(BF16), zi.astype(BF16)
        xr = _dot(fcl, zrb) + _dot(fsl, zib)
        xi = _dot(fcl, zib) - _dot(fsl, zrb)
        yr = (xr * hr - xi * hi).astype(BF16)
        yi = (xr * hi + xi * hr).astype(BF16)
        cr = (_dot(fct, yr) - _dot(fst, yi)) * (1.0 / n)
        ci = (_dot(fct, yi) + _dot(fst, yr)) * (1.0 / n)
        sk = skip_ref[conv:conv + 1, :]
        zr = gates[conv][0] * (cr + zr * sk)
        zi = gates[conv][1] * (ci + zi * sk)
    o_ref[0] = zr.astype(o_ref.dtype)
    o_ref[1] = zi.astype(o_ref.dtype)


def hyena_ctx(u, taps, skip, tc=256):
    b, length, c3 = u.shape
    assert b == 2
    c = c3 // 3
    n = 2 * length
    k = jnp.arange(n, dtype=jnp.int32)
    fc, fs = _cs((k[:, None] * k[None, :]) % n, n)
    nb = c // tc
    ub = lambda off: pl.BlockSpec((2, length, tc), lambda j: (0, 0, j + off * nb))
    mat = pl.BlockSpec((n, n), lambda j: (0, 0))
    return pl.pallas_call(
        _hyena_ctx_kernel,
        grid=(nb,),
        in_specs=[ub(0), ub(1), ub(2),
                  pl.BlockSpec((HY_ORDER, n, tc), lambda j: (0, 0, j)),
                  mat, mat,
                  pl.BlockSpec((HY_ORDER, tc), lambda j: (0, j))],
        out_specs=pl.BlockSpec((2, length, tc), lambda j: (0, 0, j)),
        out_shape=jax.ShapeDtypeStruct((b, length, c), BF16),
        compiler_params=_cparams("parallel"),
        name="hyena_ctx",
    )(u, u, u, taps, fc.astype(BF16), fs.astype(BF16), skip)


def rope_tables(length):
    rows = length // GRID_W
    row = jnp.repeat(jnp.arange(rows, dtype=F32), GRID_W)
    col = (jnp.arange(length) % GRID_W).astype(F32)
    inv = ROPE_THETA ** (-jnp.arange(0, ROPE_AXIS_DIM, 2, dtype=F32) / ROPE_AXIS_DIM)
    ang = jnp.stack([row[:, None] * inv, col[:, None] * inv], axis=1)
    cos, sin = jnp.cos(ang), jnp.sin(ang)
    zero = jnp.zeros_like(sin)
    per_map = lambda lo, hi: jnp.concatenate([lo, hi], axis=-1).reshape(length, 2 * ROPE_AXIS_DIM)
    reps = LANES // (2 * ROPE_AXIS_DIM)
    c_t = jnp.tile(per_map(cos, cos), (1, reps))
    sm_t = jnp.tile(per_map(-sin, zero), (1, reps))
    sp_t = jnp.tile(per_map(zero, sin), (1, reps))
    return c_t, sm_t, sp_t


def _rope_kernel(x_ref, c_ref, sm_ref, sp_ref, o_ref, *n_refs, scale):
    half = ROPE_AXIS_DIM // 2
    c, sm, sp = c_ref[...], sm_ref[...], sp_ref[...]
    for g in range(x_ref.shape[1] // LANES):
        ls = slice(g * LANES, (g + 1) * LANES)
        x = x_ref[:, ls].astype(F32)
        y = (x * c + pltpu.roll(x, LANES - half, 1) * sm + pltpu.roll(x, half, 1) * sp) * scale
        o_ref[:, ls] = y.astype(o_ref.dtype)
        if n_refs:
            n2 = jnp.max(jnp.sum(y * y, axis=1, keepdims=True), axis=0, keepdims=True)
            n_refs[0][g:g + 1, :] = jnp.broadcast_to(n2, (1, LANES))


def rope(x, tables, scale, with_norms=False, tr=512):
    b, t, d = x.shape
    tr = _pick_tile(t, tr)
    nh = d // LANES
    tab = pl.BlockSpec((tr, LANES), lambda bi, i: (i, 0))
    out_specs = [pl.BlockSpec((None, tr, d), lambda bi, i: (bi, i, 0))]
    out_shape = [jax.ShapeDtypeStruct((b, t, d), BF16)]
    if with_norms:
        out_specs.append(pl.BlockSpec((None, None, nh, LANES), lambda bi, i: (bi, i, 0, 0)))
        out_shape.append(jax.ShapeDtypeStruct((b, t // tr, nh, LANES), F32))
    outs = pl.pallas_call(
        functools.partial(_rope_kernel, scale=scale),
        grid=(b, t // tr),
        in_specs=[pl.BlockSpec((None, tr, d), lambda bi, i: (bi, i, 0)), tab, tab, tab],
        out_specs=out_specs,
        out_shape=out_shape,
        compiler_params=_cparams("parallel", "parallel"),
        name="rope",
    )(x, *tables)
    if with_norms:
        return outs[0], jnp.swapaxes(outs[1], 1, 2)
    return outs[0]


def _scale_cast_kernel(x_ref, o_ref, *, scale):
    o_ref[...] = (x_ref[...].astype(F32) * scale).astype(o_ref.dtype)


def scale_cast(x, scale, tr=256):
    b, t, d = x.shape
    tr = _pick_tile(t, tr)
    blk = pl.BlockSpec((None, tr, d), lambda bi, i: (bi, i, 0))
    return pl.pallas_call(
        functools.partial(_scale_cast_kernel, scale=scale),
        grid=(b, t // tr), in_specs=[blk], out_specs=blk,
        out_shape=jax.ShapeDtypeStruct((b, t, d), BF16),
        compiler_params=_cparams("parallel", "parallel"),
        name="scale_cast",
    )(x)


ATTN_GUARD_LOG2 = 80.0
ATTN_NORM_SLACK = 1.02


def _attn_kernel(*refs, lam_init, nk, with_ctx, nsub):
    if with_ctx:
        (q_ref, k_ref, v_ref, kc_ref, vc_ref, kn_ref, lam_ref, w_ref, o_ref,
         q2_ref, m_ref, acc_ref, p_scr, flag_ref) = refs
    else:
        q_ref, k_ref, v_ref, lam_ref, w_ref, o_ref, q2_ref, m_ref, acc_ref, p_scr, flag_ref = refs
    ki = pl.program_id(3)
    tq = q_ref.shape[0]

    def with_ones(v):
        lane = lax.broadcasted_iota(jnp.int32, v.shape, 1)
        return jnp.concatenate([v, (lane == 0).astype(BF16)], axis=1)

    def exact_update(k, v, p_buf):
        s = _dot_nt(q2_ref[...], k)
        m_prev = m_ref[...]
        m_new = jnp.maximum(m_prev, jnp.max(s, axis=1, keepdims=True))
        alpha = jnp.exp2(m_prev - m_new)
        p_buf[...] = jnp.exp2(s - m_new).astype(BF16)
        acc_ref[...] = alpha * acc_ref[...] + _dot(p_buf[...], with_ones(v))
        m_ref[...] = m_new

    def fast_update(k, v, p_buf):
        s = _dot_nt(q2_ref[...], k)
        p_buf[...] = jnp.exp2(s - m_ref[...]).astype(BF16)
        acc_ref[...] += _dot(p_buf[...], with_ones(v))

    @pl.when(ki == 0)
    def _():
        q = q_ref[...]
        lane = lax.broadcasted_iota(jnp.int32, q.shape, 1)
        zero = jnp.zeros_like(q)
        q2_ref[:tq, :] = jnp.where(lane < DA_HEAD_DIM, q, zero)
        q2_ref[tq:, :] = jnp.where(lane >= DA_HEAD_DIM, q, zero)
        m_ref[...] = jnp.full(m_ref.shape, -jnp.inf, F32)
        acc_ref[...] = jnp.zeros(acc_ref.shape, F32)
        flag_ref[0] = 0
        if with_ctx:
            tc = kc_ref.shape[0]
            exact_update(kc_ref[...], vc_ref[...], p_scr.at[0, :, pl.ds(0, tc)])
            q2 = q2_ref[...].astype(F32)
            qn2 = jnp.sum(q2 * q2, axis=1, keepdims=True)
            kn2 = jnp.max(kn_ref[...])
            gap = jnp.max(jnp.sqrt(qn2 * kn2) * ATTN_NORM_SLACK - m_ref[...])
            flag_ref[0] = (gap <= ATTN_GUARD_LOG2).astype(jnp.int32)

    tks = k_ref.shape[0] // nsub
    fast = flag_ref[0] == 1

    @pl.when(fast)
    def _():
        for sub in range(nsub):
            ks = slice(sub * tks, (sub + 1) * tks)
            fast_update(k_ref[ks, :], v_ref[ks, :], p_scr.at[sub])

    @pl.when(jnp.logical_not(fast))
    def _():
        for sub in range(nsub):
            ks = slice(sub * tks, (sub + 1) * tks)
            exact_update(k_ref[ks, :], v_ref[ks, :], p_scr.at[sub])

    @pl.when(ki == nk - 1)
    def _():
        lv = lam_ref[...]
        e1 = jnp.exp(jnp.sum(lv[0:1, :] * lv[1:2, :], axis=1, keepdims=True))
        e2 = jnp.exp(jnp.sum(lv[2:3, :] * lv[3:4, :], axis=1, keepdims=True))
        lam = e1 - e2 + lam_init
        acc = acc_ref[...]
        o = acc[:, :DA_V_DIM] / acc[:, DA_V_DIM:DA_V_DIM + 1]
        o = o[:tq] - lam * o[tq:]
        o_ref[...] = (_rms(o, w_ref[...]) * (1.0 - lam_init)).astype(o_ref.dtype)


def diff_attention(q, k, v, k_ctx, v_ctx, knorm, lam_vec, subln_w, lam_init, tq=1024, tk=2048):
    b, t, d = q.shape
    nh = d // DA_V_DIM
    tq = _pick_tile(t, tq)
    tk = _pick_tile(k.shape[1], tk)
    nsub = 1
    tks = tk
    nk = k.shape[1] // tk
    with_ctx = k_ctx is not None
    assert not with_ctx or k_ctx.shape[1] <= tks
    qspec = pl.BlockSpec((None, tq, DA_V_DIM), lambda bi, h, i, j: (bi, i, h))
    kspec = pl.BlockSpec((None, tk, DA_V_DIM), lambda bi, h, i, j: (bi, j, h))
    in_specs = [qspec, kspec, kspec]
    args = [q, k, v]
    if with_ctx:
        cspec = pl.BlockSpec((None, k_ctx.shape[1], DA_V_DIM), lambda bi, h, i, j: (bi, 0, h))
        nspec = pl.BlockSpec((None, None) + knorm.shape[2:], lambda bi, h, i, j: (bi, h, 0, 0))
        in_specs += [cspec, cspec, nspec]
        args += [k_ctx, v_ctx, knorm]
    in_specs += [pl.BlockSpec((4, DA_HEAD_DIM), lambda bi, h, i, j: (0, 0)),
                 pl.BlockSpec((1, DA_V_DIM), lambda bi, h, i, j: (0, 0))]
    args += [lam_vec, subln_w.reshape(1, DA_V_DIM)]
    return pl.pallas_call(
        functools.partial(_attn_kernel, lam_init=lam_init, nk=nk, with_ctx=with_ctx, nsub=nsub),
        grid=(b, nh, t // tq, nk),
        in_specs=in_specs,
        out_specs=qspec,
        out_shape=jax.ShapeDtypeStruct((b, t, d), BF16),
        scratch_shapes=[pltpu.VMEM((2 * tq, DA_V_DIM), BF16),
                        pltpu.VMEM((2 * tq, 1), F32),
                        pltpu.VMEM((2 * tq, 2 * DA_V_DIM), F32),
                        pltpu.VMEM((nsub, 2 * tq, tks), BF16),
                        pltpu.SMEM((1,), jnp.int32)],
        compiler_params=_cparams("parallel", "parallel", "parallel", "arbitrary"),
        name="diff_attention",
    )(*args)


def _merge_kernel(ya_ref, yb_ref, yc_ref, g_ref, wb_ref, wo_ref, x_ref, mod_ref, o_ref):
    d = x_ref.shape[1]
    ys = (ya_ref, yb_ref, yc_ref)
    s = None
    for k in range(N_BRANCH):
        t = jax.nn.sigmoid(g_ref[:, k * d:(k + 1) * d].astype(F32)) * _dot(ys[k][...].astype(BF16), wb_ref[k])
        s = t if s is None else s + t
    r = _dot(s.astype(BF16), wo_ref[...])
    o_ref[...] = x_ref[...] + mod_ref[2:3, :] * r


def merge(ya, yb, yc, gates, wb, wo, x, mods, tr=256):
    b, t, d = x.shape
    tr = _pick_tile(t, tr)
    per_batch = mods.shape[0] > 1
    blk = pl.BlockSpec((None, tr, d), lambda bi, i: (bi, i, 0))
    return pl.pallas_call(
        _merge_kernel,
        grid=(b, t // tr),
        in_specs=[blk, blk, blk,
                  pl.BlockSpec((None, tr, N_BRANCH * d), lambda bi, i: (bi, i, 0)),
                  pl.BlockSpec((N_BRANCH, d, d), lambda bi, i: (0, 0, 0)),
                  pl.BlockSpec((d, d), lambda bi, i: (0, 0)),
                  blk,
                  pl.BlockSpec((None, 8, d), (lambda bi, i: (bi, 0, 0)) if per_batch else (lambda bi, i: (0, 0, 0)))],
        out_specs=blk,
        out_shape=jax.ShapeDtypeStruct((b, t, d), F32),
        compiler_params=_cparams("parallel", "parallel"),
        name="merge",
    )(ya, yb, yc, gates, wb, wo, x, mods)


def _ffn_kernel(x_ref, g_ref, mod_ref, w13_ref, w2_ref, o_ref, *, nchunk):
    x = x_ref[...]
    h = (_rms(x, g_ref[...]) * (1.0 + mod_ref[4:5, :]) + mod_ref[3:4, :]).astype(BF16)
    f = w2_ref.shape[0]
    fc = f // nchunk
    acc = None
    for c in range(nchunk):
        a1 = _dot(h, w13_ref[:, c * fc:(c + 1) * fc])
        a3 = _dot(h, w13_ref[:, f + c * fc:f + (c + 1) * fc])
        t = _dot((_silu(a1) * a3).astype(BF16), w2_ref[c * fc:(c + 1) * fc, :])
        acc = t if acc is None else acc + t
    o_ref[...] = x + mod_ref[5:6, :] * acc


def ffn(x, g, mods, w13, w2, tr=256):
    b, t, d = x.shape
    f = w2.shape[0]
    tr = _pick_tile(t, tr)
    nchunk = 2 if (f // 2) % LANES == 0 else 1
    per_batch = mods.shape[0] > 1
    blk = pl.BlockSpec((None, tr, d), lambda bi, i: (bi, i, 0))
    return pl.pallas_call(
        functools.partial(_ffn_kernel, nchunk=nchunk),
        grid=(b, t // tr),
        in_specs=[blk,
                  pl.BlockSpec((1, d), lambda bi, i: (0, 0)),
                  pl.BlockSpec((None, 8, d), (lambda bi, i: (bi, 0, 0)) if per_batch else (lambda bi, i: (0, 0, 0))),
                  pl.BlockSpec((d, 2 * f), lambda bi, i: (0, 0)),
                  pl.BlockSpec((f, d), lambda bi, i: (0, 0))],
        out_specs=blk,
        out_shape=jax.ShapeDtypeStruct((b, t, d), F32),
        compiler_params=_cparams("parallel", "parallel"),
        name="ffn",
    )(x, g.reshape(1, d), mods, w13, w2)


def kernel(x, c, ctx, c_ctx, ada_w, ada_b, norm1_g, norm2_g, w_in, ssd_conv_w, ssd_conv_b,
           ssd_a_log, ssd_dt_bias, ssd_d, ssd_norm_w, hy_conv_w, hy_conv_b, hy_w1, hy_b1, hy_w2,
           hy_b2, hy_w3, hy_freq, hy_bias, da_lambda, da_subln_w, w_branch, w_out, ffn_w13,
           ffn_w2, final_g):
    bsz, length, d = x.shape
    ctx_len = ctx.shape[1]
    depth = w_in.shape[0]
    inner = SSD_HEADS * SSD_HEAD_DIM
    xbc_w = inner + 2 * SSD_GROUPS * SSD_STATE
    dt_w = 2 * SSD_HEADS
    hy_w = (HY_ORDER + 1) * d
    widths = (inner, xbc_w, dt_w, hy_w, d, d, d, N_BRANCH * d)
    offs = [0]
    for wd in widths:
        offs.append(offs[-1] + wd)

    cond = jnp.concatenate([c, c_ctx[None, :], jnp.zeros((8 - bsz - 1, d), F32)], axis=0)

    rope_tabs = rope_tables(length)
    tabs = fft_tables(2 * length // FFT_N2, FFT_N2)
    expand = (jnp.arange(LANES)[:, None] % SSD_HEADS
              == jnp.arange(inner)[None, :] // SSD_HEAD_DIM)
    expand_dir = [(expand & ((jnp.arange(LANES)[:, None] // SSD_HEADS) == dr)).astype(BF16) for dr in range(2)]
    qscale = DA_HEAD_DIM ** -0.5 * math.log2(math.e)

    x_l, x_c = x, ctx
    for layer in range(depth):
        last = layer == depth - 1
        lam_init = 0.8 - 0.6 * math.exp(-0.3 * layer)
        mod = matmul(cond, ada_w[layer].astype(BF16), ada_b[layer].reshape(1, -1), silu_in=True)
        mod = jnp.pad(mod.reshape(8, 6, d), ((0, 0), (0, 2), (0, 0)))
        mods_l, mods_c = mod[:bsz], mod[bsz:bsz + 1]

        w_l = w_in[layer].astype(BF16)
        w_parts = [w_l[:, offs[i]:offs[i + 1]] for i in range(len(widths))]
        w_parts[2] = jnp.pad(w_parts[2], ((0, 0), (0, LANES - dt_w)))
        a_neg = -jnp.exp(ssd_a_log[layer].astype(F32)).reshape(1, dt_w)
        aneg_row = jnp.pad(a_neg, ((0, 0), (0, LANES - dt_w)))
        bias_row = jnp.pad(ssd_dt_bias[layer].astype(F32).reshape(1, dt_w), ((0, 0), (0, LANES - dt_w)))
        d_row = jnp.repeat(ssd_d[layer].astype(F32), SSD_HEAD_DIM).reshape(1, inner)
        nw_row = ssd_norm_w[layer].astype(F32).reshape(1, inner)
        wb = w_branch[layer].astype(BF16)
        wo = w_out[layer].astype(BF16)
        w13 = ffn_w13[layer].astype(BF16)
        w2 = ffn_w2[layer].astype(BF16)

        def project(stream, mods):
            bb, tt, _ = stream.shape
            h = norm_mod(stream, norm1_g[layer], mods, 0).reshape(bb * tt, d)
            outs = [matmul(h, wp, out_dtype=(F32 if i == 2 else BF16)).reshape(bb, tt, -1)
                    for i, wp in enumerate(w_parts)]
            return outs

        z_c, xbc_c, dt_c, hy_c, q_c, k_c, v_c, g_c = project(x_c, mods_c)
        z_l, xbc_l, dt_l, hy_l, q_l, k_l, v_l, g_l = project(x_l, mods_l)

        xa_c = dwconv3(xbc_c, ssd_conv_w[layer], ssd_conv_b[layer], True)
        xa_l = dwconv3(xbc_l, ssd_conv_w[layer], ssd_conv_b[layer], True)
        h_zero = jnp.zeros((bsz, SSD_STATE, inner), F32)
        ys_c, ys_l = [], []
        for dr in range(2):
            y_c, st = ssd_scan(xa_c, dt_c, bias_row, aneg_row, expand_dir[dr], h_zero, dr)
            y_l, _ = ssd_scan(xa_l, dt_l, bias_row, aneg_row, expand_dir[dr], st, dr)
            ys_c.append(y_c)
            ys_l.append(y_l)
        ya_l = ssd_out(ys_l[0], ys_l[1], xa_l, z_l, d_row, nw_row)

        filt = (hy_w1[layer], hy_b1[layer], hy_w2[layer], hy_b2[layer], hy_w3[layer], hy_freq[layer])
        u_l = dwconv3(hy_l, hy_conv_w[layer], hy_conv_b[layer], False)
        yb_l = hyena_long(u_l, hyena_taps(length, *filt), hy_bias[layer], tabs)

        qr_l = rope(q_l, rope_tabs, qscale)
        kr_l, kn_l = rope(k_l, rope_tabs, 1.0, with_norms=True)
        kb_c = scale_cast(k_c, 1.0)
        yc_l = diff_attention(qr_l, kr_l, v_l, kb_c, v_c, kn_l, da_lambda[layer], da_subln_w[layer], lam_init)

        x_l_new = merge(ya_l, yb_l, yc_l, g_l, wb, wo, x_l, mods_l)
        x_l = ffn(x_l_new, norm2_g[layer], mods_l, w13, w2)

        if not last:
            ya_c = ssd_out(ys_c[0], ys_c[1], xa_c, z_c, d_row, nw_row)
            u_c = dwconv3(hy_c, hy_conv_w[layer], hy_conv_b[layer], False)
            yb_c = hyena_ctx(u_c, hyena_taps(ctx_len, *filt), hy_bias[layer])
            qb_c = scale_cast(q_c, qscale)
            yc_c = diff_attention(qb_c, kb_c, v_c, None, None, None, da_lambda[layer], da_subln_w[layer], lam_init)
            x_c_new = merge(ya_c, yb_c, yc_c, g_c, wb, wo, x_c, mods_c)
            x_c = ffn(x_c_new, norm2_g[layer], mods_c, w13, w2)
    return final_norm(x_l, final_g)
```

```python
import functools
import math

import jax
import jax.numpy as jnp
from jax import lax
from jax.experimental import pallas as pl
from jax.experimental.pallas import tpu as pltpu

F32 = jnp.float32
BF16 = jnp.bfloat16
EPS = 1e-6

SSD_HEADS = 16
SSD_HEAD_DIM = 64
SSD_GROUPS = 2
SSD_STATE = 128
SSD_CHUNK = 128
HY_ORDER = 2
HY_BANDS = 16
HY_SHORT_DECAY_PCT = 0.3
HY_LONG_DECAY_PCT = 1.5
HY_DECAY_TARGET = 1e-2
DA_HEADS = 8
DA_V_DIM = 128
DA_HEAD_DIM = 64
ROPE_AXIS_DIM = 32
ROPE_THETA = 10000.0
GRID_W = 64
N_BRANCH = 3

LANES = 128
SUBLANES = 8
FFT_N2 = 128
VMEM_LIMIT = 56 * 1024 * 1024


def _cparams(*sem):
    return pltpu.CompilerParams(dimension_semantics=sem, vmem_limit_bytes=VMEM_LIMIT)


def _dot(a, b):
    return jnp.dot(a, b, preferred_element_type=F32)


def _dot_nt(a, b):
    return lax.dot_general(a, b, (((1,), (1,)), ((), ())), preferred_element_type=F32)


def _split3(x):
    hi = x.astype(BF16)
    r1 = x - hi.astype(F32)
    mid = r1.astype(BF16)
    lo = (r1 - mid.astype(F32)).astype(BF16)
    return hi, mid, lo


def _silu(x):
    return x * jax.nn.sigmoid(x)


def _softplus(x):
    return jnp.maximum(x, 0.0) + jnp.log1p(jnp.exp(-jnp.abs(x)))


def _pick_tile(n, pref):
    t = min(pref, n)
    while n % t:
        t //= 2
    return t


def _matmul_kernel(a_ref, w_ref, b_ref, o_ref, *, silu_in):
    a = a_ref[...]
    if silu_in:
        a = _silu(a.astype(F32))
    r = _dot(a.astype(BF16), w_ref[...]) + b_ref[...]
    o_ref[...] = r.astype(o_ref.dtype)


def matmul(a, w, bias=None, out_dtype=F32, silu_in=False, tm=2048, tn=1024):
    m, k = a.shape
    n = w.shape[1]
    tm = _pick_tile(m, tm)
    tn = _pick_tile(n, tn)
    if bias is None:
        bias = jnp.zeros((1, n), F32)
    return pl.pallas_call(
        functools.partial(_matmul_kernel, silu_in=silu_in),
        grid=(m // tm, n // tn),
        in_specs=[pl.BlockSpec((tm, k), lambda i, j: (i, 0)),
                  pl.BlockSpec((k, tn), lambda i, j: (0, j)),
                  pl.BlockSpec((1, tn), lambda i, j: (0, j))],
        out_specs=pl.BlockSpec((tm, tn), lambda i, j: (i, j)),
        out_shape=jax.ShapeDtypeStruct((m, n), out_dtype),
        compiler_params=_cparams("parallel", "parallel"),
        name="matmul",
    )(a, w, bias)


def _rms(x, g):
    return x * lax.rsqrt(jnp.mean(x * x, axis=-1, keepdims=True) + EPS) * g


def _norm_mod_kernel(x_ref, g_ref, mod_ref, o_ref, *, row0):
    x = x_ref[...]
    y = _rms(x, g_ref[...])
    shift = mod_ref[row0:row0 + 1, :]
    scale = mod_ref[row0 + 1:row0 + 2, :]
    o_ref[...] = (y * (1.0 + scale) + shift).astype(o_ref.dtype)


def norm_mod(x, g, mods, row0, tr=512):
    b, t, d = x.shape
    tr = _pick_tile(t, tr)
    per_batch = mods.shape[0] > 1
    return pl.pallas_call(
        functools.partial(_norm_mod_kernel, row0=row0),
        grid=(b, t // tr),
        in_specs=[pl.BlockSpec((None, tr, d), lambda bi, i: (bi, i, 0)),
                  pl.BlockSpec((1, d), lambda bi, i: (0, 0)),
                  pl.BlockSpec((None, 8, d), (lambda bi, i: (bi, 0, 0)) if per_batch else (lambda bi, i: (0, 0, 0)))],
        out_specs=pl.BlockSpec((None, tr, d), lambda bi, i: (bi, i, 0)),
        out_shape=jax.ShapeDtypeStruct((b, t, d), BF16),
        compiler_params=_cparams("parallel", "parallel"),
        name="norm_mod",
    )(x, g.reshape(1, d), mods)


def _final_norm_kernel(x_ref, g_ref, o_ref):
    o_ref[...] = _rms(x_ref[...], g_ref[...])


def final_norm(x, g, tr=512):
    b, t, d = x.shape
    tr = _pick_tile(t, tr)
    return pl.pallas_call(
        _final_norm_kernel,
        grid=(b, t // tr),
        in_specs=[pl.BlockSpec((None, tr, d), lambda bi, i: (bi, i, 0)),
                  pl.BlockSpec((1, d), lambda bi, i: (0, 0))],
        out_specs=pl.BlockSpec((None, tr, d), lambda bi, i: (bi, i, 0)),
        out_shape=jax.ShapeDtypeStruct((b, t, d), F32),
        compiler_params=_cparams("parallel", "parallel"),
        name="final_norm",
    )(x, g.reshape(1, d))


def _dwconv_kernel(x_ref, xp_ref, xn_ref, w_ref, b_ref, o_ref, *, act, nblk):
    i = pl.program_id(1)
    x = x_ref[...].astype(F32)
    tr = x.shape[0]
    halo = xp_ref.shape[0]
    prev_row = jnp.where(i == 0, 0.0, xp_ref[halo - 1:halo, :].astype(F32))
    next_row = jnp.where(i == nblk - 1, 0.0, xn_ref[0:1, :].astype(F32))
    rows = lax.broadcasted_iota(jnp.int32, x.shape, 0)
    x_m1 = jnp.where(rows == 0, prev_row, pltpu.roll(x, 1, 0))
    x_p1 = jnp.where(rows == tr - 1, next_row, pltpu.roll(x, tr - 1, 0))
    y = b_ref[...] + x_m1 * w_ref[0:1, :] + x * w_ref[1:2, :] + x_p1 * w_ref[2:3, :]
    if act:
        y = _silu(y)
    o_ref[...] = y


def dwconv3(x, w, bias, act, tr=512, tc=1536):
    b, t, c = x.shape
    tr = _pick_tile(t, tr)
    tc = min(tc, c)
    assert c % tc == 0
    nblk = t // tr
    halo = SUBLANES * (4 // x.dtype.itemsize)
    rh = tr // halo
    lasth = t // halo - 1
    return pl.pallas_call(
        functools.partial(_dwconv_kernel, act=act, nblk=nblk),
        grid=(b, nblk, c // tc),
        in_specs=[pl.BlockSpec((None, tr, tc), lambda bi, i, j: (bi, i, j)),
                  pl.BlockSpec((None, halo, tc), lambda bi, i, j: (bi, jnp.maximum(i * rh - 1, 0), j)),
                  pl.BlockSpec((None, halo, tc), lambda bi, i, j: (bi, jnp.minimum((i + 1) * rh, lasth), j)),
                  pl.BlockSpec((3, tc), lambda bi, i, j: (0, j)),
                  pl.BlockSpec((1, tc), lambda bi, i, j: (0, j))],
        out_specs=pl.BlockSpec((None, tr, tc), lambda bi, i, j: (bi, i, j)),
        out_shape=jax.ShapeDtypeStruct((b, t, c), F32),
        compiler_params=_cparams("parallel", "parallel", "parallel"),
        name="dwconv3",
    )(x, x, x, w, bias.reshape(1, c))


def _ssd_kernel(x_ref, b_ref, c_ref, dt_ref, bias_ref, aneg_ref, e_ref, h0_ref,
                y_ref, hout_ref, st_ref, *, direction, nchunks):
    ci = pl.program_id(1)

    @pl.when(ci == 0)
    def _():
        st_ref[...] = h0_ref[...]

    q = x_ref.shape[0]
    hd = SSD_HEAD_DIM
    gw = (SSD_HEADS // SSD_GROUPS) * hd
    x = x_ref[...]
    dtv = _softplus(dt_ref[...] + bias_ref[...])
    adt = dtv * aneg_ref[...]
    row = lax.broadcasted_iota(jnp.int32, (q, q), 0)
    col = lax.broadcasted_iota(jnp.int32, (q, q), 1)
    mask = (col >= row) if direction else (col <= row)
    tri = mask.astype(BF16)
    acum = sum(_dot(tri, p) for p in _split3(adt))
    acum_t = sum(_dot_nt(p, tri) for p in _split3(adt.T))
    total = jnp.sum(adt, axis=0, keepdims=True)
    eac = jnp.exp(acum)
    dte = jnp.exp(total - acum)
    cd = jnp.broadcast_to(jnp.exp(total), (SUBLANES, LANES))
    stack = jnp.concatenate([dtv, eac, dte, cd], axis=0)
    e01 = e_ref[...]
    hi = stack.astype(BF16)
    lo = (stack - hi.astype(F32)).astype(BF16)
    ex = _dot(hi, e01) + _dot(lo, e01)
    dt_e, eac_e, dte_e, cd_e = ex[:q], ex[q:2 * q], ex[2 * q:3 * q], ex[3 * q:3 * q + 1]
    xdt = x * dt_e
    lane = lax.broadcasted_iota(jnp.int32, (q, LANES), 1)
    first = lane < hd
    for g in range(SSD_GROUPS):
        gs = slice(g * gw, (g + 1) * gw)
        c_g = c_ref[:, g * SSD_STATE:(g + 1) * SSD_STATE].astype(BF16)
        b_g32 = b_ref[:, g * SSD_STATE:(g + 1) * SSD_STATE]
        s_g = st_ref[:, gs]
        y_off = _dot(c_g, s_g.astype(BF16)) * eac_e[:, gs]
        cb = _dot_nt(c_g, b_g32.astype(BF16))
        for jp in range(gw // LANES):
            h_a = g * (SSD_HEADS // SSD_GROUPS) + 2 * jp
            ls = slice(h_a * hd, h_a * hd + LANES)
            xpair = xdt[:, ls]
            acc = y_off[:, jp * LANES:(jp + 1) * LANES]
            for k in range(2):
                cix = direction * SSD_HEADS + h_a + k
                seg = acum[:, cix:cix + 1] - acum_t[cix:cix + 1, :]
                dec = jnp.where(mask, jnp.exp(jnp.where(mask, seg, 0.0)), 0.0)
                w = (cb * dec).astype(BF16)
                xk = jnp.where(first if k == 0 else jnp.logical_not(first), xpair, 0.0).astype(BF16)
                acc = acc + _dot(w, xk)
            y_ref[:, ls] = acc
        xdte = (xdt[:, gs] * dte_e[:, gs]).astype(BF16)
        st_ref[:, gs] = s_g * cd_e[:, gs] + _dot(b_g32.T.astype(BF16), xdte)

    @pl.when(ci == nchunks - 1)
    def _():
        hout_ref[...] = st_ref[...]


def ssd_scan(xbc, dt_raw, dt_bias_row, aneg_row, expand, h0, direction):
    b, t, _ = xbc.shape
    q = SSD_CHUNK
    nchunks = t // q
    inner = SSD_HEADS * SSD_HEAD_DIM
    bn = SSD_GROUPS * SSD_STATE
    cmap = (lambda c: nchunks - 1 - c) if direction else (lambda c: c)
    y, hout = pl.pallas_call(
        functools.partial(_ssd_kernel, direction=direction, nchunks=nchunks),
        grid=(b, nchunks),
        in_specs=[pl.BlockSpec((None, q, inner), lambda bi, c: (bi, cmap(c), 0)),
                  pl.BlockSpec((None, q, bn), lambda bi, c: (bi, cmap(c), inner // bn)),
                  pl.BlockSpec((None, q, bn), lambda bi, c: (bi, cmap(c), inner // bn + 1)),
                  pl.BlockSpec((None, q, LANES), lambda bi, c: (bi, cmap(c), 0)),
                  pl.BlockSpec((1, LANES), lambda bi, c: (0, 0)),
                  pl.BlockSpec((1, LANES), lambda bi, c: (0, 0)),
                  pl.BlockSpec((LANES, inner), lambda bi, c: (0, 0)),
                  pl.BlockSpec((None, SSD_STATE, inner), lambda bi, c: (bi, 0, 0))],
        out_specs=[pl.BlockSpec((None, q, inner), lambda bi, c: (bi, cmap(c), 0)),
                   pl.BlockSpec((None, SSD_STATE, inner), lambda bi, c: (bi, 0, 0))],
        out_shape=[jax.ShapeDtypeStruct((b, t, inner), F32),
                   jax.ShapeDtypeStruct((b, SSD_STATE, inner), F32)],
        scratch_shapes=[pltpu.VMEM((SSD_STATE, inner), F32)],
        compiler_params=_cparams("parallel", "arbitrary"),
        name="ssd_scan_bwd" if direction else "ssd_scan_fwd",
    )(xbc, xbc, xbc, dt_raw, dt_bias_row, aneg_row, expand, h0)
    return y, hout


def _ssd_out_kernel(yf_ref, yb_ref, x_ref, z_ref, d_ref, w_ref, o_ref):
    y = yf_ref[...] + yb_ref[...] + d_ref[...] * x_ref[...]
    y = y * _silu(z_ref[...].astype(F32))
    gw = y.shape[1] // SSD_GROUPS
    for g in range(SSD_GROUPS):
        gs = slice(g * gw, (g + 1) * gw)
        yg = y[:, gs]
        yg = yg * lax.rsqrt(jnp.mean(yg * yg, axis=-1, keepdims=True) + EPS)
        o_ref[:, gs] = (yg * w_ref[:, gs]).astype(o_ref.dtype)


def ssd_out(yf, yb, xbc, z, d_row, w_row, tr=256):
    b, t, inner = yf.shape
    tr = _pick_tile(t, tr)
    blk = pl.BlockSpec((None, tr, inner), lambda bi, i: (bi, i, 0))
    vec = pl.BlockSpec((1, inner), lambda bi, i: (0, 0))
    return pl.pallas_call(
        _ssd_out_kernel,
        grid=(b, t // tr),
        in_specs=[blk, blk, blk, blk, vec, vec],
        out_specs=blk,
        out_shape=jax.ShapeDtypeStruct((b, t, inner), BF16),
        compiler_params=_cparams("parallel", "parallel"),
        name="ssd_out",
    )(yf, yb, xbc, z, d_row, w_row)


def _taps_kernel(f_ref, tv_ref, w1_ref, b1_ref, w2_ref, b2_ref, w3_ref, fr_ref, dl_ref, o_ref):
    hp = lax.Precision.HIGHEST
    fr = fr_ref[...]
    h = jnp.sin(fr * (jnp.dot(f_ref[...], w1_ref[...], precision=hp, preferred_element_type=F32) + b1_ref[...]))
    h = jnp.sin(fr * (jnp.dot(h, w2_ref[...], precision=hp, preferred_element_type=F32) + b2_ref[...]))
    filt = _dot(h.astype(BF16), w3_ref[...].astype(BF16))
    c = dl_ref.shape[1]
    win = jnp.exp(-tv_ref[:, 0:1] * dl_ref[...]) * tv_ref[:, 1:2]
    o_ref[0] = filt[:, :c] * win
    o_ref[1] = filt[:, c:] * win


def hyena_taps(length, w1, b1, w2, b2, w3, freq):
    hid = w1.shape[1]
    c = w3.shape[1] // (2 * HY_ORDER)
    n = 2 * length
    t = jnp.linspace(0.0, 1.0, length, dtype=F32)[:, None]
    phase = 2.0 * math.pi * jnp.arange(length, dtype=F32)[:, None] / length
    bands = jnp.linspace(1e-4, HY_BANDS - 1, HY_BANDS, dtype=F32)[None, :]
    feats = jnp.concatenate([t, jnp.cos(phase * bands), -jnp.sin(phase * bands)], axis=-1)
    nf = feats.shape[1]
    rev = jnp.concatenate([jnp.zeros((1,), jnp.int32), jnp.arange(length - 1, 0, -1, dtype=jnp.int32)])
    feats2 = jnp.concatenate([feats, feats[rev]], axis=0)
    feats2 = jnp.pad(feats2, ((0, 0), (0, LANES - nf)))
    valid = jnp.ones((n,), F32).at[length].set(0.0)
    tv = jnp.stack([jnp.concatenate([t[:, 0], t[rev, 0]]), valid], axis=1)
    max_decay = math.log(HY_DECAY_TARGET) / HY_SHORT_DECAY_PCT
    min_decay = math.log(HY_DECAY_TARGET) / HY_LONG_DECAY_PCT
    deltas = jnp.abs(jnp.linspace(min_decay, max_decay, c, dtype=F32))[None, :]
    w1p = jnp.pad(w1, ((0, LANES - nf), (0, 0)))
    w3r = w3.reshape(hid, HY_ORDER, 2, c).transpose(2, 0, 1, 3).reshape(2, hid, HY_ORDER * c)
    tr = _pick_tile(length, 1024)
    nhalf = length // tr
    return pl.pallas_call(
        _taps_kernel,
        grid=(n // tr,),
        in_specs=[pl.BlockSpec((tr, LANES), lambda i: (i, 0)),
                  pl.BlockSpec((tr, 2), lambda i: (i, 0)),
                  pl.BlockSpec((LANES, hid), lambda i: (0, 0)),
                  pl.BlockSpec((1, hid), lambda i: (0, 0)),
                  pl.BlockSpec((hid, hid), lambda i: (0, 0)),
                  pl.BlockSpec((1, hid), lambda i: (0, 0)),
                  pl.BlockSpec((None, hid, HY_ORDER * c), lambda i: (i // nhalf, 0, 0)),
                  pl.BlockSpec((1, hid), lambda i: (0, 0)),
                  pl.BlockSpec((1, c), lambda i: (0, 0))],
        out_specs=pl.BlockSpec((HY_ORDER, tr, c), lambda i: (0, i, 0)),
        out_shape=jax.ShapeDtypeStruct((HY_ORDER, n, c), F32),
        compiler_params=_cparams("parallel"),
        name="hyena_taps",
    )(feats2, tv, w1p, b1.reshape(1, hid), w2, b2.reshape(1, hid), w3r, freq.reshape(1, hid), deltas)


def _cs(num, den):
    ang = (2.0 * math.pi / den) * num.astype(F32)
    return jnp.cos(ang), jnp.sin(ang)


def fft_tables(n1, n2):
    n = n1 * n2
    h = n1 // 2
    k1 = jnp.arange(n1, dtype=jnp.int32)
    c1, s1 = _cs((k1[:, None] * k1[None, :]) % n1, n1)
    g_u = jnp.block([[c1[:, :h], s1[:, :h]], [-s1[:, :h], c1[:, :h]]])
    g_t = jnp.concatenate([c1, -s1], axis=0)
    g_i = jnp.block([[c1[:h, :], -s1[:h, :]], [s1[:h, :], c1[:h, :]]])
    s = jnp.arange(n2, dtype=jnp.int32)
    num = (n1 * (s[:, None] * s[None, :]))[None] + (k1[:, None, None] * s[None, None, :])
    cb, sb = _cs(num % n, n)
    m_f = jnp.concatenate([jnp.concatenate([cb, sb], axis=2),
                           jnp.concatenate([-sb, cb], axis=2)], axis=1)
    cbt, sbt = jnp.swapaxes(cb, 1, 2) / n, jnp.swapaxes(sb, 1, 2) / n
    m_i = jnp.concatenate([jnp.concatenate([cbt, -sbt], axis=2),
                           jnp.concatenate([sbt, cbt], axis=2)], axis=1)
    return dict(g_u=g_u.astype(BF16), g_t=g_t.astype(BF16), g_i=g_i.astype(BF16),
                m_f=m_f.astype(BF16), m_i=m_i.astype(BF16))


FFT_KGROUP = SUBLANES


def _rows_get(ref, start, n, stride):
    return ref.reshape(math.prod(ref.shape[:-1]), ref.shape[-1])[pl.ds(start, n, stride=stride), :]


def _rows_set(ref, start, n, stride, val):
    ref.reshape(math.prod(ref.shape[:-1]), ref.shape[-1])[pl.ds(start, n, stride=stride), :] = val


def _fft_a_kernel(g_ref, xa_ref, xb_ref, o_ref, *, s_major_in):
    h = g_ref.shape[1] // 2
    n1 = o_ref.shape[2]
    ts = o_ref.shape[1]
    ga = g_ref[:, :h]
    gb = g_ref[:, h:]
    for s in range(ts):
        xa = xa_ref[s] if s_major_in else _rows_get(xa_ref, s, h, ts)
        xb = xb_ref[s] if s_major_in else _rows_get(xb_ref, s, h, ts)
        r = _dot(ga, xa.astype(BF16)) + _dot(gb, xb.astype(BF16))
        o_ref[0, s] = r[:n1]
        o_ref[1, s] = r[n1:]


def fft_stage_a(g, x4, sel_a, sel_b, col_off, c, s_major_in=False, tc=LANES):
    n1 = g.shape[0] // 2
    h = n1 // 2
    n2 = x4.shape[1] if s_major_in else x4.shape[2]
    ts = SUBLANES
    coff = col_off // tc
    if s_major_in:
        spec = lambda sel: pl.BlockSpec((None, ts, h, tc), lambda j, s: (sel[0], s, sel[1], j + coff))
    else:
        spec = lambda sel: pl.BlockSpec((None, h, ts, tc), lambda j, s: (sel[0], sel[1], s, j + coff))
    return pl.pallas_call(
        functools.partial(_fft_a_kernel, s_major_in=s_major_in),
        grid=(c // tc, n2 // ts),
        in_specs=[pl.BlockSpec((2 * n1, n1), lambda j, s: (0, 0)), spec(sel_a), spec(sel_b)],
        out_specs=pl.BlockSpec((2, ts, n1, tc), lambda j, s: (0, s, 0, j)),
        out_shape=jax.ShapeDtypeStruct((2, n2, n1, c), F32),
        compiler_params=_cparams("parallel", "parallel"),
        name="fft_stage_a",
    )(g, x4, x4)


def _fft_load_k(a_ref, j):
    _, n2, kb, _ = a_ref.shape
    return _rows_get(a_ref, j, 2 * n2, kb).astype(BF16)


def _fft_bh_kernel(m_ref, a_ref, o_ref):
    for j in range(a_ref.shape[2]):
        o_ref[j] = _dot(m_ref[j], _fft_load_k(a_ref, j)).reshape(o_ref.shape[1:])


def _fft_b_kernel(mf_ref, mi_ref, a_ref, h_ref, o_ref):
    n2 = a_ref.shape[1]
    for j in range(a_ref.shape[2]):
        x = _dot(mf_ref[j], _fft_load_k(a_ref, j))
        xr, xi = x[:n2], x[n2:]
        hr, hi = h_ref[j, 0], h_ref[j, 1]
        y = jnp.concatenate([xr * hr - xi * hi, xr * hi + xi * hr], axis=0).astype(BF16)
        r = _dot(mi_ref[j], y)
        o_ref[0, j] = r[:n2]
        o_ref[1, j] = r[n2:]


def fft_stage_b(m_f, m_i, a, hspec, tc=LANES):
    _, n2, n1, c = a.shape
    kb = FFT_KGROUP
    ablk = pl.BlockSpec((2, n2, kb, tc), lambda k, j: (0, 0, k, j))
    hblk = pl.BlockSpec((kb, 2, n2, tc), lambda k, j: (k, 0, 0, j))
    mat = pl.BlockSpec((kb, 2 * n2, 2 * n2), lambda k, j: (k, 0, 0))
    if hspec is None:
        kern, in_specs, args, name = _fft_bh_kernel, [mat, ablk], (m_f, a), "fft_stage_b_spectrum"
        out_spec, out_shape = hblk, (n1, 2, n2, c)
    else:
        kern, in_specs, args, name = _fft_b_kernel, [mat, mat, ablk, hblk], (m_f, m_i, a, hspec), "fft_stage_b"
        out_spec, out_shape = pl.BlockSpec((2, kb, n2, tc), lambda k, j: (0, k, 0, j)), (2, n1, n2, c)
    return pl.pallas_call(
        kern,
        grid=(n1 // kb, c // tc),
        in_specs=in_specs,
        out_specs=out_spec,
        out_shape=jax.ShapeDtypeStruct(out_shape, F32),
        compiler_params=_cparams("parallel", "parallel"),
        name=name,
    )(*args)


def _fft_ai_kernel(g_ref, b_ref, u_ref, x_ref, skip_ref, o_ref, *, u_s_major, out_s_major):
    _, n1, ts, _ = b_ref.shape
    h = n1 // 2
    ga = g_ref[:, :n1]
    gb = g_ref[:, n1:]
    skip = skip_ref[...]
    for s in range(ts):
        y = (_dot(ga, _rows_get(b_ref, s, n1, ts).astype(BF16))
             + _dot(gb, _rows_get(b_ref, n1 * ts + s, n1, ts).astype(BF16)))
        for bi in range(2):
            start = bi * h * ts + s
            u = u_ref[bi, s] if u_s_major else _rows_get(u_ref, start, h, ts)
            val = _rows_get(x_ref, start, h, ts) * (y[bi * h:(bi + 1) * h] + u * skip)
            if out_s_major:
                o_ref[bi, s] = val
            else:
                _rows_set(o_ref, start, h, ts, val)


def fft_stage_a_inv(g_i, bm, u4, u_off, u_s_major, x4, x_off, skip_row, out_s_major, tc=LANES):
    _, n1, n2, c = bm.shape
    h = n1 // 2
    ts = SUBLANES
    uo, xo = u_off // tc, x_off // tc
    nat = lambda off: pl.BlockSpec((2, h, ts, tc), lambda j, s: (0, 0, s, j + off))
    smj = lambda off: pl.BlockSpec((2, ts, h, tc), lambda j, s: (0, s, 0, j + off))
    return pl.pallas_call(
        functools.partial(_fft_ai_kernel, u_s_major=u_s_major, out_s_major=out_s_major),
        grid=(c // tc, n2 // ts),
        in_specs=[pl.BlockSpec((n1, 2 * n1), lambda j, s: (0, 0)),
                  pl.BlockSpec((2, n1, ts, tc), lambda j, s: (0, 0, s, j)),
                  smj(uo) if u_s_major else nat(uo),
                  nat(xo),
                  pl.BlockSpec((1, tc), lambda j, s: (0, j))],
        out_specs=smj(0) if out_s_major else nat(0),
        out_shape=jax.ShapeDtypeStruct((2, n2, h, c) if out_s_major else (2, h, n2, c), F32),
        compiler_params=_cparams("parallel", "parallel"),
        name="fft_stage_a_inv",
    )(g_i, bm, u4, x4, skip_row)


def hyena_long(u, taps, skip, tabs):
    b, length, c3 = u.shape
    assert b == 2, "the two batch rows are packed as one complex sequence"
    c = c3 // 3
    n2 = FFT_N2
    n1 = 2 * length // n2
    h = n1 // 2
    u4 = u.reshape(b, h, n2, c3)
    taps4 = taps.reshape(HY_ORDER * 2, h, n2, c)
    zin, zoff, z_s_major = u4, 0, False
    out = None
    for conv in range(HY_ORDER):
        last = conv == HY_ORDER - 1
        hspec = fft_stage_b(tabs["m_f"], None,
                            fft_stage_a(tabs["g_t"], taps4, (2 * conv, 0), (2 * conv + 1, 0), 0, c), None)
        a = fft_stage_a(tabs["g_u"], zin, (0, 0), (1, 0), zoff, c, s_major_in=z_s_major)
        bm = fft_stage_b(tabs["m_f"], tabs["m_i"], a, hspec)
        out = fft_stage_a_inv(tabs["g_i"], bm, zin, zoff, z_s_major, u4, (conv + 1) * c,
                              skip[conv].reshape(1, c), not last)
        zin, zoff, z_s_major = out, 0, True
    return out.reshape(b, length, c)


def _hyena_ctx_kernel(u_ref, x1_ref, x2_ref, taps_ref, fc_ref, fs_ref, skip_ref, o_ref):
    length = u_ref.shape[1]
    n = 2 * length
    fc = fc_ref[...]
    fs = fs_ref[...]
    fcl, fsl = fc[:, :length], fs[:, :length]
    fct, fst = fc[:length, :], fs[:length, :]
    zr = u_ref[0]
    zi = u_ref[1]
    gates = (x1_ref, x2_ref)
    for conv in range(HY_ORDER):
        tp = taps_ref[conv].astype(BF16)
        hr, hi = _dot(fc, tp), -_dot(fs, tp)
        zrb, zib = zr.astype(BF16), zi.astype(BF16)
        xr = _dot(fcl, zrb) + _dot(fsl, zib)
        xi = _dot(fcl, zib) - _dot(fsl, zrb)
        yr = (xr * hr - xi * hi).astype(BF16)
        yi = (xr * hi + xi * hr).astype(BF16)
        cr = (_dot(fct, yr) - _dot(fst, yi)) * (1.0 / n)
        ci = (_dot(fct, yi) + _dot(fst, yr)) * (1.0 / n)
        sk = skip_ref[conv:conv + 1, :]
        zr = gates[conv][0] * (cr + zr * sk)
        zi = gates[conv][1] * (ci + zi * sk)
    o_ref[0] = zr.astype(o_ref.dtype)
    o_ref[1] = zi.astype(o_ref.dtype)


def hyena_ctx(u, taps, skip, tc=256):
    b, length, c3 = u.shape
    assert b == 2
    c = c3 // 3
    n = 2 * length
    k = jnp.arange(n, dtype=jnp.int32)
    fc, fs = _cs((k[:, None] * k[None, :]) % n, n)
    nb = c // tc
    ub = lambda off: pl.BlockSpec((2, length, tc), lambda j: (0, 0, j + off * nb))
    mat = pl.BlockSpec((n, n), lambda j: (0, 0))
    return pl.pallas_call(
        _hyena_ctx_kernel,
        grid=(nb,),
        in_specs=[ub(0), ub(1), ub(2),
                  pl.BlockSpec((HY_ORDER, n, tc), lambda j: (0, 0, j)),
                  mat, mat,
                  pl.BlockSpec((HY_ORDER, tc), lambda j: (0, j))],
        out_specs=pl.BlockSpec((2, length, tc), lambda j: (0, 0, j)),
        out_shape=jax.ShapeDtypeStruct((b, length, c), BF16),
        compiler_params=_cparams("parallel"),
        name="hyena_ctx",
    )(u, u, u, taps, fc.astype(BF16), fs.astype(BF16), skip)


def rope_tables(length):
    rows = length // GRID_W
    row = jnp.repeat(jnp.arange(rows, dtype=F32), GRID_W)
    col = (jnp.arange(length) % GRID_W).astype(F32)
    inv = ROPE_THETA ** (-jnp.arange(0, ROPE_AXIS_DIM, 2, dtype=F32) / ROPE_AXIS_DIM)
    ang = jnp.stack([row[:, None] * inv, col[:, None] * inv], axis=1)
    cos, sin = jnp.cos(ang), jnp.sin(ang)
    zero = jnp.zeros_like(sin)
    per_map = lambda lo, hi: jnp.concatenate([lo, hi], axis=-1).reshape(length, 2 * ROPE_AXIS_DIM)
    reps = LANES // (2 * ROPE_AXIS_DIM)
    c_t = jnp.tile(per_map(cos, cos), (1, reps))
    sm_t = jnp.tile(per_map(-sin, zero), (1, reps))
    sp_t = jnp.tile(per_map(zero, sin), (1, reps))
    return c_t, sm_t, sp_t


def _rope_kernel(x_ref, c_ref, sm_ref, sp_ref, o_ref, *n_refs, scale):
    half = ROPE_AXIS_DIM // 2
    c, sm, sp = c_ref[...], sm_ref[...], sp_ref[...]
    for g in range(x_ref.shape[1] // LANES):
        ls = slice(g * LANES, (g + 1) * LANES)
        x = x_ref[:, ls].astype(F32)
        y = (x * c + pltpu.roll(x, LANES - half, 1) * sm + pltpu.roll(x, half, 1) * sp) * scale
        o_ref[:, ls] = y.astype(o_ref.dtype)
        if n_refs:
            n2 = jnp.max(jnp.sum(y * y, axis=1, keepdims=True), axis=0, keepdims=True)
            n_refs[0][g:g + 1, :] = jnp.broadcast_to(n2, (1, LANES))


def rope(x, tables, scale, with_norms=False, tr=512):
    b, t, d = x.shape
    tr = _pick_tile(t, tr)
    nh = d // LANES
    tab = pl.BlockSpec((tr, LANES), lambda bi, i: (i, 0))
    out_specs = [pl.BlockSpec((None, tr, d), lambda bi, i: (bi, i, 0))]
    out_shape = [jax.ShapeDtypeStruct((b, t, d), BF16)]
    if with_norms:
        out_specs.append(pl.BlockSpec((None, None, nh, LANES), lambda bi, i: (bi, i, 0, 0)))
        out_shape.append(jax.ShapeDtypeStruct((b, t // tr, nh, LANES), F32))
    outs = pl.pallas_call(
        functools.partial(_rope_kernel, scale=scale),
        grid=(b, t // tr),
        in_specs=[pl.BlockSpec((None, tr, d), lambda bi, i: (bi, i, 0)), tab, tab, tab],
        out_specs=out_specs,
        out_shape=out_shape,
        compiler_params=_cparams("parallel", "parallel"),
        name="rope",
    )(x, *tables)
    if with_norms:
        return outs[0], jnp.swapaxes(outs[1], 1, 2)
    return outs[0]


def _scale_cast_kernel(x_ref, o_ref, *, scale):
    o_ref[...] = (x_ref[...].astype(F32) * scale).astype(o_ref.dtype)


def scale_cast(x, scale, tr=256):
    b, t, d = x.shape
    tr = _pick_tile(t, tr)
    blk = pl.BlockSpec((None, tr, d), lambda bi, i: (bi, i, 0))
    return pl.pallas_call(
        functools.partial(_scale_cast_kernel, scale=scale),
        grid=(b, t // tr), in_specs=[blk], out_specs=blk,
        out_shape=jax.ShapeDtypeStruct((b, t, d), BF16),
        compiler_params=_cparams("parallel", "parallel"),
        name="scale_cast",
    )(x)


ATTN_GUARD_LOG2 = 80.0
ATTN_NORM_SLACK = 1.03


def _attn_kernel(*refs, lam_init, nk, with_ctx):
    if with_ctx:
        (q_ref, k_ref, v_ref, kc_ref, vc_ref, kn_ref, lam_ref, w_ref, o_ref,
         q2_ref, m_ref, c_ref, acc_ref, p_scr, flag_ref) = refs
    else:
        q_ref, k_ref, v_ref, lam_ref, w_ref, o_ref, q2_ref, m_ref, c_ref, acc_ref, p_scr, flag_ref = refs
    ki = pl.program_id(3)
    tq = q_ref.shape[0]

    def with_ones(v):
        lane = lax.broadcasted_iota(jnp.int32, v.shape, 1)
        return jnp.concatenate([v, (lane == 0).astype(BF16)], axis=1)

    def exact_update(k, v, p_buf):
        s = _dot_nt(q2_ref[...], k)
        m_prev = m_ref[...]
        m_new = jnp.maximum(m_prev, jnp.max(s, axis=1, keepdims=True))
        alpha = jnp.exp2(m_prev - m_new)
        p_buf[...] = jnp.exp2(s - m_new).astype(BF16)
        acc_ref[...] = alpha * acc_ref[...] + _dot(p_buf[...], with_ones(v))
        m_ref[...] = m_new

    def fast_update(k, v, p_buf):
        s = _dot_nt(q2_ref[...], k)
        p_buf[...] = jnp.exp2(s - pltpu.repeat(c_ref[...], k.shape[0] // LANES, axis=1)).astype(BF16)
        acc_ref[...] += _dot(p_buf[...], with_ones(v))

    @pl.when(ki == 0)
    def _():
        q = q_ref[...]
        lane = lax.broadcasted_iota(jnp.int32, q.shape, 1)
        zero = jnp.zeros_like(q)
        q2_ref[:tq, :] = jnp.where(lane < DA_HEAD_DIM, q, zero)
        q2_ref[tq:, :] = jnp.where(lane >= DA_HEAD_DIM, q, zero)
        m_ref[...] = jnp.full(m_ref.shape, -jnp.inf, F32)
        acc_ref[...] = jnp.zeros(acc_ref.shape, F32)
        flag_ref[0] = 0
        if with_ctx:
            q2 = q2_ref[...]
            c = _dot_nt(q2, jnp.broadcast_to(kc_ref[0:1, :], (LANES, DA_V_DIM)))
            sq = q2.astype(F32) * q2.astype(F32)
            hi = sq.astype(BF16)
            lo = (sq - hi.astype(F32)).astype(BF16)
            ones = jnp.ones((DA_V_DIM, LANES), BF16)
            qn2 = _dot(hi, ones) + _dot(lo, ones)
            kn2 = jnp.max(kn_ref[...])
            gap = jnp.max(jnp.sqrt(qn2 * kn2) * ATTN_NORM_SLACK - c)
            c_ref[...] = c
            flag_ref[0] = (gap <= ATTN_GUARD_LOG2).astype(jnp.int32)

    fast = flag_ref[0] == 1
    if with_ctx:
        tc = kc_ref.shape[0]
        first = ki == 0

        @pl.when(jnp.logical_and(first, fast))
        def _():
            fast_update(kc_ref[...], vc_ref[...], p_scr.at[:, pl.ds(0, tc)])

        @pl.when(jnp.logical_and(first, jnp.logical_not(fast)))
        def _():
            exact_update(kc_ref[...], vc_ref[...], p_scr.at[:, pl.ds(0, tc)])

    @pl.when(fast)
    def _():
        fast_update(k_ref[...], v_ref[...], p_scr)

    @pl.when(jnp.logical_not(fast))
    def _():
        exact_update(k_ref[...], v_ref[...], p_scr)

    @pl.when(ki == nk - 1)
    def _():
        lv = lam_ref[...]
        e1 = jnp.exp(jnp.sum(lv[0:1, :] * lv[1:2, :], axis=1, keepdims=True))
        e2 = jnp.exp(jnp.sum(lv[2:3, :] * lv[3:4, :], axis=1, keepdims=True))
        lam = e1 - e2 + lam_init
        acc = acc_ref[...]
        o = acc[:, :DA_V_DIM] / acc[:, DA_V_DIM:DA_V_DIM + 1]
        o = o[:tq] - lam * o[tq:]
        o_ref[...] = (_rms(o, w_ref[...]) * (1.0 - lam_init)).astype(o_ref.dtype)


def diff_attention(q, k, v, k_ctx, v_ctx, knorm, lam_vec, subln_w, lam_init, tq=1024, tk=2048):
    b, t, d = q.shape
    nh = d // DA_V_DIM
    tq = _pick_tile(t, tq)
    tk = _pick_tile(k.shape[1], tk)
    nk = k.shape[1] // tk
    with_ctx = k_ctx is not None
    assert not with_ctx or k_ctx.shape[1] <= tk
    qspec = pl.BlockSpec((None, tq, DA_V_DIM), lambda bi, h, i, j: (bi, i, h))
    kspec = pl.BlockSpec((None, tk, DA_V_DIM), lambda bi, h, i, j: (bi, j, h))
    in_specs = [qspec, kspec, kspec]
    args = [q, k, v]
    if with_ctx:
        cspec = pl.BlockSpec((None, k_ctx.shape[1], DA_V_DIM), lambda bi, h, i, j: (bi, 0, h))
        nspec = pl.BlockSpec((None, None) + knorm.shape[2:], lambda bi, h, i, j: (bi, h, 0, 0))
        in_specs += [cspec, cspec, nspec]
        args += [k_ctx, v_ctx, knorm]
    in_specs += [pl.BlockSpec((4, DA_HEAD_DIM), lambda bi, h, i, j: (0, 0)),
                 pl.BlockSpec((1, DA_V_DIM), lambda bi, h, i, j: (0, 0))]
    args += [lam_vec, subln_w.reshape(1, DA_V_DIM)]
    return pl.pallas_call(
        functools.partial(_attn_kernel, lam_init=lam_init, nk=nk, with_ctx=with_ctx),
        grid=(b, nh, t // tq, nk),
        in_specs=in_specs,
        out_specs=qspec,
        out_shape=jax.ShapeDtypeStruct((b, t, d), BF16),
        scratch_shapes=[pltpu.VMEM((2 * tq, DA_V_DIM), BF16),
                        pltpu.VMEM((2 * tq, 1), F32),
                        pltpu.VMEM((2 * tq, LANES), F32),
                        pltpu.VMEM((2 * tq, 2 * DA_V_DIM), F32),
                        pltpu.VMEM((2 * tq, tk), BF16),
                        pltpu.SMEM((1,), jnp.int32)],
        compiler_params=_cparams("parallel", "parallel", "parallel", "arbitrary"),
        name="diff_attention",
    )(*args)


def _merge_kernel(ya_ref, yb_ref, yc_ref, g_ref, wb_ref, wo_ref, x_ref, mod_ref, o_ref):
    d = x_ref.shape[1]
    ys = (ya_ref, yb_ref, yc_ref)
    s = None
    for k in range(N_BRANCH):
        t = jax.nn.sigmoid(g_ref[:, k * d:(k + 1) * d].astype(F32)) * _dot(ys[k][...].astype(BF16), wb_ref[k])
        s = t if s is None else s + t
    r = _dot(s.astype(BF16), wo_ref[...])
    o_ref[...] = x_ref[...] + mod_ref[2:3, :] * r


def merge(ya, yb, yc, gates, wb, wo, x, mods, tr=256):
    b, t, d = x.shape
    tr = _pick_tile(t, tr)
    per_batch = mods.shape[0] > 1
    blk = pl.BlockSpec((None, tr, d), lambda bi, i: (bi, i, 0))
    return pl.pallas_call(
        _merge_kernel,
        grid=(b, t // tr),
        in_specs=[blk, blk, blk,
                  pl.BlockSpec((None, tr, N_BRANCH * d), lambda bi, i: (bi, i, 0)),
                  pl.BlockSpec((N_BRANCH, d, d), lambda bi, i: (0, 0, 0)),
                  pl.BlockSpec((d, d), lambda bi, i: (0, 0)),
                  blk,
                  pl.BlockSpec((None, 8, d), (lambda bi, i: (bi, 0, 0)) if per_batch else (lambda bi, i: (0, 0, 0)))],
        out_specs=blk,
        out_shape=jax.ShapeDtypeStruct((b, t, d), F32),
        compiler_params=_cparams("parallel", "parallel"),
        name="merge",
    )(ya, yb, yc, gates, wb, wo, x, mods)


def _ffn_kernel(x_ref, g_ref, mod_ref, w13_ref, w2_ref, o_ref, *, nchunk):
    x = x_ref[...]
    h = (_rms(x, g_ref[...]) * (1.0 + mod_ref[4:5, :]) + mod_ref[3:4, :]).astype(BF16)
    f = w2_ref.shape[0]
    fc = f // nchunk
    acc = None
    for c in range(nchunk):
        a1 = _dot(h, w13_ref[:, c * fc:(c + 1) * fc])
        a3 = _dot(h, w13_ref[:, f + c * fc:f + (c + 1) * fc])
        t = _dot((_silu(a1) * a3).astype(BF16), w2_ref[c * fc:(c + 1) * fc, :])
        acc = t if acc is None else acc + t
    o_ref[...] = x + mod_ref[5:6, :] * acc


def ffn(x, g, mods, w13, w2, tr=256):
    b, t, d = x.shape
    f = w2.shape[0]
    tr = _pick_tile(t, tr)
    nchunk = 2 if (f // 2) % LANES == 0 else 1
    per_batch = mods.shape[0] > 1
    blk = pl.BlockSpec((None, tr, d), lambda bi, i: (bi, i, 0))
    return pl.pallas_call(
        functools.partial(_ffn_kernel, nchunk=nchunk),
        grid=(b, t // tr),
        in_specs=[blk,
                  pl.BlockSpec((1, d), lambda bi, i: (0, 0)),
                  pl.BlockSpec((None, 8, d), (lambda bi, i: (bi, 0, 0)) if per_batch else (lambda bi, i: (0, 0, 0))),
                  pl.BlockSpec((d, 2 * f), lambda bi, i: (0, 0)),
                  pl.BlockSpec((f, d), lambda bi, i: (0, 0))],
        out_specs=blk,
        out_shape=jax.ShapeDtypeStruct((b, t, d), F32),
        compiler_params=_cparams("parallel", "parallel"),
        name="ffn",
    )(x, g.reshape(1, d), mods, w13, w2)


def kernel(x, c, ctx, c_ctx, ada_w, ada_b, norm1_g, norm2_g, w_in, ssd_conv_w, ssd_conv_b,
           ssd_a_log, ssd_dt_bias, ssd_d, ssd_norm_w, hy_conv_w, hy_conv_b, hy_w1, hy_b1, hy_w2,
           hy_b2, hy_w3, hy_freq, hy_bias, da_lambda, da_subln_w, w_branch, w_out, ffn_w13,
           ffn_w2, final_g):
    bsz, length, d = x.shape
    ctx_len = ctx.shape[1]
    depth = w_in.shape[0]
    inner = SSD_HEADS * SSD_HEAD_DIM
    xbc_w = inner + 2 * SSD_GROUPS * SSD_STATE
    dt_w = 2 * SSD_HEADS
    hy_w = (HY_ORDER + 1) * d
    widths = (inner, xbc_w, dt_w, hy_w, d, d, d, N_BRANCH * d)
    offs = [0]
    for wd in widths:
        offs.append(offs[-1] + wd)

    cond = jnp.concatenate([c, c_ctx[None, :], jnp.zeros((8 - bsz - 1, d), F32)], axis=0)

    rope_tabs = rope_tables(length)
    tabs = fft_tables(2 * length // FFT_N2, FFT_N2)
    expand = (jnp.arange(LANES)[:, None] % SSD_HEADS
              == jnp.arange(inner)[None, :] // SSD_HEAD_DIM)
    expand_dir = [(expand & ((jnp.arange(LANES)[:, None] // SSD_HEADS) == dr)).astype(BF16) for dr in range(2)]
    qscale = DA_HEAD_DIM ** -0.5 * math.log2(math.e)

    x_l, x_c = x, ctx
    for layer in range(depth):
        last = layer == depth - 1
        lam_init = 0.8 - 0.6 * math.exp(-0.3 * layer)
        mod = matmul(cond, ada_w[layer].astype(BF16), ada_b[layer].reshape(1, -1), silu_in=True)
        mod = jnp.pad(mod.reshape(8, 6, d), ((0, 0), (0, 2), (0, 0)))
        mods_l, mods_c = mod[:bsz], mod[bsz:bsz + 1]

        w_l = w_in[layer].astype(BF16)
        w_parts = [w_l[:, offs[i]:offs[i + 1]] for i in range(len(widths))]
        w_parts[2] = jnp.pad(w_parts[2], ((0, 0), (0, LANES - dt_w)))
        a_neg = -jnp.exp(ssd_a_log[layer].astype(F32)).reshape(1, dt_w)
        aneg_row = jnp.pad(a_neg, ((0, 0), (0, LANES - dt_w)))
        bias_row = jnp.pad(ssd_dt_bias[layer].astype(F32).reshape(1, dt_w), ((0, 0), (0, LANES - dt_w)))
        d_row = jnp.repeat(ssd_d[layer].astype(F32), SSD_HEAD_DIM).reshape(1, inner)
        nw_row = ssd_norm_w[layer].astype(F32).reshape(1, inner)
        wb = w_branch[layer].astype(BF16)
        wo = w_out[layer].astype(BF16)
        w13 = ffn_w13[layer].astype(BF16)
        w2 = ffn_w2[layer].astype(BF16)

        def project(stream, mods):
            bb, tt, _ = stream.shape
            h = norm_mod(stream, norm1_g[layer], mods, 0).reshape(bb * tt, d)
            outs = [matmul(h, wp, out_dtype=(F32 if i == 2 else BF16)).reshape(bb, tt, -1)
                    for i, wp in enumerate(w_parts)]
            return outs

        z_c, xbc_c, dt_c, hy_c, q_c, k_c, v_c, g_c = project(x_c, mods_c)
        z_l, xbc_l, dt_l, hy_l, q_l, k_l, v_l, g_l = project(x_l, mods_l)

        xa_c = dwconv3(xbc_c, ssd_conv_w[layer], ssd_conv_b[layer], True)
        xa_l = dwconv3(xbc_l, ssd_conv_w[layer], ssd_conv_b[layer], True)
        h_zero = jnp.zeros((bsz, SSD_STATE, inner), F32)
        ys_c, ys_l = [], []
        for dr in range(2):
            y_c, st = ssd_scan(xa_c, dt_c, bias_row, aneg_row, expand_dir[dr], h_zero, dr)
            y_l, _ = ssd_scan(xa_l, dt_l, bias_row, aneg_row, expand_dir[dr], st, dr)
            ys_c.append(y_c)
            ys_l.append(y_l)
        ya_l = ssd_out(ys_l[0], ys_l[1], xa_l, z_l, d_row, nw_row)

        filt = (hy_w1[layer], hy_b1[layer], hy_w2[layer], hy_b2[layer], hy_w3[layer], hy_freq[layer])
        u_l = dwconv3(hy_l, hy_conv_w[layer], hy_conv_b[layer], False)
        yb_l = hyena_long(u_l, hyena_taps(length, *filt), hy_bias[layer], tabs)

        qr_l = rope(q_l, rope_tabs, qscale)
        kr_l, kn_l = rope(k_l, rope_tabs, 1.0, with_norms=True)
        kb_c = scale_cast(k_c, 1.0)
        yc_l = diff_attention(qr_l, kr_l, v_l, kb_c, v_c, kn_l, da_lambda[layer], da_subln_w[layer], lam_init)

        x_l_new = merge(ya_l, yb_l, yc_l, g_l, wb, wo, x_l, mods_l)
        x_l = ffn(x_l_new, norm2_g[layer], mods_l, w13, w2)

        if not last:
            ya_c = ssd_out(ys_c[0], ys_c[1], xa_c, z_c, d_row, nw_row)
            u_c = dwconv3(hy_c, hy_conv_w[layer], hy_conv_b[layer], False)
            yb_c = hyena_ctx(u_c, hyena_taps(ctx_len, *filt), hy_bias[layer])
            qb_c = scale_cast(q_c, qscale)
            yc_c = diff_attention(qb_c, kb_c, v_c, None, None, None, da_lambda[layer], da_subln_w[layer], lam_init)
            x_c_new = merge(ya_c, yb_c, yc_c, g_c, wb, wo, x_c, mods_c)
            x_c = ffn(x_c_new, norm2_g[layer], mods_c, w13, w2)
    return final_norm(x_l, final_g)
```

```python
import functools
import math

import jax
import jax.numpy as jnp
from jax import lax
from jax.experimental import pallas as pl
from jax.experimental.pallas import tpu as pltpu

F32 = jnp.float32
BF16 = jnp.bfloat16
EPS = 1e-6

SSD_HEADS = 16
SSD_HEAD_DIM = 64
SSD_GROUPS = 2
SSD_STATE = 128
SSD_CHUNK = 128
HY_ORDER = 2
HY_BANDS = 16
HY_SHORT_DECAY_PCT = 0.3
HY_LONG_DECAY_PCT = 1.5
HY_DECAY_TARGET = 1e-2
DA_HEADS = 8
DA_V_DIM = 128
DA_HEAD_DIM = 64
ROPE_AXIS_DIM = 32
ROPE_THETA = 10000.0
GRID_W = 64
N_BRANCH = 3

LANES = 128
SUBLANES = 8
FFT_N2 = 128
VMEM_LIMIT = 56 * 1024 * 1024


def _cparams(*sem):
    return pltpu.CompilerParams(dimension_semantics=sem, vmem_limit_bytes=VMEM_LIMIT)


def _dot(a, b):
    return jnp.dot(a, b, preferred_element_type=F32)


def _dot_nt(a, b):
    return lax.dot_general(a, b, (((1,), (1,)), ((), ())), preferred_element_type=F32)


def _split3(x):
    hi = x.astype(BF16)
    r1 = x - hi.astype(F32)
    mid = r1.astype(BF16)
    lo = (r1 - mid.astype(F32)).astype(BF16)
    return hi, mid, lo


def _silu(x):
    return x * jax.nn.sigmoid(x)


def _softplus(x):
    return jnp.maximum(x, 0.0) + jnp.log1p(jnp.exp(-jnp.abs(x)))


def _pick_tile(n, pref):
    t = min(pref, n)
    while n % t:
        t //= 2
    return t


def _matmul_kernel(a_ref, w_ref, b_ref, o_ref, *, silu_in):
    a = a_ref[...]
    if silu_in:
        a = _silu(a.astype(F32))
    r = _dot(a.astype(BF16), w_ref[...]) + b_ref[...]
    o_ref[...] = r.astype(o_ref.dtype)


def matmul(a, w, bias=None, out_dtype=F32, silu_in=False, tm=2048, tn=1024):
    m, k = a.shape
    n = w.shape[1]
    tm = _pick_tile(m, tm)
    tn = _pick_tile(n, tn)
    if bias is None:
        bias = jnp.zeros((1, n), F32)
    return pl.pallas_call(
        functools.partial(_matmul_kernel, silu_in=silu_in),
        grid=(m // tm, n // tn),
        in_specs=[pl.BlockSpec((tm, k), lambda i, j: (i, 0)),
                  pl.BlockSpec((k, tn), lambda i, j: (0, j)),
                  pl.BlockSpec((1, tn), lambda i, j: (0, j))],
        out_specs=pl.BlockSpec((tm, tn), lambda i, j: (i, j)),
        out_shape=jax.ShapeDtypeStruct((m, n), out_dtype),
        compiler_params=_cparams("parallel", "parallel"),
        name="matmul",
    )(a, w, bias)


def _rms(x, g):
    return x * lax.rsqrt(jnp.mean(x * x, axis=-1, keepdims=True) + EPS) * g


def _norm_mod_kernel(x_ref, g_ref, mod_ref, o_ref, *, row0):
    x = x_ref[...]
    y = _rms(x, g_ref[...])
    shift = mod_ref[row0:row0 + 1, :]
    scale = mod_ref[row0 + 1:row0 + 2, :]
    o_ref[...] = (y * (1.0 + scale) + shift).astype(o_ref.dtype)


def norm_mod(x, g, mods, row0, tr=512):
    b, t, d = x.shape
    tr = _pick_tile(t, tr)
    per_batch = mods.shape[0] > 1
    return pl.pallas_call(
        functools.partial(_norm_mod_kernel, row0=row0),
        grid=(b, t // tr),
        in_specs=[pl.BlockSpec((None, tr, d), lambda bi, i: (bi, i, 0)),
                  pl.BlockSpec((1, d), lambda bi, i: (0, 0)),
                  pl.BlockSpec((None, 8, d), (lambda bi, i: (bi, 0, 0)) if per_batch else (lambda bi, i: (0, 0, 0)))],
        out_specs=pl.BlockSpec((None, tr, d), lambda bi, i: (bi, i, 0)),
        out_shape=jax.ShapeDtypeStruct((b, t, d), BF16),
        compiler_params=_cparams("parallel", "parallel"),
        name="norm_mod",
    )(x, g.reshape(1, d), mods)


def _final_norm_kernel(x_ref, g_ref, o_ref):
    o_ref[...] = _rms(x_ref[...], g_ref[...])


def final_norm(x, g, tr=512):
    b, t, d = x.shape
    tr = _pick_tile(t, tr)
    return pl.pallas_call(
        _final_norm_kernel,
        grid=(b, t // tr),
        in_specs=[pl.BlockSpec((None, tr, d), lambda bi, i: (bi, i, 0)),
                  pl.BlockSpec((1, d), lambda bi, i: (0, 0))],
        out_specs=pl.BlockSpec((None, tr, d), lambda bi, i: (bi, i, 0)),
        out_shape=jax.ShapeDtypeStruct((b, t, d), F32),
        compiler_params=_cparams("parallel", "parallel"),
        name="final_norm",
    )(x, g.reshape(1, d))


def _dwconv_kernel(x_ref, xp_ref, xn_ref, w_ref, b_ref, o_ref, *, act, nblk):
    i = pl.program_id(1)
    x = x_ref[...].astype(F32)
    tr = x.shape[0]
    halo = xp_ref.shape[0]
    prev_row = jnp.where(i == 0, 0.0, xp_ref[halo - 1:halo, :].astype(F32))
    next_row = jnp.where(i == nblk - 1, 0.0, xn_ref[0:1, :].astype(F32))
    rows = lax.broadcasted_iota(jnp.int32, x.shape, 0)
    x_m1 = jnp.where(rows == 0, prev_row, pltpu.roll(x, 1, 0))
    x_p1 = jnp.where(rows == tr - 1, next_row, pltpu.roll(x, tr - 1, 0))
    y = b_ref[...] + x_m1 * w_ref[0:1, :] + x * w_ref[1:2, :] + x_p1 * w_ref[2:3, :]
    if act:
        y = _silu(y)
    o_ref[...] = y


def dwconv3(x, w, bias, act, tr=512, tc=1536):
    b, t, c = x.shape
    tr = _pick_tile(t, tr)
    tc = min(tc, c)
    assert c % tc == 0
    nblk = t // tr
    halo = SUBLANES * (4 // x.dtype.itemsize)
    rh = tr // halo
    lasth = t // halo - 1
    return pl.pallas_call(
        functools.partial(_dwconv_kernel, act=act, nblk=nblk),
        grid=(b, nblk, c // tc),
        in_specs=[pl.BlockSpec((None, tr, tc), lambda bi, i, j: (bi, i, j)),
                  pl.BlockSpec((None, halo, tc), lambda bi, i, j: (bi, jnp.maximum(i * rh - 1, 0), j)),
                  pl.BlockSpec((None, halo, tc), lambda bi, i, j: (bi, jnp.minimum((i + 1) * rh, lasth), j)),
                  pl.BlockSpec((3, tc), lambda bi, i, j: (0, j)),
                  pl.BlockSpec((1, tc), lambda bi, i, j: (0, j))],
        out_specs=pl.BlockSpec((None, tr, tc), lambda bi, i, j: (bi, i, j)),
        out_shape=jax.ShapeDtypeStruct((b, t, c), F32),
        compiler_params=_cparams("parallel", "parallel", "parallel"),
        name="dwconv3",
    )(x, x, x, w, bias.reshape(1, c))


def _ssd_kernel(x_ref, b_ref, c_ref, dt_ref, bias_ref, aneg_ref, e_ref, h0_ref,
                y_ref, hout_ref, st_ref, *, direction, nchunks):
    ci = pl.program_id(1)

    @pl.when(ci == 0)
    def _():
        st_ref[...] = h0_ref[...]

    q = x_ref.shape[0]
    hd = SSD_HEAD_DIM
    gw = (SSD_HEADS // SSD_GROUPS) * hd
    x = x_ref[...]
    dtv = _softplus(dt_ref[...] + bias_ref[...])
    adt = dtv * aneg_ref[...]
    row = lax.broadcasted_iota(jnp.int32, (q, q), 0)
    col = lax.broadcasted_iota(jnp.int32, (q, q), 1)
    mask = (col >= row) if direction else (col <= row)
    tri = mask.astype(BF16)
    acum = sum(_dot(tri, p) for p in _split3(adt))
    acum_t = sum(_dot_nt(p, tri) for p in _split3(adt.T))
    total = jnp.sum(adt, axis=0, keepdims=True)
    eac = jnp.exp(acum)
    dte = jnp.exp(total - acum)
    cd = jnp.broadcast_to(jnp.exp(total), (SUBLANES, LANES))
    stack = jnp.concatenate([dtv, eac, dte, cd], axis=0)
    e01 = e_ref[...]
    hi = stack.astype(BF16)
    lo = (stack - hi.astype(F32)).astype(BF16)
    ex = _dot(hi, e01) + _dot(lo, e01)
    dt_e, eac_e, dte_e, cd_e = ex[:q], ex[q:2 * q], ex[2 * q:3 * q], ex[3 * q:3 * q + 1]
    xdt = x * dt_e
    lane = lax.broadcasted_iota(jnp.int32, (q, LANES), 1)
    first = lane < hd
    for g in range(SSD_GROUPS):
        gs = slice(g * gw, (g + 1) * gw)
        c_g = c_ref[:, g * SSD_STATE:(g + 1) * SSD_STATE].astype(BF16)
        b_g32 = b_ref[:, g * SSD_STATE:(g + 1) * SSD_STATE]
        s_g = st_ref[:, gs]
        y_off = _dot(c_g, s_g.astype(BF16)) * eac_e[:, gs]
        cb = _dot_nt(c_g, b_g32.astype(BF16))
        for jp in range(gw // LANES):
            h_a = g * (SSD_HEADS // SSD_GROUPS) + 2 * jp
            ls = slice(h_a * hd, h_a * hd + LANES)
            xpair = xdt[:, ls]
            acc = y_off[:, jp * LANES:(jp + 1) * LANES]
            for k in range(2):
                cix = direction * SSD_HEADS + h_a + k
                seg = acum[:, cix:cix + 1] - acum_t[cix:cix + 1, :]
                dec = jnp.where(mask, jnp.exp(jnp.where(mask, seg, 0.0)), 0.0)
                w = (cb * dec).astype(BF16)
                xk = jnp.where(first if k == 0 else jnp.logical_not(first), xpair, 0.0).astype(BF16)
                acc = acc + _dot(w, xk)
            y_ref[:, ls] = acc
        xdte = (xdt[:, gs] * dte_e[:, gs]).astype(BF16)
        st_ref[:, gs] = s_g * cd_e[:, gs] + _dot(b_g32.T.astype(BF16), xdte)

    @pl.when(ci == nchunks - 1)
    def _():
        hout_ref[...] = st_ref[...]


def ssd_scan(xbc, dt_raw, dt_bias_row, aneg_row, expand, h0, direction):
    b, t, _ = xbc.shape
    q = SSD_CHUNK
    nchunks = t // q
    inner = SSD_HEADS * SSD_HEAD_DIM
    bn = SSD_GROUPS * SSD_STATE
    cmap = (lambda c: nchunks - 1 - c) if direction else (lambda c: c)
    y, hout = pl.pallas_call(
        functools.partial(_ssd_kernel, direction=direction, nchunks=nchunks),
        grid=(b, nchunks),
        in_specs=[pl.BlockSpec((None, q, inner), lambda bi, c: (bi, cmap(c), 0)),
                  pl.BlockSpec((None, q, bn), lambda bi, c: (bi, cmap(c), inner // bn)),
                  pl.BlockSpec((None, q, bn), lambda bi, c: (bi, cmap(c), inner // bn + 1)),
                  pl.BlockSpec((None, q, LANES), lambda bi, c: (bi, cmap(c), 0)),
                  pl.BlockSpec((1, LANES), lambda bi, c: (0, 0)),
                  pl.BlockSpec((1, LANES), lambda bi, c: (0, 0)),
                  pl.BlockSpec((LANES, inner), lambda bi, c: (0, 0)),
                  pl.BlockSpec((None, SSD_STATE, inner), lambda bi, c: (bi, 0, 0))],
        out_specs=[pl.BlockSpec((None, q, inner), lambda bi, c: (bi, cmap(c), 0)),
                   pl.BlockSpec((None, SSD_STATE, inner), lambda bi, c: (bi, 0, 0))],
        out_shape=[jax.ShapeDtypeStruct((b, t, inner), F32),
                   jax.ShapeDtypeStruct((b, SSD_STATE, inner), F32)],
        scratch_shapes=[pltpu.VMEM((SSD_STATE, inner), F32)],
        compiler_params=_cparams("parallel", "arbitrary"),
        name="ssd_scan_bwd" if direction else "ssd_scan_fwd",
    )(xbc, xbc, xbc, dt_raw, dt_bias_row, aneg_row, expand, h0)
    return y, hout


def _ssd_out_kernel(yf_ref, yb_ref, x_ref, z_ref, d_ref, w_ref, o_ref):
    y = yf_ref[...] + yb_ref[...] + d_ref[...] * x_ref[...]
    y = y * _silu(z_ref[...].astype(F32))
    gw = y.shape[1] // SSD_GROUPS
    for g in range(SSD_GROUPS):
        gs = slice(g * gw, (g + 1) * gw)
        yg = y[:, gs]
        yg = yg * lax.rsqrt(jnp.mean(yg * yg, axis=-1, keepdims=True) + EPS)
        o_ref[:, gs] = (yg * w_ref[:, gs]).astype(o_ref.dtype)


def ssd_out(yf, yb, xbc, z, d_row, w_row, tr=256):
    b, t, inner = yf.shape
    tr = _pick_tile(t, tr)
    blk = pl.BlockSpec((None, tr, inner), lambda bi, i: (bi, i, 0))
    vec = pl.BlockSpec((1, inner), lambda bi, i: (0, 0))
    return pl.pallas_call(
        _ssd_out_kernel,
        grid=(b, t // tr),
        in_specs=[blk, blk, blk, blk, vec, vec],
        out_specs=blk,
        out_shape=jax.ShapeDtypeStruct((b, t, inner), BF16),
        compiler_params=_cparams("parallel", "parallel"),
        name="ssd_out",
    )(yf, yb, xbc, z, d_row, w_row)


def _taps_kernel(f_ref, tv_ref, w1_ref, b1_ref, w2_ref, b2_ref, w3_ref, fr_ref, dl_ref, o_ref):
    hp = lax.Precision.HIGHEST
    fr = fr_ref[...]
    h = jnp.sin(fr * (jnp.dot(f_ref[...], w1_ref[...], precision=hp, preferred_element_type=F32) + b1_ref[...]))
    h = jnp.sin(fr * (jnp.dot(h, w2_ref[...], precision=hp, preferred_element_type=F32) + b2_ref[...]))
    filt = _dot(h.astype(BF16), w3_ref[...].astype(BF16))
    c = dl_ref.shape[1]
    win = jnp.exp(-tv_ref[:, 0:1] * dl_ref[...]) * tv_ref[:, 1:2]
    o_ref[0] = filt[:, :c] * win
    o_ref[1] = filt[:, c:] * win


def hyena_taps(length, w1, b1, w2, b2, w3, freq):
    hid = w1.shape[1]
    c = w3.shape[1] // (2 * HY_ORDER)
    n = 2 * length
    t = jnp.linspace(0.0, 1.0, length, dtype=F32)[:, None]
    phase = 2.0 * math.pi * jnp.arange(length, dtype=F32)[:, None] / length
    bands = jnp.linspace(1e-4, HY_BANDS - 1, HY_BANDS, dtype=F32)[None, :]
    feats = jnp.concatenate([t, jnp.cos(phase * bands), -jnp.sin(phase * bands)], axis=-1)
    nf = feats.shape[1]
    rev = jnp.concatenate([jnp.zeros((1,), jnp.int32), jnp.arange(length - 1, 0, -1, dtype=jnp.int32)])
    feats2 = jnp.concatenate([feats, feats[rev]], axis=0)
    feats2 = jnp.pad(feats2, ((0, 0), (0, LANES - nf)))
    valid = jnp.ones((n,), F32).at[length].set(0.0)
    tv = jnp.stack([jnp.concatenate([t[:, 0], t[rev, 0]]), valid], axis=1)
    max_decay = math.log(HY_DECAY_TARGET) / HY_SHORT_DECAY_PCT
    min_decay = math.log(HY_DECAY_TARGET) / HY_LONG_DECAY_PCT
    deltas = jnp.abs(jnp.linspace(min_decay, max_decay, c, dtype=F32))[None, :]
    w1p = jnp.pad(w1, ((0, LANES - nf), (0, 0)))
    w3r = w3.reshape(hid, HY_ORDER, 2, c).transpose(2, 0, 1, 3).reshape(2, hid, HY_ORDER * c)
    tr = _pick_tile(length, 1024)
    nhalf = length // tr
    return pl.pallas_call(
        _taps_kernel,
        grid=(n // tr,),
        in_specs=[pl.BlockSpec((tr, LANES), lambda i: (i, 0)),
                  pl.BlockSpec((tr, 2), lambda i: (i, 0)),
                  pl.BlockSpec((LANES, hid), lambda i: (0, 0)),
                  pl.BlockSpec((1, hid), lambda i: (0, 0)),
                  pl.BlockSpec((hid, hid), lambda i: (0, 0)),
                  pl.BlockSpec((1, hid), lambda i: (0, 0)),
                  pl.BlockSpec((None, hid, HY_ORDER * c), lambda i: (i // nhalf, 0, 0)),
                  pl.BlockSpec((1, hid), lambda i: (0, 0)),
                  pl.BlockSpec((1, c), lambda i: (0, 0))],
        out_specs=pl.BlockSpec((HY_ORDER, tr, c), lambda i: (0, i, 0)),
        out_shape=jax.ShapeDtypeStruct((HY_ORDER, n, c), F32),
        compiler_params=_cparams("parallel"),
        name="hyena_taps",
    )(feats2, tv, w1p, b1.reshape(1, hid), w2, b2.reshape(1, hid), w3r, freq.reshape(1, hid), deltas)


def _cs(num, den):
    ang = (2.0 * math.pi / den) * num.astype(F32)
    return jnp.cos(ang), jnp.sin(ang)


def fft_tables(n1, n2):
    n = n1 * n2
    h = n1 // 2
    k1 = jnp.arange(n1, dtype=jnp.int32)
    c1, s1 = _cs((k1[:, None] * k1[None, :]) % n1, n1)
    g_u = jnp.block([[c1[:, :h], s1[:, :h]], [-s1[:, :h], c1[:, :h]]])
    g_t = jnp.concatenate([c1, -s1], axis=0)
    g_i = jnp.block([[c1[:h, :], -s1[:h, :]], [s1[:h, :], c1[:h, :]]])
    s = jnp.arange(n2, dtype=jnp.int32)
    num = (n1 * (s[:, None] * s[None, :]))[None] + (k1[:, None, None] * s[None, None, :])
    cb, sb = _cs(num % n, n)
    m_f = jnp.concatenate([jnp.concatenate([cb, sb], axis=2),
                           jnp.concatenate([-sb, cb], axis=2)], axis=1)
    cbt, sbt = jnp.swapaxes(cb, 1, 2) / n, jnp.swapaxes(sb, 1, 2) / n
    m_i = jnp.concatenate([jnp.concatenate([cbt, -sbt], axis=2),
                           jnp.concatenate([sbt, cbt], axis=2)], axis=1)
    return dict(g_u=g_u.astype(BF16), g_t=g_t.astype(BF16), g_i=g_i.astype(BF16),
                m_f=m_f.astype(BF16), m_i=m_i.astype(BF16))


FFT_KGROUP = SUBLANES


def _rows_get(ref, start, n, stride):
    return ref.reshape(math.prod(ref.shape[:-1]), ref.shape[-1])[pl.ds(start, n, stride=stride), :]


def _rows_set(ref, start, n, stride, val):
    ref.reshape(math.prod(ref.shape[:-1]), ref.shape[-1])[pl.ds(start, n, stride=stride), :] = val


def _cpack(re, im):
    rb = lax.bitcast_convert_type(re.astype(BF16).astype(F32), jnp.uint32)
    ib = lax.bitcast_convert_type(im.astype(BF16).astype(F32), jnp.uint32)
    return (rb & jnp.uint32(0xFFFF0000)) | (ib >> 16)


def _cunpack(w):
    re = lax.bitcast_convert_type(w & jnp.uint32(0xFFFF0000), F32)
    im = lax.bitcast_convert_type(w << 16, F32)
    return re, im


def _fft_a_kernel(g_ref, xa_ref, xb_ref, o_ref, *, s_major_in):
    h = g_ref.shape[1] // 2
    ts, n1, _ = o_ref.shape
    ga = g_ref[:, :h]
    gb = g_ref[:, h:]
    for s in range(ts):
        xa = xa_ref[s] if s_major_in else _rows_get(xa_ref, s, h, ts)
        xb = xb_ref[s] if s_major_in else _rows_get(xb_ref, s, h, ts)
        r = _dot(ga, xa.astype(BF16)) + _dot(gb, xb.astype(BF16))
        o_ref[s] = _cpack(r[:n1], r[n1:])


def fft_stage_a(g, x4, sel_a, sel_b, col_off, c, s_major_in=False, tc=LANES):
    n1 = g.shape[0] // 2
    h = n1 // 2
    n2 = x4.shape[1] if s_major_in else x4.shape[2]
    ts = SUBLANES
    coff = col_off // tc
    if s_major_in:
        spec = lambda sel: pl.BlockSpec((None, ts, h, tc), lambda j, s: (sel[0], s, sel[1], j + coff))
    else:
        spec = lambda sel: pl.BlockSpec((None, h, ts, tc), lambda j, s: (sel[0], sel[1], s, j + coff))
    return pl.pallas_call(
        functools.partial(_fft_a_kernel, s_major_in=s_major_in),
        grid=(c // tc, n2 // ts),
        in_specs=[pl.BlockSpec((2 * n1, n1), lambda j, s: (0, 0)), spec(sel_a), spec(sel_b)],
        out_specs=pl.BlockSpec((ts, n1, tc), lambda j, s: (s, 0, j)),
        out_shape=jax.ShapeDtypeStruct((n2, n1, c), jnp.uint32),
        compiler_params=_cparams("parallel", "parallel"),
        name="fft_stage_a",
    )(g, x4, x4)


def _fft_load_k(a_ref, j):
    n2, kb, _ = a_ref.shape
    re, im = _cunpack(_rows_get(a_ref, j, n2, kb))
    return jnp.concatenate([re, im], axis=0).astype(BF16)


def _fft_bh_kernel(m_ref, *refs):
    *a_refs, o_ref = refs
    n2, kb, _ = a_refs[0].shape
    for g, a_ref in enumerate(a_refs):
        for j in range(kb):
            x = _dot(m_ref[g * kb + j], _fft_load_k(a_ref, j))
            o_ref[g * kb + j] = _cpack(x[:n2], x[n2:])


def _fft_b_kernel(mf_ref, mi_ref, *refs):
    *a_refs, h_ref, o_ref = refs
    n2, kb, _ = a_refs[0].shape
    for g, a_ref in enumerate(a_refs):
        for j in range(kb):
            k = g * kb + j
            x = _dot(mf_ref[k], _fft_load_k(a_ref, j))
            xr, xi = x[:n2], x[n2:]
            hr, hi = _cunpack(h_ref[k])
            y = jnp.concatenate([xr * hr - xi * hi, xr * hi + xi * hr], axis=0).astype(BF16)
            r = _dot(mi_ref[k], y)
            o_ref[k] = _cpack(r[:n2], r[n2:])


def fft_stage_b(m_f, m_i, a, hspec, tc=LANES, groups=2):
    n2, n1, c = a.shape
    groups = min(groups, n1 // FFT_KGROUP)
    kb = FFT_KGROUP * groups
    ablks = [pl.BlockSpec((n2, FFT_KGROUP, tc), functools.partial(lambda k, j, g: (0, groups * k + g, j), g=g))
             for g in range(groups)]
    kblk = pl.BlockSpec((kb, n2, tc), lambda k, j: (k, 0, j))
    mat = pl.BlockSpec((kb, 2 * n2, 2 * n2), lambda k, j: (k, 0, 0))
    if hspec is None:
        kern, in_specs, args, name = _fft_bh_kernel, [mat] + ablks, (m_f,) + (a,) * groups, "fft_stage_b_spectrum"
    else:
        kern, in_specs, name = _fft_b_kernel, [mat, mat] + ablks + [kblk], "fft_stage_b"
        args = (m_f, m_i) + (a,) * groups + (hspec,)
    return pl.pallas_call(
        kern,
        grid=(n1 // kb, c // tc),
        in_specs=in_specs,
        out_specs=kblk,
        out_shape=jax.ShapeDtypeStruct((n1, n2, c), jnp.uint32),
        compiler_params=_cparams("parallel", "parallel"),
        name=name,
    )(*args)


def _fft_ai_kernel(g_ref, b_ref, u_ref, x_ref, skip_ref, o_ref, *, u_s_major, out_s_major):
    n1, ts, _ = b_ref.shape
    h = n1 // 2
    ga = g_ref[:, :n1]
    gb = g_ref[:, n1:]
    skip = skip_ref[...]
    for s in range(ts):
        br, bi_ = _cunpack(_rows_get(b_ref, s, n1, ts))
        y = _dot(ga, br.astype(BF16)) + _dot(gb, bi_.astype(BF16))
        for bi in range(2):
            start = bi * h * ts + s
            u = u_ref[bi, s] if u_s_major else _rows_get(u_ref, start, h, ts)
            val = _rows_get(x_ref, start, h, ts) * (y[bi * h:(bi + 1) * h] + u * skip)
            if out_s_major:
                o_ref[bi, s] = val
            else:
                _rows_set(o_ref, start, h, ts, val)


def fft_stage_a_inv(g_i, bm, u4, u_off, u_s_major, x4, x_off, skip_row, out_s_major, tc=LANES):
    n1, n2, c = bm.shape
    h = n1 // 2
    ts = SUBLANES
    uo, xo = u_off // tc, x_off // tc
    nat = lambda off: pl.BlockSpec((2, h, ts, tc), lambda j, s: (0, 0, s, j + off))
    smj = lambda off: pl.BlockSpec((2, ts, h, tc), lambda j, s: (0, s, 0, j + off))
    return pl.pallas_call(
        functools.partial(_fft_ai_kernel, u_s_major=u_s_major, out_s_major=out_s_major),
        grid=(c // tc, n2 // ts),
        in_specs=[pl.BlockSpec((n1, 2 * n1), lambda j, s: (0, 0)),
                  pl.BlockSpec((n1, ts, tc), lambda j, s: (0, s, j)),
                  smj(uo) if u_s_major else nat(uo),
                  nat(xo),
                  pl.BlockSpec((1, tc), lambda j, s: (0, j))],
        out_specs=smj(0) if out_s_major else nat(0),
        out_shape=jax.ShapeDtypeStruct((2, n2, h, c) if out_s_major else (2, h, n2, c), F32),
        compiler_params=_cparams("parallel", "parallel"),
        name="fft_stage_a_inv",
    )(g_i, bm, u4, x4, skip_row)


def hyena_long(u, taps, skip, tabs):
    b, length, c3 = u.shape
    assert b == 2, "the two batch rows are packed as one complex sequence"
    c = c3 // 3
    n2 = FFT_N2
    n1 = 2 * length // n2
    h = n1 // 2
    u4 = u.reshape(b, h, n2, c3)
    taps4 = taps.reshape(HY_ORDER * 2, h, n2, c)
    zin, zoff, z_s_major = u4, 0, False
    out = None
    for conv in range(HY_ORDER):
        last = conv == HY_ORDER - 1
        hspec = fft_stage_b(tabs["m_f"], None,
                            fft_stage_a(tabs["g_t"], taps4, (2 * conv, 0), (2 * conv + 1, 0), 0, c), None)
        a = fft_stage_a(tabs["g_u"], zin, (0, 0), (1, 0), zoff, c, s_major_in=z_s_major)
        bm = fft_stage_b(tabs["m_f"], tabs["m_i"], a, hspec)
        out = fft_stage_a_inv(tabs["g_i"], bm, zin, zoff, z_s_major, u4, (conv + 1) * c,
                              skip[conv].reshape(1, c), not last)
        zin, zoff, z_s_major = out, 0, True
    return out.reshape(b, length, c)


def _hyena_ctx_kernel(u_ref, x1_ref, x2_ref, taps_ref, fc_ref, fs_ref, skip_ref, o_ref):
    length = u_ref.shape[1]
    n = 2 * length
    fc = fc_ref[...]
    fs = fs_ref[...]
    fcl, fsl = fc[:, :length], fs[:, :length]
    fct, fst = fc[:length, :], fs[:length, :]
    zr = u_ref[0]
    zi = u_ref[1]
    gates = (x1_ref, x2_ref)
    for conv in range(HY_ORDER):
        tp = taps_ref[conv].astype(BF16)
        hr, hi = _dot(fc, tp), -_dot(fs, tp)
        zrb, zib = zr.astype(BF16), zi.astype(BF16)
        xr = _dot(fcl, zrb) + _dot(fsl, zib)
        xi = _dot(fcl, zib) - _dot(fsl, zrb)
        yr = (xr * hr - xi * hi).astype(BF16)
        yi = (xr * hi + xi * hr).astype(BF16)
        cr = (_dot(fct, yr) - _dot(fst, yi)) * (1.0 / n)
        ci = (_dot(fct, yi) + _dot(fst, yr)) * (1.0 / n)
        sk = skip_ref[conv:conv + 1, :]
        zr = gates[conv][0] * (cr + zr * sk)
        zi = gates[conv][1] * (ci + zi * sk)
    o_ref[0] = zr.astype(o_ref.dtype)
    o_ref[1] = zi.astype(o_ref.dtype)


def hyena_ctx(u, taps, skip, tc=256):
    b, length, c3 = u.shape
    assert b == 2
    c = c3 // 3
    n = 2 * length
    k = jnp.arange(n, dtype=jnp.int32)
    fc, fs = _cs((k[:, None] * k[None, :]) % n, n)
    nb = c // tc
    ub = lambda off: pl.BlockSpec((2, length, tc), lambda j: (0, 0, j + off * nb))
    mat = pl.BlockSpec((n, n), lambda j: (0, 0))
    return pl.pallas_call(
        _hyena_ctx_kernel,
        grid=(nb,),
        in_specs=[ub(0), ub(1), ub(2),
                  pl.BlockSpec((HY_ORDER, n, tc), lambda j: (0, 0, j)),
                  mat, mat,
                  pl.BlockSpec((HY_ORDER, tc), lambda j: (0, j))],
        out_specs=pl.BlockSpec((2, length, tc), lambda j: (0, 0, j)),
        out_shape=jax.ShapeDtypeStruct((b, length, c), BF16),
        compiler_params=_cparams("parallel"),
        name="hyena_ctx",
    )(u, u, u, taps, fc.astype(BF16), fs.astype(BF16), skip)


def rope_tables(length):
    rows = length // GRID_W
    row = jnp.repeat(jnp.arange(rows, dtype=F32), GRID_W)
    col = (jnp.arange(length) % GRID_W).astype(F32)
    inv = ROPE_THETA ** (-jnp.arange(0, ROPE_AXIS_DIM, 2, dtype=F32) / ROPE_AXIS_DIM)
    ang = jnp.stack([row[:, None] * inv, col[:, None] * inv], axis=1)
    cos, sin = jnp.cos(ang), jnp.sin(ang)
    zero = jnp.zeros_like(sin)
    per_map = lambda lo, hi: jnp.concatenate([lo, hi], axis=-1).reshape(length, 2 * ROPE_AXIS_DIM)
    reps = LANES // (2 * ROPE_AXIS_DIM)
    c_t = jnp.tile(per_map(cos, cos), (1, reps))
    sm_t = jnp.tile(per_map(-sin, zero), (1, reps))
    sp_t = jnp.tile(per_map(zero, sin), (1, reps))
    return c_t, sm_t, sp_t


def _rope_kernel(x_ref, c_ref, sm_ref, sp_ref, o_ref, *n_refs, scale):
    half = ROPE_AXIS_DIM // 2
    c, sm, sp = c_ref[...], sm_ref[...], sp_ref[...]
    for g in range(x_ref.shape[1] // LANES):
        ls = slice(g * LANES, (g + 1) * LANES)
        x = x_ref[:, ls].astype(F32)
        y = (x * c + pltpu.roll(x, LANES - half, 1) * sm + pltpu.roll(x, half, 1) * sp) * scale
        o_ref[:, ls] = y.astype(o_ref.dtype)
        if n_refs:
            n2 = jnp.max(jnp.sum(y * y, axis=1, keepdims=True), axis=0, keepdims=True)
            n_refs[0][g:g + 1, :] = jnp.broadcast_to(n2, (1, LANES))


def rope(x, tables, scale, with_norms=False, tr=512):
    b, t, d = x.shape
    tr = _pick_tile(t, tr)
    nh = d // LANES
    tab = pl.BlockSpec((tr, LANES), lambda bi, i: (i, 0))
    out_specs = [pl.BlockSpec((None, tr, d), lambda bi, i: (bi, i, 0))]
    out_shape = [jax.ShapeDtypeStruct((b, t, d), BF16)]
    if with_norms:
        out_specs.append(pl.BlockSpec((None, None, nh, LANES), lambda bi, i: (bi, i, 0, 0)))
        out_shape.append(jax.ShapeDtypeStruct((b, t // tr, nh, LANES), F32))
    outs = pl.pallas_call(
        functools.partial(_rope_kernel, scale=scale),
        grid=(b, t // tr),
        in_specs=[pl.BlockSpec((None, tr, d), lambda bi, i: (bi, i, 0)), tab, tab, tab],
        out_specs=out_specs,
        out_shape=out_shape,
        compiler_params=_cparams("parallel", "parallel"),
        name="rope",
    )(x, *tables)
    if with_norms:
        return outs[0], jnp.swapaxes(outs[1], 1, 2)
    return outs[0]


def _scale_cast_kernel(x_ref, o_ref, *, scale):
    o_ref[...] = (x_ref[...].astype(F32) * scale).astype(o_ref.dtype)


def scale_cast(x, scale, tr=256):
    b, t, d = x.shape
    tr = _pick_tile(t, tr)
    blk = pl.BlockSpec((None, tr, d), lambda bi, i: (bi, i, 0))
    return pl.pallas_call(
        functools.partial(_scale_cast_kernel, scale=scale),
        grid=(b, t // tr), in_specs=[blk], out_specs=blk,
        out_shape=jax.ShapeDtypeStruct((b, t, d), BF16),
        compiler_params=_cparams("parallel", "parallel"),
        name="scale_cast",
    )(x)


ATTN_GUARD_LOG2 = 80.0
ATTN_NORM_SLACK = 1.03


def _attn_kernel(*refs, lam_init, nk, with_ctx):
    if with_ctx:
        (q_ref, k_ref, v_ref, kc_ref, vc_ref, kn_ref, lam_ref, w_ref, o_ref,
         q2_ref, m_ref, c_ref, acc_ref, p_scr, flag_ref) = refs
    else:
        q_ref, k_ref, v_ref, lam_ref, w_ref, o_ref, q2_ref, m_ref, c_ref, acc_ref, p_scr, flag_ref = refs
    ki = pl.program_id(3)
    tq = q_ref.shape[0]

    def with_ones(v):
        lane = lax.broadcasted_iota(jnp.int32, v.shape, 1)
        return jnp.concatenate([v, (lane == 0).astype(BF16)], axis=1)

    def exact_update(k, v, p_buf):
        s = _dot_nt(q2_ref[...], k)
        m_prev = m_ref[...]
        m_new = jnp.maximum(m_prev, jnp.max(s, axis=1, keepdims=True))
        alpha = jnp.exp2(m_prev - m_new)
        p_buf[...] = jnp.exp2(s - m_new).astype(BF16)
        acc_ref[...] = alpha * acc_ref[...] + _dot(p_buf[...], with_ones(v))
        m_ref[...] = m_new

    def fast_update(k, v, p_buf):
        s = _dot_nt(q2_ref[...], k)
        p_buf[...] = jnp.exp2(s - pltpu.repeat(c_ref[...], k.shape[0] // LANES, axis=1)).astype(BF16)
        acc_ref[...] += _dot(p_buf[...], with_ones(v))

    @pl.when(ki == 0)
    def _():
        q = q_ref[...]
        lane = lax.broadcasted_iota(jnp.int32, q.shape, 1)
        zero = jnp.zeros_like(q)
        q2_ref[:tq, :] = jnp.where(lane < DA_HEAD_DIM, q, zero)
        q2_ref[tq:, :] = jnp.where(lane >= DA_HEAD_DIM, q, zero)
        m_ref[...] = jnp.full(m_ref.shape, -jnp.inf, F32)
        acc_ref[...] = jnp.zeros(acc_ref.shape, F32)
        flag_ref[0] = 0
        if with_ctx:
            q2 = q2_ref[...]
            c = _dot_nt(q2, jnp.broadcast_to(kc_ref[0:1, :], (LANES, DA_V_DIM)))
            sq = q2.astype(F32) * q2.astype(F32)
            hi = sq.astype(BF16)
            lo = (sq - hi.astype(F32)).astype(BF16)
            ones = jnp.ones((DA_V_DIM, LANES), BF16)
            qn2 = _dot(hi, ones) + _dot(lo, ones)
            kn2 = jnp.max(kn_ref[...])
            gap = jnp.max(jnp.sqrt(qn2 * kn2) * ATTN_NORM_SLACK - c)
            c_ref[...] = c
            flag_ref[0] = (gap <= ATTN_GUARD_LOG2).astype(jnp.int32)

    fast = flag_ref[0] == 1
    if with_ctx:
        tc = kc_ref.shape[0]
        first = ki == 0

        @pl.when(jnp.logical_and(first, fast))
        def _():
            fast_update(kc_ref[...], vc_ref[...], p_scr.at[:, pl.ds(0, tc)])

        @pl.when(jnp.logical_and(first, jnp.logical_not(fast)))
        def _():
            exact_update(kc_ref[...], vc_ref[...], p_scr.at[:, pl.ds(0, tc)])

    @pl.when(fast)
    def _():
        fast_update(k_ref[...], v_ref[...], p_scr)

    @pl.when(jnp.logical_not(fast))
    def _():
        exact_update(k_ref[...], v_ref[...], p_scr)

    @pl.when(ki == nk - 1)
    def _():
        lv = lam_ref[...]
        e1 = jnp.exp(jnp.sum(lv[0:1, :] * lv[1:2, :], axis=1, keepdims=True))
        e2 = jnp.exp(jnp.sum(lv[2:3, :] * lv[3:4, :], axis=1, keepdims=True))
        lam = e1 - e2 + lam_init
        acc = acc_ref[...]
        o = acc[:, :DA_V_DIM] / acc[:, DA_V_DIM:DA_V_DIM + 1]
        o = o[:tq] - lam * o[tq:]
        o_ref[...] = (_rms(o, w_ref[...]) * (1.0 - lam_init)).astype(o_ref.dtype)


def diff_attention(q, k, v, k_ctx, v_ctx, knorm, lam_vec, subln_w, lam_init, tq=1024, tk=2048):
    b, t, d = q.shape
    nh = d // DA_V_DIM
    tq = _pick_tile(t, tq)
    tk = _pick_tile(k.shape[1], tk)
    nk = k.shape[1] // tk
    with_ctx = k_ctx is not None
    assert not with_ctx or k_ctx.shape[1] <= tk
    qspec = pl.BlockSpec((None, tq, DA_V_DIM), lambda bi, h, i, j: (bi, i, h))
    kspec = pl.BlockSpec((None, tk, DA_V_DIM), lambda bi, h, i, j: (bi, j, h))
    in_specs = [qspec, kspec, kspec]
    args = [q, k, v]
    if with_ctx:
        cspec = pl.BlockSpec((None, k_ctx.shape[1], DA_V_DIM), lambda bi, h, i, j: (bi, 0, h))
        nspec = pl.BlockSpec((None, None) + knorm.shape[2:], lambda bi, h, i, j: (bi, h, 0, 0))
        in_specs += [cspec, cspec, nspec]
        args += [k_ctx, v_ctx, knorm]
    in_specs += [pl.BlockSpec((4, DA_HEAD_DIM), lambda bi, h, i, j: (0, 0)),
                 pl.BlockSpec((1, DA_V_DIM), lambda bi, h, i, j: (0, 0))]
    args += [lam_vec, subln_w.reshape(1, DA_V_DIM)]
    return pl.pallas_call(
        functools.partial(_attn_kernel, lam_init=lam_init, nk=nk, with_ctx=with_ctx),
        grid=(b, nh, t // tq, nk),
        in_specs=in_specs,
        out_specs=qspec,
        out_shape=jax.ShapeDtypeStruct((b, t, d), BF16),
        scratch_shapes=[pltpu.VMEM((2 * tq, DA_V_DIM), BF16),
                        pltpu.VMEM((2 * tq, 1), F32),
                        pltpu.VMEM((2 * tq, LANES), F32),
                        pltpu.VMEM((2 * tq, 2 * DA_V_DIM), F32),
                        pltpu.VMEM((2 * tq, tk), BF16),
                        pltpu.SMEM((1,), jnp.int32)],
        compiler_params=_cparams("parallel", "parallel", "parallel", "arbitrary"),
        name="diff_attention",
    )(*args)


def _merge_kernel(ya_ref, yb_ref, yc_ref, g_ref, wb_ref, wo_ref, x_ref, mod_ref, o_ref):
    d = x_ref.shape[1]
    ys = (ya_ref, yb_ref, yc_ref)
    s = None
    for k in range(N_BRANCH):
        t = jax.nn.sigmoid(g_ref[:, k * d:(k + 1) * d].astype(F32)) * _dot(ys[k][...].astype(BF16), wb_ref[k])
        s = t if s is None else s + t
    r = _dot(s.astype(BF16), wo_ref[...])
    o_ref[...] = x_ref[...] + mod_ref[2:3, :] * r


def merge(ya, yb, yc, gates, wb, wo, x, mods, tr=256):
    b, t, d = x.shape
    tr = _pick_tile(t, tr)
    per_batch = mods.shape[0] > 1
    blk = pl.BlockSpec((None, tr, d), lambda bi, i: (bi, i, 0))
    return pl.pallas_call(
        _merge_kernel,
        grid=(b, t // tr),
        in_specs=[blk, blk, blk,
                  pl.BlockSpec((None, tr, N_BRANCH * d), lambda bi, i: (bi, i, 0)),
                  pl.BlockSpec((N_BRANCH, d, d), lambda bi, i: (0, 0, 0)),
                  pl.BlockSpec((d, d), lambda bi, i: (0, 0)),
                  blk,
                  pl.BlockSpec((None, 8, d), (lambda bi, i: (bi, 0, 0)) if per_batch else (lambda bi, i: (0, 0, 0)))],
        out_specs=blk,
        out_shape=jax.ShapeDtypeStruct((b, t, d), F32),
        compiler_params=_cparams("parallel", "parallel"),
        name="merge",
    )(ya, yb, yc, gates, wb, wo, x, mods)


def _ffn_kernel(x_ref, g_ref, mod_ref, w13_ref, w2_ref, o_ref, *, nchunk):
    x = x_ref[...]
    h = (_rms(x, g_ref[...]) * (1.0 + mod_ref[4:5, :]) + mod_ref[3:4, :]).astype(BF16)
    f = w2_ref.shape[0]
    fc = f // nchunk
    acc = None
    for c in range(nchunk):
        a1 = _dot(h, w13_ref[:, c * fc:(c + 1) * fc])
        a3 = _dot(h, w13_ref[:, f + c * fc:f + (c + 1) * fc])
        t = _dot((_silu(a1) * a3).astype(BF16), w2_ref[c * fc:(c + 1) * fc, :])
        acc = t if acc is None else acc + t
    o_ref[...] = x + mod_ref[5:6, :] * acc


def ffn(x, g, mods, w13, w2, tr=256):
    b, t, d = x.shape
    f = w2.shape[0]
    tr = _pick_tile(t, tr)
    nchunk = 2 if (f // 2) % LANES == 0 else 1
    per_batch = mods.shape[0] > 1
    blk = pl.BlockSpec((None, tr, d), lambda bi, i: (bi, i, 0))
    return pl.pallas_call(
        functools.partial(_ffn_kernel, nchunk=nchunk),
        grid=(b, t // tr),
        in_specs=[blk,
                  pl.BlockSpec((1, d), lambda bi, i: (0, 0)),
                  pl.BlockSpec((None, 8, d), (lambda bi, i: (bi, 0, 0)) if per_batch else (lambda bi, i: (0, 0, 0))),
                  pl.BlockSpec((d, 2 * f), lambda bi, i: (0, 0)),
                  pl.BlockSpec((f, d), lambda bi, i: (0, 0))],
        out_specs=blk,
        out_shape=jax.ShapeDtypeStruct((b, t, d), F32),
        compiler_params=_cparams("parallel", "parallel"),
        name="ffn",
    )(x, g.reshape(1, d), mods, w13, w2)


def kernel(x, c, ctx, c_ctx, ada_w, ada_b, norm1_g, norm2_g, w_in, ssd_conv_w, ssd_conv_b,
           ssd_a_log, ssd_dt_bias, ssd_d, ssd_norm_w, hy_conv_w, hy_conv_b, hy_w1, hy_b1, hy_w2,
           hy_b2, hy_w3, hy_freq, hy_bias, da_lambda, da_subln_w, w_branch, w_out, ffn_w13,
           ffn_w2, final_g):
    bsz, length, d = x.shape
    ctx_len = ctx.shape[1]
    depth = w_in.shape[0]
    inner = SSD_HEADS * SSD_HEAD_DIM
    xbc_w = inner + 2 * SSD_GROUPS * SSD_STATE
    dt_w = 2 * SSD_HEADS
    hy_w = (HY_ORDER + 1) * d
    widths = (inner, xbc_w, dt_w, hy_w, d, d, d, N_BRANCH * d)
    offs = [0]
    for wd in widths:
        offs.append(offs[-1] + wd)

    cond = jnp.concatenate([c, c_ctx[None, :], jnp.zeros((8 - bsz - 1, d), F32)], axis=0)

    rope_tabs = rope_tables(length)
    tabs = fft_tables(2 * length // FFT_N2, FFT_N2)
    expand = (jnp.arange(LANES)[:, None] % SSD_HEADS
              == jnp.arange(inner)[None, :] // SSD_HEAD_DIM)
    expand_dir = [(expand & ((jnp.arange(LANES)[:, None] // SSD_HEADS) == dr)).astype(BF16) for dr in range(2)]
    qscale = DA_HEAD_DIM ** -0.5 * math.log2(math.e)

    x_l, x_c = x, ctx
    for layer in range(depth):
        last = layer == depth - 1
        lam_init = 0.8 - 0.6 * math.exp(-0.3 * layer)
        mod = matmul(cond, ada_w[layer].astype(BF16), ada_b[layer].reshape(1, -1), silu_in=True)
        mod = jnp.pad(mod.reshape(8, 6, d), ((0, 0), (0, 2), (0, 0)))
        mods_l, mods_c = mod[:bsz], mod[bsz:bsz + 1]

        w_l = w_in[layer].astype(BF16)
        w_parts = [w_l[:, offs[i]:offs[i + 1]] for i in range(len(widths))]
        w_parts[2] = jnp.pad(w_parts[2], ((0, 0), (0, LANES - dt_w)))
        a_neg = -jnp.exp(ssd_a_log[layer].astype(F32)).reshape(1, dt_w)
        aneg_row = jnp.pad(a_neg, ((0, 0), (0, LANES - dt_w)))
        bias_row = jnp.pad(ssd_dt_bias[layer].astype(F32).reshape(1, dt_w), ((0, 0), (0, LANES - dt_w)))
        d_row = jnp.repeat(ssd_d[layer].astype(F32), SSD_HEAD_DIM).reshape(1, inner)
        nw_row = ssd_norm_w[layer].astype(F32).reshape(1, inner)
        wb = w_branch[layer].astype(BF16)
        wo = w_out[layer].astype(BF16)
        w13 = ffn_w13[layer].astype(BF16)
        w2 = ffn_w2[layer].astype(BF16)

        def project(stream, mods):
            bb, tt, _ = stream.shape
            h = norm_mod(stream, norm1_g[layer], mods, 0).reshape(bb * tt, d)
            outs = [matmul(h, wp, out_dtype=(F32 if i == 2 else BF16)).reshape(bb, tt, -1)
                    for i, wp in enumerate(w_parts)]
            return outs

        z_c, xbc_c, dt_c, hy_c, q_c, k_c, v_c, g_c = project(x_c, mods_c)
        z_l, xbc_l, dt_l, hy_l, q_l, k_l, v_l, g_l = project(x_l, mods_l)

        xa_c = dwconv3(xbc_c, ssd_conv_w[layer], ssd_conv_b[layer], True)
        xa_l = dwconv3(xbc_l, ssd_conv_w[layer], ssd_conv_b[layer], True)
        h_zero = jnp.zeros((bsz, SSD_STATE, inner), F32)
        ys_c, ys_l = [], []
        for dr in range(2):
            y_c, st = ssd_scan(xa_c, dt_c, bias_row, aneg_row, expand_dir[dr], h_zero, dr)
            y_l, _ = ssd_scan(xa_l, dt_l, bias_row, aneg_row, expand_dir[dr], st, dr)
            ys_c.append(y_c)
            ys_l.append(y_l)
        ya_l = ssd_out(ys_l[0], ys_l[1], xa_l, z_l, d_row, nw_row)

        filt = (hy_w1[layer], hy_b1[layer], hy_w2[layer], hy_b2[layer], hy_w3[layer], hy_freq[layer])
        u_l = dwconv3(hy_l, hy_conv_w[layer], hy_conv_b[layer], False)
        yb_l = hyena_long(u_l, hyena_taps(length, *filt), hy_bias[layer], tabs)

        qr_l = rope(q_l, rope_tabs, qscale)
        kr_l, kn_l = rope(k_l, rope_tabs, 1.0, with_norms=True)
        kb_c = scale_cast(k_c, 1.0)
        yc_l = diff_attention(qr_l, kr_l, v_l, kb_c, v_c, kn_l, da_lambda[layer], da_subln_w[layer], lam_init)

        x_l_new = merge(ya_l, yb_l, yc_l, g_l, wb, wo, x_l, mods_l)
        x_l = ffn(x_l_new, norm2_g[layer], mods_l, w13, w2)

        if not last:
            ya_c = ssd_out(ys_c[0], ys_c[1], xa_c, z_c, d_row, nw_row)
            u_c = dwconv3(hy_c, hy_conv_w[layer], hy_conv_b[layer], False)
            yb_c = hyena_ctx(u_c, hyena_taps(ctx_len, *filt), hy_bias[layer])
            qb_c = scale_cast(q_c, qscale)
            yc_c = diff_attention(qb_c, kb_c, v_c, None, None, None, da_lambda[layer], da_subln_w[layer], lam_init)
            x_c_new = merge(ya_c, yb_c, yc_c, g_c, wb, wo, x_c, mods_c)
            x_c = ffn(x_c_new, norm2_g[layer], mods_c, w13, w2)
    return final_norm(x_l, final_g)
```

```python
import functools
import math

import jax
import jax.numpy as jnp
from jax import lax
from jax.experimental import pallas as pl
from jax.experimental.pallas import tpu as pltpu

F32 = jnp.float32
BF16 = jnp.bfloat16
EPS = 1e-6

SSD_HEADS = 16
SSD_HEAD_DIM = 64
SSD_GROUPS = 2
SSD_STATE = 128
SSD_CHUNK = 128
HY_ORDER = 2
HY_BANDS = 16
HY_SHORT_DECAY_PCT = 0.3
HY_LONG_DECAY_PCT = 1.5
HY_DECAY_TARGET = 1e-2
DA_HEADS = 8
DA_V_DIM = 128
DA_HEAD_DIM = 64
ROPE_AXIS_DIM = 32
ROPE_THETA = 10000.0
GRID_W = 64
N_BRANCH = 3

LANES = 128
SUBLANES = 8
FFT_N2 = 128
VMEM_LIMIT = 56 * 1024 * 1024


def _cparams(*sem):
    return pltpu.CompilerParams(dimension_semantics=sem, vmem_limit_bytes=VMEM_LIMIT)


def _dot(a, b):
    return jnp.dot(a, b, preferred_element_type=F32)


def _dot_nt(a, b):
    return lax.dot_general(a, b, (((1,), (1,)), ((), ())), preferred_element_type=F32)


def _split3(x):
    hi = x.astype(BF16)
    r1 = x - hi.astype(F32)
    mid = r1.astype(BF16)
    lo = (r1 - mid.astype(F32)).astype(BF16)
    return hi, mid, lo


def _silu(x):
    return x * jax.nn.sigmoid(x)


def _softplus(x):
    return jnp.maximum(x, 0.0) + jnp.log1p(jnp.exp(-jnp.abs(x)))


def _pick_tile(n, pref):
    t = min(pref, n)
    while n % t:
        t //= 2
    return t


def _matmul_kernel(a_ref, w_ref, b_ref, o_ref, *, silu_in):
    a = a_ref[...]
    if silu_in:
        a = _silu(a.astype(F32))
    r = _dot(a.astype(BF16), w_ref[...]) + b_ref[...]
    o_ref[...] = r.astype(o_ref.dtype)


def matmul(a, w, bias=None, out_dtype=F32, silu_in=False, tm=2048, tn=1024):
    m, k = a.shape
    n = w.shape[1]
    tm = _pick_tile(m, tm)
    tn = _pick_tile(n, tn)
    if bias is None:
        bias = jnp.zeros((1, n), F32)
    return pl.pallas_call(
        functools.partial(_matmul_kernel, silu_in=silu_in),
        grid=(m // tm, n // tn),
        in_specs=[pl.BlockSpec((tm, k), lambda i, j: (i, 0)),
                  pl.BlockSpec((k, tn), lambda i, j: (0, j)),
                  pl.BlockSpec((1, tn), lambda i, j: (0, j))],
        out_specs=pl.BlockSpec((tm, tn), lambda i, j: (i, j)),
        out_shape=jax.ShapeDtypeStruct((m, n), out_dtype),
        compiler_params=_cparams("parallel", "parallel"),
        name="matmul",
    )(a, w, bias)


def _rms(x, g):
    return x * lax.rsqrt(jnp.mean(x * x, axis=-1, keepdims=True) + EPS) * g


def _norm_mod_kernel(x_ref, g_ref, mod_ref, o_ref, *, row0):
    x = x_ref[...]
    y = _rms(x, g_ref[...])
    shift = mod_ref[row0:row0 + 1, :]
    scale = mod_ref[row0 + 1:row0 + 2, :]
    o_ref[...] = (y * (1.0 + scale) + shift).astype(o_ref.dtype)


def norm_mod(x, g, mods, row0, tr=512):
    b, t, d = x.shape
    tr = _pick_tile(t, tr)
    per_batch = mods.shape[0] > 1
    return pl.pallas_call(
        functools.partial(_norm_mod_kernel, row0=row0),
        grid=(b, t // tr),
        in_specs=[pl.BlockSpec((None, tr, d), lambda bi, i: (bi, i, 0)),
                  pl.BlockSpec((1, d), lambda bi, i: (0, 0)),
                  pl.BlockSpec((None, 8, d), (lambda bi, i: (bi, 0, 0)) if per_batch else (lambda bi, i: (0, 0, 0)))],
        out_specs=pl.BlockSpec((None, tr, d), lambda bi, i: (bi, i, 0)),
        out_shape=jax.ShapeDtypeStruct((b, t, d), BF16),
        compiler_params=_cparams("parallel", "parallel"),
        name="norm_mod",
    )(x, g.reshape(1, d), mods)


def _final_norm_kernel(x_ref, g_ref, o_ref):
    o_ref[...] = _rms(x_ref[...], g_ref[...])


def final_norm(x, g, tr=512):
    b, t, d = x.shape
    tr = _pick_tile(t, tr)
    return pl.pallas_call(
        _final_norm_kernel,
        grid=(b, t // tr),
        in_specs=[pl.BlockSpec((None, tr, d), lambda bi, i: (bi, i, 0)),
                  pl.BlockSpec((1, d), lambda bi, i: (0, 0))],
        out_specs=pl.BlockSpec((None, tr, d), lambda bi, i: (bi, i, 0)),
        out_shape=jax.ShapeDtypeStruct((b, t, d), F32),
        compiler_params=_cparams("parallel", "parallel"),
        name="final_norm",
    )(x, g.reshape(1, d))


def _dwconv_kernel(x_ref, xp_ref, xn_ref, w_ref, b_ref, o_ref, *, act, nblk):
    i = pl.program_id(1)
    x = x_ref[...].astype(F32)
    tr = x.shape[0]
    halo = xp_ref.shape[0]
    prev_row = jnp.where(i == 0, 0.0, xp_ref[halo - 1:halo, :].astype(F32))
    next_row = jnp.where(i == nblk - 1, 0.0, xn_ref[0:1, :].astype(F32))
    rows = lax.broadcasted_iota(jnp.int32, x.shape, 0)
    x_m1 = jnp.where(rows == 0, prev_row, pltpu.roll(x, 1, 0))
    x_p1 = jnp.where(rows == tr - 1, next_row, pltpu.roll(x, tr - 1, 0))
    y = b_ref[...] + x_m1 * w_ref[0:1, :] + x * w_ref[1:2, :] + x_p1 * w_ref[2:3, :]
    if act:
        y = _silu(y)
    o_ref[...] = y


def dwconv3(x, w, bias, act, tr=512, tc=1536):
    b, t, c = x.shape
    tr = _pick_tile(t, tr)
    tc = min(tc, c)
    assert c % tc == 0
    nblk = t // tr
    halo = SUBLANES * (4 // x.dtype.itemsize)
    rh = tr // halo
    lasth = t // halo - 1
    return pl.pallas_call(
        functools.partial(_dwconv_kernel, act=act, nblk=nblk),
        grid=(b, nblk, c // tc),
        in_specs=[pl.BlockSpec((None, tr, tc), lambda bi, i, j: (bi, i, j)),
                  pl.BlockSpec((None, halo, tc), lambda bi, i, j: (bi, jnp.maximum(i * rh - 1, 0), j)),
                  pl.BlockSpec((None, halo, tc), lambda bi, i, j: (bi, jnp.minimum((i + 1) * rh, lasth), j)),
                  pl.BlockSpec((3, tc), lambda bi, i, j: (0, j)),
                  pl.BlockSpec((1, tc), lambda bi, i, j: (0, j))],
        out_specs=pl.BlockSpec((None, tr, tc), lambda bi, i, j: (bi, i, j)),
        out_shape=jax.ShapeDtypeStruct((b, t, c), F32),
        compiler_params=_cparams("parallel", "parallel", "parallel"),
        name="dwconv3",
    )(x, x, x, w, bias.reshape(1, c))


def _ssd_kernel(x_ref, b_ref, c_ref, dt_ref, bias_ref, aneg_ref, e_ref, h0_ref,
                y_ref, hout_ref, st_ref, *, direction, nchunks):
    ci = pl.program_id(1)

    @pl.when(ci == 0)
    def _():
        st_ref[...] = h0_ref[...]

    q = x_ref.shape[0]
    hd = SSD_HEAD_DIM
    gw = (SSD_HEADS // SSD_GROUPS) * hd
    x = x_ref[...]
    dtv = _softplus(dt_ref[...] + bias_ref[...])
    adt = dtv * aneg_ref[...]
    row = lax.broadcasted_iota(jnp.int32, (q, q), 0)
    col = lax.broadcasted_iota(jnp.int32, (q, q), 1)
    mask = (col >= row) if direction else (col <= row)
    tri = mask.astype(BF16)
    acum = sum(_dot(tri, p) for p in _split3(adt))
    acum_t = sum(_dot_nt(p, tri) for p in _split3(adt.T))
    total = jnp.sum(adt, axis=0, keepdims=True)
    eac = jnp.exp(acum)
    dte = jnp.exp(total - acum)
    cd = jnp.broadcast_to(jnp.exp(total), (SUBLANES, LANES))
    stack = jnp.concatenate([dtv, eac, dte, cd], axis=0)
    e01 = e_ref[...]
    hi = stack.astype(BF16)
    lo = (stack - hi.astype(F32)).astype(BF16)
    ex = _dot(hi, e01) + _dot(lo, e01)
    dt_e, eac_e, dte_e, cd_e = ex[:q], ex[q:2 * q], ex[2 * q:3 * q], ex[3 * q:3 * q + 1]
    xdt = x * dt_e
    lane = lax.broadcasted_iota(jnp.int32, (q, LANES), 1)
    first = lane < hd
    for g in range(SSD_GROUPS):
        gs = slice(g * gw, (g + 1) * gw)
        c_g = c_ref[:, g * SSD_STATE:(g + 1) * SSD_STATE].astype(BF16)
        b_g32 = b_ref[:, g * SSD_STATE:(g + 1) * SSD_STATE]
        s_g = st_ref[:, gs]
        y_off = _dot(c_g, s_g.astype(BF16)) * eac_e[:, gs]
        cb = _dot_nt(c_g, b_g32.astype(BF16))
        for jp in range(gw // LANES):
            h_a = g * (SSD_HEADS // SSD_GROUPS) + 2 * jp
            ls = slice(h_a * hd, h_a * hd + LANES)
            xpair = xdt[:, ls]
            acc = y_off[:, jp * LANES:(jp + 1) * LANES]
            for k in range(2):
                cix = direction * SSD_HEADS + h_a + k
                seg = acum[:, cix:cix + 1] - acum_t[cix:cix + 1, :]
                dec = jnp.where(mask, jnp.exp(jnp.where(mask, seg, 0.0)), 0.0)
                w = (cb * dec).astype(BF16)
                xk = jnp.where(first if k == 0 else jnp.logical_not(first), xpair, 0.0).astype(BF16)
                acc = acc + _dot(w, xk)
            y_ref[:, ls] = acc
        xdte = (xdt[:, gs] * dte_e[:, gs]).astype(BF16)
        st_ref[:, gs] = s_g * cd_e[:, gs] + _dot(b_g32.T.astype(BF16), xdte)

    @pl.when(ci == nchunks - 1)
    def _():
        hout_ref[...] = st_ref[...]


def ssd_scan(xbc, dt_raw, dt_bias_row, aneg_row, expand, h0, direction):
    b, t, _ = xbc.shape
    q = SSD_CHUNK
    nchunks = t // q
    inner = SSD_HEADS * SSD_HEAD_DIM
    bn = SSD_GROUPS * SSD_STATE
    cmap = (lambda c: nchunks - 1 - c) if direction else (lambda c: c)
    y, hout = pl.pallas_call(
        functools.partial(_ssd_kernel, direction=direction, nchunks=nchunks),
        grid=(b, nchunks),
        in_specs=[pl.BlockSpec((None, q, inner), lambda bi, c: (bi, cmap(c), 0)),
                  pl.BlockSpec((None, q, bn), lambda bi, c: (bi, cmap(c), inner // bn)),
                  pl.BlockSpec((None, q, bn), lambda bi, c: (bi, cmap(c), inner // bn + 1)),
                  pl.BlockSpec((None, q, LANES), lambda bi, c: (bi, cmap(c), 0)),
                  pl.BlockSpec((1, LANES), lambda bi, c: (0, 0)),
                  pl.BlockSpec((1, LANES), lambda bi, c: (0, 0)),
                  pl.BlockSpec((LANES, inner), lambda bi, c: (0, 0)),
                  pl.BlockSpec((None, SSD_STATE, inner), lambda bi, c: (bi, 0, 0))],
        out_specs=[pl.BlockSpec((None, q, inner), lambda bi, c: (bi, cmap(c), 0)),
                   pl.BlockSpec((None, SSD_STATE, inner), lambda bi, c: (bi, 0, 0))],
        out_shape=[jax.ShapeDtypeStruct((b, t, inner), F32),
                   jax.ShapeDtypeStruct((b, SSD_STATE, inner), F32)],
        scratch_shapes=[pltpu.VMEM((SSD_STATE, inner), F32)],
        compiler_params=_cparams("parallel", "arbitrary"),
        name="ssd_scan_bwd" if direction else "ssd_scan_fwd",
    )(xbc, xbc, xbc, dt_raw, dt_bias_row, aneg_row, expand, h0)
    return y, hout


def _ssd_out_kernel(yf_ref, yb_ref, x_ref, z_ref, d_ref, w_ref, o_ref):
    y = yf_ref[...] + yb_ref[...] + d_ref[...] * x_ref[...]
    y = y * _silu(z_ref[...].astype(F32))
    gw = y.shape[1] // SSD_GROUPS
    for g in range(SSD_GROUPS):
        gs = slice(g * gw, (g + 1) * gw)
        yg = y[:, gs]
        yg = yg * lax.rsqrt(jnp.mean(yg * yg, axis=-1, keepdims=True) + EPS)
        o_ref[:, gs] = (yg * w_ref[:, gs]).astype(o_ref.dtype)


def ssd_out(yf, yb, xbc, z, d_row, w_row, tr=256):
    b, t, inner = yf.shape
    tr = _pick_tile(t, tr)
    blk = pl.BlockSpec((None, tr, inner), lambda bi, i: (bi, i, 0))
    vec = pl.BlockSpec((1, inner), lambda bi, i: (0, 0))
    return pl.pallas_call(
        _ssd_out_kernel,
        grid=(b, t // tr),
        in_specs=[blk, blk, blk, blk, vec, vec],
        out_specs=blk,
        out_shape=jax.ShapeDtypeStruct((b, t, inner), BF16),
        compiler_params=_cparams("parallel", "parallel"),
        name="ssd_out",
    )(yf, yb, xbc, z, d_row, w_row)


def _taps_kernel(f_ref, tv_ref, w1_ref, b1_ref, w2_ref, b2_ref, w3_ref, fr_ref, dl_ref, o_ref):
    hp = lax.Precision.HIGHEST
    fr = fr_ref[...]
    h = jnp.sin(fr * (jnp.dot(f_ref[...], w1_ref[...], precision=hp, preferred_element_type=F32) + b1_ref[...]))
    h = jnp.sin(fr * (jnp.dot(h, w2_ref[...], precision=hp, preferred_element_type=F32) + b2_ref[...]))
    filt = _dot(h.astype(BF16), w3_ref[...].astype(BF16))
    c = dl_ref.shape[1]
    win = jnp.exp(-tv_ref[:, 0:1] * dl_ref[...]) * tv_ref[:, 1:2]
    o_ref[0] = filt[:, :c] * win
    o_ref[1] = filt[:, c:] * win


def hyena_taps(length, w1, b1, w2, b2, w3, freq):
    hid = w1.shape[1]
    c = w3.shape[1] // (2 * HY_ORDER)
    n = 2 * length
    t = jnp.linspace(0.0, 1.0, length, dtype=F32)[:, None]
    phase = 2.0 * math.pi * jnp.arange(length, dtype=F32)[:, None] / length
    bands = jnp.linspace(1e-4, HY_BANDS - 1, HY_BANDS, dtype=F32)[None, :]
    feats = jnp.concatenate([t, jnp.cos(phase * bands), -jnp.sin(phase * bands)], axis=-1)
    nf = feats.shape[1]
    rev = jnp.concatenate([jnp.zeros((1,), jnp.int32), jnp.arange(length - 1, 0, -1, dtype=jnp.int32)])
    feats2 = jnp.concatenate([feats, feats[rev]], axis=0)
    feats2 = jnp.pad(feats2, ((0, 0), (0, LANES - nf)))
    valid = jnp.ones((n,), F32).at[length].set(0.0)
    tv = jnp.stack([jnp.concatenate([t[:, 0], t[rev, 0]]), valid], axis=1)
    max_decay = math.log(HY_DECAY_TARGET) / HY_SHORT_DECAY_PCT
    min_decay = math.log(HY_DECAY_TARGET) / HY_LONG_DECAY_PCT
    deltas = jnp.abs(jnp.linspace(min_decay, max_decay, c, dtype=F32))[None, :]
    w1p = jnp.pad(w1, ((0, LANES - nf), (0, 0)))
    w3r = w3.reshape(hid, HY_ORDER, 2, c).transpose(2, 0, 1, 3).reshape(2, hid, HY_ORDER * c)
    tr = _pick_tile(length, 1024)
    nhalf = length // tr
    return pl.pallas_call(
        _taps_kernel,
        grid=(n // tr,),
        in_specs=[pl.BlockSpec((tr, LANES), lambda i: (i, 0)),
                  pl.BlockSpec((tr, 2), lambda i: (i, 0)),
                  pl.BlockSpec((LANES, hid), lambda i: (0, 0)),
                  pl.BlockSpec((1, hid), lambda i: (0, 0)),
                  pl.BlockSpec((hid, hid), lambda i: (0, 0)),
                  pl.BlockSpec((1, hid), lambda i: (0, 0)),
                  pl.BlockSpec((None, hid, HY_ORDER * c), lambda i: (i // nhalf, 0, 0)),
                  pl.BlockSpec((1, hid), lambda i: (0, 0)),
                  pl.BlockSpec((1, c), lambda i: (0, 0))],
        out_specs=pl.BlockSpec((HY_ORDER, tr, c), lambda i: (0, i, 0)),
        out_shape=jax.ShapeDtypeStruct((HY_ORDER, n, c), F32),
        compiler_params=_cparams("parallel"),
        name="hyena_taps",
    )(feats2, tv, w1p, b1.reshape(1, hid), w2, b2.reshape(1, hid), w3r, freq.reshape(1, hid), deltas)


def _cs(num, den):
    ang = (2.0 * math.pi / den) * num.astype(F32)
    return jnp.cos(ang), jnp.sin(ang)


def fft_tables(n1, n2):
    n = n1 * n2
    h = n1 // 2
    k1 = jnp.arange(n1, dtype=jnp.int32)
    c1, s1 = _cs((k1[:, None] * k1[None, :]) % n1, n1)
    g_u = jnp.block([[c1[:, :h], s1[:, :h]], [-s1[:, :h], c1[:, :h]]])
    g_t = jnp.concatenate([c1, -s1], axis=0)
    g_i = jnp.block([[c1[:h, :], -s1[:h, :]], [s1[:h, :], c1[:h, :]]])
    s = jnp.arange(n2, dtype=jnp.int32)
    num = (n1 * (s[:, None] * s[None, :]))[None] + (k1[:, None, None] * s[None, None, :])
    cb, sb = _cs(num % n, n)
    m_f = jnp.concatenate([jnp.concatenate([cb, sb], axis=2),
                           jnp.concatenate([-sb, cb], axis=2)], axis=1)
    cbt, sbt = jnp.swapaxes(cb, 1, 2) / n, jnp.swapaxes(sb, 1, 2) / n
    m_i = jnp.concatenate([jnp.concatenate([cbt, -sbt], axis=2),
                           jnp.concatenate([sbt, cbt], axis=2)], axis=1)
    return dict(g_u=g_u.astype(BF16), g_t=g_t.astype(BF16), g_i=g_i.astype(BF16),
                m_f=m_f.astype(BF16), m_i=m_i.astype(BF16))


FFT_KGROUP = SUBLANES


def _rows_get(ref, start, n, stride):
    return ref.reshape(math.prod(ref.shape[:-1]), ref.shape[-1])[pl.ds(start, n, stride=stride), :]


def _rows_set(ref, start, n, stride, val):
    ref.reshape(math.prod(ref.shape[:-1]), ref.shape[-1])[pl.ds(start, n, stride=stride), :] = val


def _cpack(re, im):
    rb = lax.bitcast_convert_type(re.astype(BF16).astype(F32), jnp.uint32)
    ib = lax.bitcast_convert_type(im.astype(BF16).astype(F32), jnp.uint32)
    return (rb & jnp.uint32(0xFFFF0000)) | (ib >> 16)


def _cunpack(w):
    re = lax.bitcast_convert_type(w & jnp.uint32(0xFFFF0000), F32)
    im = lax.bitcast_convert_type(w << 16, F32)
    return re, im


FFT_LANE_BLOCKS = 2


def _fft_a_kernel(g_ref, *refs, s_major_in):
    *x_refs, o_ref = refs
    xa_refs, xb_refs = x_refs[:len(x_refs) // 2], x_refs[len(x_refs) // 2:]
    h = g_ref.shape[1] // 2
    ts, n1, _ = o_ref.shape
    g = g_ref[...]

    def rows(x_refs_, s):
        if s_major_in:
            return x_refs_[0][s]
        return jnp.concatenate([_rows_get(r, s, h, ts) for r in x_refs_], axis=1)

    for s in range(ts):
        x = jnp.concatenate([rows(xa_refs, s), rows(xb_refs, s)], axis=0).astype(BF16)
        r = _dot(g, x)
        o_ref[s] = _cpack(r[:n1], r[n1:])


def fft_stage_a(g, x4, sel_a, sel_b, col_off, c, s_major_in=False):
    n1 = g.shape[0] // 2
    h = n1 // 2
    n2 = x4.shape[1] if s_major_in else x4.shape[2]
    ts = SUBLANES
    nl = FFT_LANE_BLOCKS
    tc = nl * LANES
    if s_major_in:
        coff = col_off // tc
        specs = lambda sel: [pl.BlockSpec((None, ts, h, tc), lambda j, s: (sel[0], s, sel[1], j + coff))]
    else:
        coff = col_off // LANES
        specs = lambda sel: [pl.BlockSpec((None, h, ts, LANES),
                                          functools.partial(lambda j, s, l: (sel[0], sel[1], s, nl * j + l + coff), l=l))
                             for l in range(nl)]
    x_specs = specs(sel_a) + specs(sel_b)
    return pl.pallas_call(
        functools.partial(_fft_a_kernel, s_major_in=s_major_in),
        grid=(c // tc, n2 // ts),
        in_specs=[pl.BlockSpec((2 * n1, n1), lambda j, s: (0, 0))] + x_specs,
        out_specs=pl.BlockSpec((ts, n1, tc), lambda j, s: (s, 0, j)),
        out_shape=jax.ShapeDtypeStruct((n2, n1, c), jnp.uint32),
        compiler_params=_cparams("parallel", "parallel"),
        name="fft_stage_a",
    )(g, *([x4] * len(x_specs)))


def _fft_load_k(a_refs, j):
    n2, kb, _ = a_refs[0].shape
    re, im = _cunpack(jnp.concatenate([_rows_get(r, j, n2, kb) for r in a_refs], axis=1))
    return jnp.concatenate([re, im], axis=0).astype(BF16)


def _fft_bh_kernel(m_ref, *refs, groups):
    *a_refs, o_ref = refs
    n2, kb, _ = a_refs[0].shape
    nl = len(a_refs) // groups
    for g in range(groups):
        for j in range(kb):
            x = _dot(m_ref[g * kb + j], _fft_load_k(a_refs[g * nl:(g + 1) * nl], j))
            o_ref[g * kb + j] = _cpack(x[:n2], x[n2:])


def _fft_b_kernel(mf_ref, mi_ref, *refs, groups):
    *a_refs, h_ref, o_ref = refs
    n2, kb, _ = a_refs[0].shape
    nl = len(a_refs) // groups
    for g in range(groups):
        for j in range(kb):
            k = g * kb + j
            x = _dot(mf_ref[k], _fft_load_k(a_refs[g * nl:(g + 1) * nl], j))
            xr, xi = x[:n2], x[n2:]
            hr, hi = _cunpack(h_ref[k])
            y = jnp.concatenate([xr * hr - xi * hi, xr * hi + xi * hr], axis=0).astype(BF16)
            r = _dot(mi_ref[k], y)
            o_ref[k] = _cpack(r[:n2], r[n2:])


def fft_stage_b(m_f, m_i, a, hspec, groups=2):
    n2, n1, c = a.shape
    groups = min(groups, n1 // FFT_KGROUP)
    kb = FFT_KGROUP * groups
    nl = FFT_LANE_BLOCKS
    tc = nl * LANES
    ablks = [pl.BlockSpec((n2, FFT_KGROUP, LANES),
                          functools.partial(lambda k, j, g, l: (0, groups * k + g, nl * j + l), g=g, l=l))
             for g in range(groups) for l in range(nl)]
    kblk = pl.BlockSpec((kb, n2, tc), lambda k, j: (k, 0, j))
    mat = pl.BlockSpec((kb, 2 * n2, 2 * n2), lambda k, j: (k, 0, 0))
    if hspec is None:
        kern, in_specs, name = _fft_bh_kernel, [mat] + ablks, "fft_stage_b_spectrum"
        args = (m_f,) + (a,) * len(ablks)
    else:
        kern, in_specs, name = _fft_b_kernel, [mat, mat] + ablks + [kblk], "fft_stage_b"
        args = (m_f, m_i) + (a,) * len(ablks) + (hspec,)
    return pl.pallas_call(
        functools.partial(kern, groups=groups),
        grid=(n1 // kb, c // tc),
        in_specs=in_specs,
        out_specs=kblk,
        out_shape=jax.ShapeDtypeStruct((n1, n2, c), jnp.uint32),
        compiler_params=_cparams("parallel", "parallel"),
        name=name,
    )(*args)


def _fft_ai_kernel(g_ref, b_ref, u_ref, x_ref, skip_ref, o_ref, *, u_s_major, out_s_major):
    n1, ts, _ = b_ref.shape
    h = n1 // 2
    ga = g_ref[:, :n1]
    gb = g_ref[:, n1:]
    skip = skip_ref[...]
    for s in range(ts):
        br, bi_ = _cunpack(_rows_get(b_ref, s, n1, ts))
        y = _dot(ga, br.astype(BF16)) + _dot(gb, bi_.astype(BF16))
        for bi in range(2):
            start = bi * h * ts + s
            u = u_ref[bi, s] if u_s_major else _rows_get(u_ref, start, h, ts)
            val = _rows_get(x_ref, start, h, ts) * (y[bi * h:(bi + 1) * h] + u * skip)
            if out_s_major:
                o_ref[bi, s] = val
            else:
                _rows_set(o_ref, start, h, ts, val)


def fft_stage_a_inv(g_i, bm, u4, u_off, u_s_major, x4, x_off, skip_row, out_s_major, tc=LANES):
    n1, n2, c = bm.shape
    h = n1 // 2
    ts = SUBLANES
    uo, xo = u_off // tc, x_off // tc
    nat = lambda off: pl.BlockSpec((2, h, ts, tc), lambda j, s: (0, 0, s, j + off))
    smj = lambda off: pl.BlockSpec((2, ts, h, tc), lambda j, s: (0, s, 0, j + off))
    return pl.pallas_call(
        functools.partial(_fft_ai_kernel, u_s_major=u_s_major, out_s_major=out_s_major),
        grid=(c // tc, n2 // ts),
        in_specs=[pl.BlockSpec((n1, 2 * n1), lambda j, s: (0, 0)),
                  pl.BlockSpec((n1, ts, tc), lambda j, s: (0, s, j)),
                  smj(uo) if u_s_major else nat(uo),
                  nat(xo),
                  pl.BlockSpec((1, tc), lambda j, s: (0, j))],
        out_specs=smj(0) if out_s_major else nat(0),
        out_shape=jax.ShapeDtypeStruct((2, n2, h, c) if out_s_major else (2, h, n2, c), F32),
        compiler_params=_cparams("parallel", "parallel"),
        name="fft_stage_a_inv",
    )(g_i, bm, u4, x4, skip_row)


def hyena_long(u, taps, skip, tabs):
    b, length, c3 = u.shape
    assert b == 2, "the two batch rows are packed as one complex sequence"
    c = c3 // 3
    n2 = FFT_N2
    n1 = 2 * length // n2
    h = n1 // 2
    u4 = u.reshape(b, h, n2, c3)
    taps4 = taps.reshape(HY_ORDER * 2, h, n2, c)
    zin, zoff, z_s_major = u4, 0, False
    out = None
    for conv in range(HY_ORDER):
        last = conv == HY_ORDER - 1
        hspec = fft_stage_b(tabs["m_f"], None,
                            fft_stage_a(tabs["g_t"], taps4, (2 * conv, 0), (2 * conv + 1, 0), 0, c), None)
        a = fft_stage_a(tabs["g_u"], zin, (0, 0), (1, 0), zoff, c, s_major_in=z_s_major)
        bm = fft_stage_b(tabs["m_f"], tabs["m_i"], a, hspec)
        out = fft_stage_a_inv(tabs["g_i"], bm, zin, zoff, z_s_major, u4, (conv + 1) * c,
                              skip[conv].reshape(1, c), not last)
        zin, zoff, z_s_major = out, 0, True
    return out.reshape(b, length, c)


def _hyena_ctx_kernel(u_ref, x1_ref, x2_ref, taps_ref, fc_ref, fs_ref, skip_ref, o_ref):
    length = u_ref.shape[1]
    n = 2 * length
    fc = fc_ref[...]
    fs = fs_ref[...]
    fcl, fsl = fc[:, :length], fs[:, :length]
    fct, fst = fc[:length, :], fs[:length, :]
    zr = u_ref[0]
    zi = u_ref[1]
    gates = (x1_ref, x2_ref)
    for conv in range(HY_ORDER):
        tp = taps_ref[conv].astype(BF16)
        hr, hi = _dot(fc, tp), -_dot(fs, tp)
        zrb, zib = zr.astype(BF16), zi.astype(BF16)
        xr = _dot(fcl, zrb) + _dot(fsl, zib)
        xi = _dot(fcl, zib) - _dot(fsl, zrb)
        yr = (xr * hr - xi * hi).astype(BF16)
        yi = (xr * hi + xi * hr).astype(BF16)
        cr = (_dot(fct, yr) - _dot(fst, yi)) * (1.0 / n)
        ci = (_dot(fct, yi) + _dot(fst, yr)) * (1.0 / n)
        sk = skip_ref[conv:conv + 1, :]
        zr = gates[conv][0] * (cr + zr * sk)
        zi = gates[conv][1] * (ci + zi * sk)
    o_ref[0] = zr.astype(o_ref.dtype)
    o_ref[1] = zi.astype(o_ref.dtype)


def hyena_ctx(u, taps, skip, tc=256):
    b, length, c3 = u.shape
    assert b == 2
    c = c3 // 3
    n = 2 * length
    k = jnp.arange(n, dtype=jnp.int32)
    fc, fs = _cs((k[:, None] * k[None, :]) % n, n)
    nb = c // tc
    ub = lambda off: pl.BlockSpec((2, length, tc), lambda j: (0, 0, j + off * nb))
    mat = pl.BlockSpec((n, n), lambda j: (0, 0))
    return pl.pallas_call(
        _hyena_ctx_kernel,
        grid=(nb,),
        in_specs=[ub(0), ub(1), ub(2),
                  pl.BlockSpec((HY_ORDER, n, tc), lambda j: (0, 0, j)),
                  mat, mat,
                  pl.BlockSpec((HY_ORDER, tc), lambda j: (0, j))],
        out_specs=pl.BlockSpec((2, length, tc), lambda j: (0, 0, j)),
        out_shape=jax.ShapeDtypeStruct((b, length, c), BF16),
        compiler_params=_cparams("parallel"),
        name="hyena_ctx",
    )(u, u, u, taps, fc.astype(BF16), fs.astype(BF16), skip)


def rope_tables(length):
    rows = length // GRID_W
    row = jnp.repeat(jnp.arange(rows, dtype=F32), GRID_W)
    col = (jnp.arange(length) % GRID_W).astype(F32)
    inv = ROPE_THETA ** (-jnp.arange(0, ROPE_AXIS_DIM, 2, dtype=F32) / ROPE_AXIS_DIM)
    ang = jnp.stack([row[:, None] * inv, col[:, None] * inv], axis=1)
    cos, sin = jnp.cos(ang), jnp.sin(ang)
    zero = jnp.zeros_like(sin)
    per_map = lambda lo, hi: jnp.concatenate([lo, hi], axis=-1).reshape(length, 2 * ROPE_AXIS_DIM)
    reps = LANES // (2 * ROPE_AXIS_DIM)
    c_t = jnp.tile(per_map(cos, cos), (1, reps))
    sm_t = jnp.tile(per_map(-sin, zero), (1, reps))
    sp_t = jnp.tile(per_map(zero, sin), (1, reps))
    return c_t, sm_t, sp_t


def _rope_kernel(x_ref, c_ref, sm_ref, sp_ref, o_ref, *n_refs, scale):
    half = ROPE_AXIS_DIM // 2
    c, sm, sp = c_ref[...], sm_ref[...], sp_ref[...]
    for g in range(x_ref.shape[1] // LANES):
        ls = slice(g * LANES, (g + 1) * LANES)
        x = x_ref[:, ls].astype(F32)
        y = (x * c + pltpu.roll(x, LANES - half, 1) * sm + pltpu.roll(x, half, 1) * sp) * scale
        o_ref[:, ls] = y.astype(o_ref.dtype)
        if n_refs:
            n2 = jnp.max(jnp.sum(y * y, axis=1, keepdims=True), axis=0, keepdims=True)
            n_refs[0][g:g + 1, :] = jnp.broadcast_to(n2, (1, LANES))


def rope(x, tables, scale, with_norms=False, tr=512):
    b, t, d = x.shape
    tr = _pick_tile(t, tr)
    nh = d // LANES
    tab = pl.BlockSpec((tr, LANES), lambda bi, i: (i, 0))
    out_specs = [pl.BlockSpec((None, tr, d), lambda bi, i: (bi, i, 0))]
    out_shape = [jax.ShapeDtypeStruct((b, t, d), BF16)]
    if with_norms:
        out_specs.append(pl.BlockSpec((None, None, nh, LANES), lambda bi, i: (bi, i, 0, 0)))
        out_shape.append(jax.ShapeDtypeStruct((b, t // tr, nh, LANES), F32))
    outs = pl.pallas_call(
        functools.partial(_rope_kernel, scale=scale),
        grid=(b, t // tr),
        in_specs=[pl.BlockSpec((None, tr, d), lambda bi, i: (bi, i, 0)), tab, tab, tab],
        out_specs=out_specs,
        out_shape=out_shape,
        compiler_params=_cparams("parallel", "parallel"),
        name="rope",
    )(x, *tables)
    if with_norms:
        return outs[0], jnp.swapaxes(outs[1], 1, 2)
    return outs[0]


def _scale_cast_kernel(x_ref, o_ref, *, scale):
    o_ref[...] = (x_ref[...].astype(F32) * scale).astype(o_ref.dtype)


def scale_cast(x, scale, tr=256):
    b, t, d = x.shape
    tr = _pick_tile(t, tr)
    blk = pl.BlockSpec((None, tr, d), lambda bi, i: (bi, i, 0))
    return pl.pallas_call(
        functools.partial(_scale_cast_kernel, scale=scale),
        grid=(b, t // tr), in_specs=[blk], out_specs=blk,
        out_shape=jax.ShapeDtypeStruct((b, t, d), BF16),
        compiler_params=_cparams("parallel", "parallel"),
        name="scale_cast",
    )(x)


ATTN_GUARD_LOG2 = 80.0
ATTN_NORM_SLACK = 1.03


def _attn_kernel(*refs, lam_init, nk, with_ctx):
    if with_ctx:
        (q_ref, k_ref, v_ref, kc_ref, vc_ref, kn_ref, lam_ref, w_ref, o_ref,
         q2_ref, m_ref, c_ref, acc_ref, p_scr, flag_ref) = refs
    else:
        q_ref, k_ref, v_ref, lam_ref, w_ref, o_ref, q2_ref, m_ref, c_ref, acc_ref, p_scr, flag_ref = refs
    ki = pl.program_id(3)
    tq = q_ref.shape[0]

    def with_ones(v):
        lane = lax.broadcasted_iota(jnp.int32, v.shape, 1)
        return jnp.concatenate([v, (lane == 0).astype(BF16)], axis=1)

    def exact_update(k, v, p_buf):
        s = _dot_nt(q2_ref[...], k)
        m_prev = m_ref[...]
        m_new = jnp.maximum(m_prev, jnp.max(s, axis=1, keepdims=True))
        alpha = jnp.exp2(m_prev - m_new)
        p_buf[...] = jnp.exp2(s - m_new).astype(BF16)
        acc_ref[...] = alpha * acc_ref[...] + _dot(p_buf[...], with_ones(v))
        m_ref[...] = m_new

    def fast_update(k, v, p_buf):
        s = _dot_nt(q2_ref[...], k)
        p_buf[...] = jnp.exp2(s - pltpu.repeat(c_ref[...], k.shape[0] // LANES, axis=1)).astype(BF16)
        acc_ref[...] += _dot(p_buf[...], with_ones(v))

    @pl.when(ki == 0)
    def _():
        q = q_ref[...]
        lane = lax.broadcasted_iota(jnp.int32, q.shape, 1)
        zero = jnp.zeros_like(q)
        q2_ref[:tq, :] = jnp.where(lane < DA_HEAD_DIM, q, zero)
        q2_ref[tq:, :] = jnp.where(lane >= DA_HEAD_DIM, q, zero)
        m_ref[...] = jnp.full(m_ref.shape, -jnp.inf, F32)
        acc_ref[...] = jnp.zeros(acc_ref.shape, F32)
        flag_ref[0] = 0
        if with_ctx:
            q2 = q2_ref[...]
            c = _dot_nt(q2, jnp.broadcast_to(kc_ref[0:1, :], (LANES, DA_V_DIM)))
            sq = q2.astype(F32) * q2.astype(F32)
            hi = sq.astype(BF16)
            lo = (sq - hi.astype(F32)).astype(BF16)
            ones = jnp.ones((DA_V_DIM, LANES), BF16)
            qn2 = _dot(hi, ones) + _dot(lo, ones)
            kn2 = jnp.max(kn_ref[...])
            gap = jnp.max(jnp.sqrt(qn2 * kn2) * ATTN_NORM_SLACK - c)
            c_ref[...] = c
            flag_ref[0] = (gap <= ATTN_GUARD_LOG2).astype(jnp.int32)

    fast = flag_ref[0] == 1
    if with_ctx:
        tc = kc_ref.shape[0]
        first = ki == 0

        @pl.when(jnp.logical_and(first, fast))
        def _():
            fast_update(kc_ref[...], vc_ref[...], p_scr.at[:, pl.ds(0, tc)])

        @pl.when(jnp.logical_and(first, jnp.logical_not(fast)))
        def _():
            exact_update(kc_ref[...], vc_ref[...], p_scr.at[:, pl.ds(0, tc)])

    @pl.when(fast)
    def _():
        fast_update(k_ref[...], v_ref[...], p_scr)

    @pl.when(jnp.logical_not(fast))
    def _():
        exact_update(k_ref[...], v_ref[...], p_scr)

    @pl.when(ki == nk - 1)
    def _():
        lv = lam_ref[...]
        e1 = jnp.exp(jnp.sum(lv[0:1, :] * lv[1:2, :], axis=1, keepdims=True))
        e2 = jnp.exp(jnp.sum(lv[2:3, :] * lv[3:4, :], axis=1, keepdims=True))
        lam = e1 - e2 + lam_init
        acc = acc_ref[...]
        o = acc[:, :DA_V_DIM] / acc[:, DA_V_DIM:DA_V_DIM + 1]
        o = o[:tq] - lam * o[tq:]
        o_ref[...] = (_rms(o, w_ref[...]) * (1.0 - lam_init)).astype(o_ref.dtype)


def diff_attention(q, k, v, k_ctx, v_ctx, knorm, lam_vec, subln_w, lam_init, tq=1024, tk=2048):
    b, t, d = q.shape
    nh = d // DA_V_DIM
    tq = _pick_tile(t, tq)
    tk = _pick_tile(k.shape[1], tk)
    nk = k.shape[1] // tk
    with_ctx = k_ctx is not None
    assert not with_ctx or k_ctx.shape[1] <= tk
    qspec = pl.BlockSpec((None, tq, DA_V_DIM), lambda bi, h, i, j: (bi, i, h))
    kspec = pl.BlockSpec((None, tk, DA_V_DIM), lambda bi, h, i, j: (bi, j, h))
    in_specs = [qspec, kspec, kspec]
    args = [q, k, v]
    if with_ctx:
        cspec = pl.BlockSpec((None, k_ctx.shape[1], DA_V_DIM), lambda bi, h, i, j: (bi, 0, h))
        nspec = pl.BlockSpec((None, None) + knorm.shape[2:], lambda bi, h, i, j: (bi, h, 0, 0))
        in_specs += [cspec, cspec, nspec]
        args += [k_ctx, v_ctx, knorm]
    in_specs += [pl.BlockSpec((4, DA_HEAD_DIM), lambda bi, h, i, j: (0, 0)),
                 pl.BlockSpec((1, DA_V_DIM), lambda bi, h, i, j: (0, 0))]
    args += [lam_vec, subln_w.reshape(1, DA_V_DIM)]
    return pl.pallas_call(
        functools.partial(_attn_kernel, lam_init=lam_init, nk=nk, with_ctx=with_ctx),
        grid=(b, nh, t // tq, nk),
        in_specs=in_specs,
        out_specs=qspec,
        out_shape=jax.ShapeDtypeStruct((b, t, d), BF16),
        scratch_shapes=[pltpu.VMEM((2 * tq, DA_V_DIM), BF16),
                        pltpu.VMEM((2 * tq, 1), F32),
                        pltpu.VMEM((2 * tq, LANES), F32),
                        pltpu.VMEM((2 * tq, 2 * DA_V_DIM), F32),
                        pltpu.VMEM((2 * tq, tk), BF16),
                        pltpu.SMEM((1,), jnp.int32)],
        compiler_params=_cparams("parallel", "parallel", "parallel", "arbitrary"),
        name="diff_attention",
    )(*args)


def _merge_kernel(ya_ref, yb_ref, yc_ref, g_ref, wb_ref, wo_ref, x_ref, mod_ref, o_ref):
    d = x_ref.shape[1]
    ys = (ya_ref, yb_ref, yc_ref)
    s = None
    for k in range(N_BRANCH):
        t = jax.nn.sigmoid(g_ref[:, k * d:(k + 1) * d].astype(F32)) * _dot(ys[k][...].astype(BF16), wb_ref[k])
        s = t if s is None else s + t
    r = _dot(s.astype(BF16), wo_ref[...])
    o_ref[...] = x_ref[...] + mod_ref[2:3, :] * r


def merge(ya, yb, yc, gates, wb, wo, x, mods, tr=256):
    b, t, d = x.shape
    tr = _pick_tile(t, tr)
    per_batch = mods.shape[0] > 1
    blk = pl.BlockSpec((None, tr, d), lambda bi, i: (bi, i, 0))
    return pl.pallas_call(
        _merge_kernel,
        grid=(b, t // tr),
        in_specs=[blk, blk, blk,
                  pl.BlockSpec((None, tr, N_BRANCH * d), lambda bi, i: (bi, i, 0)),
                  pl.BlockSpec((N_BRANCH, d, d), lambda bi, i: (0, 0, 0)),
                  pl.BlockSpec((d, d), lambda bi, i: (0, 0)),
                  blk,
                  pl.BlockSpec((None, 8, d), (lambda bi, i: (bi, 0, 0)) if per_batch else (lambda bi, i: (0, 0, 0)))],
        out_specs=blk,
        out_shape=jax.ShapeDtypeStruct((b, t, d), F32),
        compiler_params=_cparams("parallel", "parallel"),
        name="merge",
    )(ya, yb, yc, gates, wb, wo, x, mods)


def _ffn_kernel(x_ref, g_ref, mod_ref, w13_ref, w2_ref, o_ref, *, nchunk):
    x = x_ref[...]
    h = (_rms(x, g_ref[...]) * (1.0 + mod_ref[4:5, :]) + mod_ref[3:4, :]).astype(BF16)
    f = w2_ref.shape[0]
    fc = f // nchunk
    acc = None
    for c in range(nchunk):
        a1 = _dot(h, w13_ref[:, c * fc:(c + 1) * fc])
        a3 = _dot(h, w13_ref[:, f + c * fc:f + (c + 1) * fc])
        t = _dot((_silu(a1) * a3).astype(BF16), w2_ref[c * fc:(c + 1) * fc, :])
        acc = t if acc is None else acc + t
    o_ref[...] = x + mod_ref[5:6, :] * acc


def ffn(x, g, mods, w13, w2, tr=256):
    b, t, d = x.shape
    f = w2.shape[0]
    tr = _pick_tile(t, tr)
    nchunk = 2 if (f // 2) % LANES == 0 else 1
    per_batch = mods.shape[0] > 1
    blk = pl.BlockSpec((None, tr, d), lambda bi, i: (bi, i, 0))
    return pl.pallas_call(
        functools.partial(_ffn_kernel, nchunk=nchunk),
        grid=(b, t // tr),
        in_specs=[blk,
                  pl.BlockSpec((1, d), lambda bi, i: (0, 0)),
                  pl.BlockSpec((None, 8, d), (lambda bi, i: (bi, 0, 0)) if per_batch else (lambda bi, i: (0, 0, 0))),
                  pl.BlockSpec((d, 2 * f), lambda bi, i: (0, 0)),
                  pl.BlockSpec((f, d), lambda bi, i: (0, 0))],
        out_specs=blk,
        out_shape=jax.ShapeDtypeStruct((b, t, d), F32),
        compiler_params=_cparams("parallel", "parallel"),
        name="ffn",
    )(x, g.reshape(1, d), mods, w13, w2)


def kernel(x, c, ctx, c_ctx, ada_w, ada_b, norm1_g, norm2_g, w_in, ssd_conv_w, ssd_conv_b,
           ssd_a_log, ssd_dt_bias, ssd_d, ssd_norm_w, hy_conv_w, hy_conv_b, hy_w1, hy_b1, hy_w2,
           hy_b2, hy_w3, hy_freq, hy_bias, da_lambda, da_subln_w, w_branch, w_out, ffn_w13,
           ffn_w2, final_g):
    bsz, length, d = x.shape
    ctx_len = ctx.shape[1]
    depth = w_in.shape[0]
    inner = SSD_HEADS * SSD_HEAD_DIM
    xbc_w = inner + 2 * SSD_GROUPS * SSD_STATE
    dt_w = 2 * SSD_HEADS
    hy_w = (HY_ORDER + 1) * d
    widths = (inner, xbc_w, dt_w, hy_w, d, d, d, N_BRANCH * d)
    offs = [0]
    for wd in widths:
        offs.append(offs[-1] + wd)

    cond = jnp.concatenate([c, c_ctx[None, :], jnp.zeros((8 - bsz - 1, d), F32)], axis=0)

    rope_tabs = rope_tables(length)
    tabs = fft_tables(2 * length // FFT_N2, FFT_N2)
    expand = (jnp.arange(LANES)[:, None] % SSD_HEADS
              == jnp.arange(inner)[None, :] // SSD_HEAD_DIM)
    expand_dir = [(expand & ((jnp.arange(LANES)[:, None] // SSD_HEADS) == dr)).astype(BF16) for dr in range(2)]
    qscale = DA_HEAD_DIM ** -0.5 * math.log2(math.e)

    x_l, x_c = x, ctx
    for layer in range(depth):
        last = layer == depth - 1
        lam_init = 0.8 - 0.6 * math.exp(-0.3 * layer)
        mod = matmul(cond, ada_w[layer].astype(BF16), ada_b[layer].reshape(1, -1), silu_in=True)
        mod = jnp.pad(mod.reshape(8, 6, d), ((0, 0), (0, 2), (0, 0)))
        mods_l, mods_c = mod[:bsz], mod[bsz:bsz + 1]

        w_l = w_in[layer].astype(BF16)
        w_parts = [w_l[:, offs[i]:offs[i + 1]] for i in range(len(widths))]
        w_parts[2] = jnp.pad(w_parts[2], ((0, 0), (0, LANES - dt_w)))
        a_neg = -jnp.exp(ssd_a_log[layer].astype(F32)).reshape(1, dt_w)
        aneg_row = jnp.pad(a_neg, ((0, 0), (0, LANES - dt_w)))
        bias_row = jnp.pad(ssd_dt_bias[layer].astype(F32).reshape(1, dt_w), ((0, 0), (0, LANES - dt_w)))
        d_row = jnp.repeat(ssd_d[layer].astype(F32), SSD_HEAD_DIM).reshape(1, inner)
        nw_row = ssd_norm_w[layer].astype(F32).reshape(1, inner)
        wb = w_branch[layer].astype(BF16)
        wo = w_out[layer].astype(BF16)
        w13 = ffn_w13[layer].astype(BF16)
        w2 = ffn_w2[layer].astype(BF16)

        def project(stream, mods):
            bb, tt, _ = stream.shape
            h = norm_mod(stream, norm1_g[layer], mods, 0).reshape(bb * tt, d)
            outs = [matmul(h, wp, out_dtype=(F32 if i == 2 else BF16)).reshape(bb, tt, -1)
                    for i, wp in enumerate(w_parts)]
            return outs

        z_c, xbc_c, dt_c, hy_c, q_c, k_c, v_c, g_c = project(x_c, mods_c)
        z_l, xbc_l, dt_l, hy_l, q_l, k_l, v_l, g_l = project(x_l, mods_l)

        xa_c = dwconv3(xbc_c, ssd_conv_w[layer], ssd_conv_b[layer], True)
        xa_l = dwconv3(xbc_l, ssd_conv_w[layer], ssd_conv_b[layer], True)
        h_zero = jnp.zeros((bsz, SSD_STATE, inner), F32)
        ys_c, ys_l = [], []
        for dr in range(2):
            y_c, st = ssd_scan(xa_c, dt_c, bias_row, aneg_row, expand_dir[dr], h_zero, dr)
            y_l, _ = ssd_scan(xa_l, dt_l, bias_row, aneg_row, expand_dir[dr], st, dr)
            ys_c.append(y_c)
            ys_l.append(y_l)
        ya_l = ssd_out(ys_l[0], ys_l[1], xa_l, z_l, d_row, nw_row)

        filt = (hy_w1[layer], hy_b1[layer], hy_w2[layer], hy_b2[layer], hy_w3[layer], hy_freq[layer])
        u_l = dwconv3(hy_l, hy_conv_w[layer], hy_conv_b[layer], False)
        yb_l = hyena_long(u_l, hyena_taps(length, *filt), hy_bias[layer], tabs)

        qr_l = rope(q_l, rope_tabs, qscale)
        kr_l, kn_l = rope(k_l, rope_tabs, 1.0, with_norms=True)
        kb_c = scale_cast(k_c, 1.0)
        yc_l = diff_attention(qr_l, kr_l, v_l, kb_c, v_c, kn_l, da_lambda[layer], da_subln_w[layer], lam_init)

        x_l_new = merge(ya_l, yb_l, yc_l, g_l, wb, wo, x_l, mods_l)
        x_l = ffn(x_l_new, norm2_g[layer], mods_l, w13, w2)

        if not last:
            ya_c = ssd_out(ys_c[0], ys_c[1], xa_c, z_c, d_row, nw_row)
            u_c = dwconv3(hy_c, hy_conv_w[layer], hy_conv_b[layer], False)
            yb_c = hyena_ctx(u_c, hyena_taps(ctx_len, *filt), hy_bias[layer])
            qb_c = scale_cast(q_c, qscale)
            yc_c = diff_attention(qb_c, kb_c, v_c, None, None, None, da_lambda[layer], da_subln_w[layer], lam_init)
            x_c_new = merge(ya_c, yb_c, yc_c, g_c, wb, wo, x_c, mods_c)
            x_c = ffn(x_c_new, norm2_g[layer], mods_c, w13, w2)
    return final_norm(x_l, final_g)
```

```python
import functools
import math

import jax
import jax.numpy as jnp
from jax import lax
from jax.experimental import pallas as pl
from jax.experimental.pallas import tpu as pltpu

F32 = jnp.float32
BF16 = jnp.bfloat16
EPS = 1e-6

SSD_HEADS = 16
SSD_HEAD_DIM = 64
SSD_GROUPS = 2
SSD_STATE = 128
SSD_CHUNK = 128
HY_ORDER = 2
HY_BANDS = 16
HY_SHORT_DECAY_PCT = 0.3
HY_LONG_DECAY_PCT = 1.5
HY_DECAY_TARGET = 1e-2
DA_HEADS = 8
DA_V_DIM = 128
DA_HEAD_DIM = 64
ROPE_AXIS_DIM = 32
ROPE_THETA = 10000.0
GRID_W = 64
N_BRANCH = 3

LANES = 128
SUBLANES = 8
FFT_N2 = 128
VMEM_LIMIT = 56 * 1024 * 1024


def _cparams(*sem):
    return pltpu.CompilerParams(dimension_semantics=sem, vmem_limit_bytes=VMEM_LIMIT)


def _dot(a, b):
    return jnp.dot(a, b, preferred_element_type=F32)


def _dot_nt(a, b):
    return lax.dot_general(a, b, (((1,), (1,)), ((), ())), preferred_element_type=F32)


def _split3(x):
    hi = x.astype(BF16)
    r1 = x - hi.astype(F32)
    mid = r1.astype(BF16)
    lo = (r1 - mid.astype(F32)).astype(BF16)
    return hi, mid, lo


def _silu(x):
    return x * jax.nn.sigmoid(x)


def _softplus(x):
    return jnp.maximum(x, 0.0) + jnp.log1p(jnp.exp(-jnp.abs(x)))


def _pick_tile(n, pref):
    t = min(pref, n)
    while n % t:
        t //= 2
    return t


def _matmul_kernel(a_ref, w_ref, b_ref, o_ref, *, silu_in):
    a = a_ref[...]
    if silu_in:
        a = _silu(a.astype(F32))
    r = _dot(a.astype(BF16), w_ref[...]) + b_ref[...]
    o_ref[...] = r.astype(o_ref.dtype)


def matmul(a, w, bias=None, out_dtype=F32, silu_in=False, tm=2048, tn=1024):
    m, k = a.shape
    n = w.shape[1]
    tm = _pick_tile(m, tm)
    tn = _pick_tile(n, tn)
    if bias is None:
        bias = jnp.zeros((1, n), F32)
    return pl.pallas_call(
        functools.partial(_matmul_kernel, silu_in=silu_in),
        grid=(m // tm, n // tn),
        in_specs=[pl.BlockSpec((tm, k), lambda i, j: (i, 0)),
                  pl.BlockSpec((k, tn), lambda i, j: (0, j)),
                  pl.BlockSpec((1, tn), lambda i, j: (0, j))],
        out_specs=pl.BlockSpec((tm, tn), lambda i, j: (i, j)),
        out_shape=jax.ShapeDtypeStruct((m, n), out_dtype),
        compiler_params=_cparams("parallel", "parallel"),
        name="matmul",
    )(a, w, bias)


def _rms(x, g):
    return x * lax.rsqrt(jnp.mean(x * x, axis=-1, keepdims=True) + EPS) * g


def _norm_mod_kernel(x_ref, g_ref, mod_ref, o_ref, *, row0):
    x = x_ref[...]
    y = _rms(x, g_ref[...])
    shift = mod_ref[row0:row0 + 1, :]
    scale = mod_ref[row0 + 1:row0 + 2, :]
    o_ref[...] = (y * (1.0 + scale) + shift).astype(o_ref.dtype)


def norm_mod(x, g, mods, row0, tr=512):
    b, t, d = x.shape
    tr = _pick_tile(t, tr)
    per_batch = mods.shape[0] > 1
    return pl.pallas_call(
        functools.partial(_norm_mod_kernel, row0=row0),
        grid=(b, t // tr),
        in_specs=[pl.BlockSpec((None, tr, d), lambda bi, i: (bi, i, 0)),
                  pl.BlockSpec((1, d), lambda bi, i: (0, 0)),
                  pl.BlockSpec((None, 8, d), (lambda bi, i: (bi, 0, 0)) if per_batch else (lambda bi, i: (0, 0, 0)))],
        out_specs=pl.BlockSpec((None, tr, d), lambda bi, i: (bi, i, 0)),
        out_shape=jax.ShapeDtypeStruct((b, t, d), BF16),
        compiler_params=_cparams("parallel", "parallel"),
        name="norm_mod",
    )(x, g.reshape(1, d), mods)


def _final_norm_kernel(x_ref, g_ref, o_ref):
    o_ref[...] = _rms(x_ref[...], g_ref[...])


def final_norm(x, g, tr=512):
    b, t, d = x.shape
    tr = _pick_tile(t, tr)
    return pl.pallas_call(
        _final_norm_kernel,
        grid=(b, t // tr),
        in_specs=[pl.BlockSpec((None, tr, d), lambda bi, i: (bi, i, 0)),
                  pl.BlockSpec((1, d), lambda bi, i: (0, 0))],
        out_specs=pl.BlockSpec((None, tr, d), lambda bi, i: (bi, i, 0)),
        out_shape=jax.ShapeDtypeStruct((b, t, d), F32),
        compiler_params=_cparams("parallel", "parallel"),
        name="final_norm",
    )(x, g.reshape(1, d))


def _dwconv_kernel(x_ref, xp_ref, xn_ref, w_ref, b_ref, o_ref, *, act, nblk):
    i = pl.program_id(1)
    x = x_ref[...].astype(F32)
    tr = x.shape[0]
    halo = xp_ref.shape[0]
    prev_row = jnp.where(i == 0, 0.0, xp_ref[halo - 1:halo, :].astype(F32))
    next_row = jnp.where(i == nblk - 1, 0.0, xn_ref[0:1, :].astype(F32))
    rows = lax.broadcasted_iota(jnp.int32, x.shape, 0)
    x_m1 = jnp.where(rows == 0, prev_row, pltpu.roll(x, 1, 0))
    x_p1 = jnp.where(rows == tr - 1, next_row, pltpu.roll(x, tr - 1, 0))
    y = b_ref[...] + x_m1 * w_ref[0:1, :] + x * w_ref[1:2, :] + x_p1 * w_ref[2:3, :]
    if act:
        y = _silu(y)
    o_ref[...] = y


def dwconv3(x, w, bias, act, tr=512, tc=1536):
    b, t, c = x.shape
    tr = _pick_tile(t, tr)
    tc = min(tc, c)
    assert c % tc == 0
    nblk = t // tr
    halo = SUBLANES * (4 // x.dtype.itemsize)
    rh = tr // halo
    lasth = t // halo - 1
    return pl.pallas_call(
        functools.partial(_dwconv_kernel, act=act, nblk=nblk),
        grid=(b, nblk, c // tc),
        in_specs=[pl.BlockSpec((None, tr, tc), lambda bi, i, j: (bi, i, j)),
                  pl.BlockSpec((None, halo, tc), lambda bi, i, j: (bi, jnp.maximum(i * rh - 1, 0), j)),
                  pl.BlockSpec((None, halo, tc), lambda bi, i, j: (bi, jnp.minimum((i + 1) * rh, lasth), j)),
                  pl.BlockSpec((3, tc), lambda bi, i, j: (0, j)),
                  pl.BlockSpec((1, tc), lambda bi, i, j: (0, j))],
        out_specs=pl.BlockSpec((None, tr, tc), lambda bi, i, j: (bi, i, j)),
        out_shape=jax.ShapeDtypeStruct((b, t, c), F32),
        compiler_params=_cparams("parallel", "parallel", "parallel"),
        name="dwconv3",
    )(x, x, x, w, bias.reshape(1, c))


def _ssd_kernel(x_ref, b_ref, c_ref, dt_ref, bias_ref, aneg_ref, e_ref, h0_ref,
                y_ref, hout_ref, st_ref, *, direction, nchunks):
    ci = pl.program_id(1)

    @pl.when(ci == 0)
    def _():
        st_ref[...] = h0_ref[...]

    q = x_ref.shape[0]
    hd = SSD_HEAD_DIM
    gw = (SSD_HEADS // SSD_GROUPS) * hd
    x = x_ref[...]
    dtv = _softplus(dt_ref[...] + bias_ref[...])
    adt = dtv * aneg_ref[...]
    row = lax.broadcasted_iota(jnp.int32, (q, q), 0)
    col = lax.broadcasted_iota(jnp.int32, (q, q), 1)
    mask = (col >= row) if direction else (col <= row)
    tri = mask.astype(BF16)
    acum = sum(_dot(tri, p) for p in _split3(adt))
    acum_t = sum(_dot_nt(p, tri) for p in _split3(adt.T))
    total = jnp.sum(adt, axis=0, keepdims=True)
    eac = jnp.exp(acum)
    dte = jnp.exp(total - acum)
    cd = jnp.broadcast_to(jnp.exp(total), (SUBLANES, LANES))
    stack = jnp.concatenate([dtv, eac, dte, cd], axis=0)
    ex = _dot(stack.astype(BF16), e_ref[...])
    dt_e, eac_e, dte_e, cd_e = ex[:q], ex[q:2 * q], ex[2 * q:3 * q], ex[3 * q:3 * q + 1]
    xdt = x * dt_e
    lane = lax.broadcasted_iota(jnp.int32, (q, LANES), 1)
    first = lane < hd
    for g in range(SSD_GROUPS):
        gs = slice(g * gw, (g + 1) * gw)
        c_g = c_ref[:, g * SSD_STATE:(g + 1) * SSD_STATE].astype(BF16)
        b_g32 = b_ref[:, g * SSD_STATE:(g + 1) * SSD_STATE]
        s_g = st_ref[:, gs]
        y_off = _dot(c_g, s_g.astype(BF16)) * eac_e[:, gs]
        cb = _dot_nt(c_g, b_g32.astype(BF16))
        for jp in range(gw // LANES):
            h_a = g * (SSD_HEADS // SSD_GROUPS) + 2 * jp
            ls = slice(h_a * hd, h_a * hd + LANES)
            xpair = xdt[:, ls]
            acc = y_off[:, jp * LANES:(jp + 1) * LANES]
            for k in range(2):
                cix = direction * SSD_HEADS + h_a + k
                seg = acum[:, cix:cix + 1] - acum_t[cix:cix + 1, :]
                dec = jnp.where(mask, jnp.exp(jnp.where(mask, seg, 0.0)), 0.0)
                w = (cb * dec).astype(BF16)
                xk = jnp.where(first if k == 0 else jnp.logical_not(first), xpair, 0.0).astype(BF16)
                acc = acc + _dot(w, xk)
            y_ref[:, ls] = acc
        xdte = (xdt[:, gs] * dte_e[:, gs]).astype(BF16)
        st_ref[:, gs] = s_g * cd_e[:, gs] + _dot(b_g32.T.astype(BF16), xdte)

    @pl.when(ci == nchunks - 1)
    def _():
        hout_ref[...] = st_ref[...]


def ssd_scan(xbc, dt_raw, dt_bias_row, aneg_row, expand, h0, direction):
    b, t, _ = xbc.shape
    q = SSD_CHUNK
    nchunks = t // q
    inner = SSD_HEADS * SSD_HEAD_DIM
    bn = SSD_GROUPS * SSD_STATE
    cmap = (lambda c: nchunks - 1 - c) if direction else (lambda c: c)
    y, hout = pl.pallas_call(
        functools.partial(_ssd_kernel, direction=direction, nchunks=nchunks),
        grid=(b, nchunks),
        in_specs=[pl.BlockSpec((None, q, inner), lambda bi, c: (bi, cmap(c), 0)),
                  pl.BlockSpec((None, q, bn), lambda bi, c: (bi, cmap(c), inner // bn)),
                  pl.BlockSpec((None, q, bn), lambda bi, c: (bi, cmap(c), inner // bn + 1)),
                  pl.BlockSpec((None, q, LANES), lambda bi, c: (bi, cmap(c), 0)),
                  pl.BlockSpec((1, LANES), lambda bi, c: (0, 0)),
                  pl.BlockSpec((1, LANES), lambda bi, c: (0, 0)),
                  pl.BlockSpec((LANES, inner), lambda bi, c: (0, 0)),
                  pl.BlockSpec((None, SSD_STATE, inner), lambda bi, c: (bi, 0, 0))],
        out_specs=[pl.BlockSpec((None, q, inner), lambda bi, c: (bi, cmap(c), 0)),
                   pl.BlockSpec((None, SSD_STATE, inner), lambda bi, c: (bi, 0, 0))],
        out_shape=[jax.ShapeDtypeStruct((b, t, inner), F32),
                   jax.ShapeDtypeStruct((b, SSD_STATE, inner), F32)],
        scratch_shapes=[pltpu.VMEM((SSD_STATE, inner), F32)],
        compiler_params=_cparams("parallel", "arbitrary"),
        name="ssd_scan_bwd" if direction else "ssd_scan_fwd",
    )(xbc, xbc, xbc, dt_raw, dt_bias_row, aneg_row, expand, h0)
    return y, hout


def _ssd_out_kernel(yf_ref, yb_ref, x_ref, z_ref, d_ref, w_ref, o_ref):
    y = yf_ref[...] + yb_ref[...] + d_ref[...] * x_ref[...]
    y = y * _silu(z_ref[...].astype(F32))
    gw = y.shape[1] // SSD_GROUPS
    for g in range(SSD_GROUPS):
        gs = slice(g * gw, (g + 1) * gw)
        yg = y[:, gs]
        yg = yg * lax.rsqrt(jnp.mean(yg * yg, axis=-1, keepdims=True) + EPS)
        o_ref[:, gs] = (yg * w_ref[:, gs]).astype(o_ref.dtype)


def ssd_out(yf, yb, xbc, z, d_row, w_row, tr=256):
    b, t, inner = yf.shape
    tr = _pick_tile(t, tr)
    blk = pl.BlockSpec((None, tr, inner), lambda bi, i: (bi, i, 0))
    vec = pl.BlockSpec((1, inner), lambda bi, i: (0, 0))
    return pl.pallas_call(
        _ssd_out_kernel,
        grid=(b, t // tr),
        in_specs=[blk, blk, blk, blk, vec, vec],
        out_specs=blk,
        out_shape=jax.ShapeDtypeStruct((b, t, inner), BF16),
        compiler_params=_cparams("parallel", "parallel"),
        name="ssd_out",
    )(yf, yb, xbc, z, d_row, w_row)


def _taps_kernel(f_ref, tv_ref, w1_ref, b1_ref, w2_ref, b2_ref, w3_ref, fr_ref, dl_ref, o_ref):
    hp = lax.Precision.HIGHEST
    fr = fr_ref[...]
    h = jnp.sin(fr * (jnp.dot(f_ref[...], w1_ref[...], precision=hp, preferred_element_type=F32) + b1_ref[...]))
    h = jnp.sin(fr * (jnp.dot(h, w2_ref[...], precision=hp, preferred_element_type=F32) + b2_ref[...]))
    filt = _dot(h.astype(BF16), w3_ref[...].astype(BF16))
    c = dl_ref.shape[1]
    win = jnp.exp(-tv_ref[:, 0:1] * dl_ref[...]) * tv_ref[:, 1:2]
    o_ref[0] = filt[:, :c] * win
    o_ref[1] = filt[:, c:] * win


def hyena_taps(length, w1, b1, w2, b2, w3, freq):
    hid = w1.shape[1]
    c = w3.shape[1] // (2 * HY_ORDER)
    n = 2 * length
    t = jnp.linspace(0.0, 1.0, length, dtype=F32)[:, None]
    phase = 2.0 * math.pi * jnp.arange(length, dtype=F32)[:, None] / length
    bands = jnp.linspace(1e-4, HY_BANDS - 1, HY_BANDS, dtype=F32)[None, :]
    feats = jnp.concatenate([t, jnp.cos(phase * bands), -jnp.sin(phase * bands)], axis=-1)
    nf = feats.shape[1]
    rev = jnp.concatenate([jnp.zeros((1,), jnp.int32), jnp.arange(length - 1, 0, -1, dtype=jnp.int32)])
    feats2 = jnp.concatenate([feats, feats[rev]], axis=0)
    feats2 = jnp.pad(feats2, ((0, 0), (0, LANES - nf)))
    valid = jnp.ones((n,), F32).at[length].set(0.0)
    tv = jnp.stack([jnp.concatenate([t[:, 0], t[rev, 0]]), valid], axis=1)
    max_decay = math.log(HY_DECAY_TARGET) / HY_SHORT_DECAY_PCT
    min_decay = math.log(HY_DECAY_TARGET) / HY_LONG_DECAY_PCT
    deltas = jnp.abs(jnp.linspace(min_decay, max_decay, c, dtype=F32))[None, :]
    w1p = jnp.pad(w1, ((0, LANES - nf), (0, 0)))
    w3r = w3.reshape(hid, HY_ORDER, 2, c).transpose(2, 0, 1, 3).reshape(2, hid, HY_ORDER * c)
    tr = _pick_tile(length, 1024)
    nhalf = length // tr
    return pl.pallas_call(
        _taps_kernel,
        grid=(n // tr,),
        in_specs=[pl.BlockSpec((tr, LANES), lambda i: (i, 0)),
                  pl.BlockSpec((tr, 2), lambda i: (i, 0)),
                  pl.BlockSpec((LANES, hid), lambda i: (0, 0)),
                  pl.BlockSpec((1, hid), lambda i: (0, 0)),
                  pl.BlockSpec((hid, hid), lambda i: (0, 0)),
                  pl.BlockSpec((1, hid), lambda i: (0, 0)),
                  pl.BlockSpec((None, hid, HY_ORDER * c), lambda i: (i // nhalf, 0, 0)),
                  pl.BlockSpec((1, hid), lambda i: (0, 0)),
                  pl.BlockSpec((1, c), lambda i: (0, 0))],
        out_specs=pl.BlockSpec((HY_ORDER, tr, c), lambda i: (0, i, 0)),
        out_shape=jax.ShapeDtypeStruct((HY_ORDER, n, c), F32),
        compiler_params=_cparams("parallel"),
        name="hyena_taps",
    )(feats2, tv, w1p, b1.reshape(1, hid), w2, b2.reshape(1, hid), w3r, freq.reshape(1, hid), deltas)


def _cs(num, den):
    ang = (2.0 * math.pi / den) * num.astype(F32)
    return jnp.cos(ang), jnp.sin(ang)


def fft_tables(n1, n2):
    n = n1 * n2
    h = n1 // 2
    k1 = jnp.arange(n1, dtype=jnp.int32)
    c1, s1 = _cs((k1[:, None] * k1[None, :]) % n1, n1)
    g_u = jnp.block([[c1[:, :h], s1[:, :h]], [-s1[:, :h], c1[:, :h]]])
    g_t = jnp.concatenate([c1, -s1], axis=0)
    g_i = jnp.block([[c1[:h, :], -s1[:h, :]], [s1[:h, :], c1[:h, :]]])
    s = jnp.arange(n2, dtype=jnp.int32)
    num = (n1 * (s[:, None] * s[None, :]))[None] + (k1[:, None, None] * s[None, None, :])
    cb, sb = _cs(num % n, n)
    m_f = jnp.concatenate([jnp.concatenate([cb, sb], axis=2),
                           jnp.concatenate([-sb, cb], axis=2)], axis=1)
    cbt, sbt = jnp.swapaxes(cb, 1, 2) / n, jnp.swapaxes(sb, 1, 2) / n
    m_i = jnp.concatenate([jnp.concatenate([cbt, -sbt], axis=2),
                           jnp.concatenate([sbt, cbt], axis=2)], axis=1)
    return dict(g_u=g_u.astype(BF16), g_t=g_t.astype(BF16), g_i=g_i.astype(BF16),
                m_f=m_f.astype(BF16), m_i=m_i.astype(BF16))


FFT_KGROUP = SUBLANES


def _rows_get(ref, start, n, stride):
    return ref.reshape(math.prod(ref.shape[:-1]), ref.shape[-1])[pl.ds(start, n, stride=stride), :]


def _rows_set(ref, start, n, stride, val):
    ref.reshape(math.prod(ref.shape[:-1]), ref.shape[-1])[pl.ds(start, n, stride=stride), :] = val


def _cpack(re, im):
    rb = lax.bitcast_convert_type(re.astype(BF16).astype(F32), jnp.uint32)
    ib = lax.bitcast_convert_type(im.astype(BF16).astype(F32), jnp.uint32)
    return (rb & jnp.uint32(0xFFFF0000)) | (ib >> 16)


def _cunpack(w):
    re = lax.bitcast_convert_type(w & jnp.uint32(0xFFFF0000), F32)
    im = lax.bitcast_convert_type(w << 16, F32)
    return re, im


FFT_LANE_BLOCKS = 2


def _fft_a_kernel(g_ref, *refs, s_major_in):
    *x_refs, o_ref = refs
    xa_refs, xb_refs = x_refs[:len(x_refs) // 2], x_refs[len(x_refs) // 2:]
    h = g_ref.shape[1] // 2
    ts, n1, _ = o_ref.shape
    g = g_ref[...]

    def rows(x_refs_, s):
        if s_major_in:
            return x_refs_[0][s]
        return jnp.concatenate([_rows_get(r, s, h, ts) for r in x_refs_], axis=1)

    for s in range(ts):
        x = jnp.concatenate([rows(xa_refs, s), rows(xb_refs, s)], axis=0).astype(BF16)
        r = _dot(g, x)
        o_ref[s] = _cpack(r[:n1], r[n1:])


def fft_stage_a(g, x4, sel_a, sel_b, col_off, c, s_major_in=False):
    n1 = g.shape[0] // 2
    h = n1 // 2
    n2 = x4.shape[1] if s_major_in else x4.shape[2]
    ts = SUBLANES
    nl = FFT_LANE_BLOCKS
    tc = nl * LANES
    if s_major_in:
        coff = col_off // tc
        specs = lambda sel: [pl.BlockSpec((None, ts, h, tc), lambda j, s: (sel[0], s, sel[1], j + coff))]
    else:
        coff = col_off // LANES
        specs = lambda sel: [pl.BlockSpec((None, h, ts, LANES),
                                          functools.partial(lambda j, s, l: (sel[0], sel[1], s, nl * j + l + coff), l=l))
                             for l in range(nl)]
    x_specs = specs(sel_a) + specs(sel_b)
    return pl.pallas_call(
        functools.partial(_fft_a_kernel, s_major_in=s_major_in),
        grid=(c // tc, n2 // ts),
        in_specs=[pl.BlockSpec((2 * n1, n1), lambda j, s: (0, 0))] + x_specs,
        out_specs=pl.BlockSpec((ts, n1, tc), lambda j, s: (s, 0, j)),
        out_shape=jax.ShapeDtypeStruct((n2, n1, c), jnp.uint32),
        compiler_params=_cparams("parallel", "parallel"),
        name="fft_stage_a",
    )(g, *([x4] * len(x_specs)))


def _fft_load_k(a_refs, j):
    n2, kb, _ = a_refs[0].shape
    re, im = _cunpack(jnp.concatenate([_rows_get(r, j, n2, kb) for r in a_refs], axis=1))
    return jnp.concatenate([re, im], axis=0).astype(BF16)


def _fft_bh_kernel(m_ref, *refs, groups):
    *a_refs, o_ref = refs
    n2, kb, _ = a_refs[0].shape
    nl = len(a_refs) // groups
    for g in range(groups):
        for j in range(kb):
            x = _dot(m_ref[g * kb + j], _fft_load_k(a_refs[g * nl:(g + 1) * nl], j))
            o_ref[g * kb + j] = _cpack(x[:n2], x[n2:])


def _fft_b_kernel(mf_ref, mi_ref, *refs, groups):
    *a_refs, h_ref, o_ref = refs
    n2, kb, _ = a_refs[0].shape
    nl = len(a_refs) // groups
    for g in range(groups):
        for j in range(kb):
            k = g * kb + j
            x = _dot(mf_ref[k], _fft_load_k(a_refs[g * nl:(g + 1) * nl], j))
            xr, xi = x[:n2], x[n2:]
            hr, hi = _cunpack(h_ref[k])
            y = jnp.concatenate([xr * hr - xi * hi, xr * hi + xi * hr], axis=0).astype(BF16)
            r = _dot(mi_ref[k], y)
            o_ref[k] = _cpack(r[:n2], r[n2:])


def fft_stage_b(m_f, m_i, a, hspec, groups=2):
    n2, n1, c = a.shape
    groups = min(groups, n1 // FFT_KGROUP)
    kb = FFT_KGROUP * groups
    nl = FFT_LANE_BLOCKS
    tc = nl * LANES
    ablks = [pl.BlockSpec((n2, FFT_KGROUP, LANES),
                          functools.partial(lambda k, j, g, l: (0, groups * k + g, nl * j + l), g=g, l=l))
             for g in range(groups) for l in range(nl)]
    kblk = pl.BlockSpec((kb, n2, tc), lambda k, j: (k, 0, j))
    mat = pl.BlockSpec((kb, 2 * n2, 2 * n2), lambda k, j: (k, 0, 0))
    if hspec is None:
        kern, in_specs, name = _fft_bh_kernel, [mat] + ablks, "fft_stage_b_spectrum"
        args = (m_f,) + (a,) * len(ablks)
    else:
        kern, in_specs, name = _fft_b_kernel, [mat, mat] + ablks + [kblk], "fft_stage_b"
        args = (m_f, m_i) + (a,) * len(ablks) + (hspec,)
    return pl.pallas_call(
        functools.partial(kern, groups=groups),
        grid=(n1 // kb, c // tc),
        in_specs=in_specs,
        out_specs=kblk,
        out_shape=jax.ShapeDtypeStruct((n1, n2, c), jnp.uint32),
        compiler_params=_cparams("parallel", "parallel"),
        name=name,
    )(*args)


def _fft_ai_kernel(g_ref, b_ref, u_ref, x_ref, skip_ref, o_ref, *, u_s_major, out_s_major):
    n1, ts, _ = b_ref.shape
    h = n1 // 2
    ga = g_ref[:, :n1]
    gb = g_ref[:, n1:]
    skip = skip_ref[...]
    for s in range(ts):
        br, bi_ = _cunpack(_rows_get(b_ref, s, n1, ts))
        y = _dot(ga, br.astype(BF16)) + _dot(gb, bi_.astype(BF16))
        for bi in range(2):
            start = bi * h * ts + s
            u = u_ref[bi, s] if u_s_major else _rows_get(u_ref, start, h, ts)
            val = _rows_get(x_ref, start, h, ts) * (y[bi * h:(bi + 1) * h] + u * skip)
            if out_s_major:
                o_ref[bi, s] = val
            else:
                _rows_set(o_ref, start, h, ts, val)


def fft_stage_a_inv(g_i, bm, u4, u_off, u_s_major, x4, x_off, skip_row, out_s_major, tc=LANES):
    n1, n2, c = bm.shape
    h = n1 // 2
    ts = SUBLANES
    uo, xo = u_off // tc, x_off // tc
    nat = lambda off: pl.BlockSpec((2, h, ts, tc), lambda j, s: (0, 0, s, j + off))
    smj = lambda off: pl.BlockSpec((2, ts, h, tc), lambda j, s: (0, s, 0, j + off))
    return pl.pallas_call(
        functools.partial(_fft_ai_kernel, u_s_major=u_s_major, out_s_major=out_s_major),
        grid=(c // tc, n2 // ts),
        in_specs=[pl.BlockSpec((n1, 2 * n1), lambda j, s: (0, 0)),
                  pl.BlockSpec((n1, ts, tc), lambda j, s: (0, s, j)),
                  smj(uo) if u_s_major else nat(uo),
                  nat(xo),
                  pl.BlockSpec((1, tc), lambda j, s: (0, j))],
        out_specs=smj(0) if out_s_major else nat(0),
        out_shape=jax.ShapeDtypeStruct((2, n2, h, c) if out_s_major else (2, h, n2, c), F32),
        compiler_params=_cparams("parallel", "parallel"),
        name="fft_stage_a_inv",
    )(g_i, bm, u4, x4, skip_row)


def hyena_long(u, taps, skip, tabs):
    b, length, c3 = u.shape
    assert b == 2, "the two batch rows are packed as one complex sequence"
    c = c3 // 3
    n2 = FFT_N2
    n1 = 2 * length // n2
    h = n1 // 2
    u4 = u.reshape(b, h, n2, c3)
    taps4 = taps.reshape(HY_ORDER * 2, h, n2, c)
    zin, zoff, z_s_major = u4, 0, False
    out = None
    for conv in range(HY_ORDER):
        last = conv == HY_ORDER - 1
        hspec = fft_stage_b(tabs["m_f"], None,
                            fft_stage_a(tabs["g_t"], taps4, (2 * conv, 0), (2 * conv + 1, 0), 0, c), None)
        a = fft_stage_a(tabs["g_u"], zin, (0, 0), (1, 0), zoff, c, s_major_in=z_s_major)
        bm = fft_stage_b(tabs["m_f"], tabs["m_i"], a, hspec)
        out = fft_stage_a_inv(tabs["g_i"], bm, zin, zoff, z_s_major, u4, (conv + 1) * c,
                              skip[conv].reshape(1, c), not last)
        zin, zoff, z_s_major = out, 0, True
    return out.reshape(b, length, c)


def _hyena_ctx_kernel(u_ref, x1_ref, x2_ref, taps_ref, fc_ref, fs_ref, skip_ref, o_ref):
    length = u_ref.shape[1]
    n = 2 * length
    fc = fc_ref[...]
    fs = fs_ref[...]
    fcl, fsl = fc[:, :length], fs[:, :length]
    fct, fst = fc[:length, :], fs[:length, :]
    zr = u_ref[0]
    zi = u_ref[1]
    gates = (x1_ref, x2_ref)
    for conv in range(HY_ORDER):
        tp = taps_ref[conv].astype(BF16)
        hr, hi = _dot(fc, tp), -_dot(fs, tp)
        zrb, zib = zr.astype(BF16), zi.astype(BF16)
        xr = _dot(fcl, zrb) + _dot(fsl, zib)
        xi = _dot(fcl, zib) - _dot(fsl, zrb)
        yr = (xr * hr - xi * hi).astype(BF16)
        yi = (xr * hi + xi * hr).astype(BF16)
        cr = (_dot(fct, yr) - _dot(fst, yi)) * (1.0 / n)
        ci = (_dot(fct, yi) + _dot(fst, yr)) * (1.0 / n)
        sk = skip_ref[conv:conv + 1, :]
        zr = gates[conv][0] * (cr + zr * sk)
        zi = gates[conv][1] * (ci + zi * sk)
    o_ref[0] = zr.astype(o_ref.dtype)
    o_ref[1] = zi.astype(o_ref.dtype)


def hyena_ctx(u, taps, skip, tc=256):
    b, length, c3 = u.shape
    assert b == 2
    c = c3 // 3
    n = 2 * length
    k = jnp.arange(n, dtype=jnp.int32)
    fc, fs = _cs((k[:, None] * k[None, :]) % n, n)
    nb = c // tc
    ub = lambda off: pl.BlockSpec((2, length, tc), lambda j: (0, 0, j + off * nb))
    mat = pl.BlockSpec((n, n), lambda j: (0, 0))
    return pl.pallas_call(
        _hyena_ctx_kernel,
        grid=(nb,),
        in_specs=[ub(0), ub(1), ub(2),
                  pl.BlockSpec((HY_ORDER, n, tc), lambda j: (0, 0, j)),
                  mat, mat,
                  pl.BlockSpec((HY_ORDER, tc), lambda j: (0, j))],
        out_specs=pl.BlockSpec((2, length, tc), lambda j: (0, 0, j)),
        out_shape=jax.ShapeDtypeStruct((b, length, c), BF16),
        compiler_params=_cparams("parallel"),
        name="hyena_ctx",
    )(u, u, u, taps, fc.astype(BF16), fs.astype(BF16), skip)


def rope_tables(length):
    rows = length // GRID_W
    row = jnp.repeat(jnp.arange(rows, dtype=F32), GRID_W)
    col = (jnp.arange(length) % GRID_W).astype(F32)
    inv = ROPE_THETA ** (-jnp.arange(0, ROPE_AXIS_DIM, 2, dtype=F32) / ROPE_AXIS_DIM)
    ang = jnp.stack([row[:, None] * inv, col[:, None] * inv], axis=1)
    cos, sin = jnp.cos(ang), jnp.sin(ang)
    zero = jnp.zeros_like(sin)
    per_map = lambda lo, hi: jnp.concatenate([lo, hi], axis=-1).reshape(length, 2 * ROPE_AXIS_DIM)
    reps = LANES // (2 * ROPE_AXIS_DIM)
    c_t = jnp.tile(per_map(cos, cos), (1, reps))
    sm_t = jnp.tile(per_map(-sin, zero), (1, reps))
    sp_t = jnp.tile(per_map(zero, sin), (1, reps))
    return c_t, sm_t, sp_t


def _rope_kernel(x_ref, c_ref, sm_ref, sp_ref, o_ref, *n_refs, scale):
    half = ROPE_AXIS_DIM // 2
    c, sm, sp = c_ref[...], sm_ref[...], sp_ref[...]
    for g in range(x_ref.shape[1] // LANES):
        ls = slice(g * LANES, (g + 1) * LANES)
        x = x_ref[:, ls].astype(F32)
        y = (x * c + pltpu.roll(x, LANES - half, 1) * sm + pltpu.roll(x, half, 1) * sp) * scale
        o_ref[:, ls] = y.astype(o_ref.dtype)
        if n_refs:
            n2 = jnp.max(jnp.sum(y * y, axis=1, keepdims=True), axis=0, keepdims=True)
            n_refs[0][g:g + 1, :] = jnp.broadcast_to(n2, (1, LANES))


def rope(x, tables, scale, with_norms=False, tr=512):
    b, t, d = x.shape
    tr = _pick_tile(t, tr)
    nh = d // LANES
    tab = pl.BlockSpec((tr, LANES), lambda bi, i: (i, 0))
    out_specs = [pl.BlockSpec((None, tr, d), lambda bi, i: (bi, i, 0))]
    out_shape = [jax.ShapeDtypeStruct((b, t, d), BF16)]
    if with_norms:
        out_specs.append(pl.BlockSpec((None, None, nh, LANES), lambda bi, i: (bi, i, 0, 0)))
        out_shape.append(jax.ShapeDtypeStruct((b, t // tr, nh, LANES), F32))
    outs = pl.pallas_call(
        functools.partial(_rope_kernel, scale=scale),
        grid=(b, t // tr),
        in_specs=[pl.BlockSpec((None, tr, d), lambda bi, i: (bi, i, 0)), tab, tab, tab],
        out_specs=out_specs,
        out_shape=out_shape,
        compiler_params=_cparams("parallel", "parallel"),
        name="rope",
    )(x, *tables)
    if with_norms:
        return outs[0], jnp.swapaxes(outs[1], 1, 2)
    return outs[0]


def _scale_cast_kernel(x_ref, o_ref, *, scale):
    o_ref[...] = (x_ref[...].astype(F32) * scale).astype(o_ref.dtype)


def scale_cast(x, scale, tr=256):
    b, t, d = x.shape
    tr = _pick_tile(t, tr)
    blk = pl.BlockSpec((None, tr, d), lambda bi, i: (bi, i, 0))
    return pl.pallas_call(
        functools.partial(_scale_cast_kernel, scale=scale),
        grid=(b, t // tr), in_specs=[blk], out_specs=blk,
        out_shape=jax.ShapeDtypeStruct((b, t, d), BF16),
        compiler_params=_cparams("parallel", "parallel"),
        name="scale_cast",
    )(x)


ATTN_GUARD_LOG2 = 80.0
ATTN_NORM_SLACK = 1.03


def _attn_kernel(*refs, lam_init, nk, with_ctx):
    if with_ctx:
        (q_ref, k_ref, v_ref, kc_ref, vc_ref, kn_ref, lam_ref, w_ref, o_ref,
         q2_ref, m_ref, c_ref, acc_ref, p_scr, flag_ref) = refs
    else:
        q_ref, k_ref, v_ref, lam_ref, w_ref, o_ref, q2_ref, m_ref, c_ref, acc_ref, p_scr, flag_ref = refs
    ki = pl.program_id(3)
    tq = q_ref.shape[0]

    def with_ones(v):
        return jnp.concatenate([v, jnp.ones_like(v)], axis=1)

    def exact_update(k, v, p_buf):
        s = _dot_nt(q2_ref[...], k)
        m_prev = m_ref[...]
        m_new = jnp.maximum(m_prev, jnp.max(s, axis=1, keepdims=True))
        alpha = jnp.exp2(m_prev - m_new)
        p_buf[...] = jnp.exp2(s - m_new).astype(BF16)
        acc_ref[...] = alpha * acc_ref[...] + _dot(p_buf[...], with_ones(v))
        m_ref[...] = m_new

    def fast_update(k, v, p_buf):
        s = _dot_nt(q2_ref[...], k)
        p_buf[...] = jnp.exp2(s - pltpu.repeat(c_ref[...], k.shape[0] // LANES, axis=1)).astype(BF16)
        acc_ref[...] += _dot(p_buf[...], with_ones(v))

    @pl.when(ki == 0)
    def _():
        q = q_ref[...]
        lane = lax.broadcasted_iota(jnp.int32, q.shape, 1)
        zero = jnp.zeros_like(q)
        q2_ref[:tq, :] = jnp.where(lane < DA_HEAD_DIM, q, zero)
        q2_ref[tq:, :] = jnp.where(lane >= DA_HEAD_DIM, q, zero)
        m_ref[...] = jnp.full(m_ref.shape, -jnp.inf, F32)
        acc_ref[...] = jnp.zeros(acc_ref.shape, F32)
        flag_ref[0] = 0
        if with_ctx:
            q2 = q2_ref[...]
            c = _dot_nt(q2, jnp.broadcast_to(kc_ref[0:1, :], (LANES, DA_V_DIM)))
            sq = q2.astype(F32) * q2.astype(F32)
            hi = sq.astype(BF16)
            lo = (sq - hi.astype(F32)).astype(BF16)
            ones = jnp.ones((DA_V_DIM, LANES), BF16)
            qn2 = _dot(hi, ones) + _dot(lo, ones)
            kn2 = jnp.max(kn_ref[...])
            gap = jnp.max(jnp.sqrt(qn2 * kn2) * ATTN_NORM_SLACK - c)
            c_ref[...] = c
            flag_ref[0] = (gap <= ATTN_GUARD_LOG2).astype(jnp.int32)

    fast = flag_ref[0] == 1
    if with_ctx:
        tc = kc_ref.shape[0]
        first = ki == 0

        @pl.when(jnp.logical_and(first, fast))
        def _():
            fast_update(kc_ref[...], vc_ref[...], p_scr.at[:, pl.ds(0, tc)])

        @pl.when(jnp.logical_and(first, jnp.logical_not(fast)))
        def _():
            exact_update(kc_ref[...], vc_ref[...], p_scr.at[:, pl.ds(0, tc)])

    @pl.when(fast)
    def _():
        fast_update(k_ref[...], v_ref[...], p_scr)

    @pl.when(jnp.logical_not(fast))
    def _():
        exact_update(k_ref[...], v_ref[...], p_scr)

    @pl.when(ki == nk - 1)
    def _():
        lv = lam_ref[...]
        e1 = jnp.exp(jnp.sum(lv[0:1, :] * lv[1:2, :], axis=1, keepdims=True))
        e2 = jnp.exp(jnp.sum(lv[2:3, :] * lv[3:4, :], axis=1, keepdims=True))
        lam = e1 - e2 + lam_init
        acc = acc_ref[...]
        o = acc[:, :DA_V_DIM] / acc[:, DA_V_DIM:]
        o = o[:tq] - lam * o[tq:]
        o_ref[...] = (_rms(o, w_ref[...]) * (1.0 - lam_init)).astype(o_ref.dtype)


def diff_attention(q, k, v, k_ctx, v_ctx, knorm, lam_vec, subln_w, lam_init, tq=1024, tk=2048):
    b, t, d = q.shape
    nh = d // DA_V_DIM
    tq = _pick_tile(t, tq)
    tk = _pick_tile(k.shape[1], tk)
    nk = k.shape[1] // tk
    with_ctx = k_ctx is not None
    assert not with_ctx or k_ctx.shape[1] <= tk
    qspec = pl.BlockSpec((None, tq, DA_V_DIM), lambda bi, h, i, j: (bi, i, h))
    kspec = pl.BlockSpec((None, tk, DA_V_DIM), lambda bi, h, i, j: (bi, j, h))
    in_specs = [qspec, kspec, kspec]
    args = [q, k, v]
    if with_ctx:
        cspec = pl.BlockSpec((None, k_ctx.shape[1], DA_V_DIM), lambda bi, h, i, j: (bi, 0, h))
        nspec = pl.BlockSpec((None, None) + knorm.shape[2:], lambda bi, h, i, j: (bi, h, 0, 0))
        in_specs += [cspec, cspec, nspec]
        args += [k_ctx, v_ctx, knorm]
    in_specs += [pl.BlockSpec((4, DA_HEAD_DIM), lambda bi, h, i, j: (0, 0)),
                 pl.BlockSpec((1, DA_V_DIM), lambda bi, h, i, j: (0, 0))]
    args += [lam_vec, subln_w.reshape(1, DA_V_DIM)]
    return pl.pallas_call(
        functools.partial(_attn_kernel, lam_init=lam_init, nk=nk, with_ctx=with_ctx),
        grid=(b, nh, t // tq, nk),
        in_specs=in_specs,
        out_specs=qspec,
        out_shape=jax.ShapeDtypeStruct((b, t, d), BF16),
        scratch_shapes=[pltpu.VMEM((2 * tq, DA_V_DIM), BF16),
                        pltpu.VMEM((2 * tq, 1), F32),
                        pltpu.VMEM((2 * tq, LANES), F32),
                        pltpu.VMEM((2 * tq, 2 * DA_V_DIM), F32),
                        pltpu.VMEM((2 * tq, tk), BF16),
                        pltpu.SMEM((1,), jnp.int32)],
        compiler_params=_cparams("parallel", "parallel", "parallel", "arbitrary"),
        name="diff_attention",
    )(*args)


def _merge_kernel(ya_ref, yb_ref, yc_ref, g_ref, wb_ref, wo_ref, x_ref, mod_ref, o_ref):
    d = x_ref.shape[1]
    ys = (ya_ref, yb_ref, yc_ref)
    s = None
    for k in range(N_BRANCH):
        t = jax.nn.sigmoid(g_ref[:, k * d:(k + 1) * d].astype(F32)) * _dot(ys[k][...].astype(BF16), wb_ref[k])
        s = t if s is None else s + t
    r = _dot(s.astype(BF16), wo_ref[...])
    o_ref[...] = x_ref[...] + mod_ref[2:3, :] * r


def merge(ya, yb, yc, gates, wb, wo, x, mods, tr=256):
    b, t, d = x.shape
    tr = _pick_tile(t, tr)
    per_batch = mods.shape[0] > 1
    blk = pl.BlockSpec((None, tr, d), lambda bi, i: (bi, i, 0))
    return pl.pallas_call(
        _merge_kernel,
        grid=(b, t // tr),
        in_specs=[blk, blk, blk,
                  pl.BlockSpec((None, tr, N_BRANCH * d), lambda bi, i: (bi, i, 0)),
                  pl.BlockSpec((N_BRANCH, d, d), lambda bi, i: (0, 0, 0)),
                  pl.BlockSpec((d, d), lambda bi, i: (0, 0)),
                  blk,
                  pl.BlockSpec((None, 8, d), (lambda bi, i: (bi, 0, 0)) if per_batch else (lambda bi, i: (0, 0, 0)))],
        out_specs=blk,
        out_shape=jax.ShapeDtypeStruct((b, t, d), F32),
        compiler_params=_cparams("parallel", "parallel"),
        name="merge",
    )(ya, yb, yc, gates, wb, wo, x, mods)


def _ffn_kernel(x_ref, g_ref, mod_ref, w13_ref, w2_ref, o_ref, *, nchunk):
    x = x_ref[...]
    h = (_rms(x, g_ref[...]) * (1.0 + mod_ref[4:5, :]) + mod_ref[3:4, :]).astype(BF16)
    f = w2_ref.shape[0]
    fc = f // nchunk
    acc = None
    for c in range(nchunk):
        a1 = _dot(h, w13_ref[:, c * fc:(c + 1) * fc])
        a3 = _dot(h, w13_ref[:, f + c * fc:f + (c + 1) * fc])
        t = _dot((_silu(a1) * a3).astype(BF16), w2_ref[c * fc:(c + 1) * fc, :])
        acc = t if acc is None else acc + t
    o_ref[...] = x + mod_ref[5:6, :] * acc


def ffn(x, g, mods, w13, w2, tr=256):
    b, t, d = x.shape
    f = w2.shape[0]
    tr = _pick_tile(t, tr)
    nchunk = 2 if (f // 2) % LANES == 0 else 1
    per_batch = mods.shape[0] > 1
    blk = pl.BlockSpec((None, tr, d), lambda bi, i: (bi, i, 0))
    return pl.pallas_call(
        functools.partial(_ffn_kernel, nchunk=nchunk),
        grid=(b, t // tr),
        in_specs=[blk,
                  pl.BlockSpec((1, d), lambda bi, i: (0, 0)),
                  pl.BlockSpec((None, 8, d), (lambda bi, i: (bi, 0, 0)) if per_batch else (lambda bi, i: (0, 0, 0))),
                  pl.BlockSpec((d, 2 * f), lambda bi, i: (0, 0)),
                  pl.BlockSpec((f, d), lambda bi, i: (0, 0))],
        out_specs=blk,
        out_shape=jax.ShapeDtypeStruct((b, t, d), F32),
        compiler_params=_cparams("parallel", "parallel"),
        name="ffn",
    )(x, g.reshape(1, d), mods, w13, w2)


def kernel(x, c, ctx, c_ctx, ada_w, ada_b, norm1_g, norm2_g, w_in, ssd_conv_w, ssd_conv_b,
           ssd_a_log, ssd_dt_bias, ssd_d, ssd_norm_w, hy_conv_w, hy_conv_b, hy_w1, hy_b1, hy_w2,
           hy_b2, hy_w3, hy_freq, hy_bias, da_lambda, da_subln_w, w_branch, w_out, ffn_w13,
           ffn_w2, final_g):
    bsz, length, d = x.shape
    ctx_len = ctx.shape[1]
    depth = w_in.shape[0]
    inner = SSD_HEADS * SSD_HEAD_DIM
    xbc_w = inner + 2 * SSD_GROUPS * SSD_STATE
    dt_w = 2 * SSD_HEADS
    hy_w = (HY_ORDER + 1) * d
    widths = (inner, xbc_w, dt_w, hy_w, d, d, d, N_BRANCH * d)
    offs = [0]
    for wd in widths:
        offs.append(offs[-1] + wd)

    cond = jnp.concatenate([c, c_ctx[None, :], jnp.zeros((8 - bsz - 1, d), F32)], axis=0)

    rope_tabs = rope_tables(length)
    tabs = fft_tables(2 * length // FFT_N2, FFT_N2)
    expand = (jnp.arange(LANES)[:, None] % SSD_HEADS
              == jnp.arange(inner)[None, :] // SSD_HEAD_DIM)
    expand_dir = [(expand & ((jnp.arange(LANES)[:, None] // SSD_HEADS) == dr)).astype(BF16) for dr in range(2)]
    qscale = DA_HEAD_DIM ** -0.5 * math.log2(math.e)

    x_l, x_c = x, ctx
    for layer in range(depth):
        last = layer == depth - 1
        lam_init = 0.8 - 0.6 * math.exp(-0.3 * layer)
        mod = matmul(cond, ada_w[layer].astype(BF16), ada_b[layer].reshape(1, -1), silu_in=True)
        mod = jnp.pad(mod.reshape(8, 6, d), ((0, 0), (0, 2), (0, 0)))
        mods_l, mods_c = mod[:bsz], mod[bsz:bsz + 1]

        w_l = w_in[layer].astype(BF16)
        w_parts = [w_l[:, offs[i]:offs[i + 1]] for i in range(len(widths))]
        w_parts[2] = jnp.pad(w_parts[2], ((0, 0), (0, LANES - dt_w)))
        a_neg = -jnp.exp(ssd_a_log[layer].astype(F32)).reshape(1, dt_w)
        aneg_row = jnp.pad(a_neg, ((0, 0), (0, LANES - dt_w)))
        bias_row = jnp.pad(ssd_dt_bias[layer].astype(F32).reshape(1, dt_w), ((0, 0), (0, LANES - dt_w)))
        d_row = jnp.repeat(ssd_d[layer].astype(F32), SSD_HEAD_DIM).reshape(1, inner)
        nw_row = ssd_norm_w[layer].astype(F32).reshape(1, inner)
        wb = w_branch[layer].astype(BF16)
        wo = w_out[layer].astype(BF16)
        w13 = ffn_w13[layer].astype(BF16)
        w2 = ffn_w2[layer].astype(BF16)

        def project(stream, mods):
            bb, tt, _ = stream.shape
            h = norm_mod(stream, norm1_g[layer], mods, 0).reshape(bb * tt, d)
            outs = [matmul(h, wp, out_dtype=(F32 if i == 2 else BF16)).reshape(bb, tt, -1)
                    for i, wp in enumerate(w_parts)]
            return outs

        z_c, xbc_c, dt_c, hy_c, q_c, k_c, v_c, g_c = project(x_c, mods_c)
        z_l, xbc_l, dt_l, hy_l, q_l, k_l, v_l, g_l = project(x_l, mods_l)

        xa_c = dwconv3(xbc_c, ssd_conv_w[layer], ssd_conv_b[layer], True)
        xa_l = dwconv3(xbc_l, ssd_conv_w[layer], ssd_conv_b[layer], True)
        h_zero = jnp.zeros((bsz, SSD_STATE, inner), F32)
        ys_c, ys_l = [], []
        for dr in range(2):
            y_c, st = ssd_scan(xa_c, dt_c, bias_row, aneg_row, expand_dir[dr], h_zero, dr)
            y_l, _ = ssd_scan(xa_l, dt_l, bias_row, aneg_row, expand_dir[dr], st, dr)
            ys_c.append(y_c)
            ys_l.append(y_l)
        ya_l = ssd_out(ys_l[0], ys_l[1], xa_l, z_l, d_row, nw_row)

        filt = (hy_w1[layer], hy_b1[layer], hy_w2[layer], hy_b2[layer], hy_w3[layer], hy_freq[layer])
        u_l = dwconv3(hy_l, hy_conv_w[layer], hy_conv_b[layer], False)
        yb_l = hyena_long(u_l, hyena_taps(length, *filt), hy_bias[layer], tabs)

        qr_l = rope(q_l, rope_tabs, qscale)
        kr_l, kn_l = rope(k_l, rope_tabs, 1.0, with_norms=True)
        kb_c = scale_cast(k_c, 1.0)
        yc_l = diff_attention(qr_l, kr_l, v_l, kb_c, v_c, kn_l, da_lambda[layer], da_subln_w[layer], lam_init)

        x_l_new = merge(ya_l, yb_l, yc_l, g_l, wb, wo, x_l, mods_l)
        x_l = ffn(x_l_new, norm2_g[layer], mods_l, w13, w2)

        if not last:
            ya_c = ssd_out(ys_c[0], ys_c[1], xa_c, z_c, d_row, nw_row)
            u_c = dwconv3(hy_c, hy_conv_w[layer], hy_conv_b[layer], False)
            yb_c = hyena_ctx(u_c, hyena_taps(ctx_len, *filt), hy_bias[layer])
            qb_c = scale_cast(q_c, qscale)
            yc_c = diff_attention(qb_c, kb_c, v_c, None, None, None, da_lambda[layer], da_subln_w[layer], lam_init)
            x_c_new = merge(ya_c, yb_c, yc_c, g_c, wb, wo, x_c, mods_c)
            x_c = ffn(x_c_new, norm2_g[layer], mods_c, w13, w2)
    return final_norm(x_l, final_g)
```

```python
import functools
import math

import jax
import jax.numpy as jnp
from jax import lax
from jax.experimental import pallas as pl
from jax.experimental.pallas import tpu as pltpu

F32 = jnp.float32
BF16 = jnp.bfloat16
EPS = 1e-6

SSD_HEADS = 16
SSD_HEAD_DIM = 64
SSD_GROUPS = 2
SSD_STATE = 128
SSD_CHUNK = 128
HY_ORDER = 2
HY_BANDS = 16
HY_SHORT_DECAY_PCT = 0.3
HY_LONG_DECAY_PCT = 1.5
HY_DECAY_TARGET = 1e-2
DA_HEADS = 8
DA_V_DIM = 128
DA_HEAD_DIM = 64
ROPE_AXIS_DIM = 32
ROPE_THETA = 10000.0
GRID_W = 64
N_BRANCH = 3

LANES = 128
SUBLANES = 8
FFT_N2 = 128
VMEM_LIMIT = 56 * 1024 * 1024


def _cparams(*sem):
    return pltpu.CompilerParams(dimension_semantics=sem, vmem_limit_bytes=VMEM_LIMIT)


def _dot(a, b):
    return jnp.dot(a, b, preferred_element_type=F32)


def _dot_nt(a, b):
    return lax.dot_general(a, b, (((1,), (1,)), ((), ())), preferred_element_type=F32)


def _split3(x):
    hi = x.astype(BF16)
    r1 = x - hi.astype(F32)
    mid = r1.astype(BF16)
    lo = (r1 - mid.astype(F32)).astype(BF16)
    return hi, mid, lo


def _silu(x):
    return x * jax.nn.sigmoid(x)


def _softplus(x):
    return jnp.maximum(x, 0.0) + jnp.log1p(jnp.exp(-jnp.abs(x)))


def _pick_tile(n, pref):
    t = min(pref, n)
    while n % t:
        t //= 2
    return t


def _matmul_kernel(a_ref, w_ref, b_ref, o_ref, *, silu_in):
    a = a_ref[...]
    if silu_in:
        a = _silu(a.astype(F32))
    r = _dot(a.astype(BF16), w_ref[...]) + b_ref[...]
    o_ref[...] = r.astype(o_ref.dtype)


def matmul(a, w, bias=None, out_dtype=F32, silu_in=False, tm=2048, tn=1024):
    m, k = a.shape
    n = w.shape[1]
    tm = _pick_tile(m, tm)
    tn = _pick_tile(n, tn)
    if bias is None:
        bias = jnp.zeros((1, n), F32)
    return pl.pallas_call(
        functools.partial(_matmul_kernel, silu_in=silu_in),
        grid=(m // tm, n // tn),
        in_specs=[pl.BlockSpec((tm, k), lambda i, j: (i, 0)),
                  pl.BlockSpec((k, tn), lambda i, j: (0, j)),
                  pl.BlockSpec((1, tn), lambda i, j: (0, j))],
        out_specs=pl.BlockSpec((tm, tn), lambda i, j: (i, j)),
        out_shape=jax.ShapeDtypeStruct((m, n), out_dtype),
        compiler_params=_cparams("parallel", "parallel"),
        name="matmul",
    )(a, w, bias)


def _rms(x, g):
    return x * lax.rsqrt(jnp.mean(x * x, axis=-1, keepdims=True) + EPS) * g


def _norm_mod_kernel(x_ref, g_ref, mod_ref, o_ref, *, row0):
    x = x_ref[...]
    y = _rms(x, g_ref[...])
    shift = mod_ref[row0:row0 + 1, :]
    scale = mod_ref[row0 + 1:row0 + 2, :]
    o_ref[...] = (y * (1.0 + scale) + shift).astype(o_ref.dtype)


def norm_mod(x, g, mods, row0, tr=512):
    b, t, d = x.shape
    tr = _pick_tile(t, tr)
    per_batch = mods.shape[0] > 1
    return pl.pallas_call(
        functools.partial(_norm_mod_kernel, row0=row0),
        grid=(b, t // tr),
        in_specs=[pl.BlockSpec((None, tr, d), lambda bi, i: (bi, i, 0)),
                  pl.BlockSpec((1, d), lambda bi, i: (0, 0)),
                  pl.BlockSpec((None, 8, d), (lambda bi, i: (bi, 0, 0)) if per_batch else (lambda bi, i: (0, 0, 0)))],
        out_specs=pl.BlockSpec((None, tr, d), lambda bi, i: (bi, i, 0)),
        out_shape=jax.ShapeDtypeStruct((b, t, d), BF16),
        compiler_params=_cparams("parallel", "parallel"),
        name="norm_mod",
    )(x, g.reshape(1, d), mods)


def _final_norm_kernel(x_ref, g_ref, o_ref):
    o_ref[...] = _rms(x_ref[...], g_ref[...])


def final_norm(x, g, tr=512):
    b, t, d = x.shape
    tr = _pick_tile(t, tr)
    return pl.pallas_call(
        _final_norm_kernel,
        grid=(b, t // tr),
        in_specs=[pl.BlockSpec((None, tr, d), lambda bi, i: (bi, i, 0)),
                  pl.BlockSpec((1, d), lambda bi, i: (0, 0))],
        out_specs=pl.BlockSpec((None, tr, d), lambda bi, i: (bi, i, 0)),
        out_shape=jax.ShapeDtypeStruct((b, t, d), F32),
        compiler_params=_cparams("parallel", "parallel"),
        name="final_norm",
    )(x, g.reshape(1, d))


def _dwconv_kernel(x_ref, xp_ref, xn_ref, w_ref, b_ref, o_ref, *, act, nblk):
    i = pl.program_id(1)
    x = x_ref[...].astype(F32)
    tr = x.shape[0]
    halo = xp_ref.shape[0]
    prev_row = jnp.where(i == 0, 0.0, xp_ref[halo - 1:halo, :].astype(F32))
    next_row = jnp.where(i == nblk - 1, 0.0, xn_ref[0:1, :].astype(F32))
    rows = lax.broadcasted_iota(jnp.int32, x.shape, 0)
    x_m1 = jnp.where(rows == 0, prev_row, pltpu.roll(x, 1, 0))
    x_p1 = jnp.where(rows == tr - 1, next_row, pltpu.roll(x, tr - 1, 0))
    y = b_ref[...] + x_m1 * w_ref[0:1, :] + x * w_ref[1:2, :] + x_p1 * w_ref[2:3, :]
    if act:
        y = _silu(y)
    o_ref[...] = y


def dwconv3(x, w, bias, act, tr=512, tc=1536):
    b, t, c = x.shape
    tr = _pick_tile(t, tr)
    tc = min(tc, c)
    assert c % tc == 0
    nblk = t // tr
    halo = SUBLANES * (4 // x.dtype.itemsize)
    rh = tr // halo
    lasth = t // halo - 1
    return pl.pallas_call(
        functools.partial(_dwconv_kernel, act=act, nblk=nblk),
        grid=(b, nblk, c // tc),
        in_specs=[pl.BlockSpec((None, tr, tc), lambda bi, i, j: (bi, i, j)),
                  pl.BlockSpec((None, halo, tc), lambda bi, i, j: (bi, jnp.maximum(i * rh - 1, 0), j)),
                  pl.BlockSpec((None, halo, tc), lambda bi, i, j: (bi, jnp.minimum((i + 1) * rh, lasth), j)),
                  pl.BlockSpec((3, tc), lambda bi, i, j: (0, j)),
                  pl.BlockSpec((1, tc), lambda bi, i, j: (0, j))],
        out_specs=pl.BlockSpec((None, tr, tc), lambda bi, i, j: (bi, i, j)),
        out_shape=jax.ShapeDtypeStruct((b, t, c), F32),
        compiler_params=_cparams("parallel", "parallel", "parallel"),
        name="dwconv3",
    )(x, x, x, w, bias.reshape(1, c))


SSD_STEP_CHUNKS = 4


def _ssd_kernel(x_ref, b_ref, c_ref, dt_ref, bias_ref, aneg_ref, e_ref, h0_ref,
                y_ref, hout_ref, st_ref, *, direction, nsteps):
    ci = pl.program_id(1)

    @pl.when(ci == 0)
    def _():
        st_ref[...] = h0_ref[...]

    q = SSD_CHUNK
    hd = SSD_HEAD_DIM
    gw = (SSD_HEADS // SSD_GROUPS) * hd
    row = lax.broadcasted_iota(jnp.int32, (q, q), 0)
    col = lax.broadcasted_iota(jnp.int32, (q, q), 1)
    mask = (col >= row) if direction else (col <= row)
    tri = mask.astype(BF16)
    lane = lax.broadcasted_iota(jnp.int32, (q, LANES), 1)
    first = lane < hd

    def state_free(rows):
        x = x_ref[rows, :]
        dtv = _softplus(dt_ref[rows, :] + bias_ref[...])
        adt = dtv * aneg_ref[...]
        acum = sum(_dot(tri, p) for p in _split3(adt))
        acum_t = sum(_dot_nt(p, tri) for p in _split3(adt.T))
        total = jnp.sum(adt, axis=0, keepdims=True)
        eac = jnp.exp(acum)
        dte = jnp.exp(total - acum)
        cd = jnp.broadcast_to(jnp.exp(total), (SUBLANES, LANES))
        stack = jnp.concatenate([dtv, eac, dte, cd], axis=0)
        ex = _dot(stack.astype(BF16), e_ref[...])
        dt_e, eac_e, dte_e, cd_e = ex[:q], ex[q:2 * q], ex[2 * q:3 * q], ex[3 * q:3 * q + 1]
        xdt = x * dt_e
        y_diag, c_gs, b_ts = [], [], []
        for g in range(SSD_GROUPS):
            c_g = c_ref[rows, g * SSD_STATE:(g + 1) * SSD_STATE].astype(BF16)
            b_g32 = b_ref[rows, g * SSD_STATE:(g + 1) * SSD_STATE]
            cb = _dot_nt(c_g, b_g32.astype(BF16))
            for jp in range(gw // LANES):
                h_a = g * (SSD_HEADS // SSD_GROUPS) + 2 * jp
                xpair = xdt[:, h_a * hd:h_a * hd + LANES]
                acc = None
                for k in range(2):
                    cix = direction * SSD_HEADS + h_a + k
                    seg = acum[:, cix:cix + 1] - acum_t[cix:cix + 1, :]
                    dec = jnp.where(mask, jnp.exp(jnp.where(mask, seg, 0.0)), 0.0)
                    w = (cb * dec).astype(BF16)
                    xk = jnp.where(first if k == 0 else jnp.logical_not(first), xpair, 0.0).astype(BF16)
                    d = _dot(w, xk)
                    acc = d if acc is None else acc + d
                y_diag.append(acc)
            c_gs.append(c_g)
            b_ts.append(b_g32.T.astype(BF16))
        xdte = (xdt * dte_e).astype(BF16)
        return y_diag, c_gs, b_ts, eac_e, cd_e, xdte

    nsub = x_ref.shape[0] // q
    order = list(range(nsub - 1, -1, -1) if direction else range(nsub))
    pre = {s: state_free(slice(s * q, (s + 1) * q)) for s in order}
    for s in order:
        y_diag, c_gs, b_ts, eac_e, cd_e, xdte = pre[s]
        for g in range(SSD_GROUPS):
            gs = slice(g * gw, (g + 1) * gw)
            s_g = st_ref[:, gs]
            y_off = _dot(c_gs[g], s_g.astype(BF16)) * eac_e[:, gs]
            for jp in range(gw // LANES):
                ls = slice(g * gw + jp * LANES, g * gw + (jp + 1) * LANES)
                y_ref[s * q:(s + 1) * q, ls] = y_diag[g * (gw // LANES) + jp] + y_off[:, jp * LANES:(jp + 1) * LANES]
            st_ref[:, gs] = s_g * cd_e[:, gs] + _dot(b_ts[g], xdte[:, gs])

    @pl.when(ci == nsteps - 1)
    def _():
        hout_ref[...] = st_ref[...]


def ssd_scan(xbc, dt_raw, dt_bias_row, aneg_row, expand, h0, direction):
    b, t, _ = xbc.shape
    q = SSD_CHUNK * min(SSD_STEP_CHUNKS, t // SSD_CHUNK)
    nsteps = t // q
    inner = SSD_HEADS * SSD_HEAD_DIM
    bn = SSD_GROUPS * SSD_STATE
    cmap = (lambda c: nsteps - 1 - c) if direction else (lambda c: c)
    y, hout = pl.pallas_call(
        functools.partial(_ssd_kernel, direction=direction, nsteps=nsteps),
        grid=(b, nsteps),
        in_specs=[pl.BlockSpec((None, q, inner), lambda bi, c: (bi, cmap(c), 0)),
                  pl.BlockSpec((None, q, bn), lambda bi, c: (bi, cmap(c), inner // bn)),
                  pl.BlockSpec((None, q, bn), lambda bi, c: (bi, cmap(c), inner // bn + 1)),
                  pl.BlockSpec((None, q, LANES), lambda bi, c: (bi, cmap(c), 0)),
                  pl.BlockSpec((1, LANES), lambda bi, c: (0, 0)),
                  pl.BlockSpec((1, LANES), lambda bi, c: (0, 0)),
                  pl.BlockSpec((LANES, inner), lambda bi, c: (0, 0)),
                  pl.BlockSpec((None, SSD_STATE, inner), lambda bi, c: (bi, 0, 0))],
        out_specs=[pl.BlockSpec((None, q, inner), lambda bi, c: (bi, cmap(c), 0)),
                   pl.BlockSpec((None, SSD_STATE, inner), lambda bi, c: (bi, 0, 0))],
        out_shape=[jax.ShapeDtypeStruct((b, t, inner), F32),
                   jax.ShapeDtypeStruct((b, SSD_STATE, inner), F32)],
        scratch_shapes=[pltpu.VMEM((SSD_STATE, inner), F32)],
        compiler_params=_cparams("parallel", "arbitrary"),
        name="ssd_scan_bwd" if direction else "ssd_scan_fwd",
    )(xbc, xbc, xbc, dt_raw, dt_bias_row, aneg_row, expand, h0)
    return y, hout


def _ssd_out_kernel(yf_ref, yb_ref, x_ref, z_ref, d_ref, w_ref, o_ref):
    y = yf_ref[...] + yb_ref[...] + d_ref[...] * x_ref[...]
    y = y * _silu(z_ref[...].astype(F32))
    gw = y.shape[1] // SSD_GROUPS
    for g in range(SSD_GROUPS):
        gs = slice(g * gw, (g + 1) * gw)
        yg = y[:, gs]
        yg = yg * lax.rsqrt(jnp.mean(yg * yg, axis=-1, keepdims=True) + EPS)
        o_ref[:, gs] = (yg * w_ref[:, gs]).astype(o_ref.dtype)


def ssd_out(yf, yb, xbc, z, d_row, w_row, tr=256):
    b, t, inner = yf.shape
    tr = _pick_tile(t, tr)
    blk = pl.BlockSpec((None, tr, inner), lambda bi, i: (bi, i, 0))
    vec = pl.BlockSpec((1, inner), lambda bi, i: (0, 0))
    return pl.pallas_call(
        _ssd_out_kernel,
        grid=(b, t // tr),
        in_specs=[blk, blk, blk, blk, vec, vec],
        out_specs=blk,
        out_shape=jax.ShapeDtypeStruct((b, t, inner), BF16),
        compiler_params=_cparams("parallel", "parallel"),
        name="ssd_out",
    )(yf, yb, xbc, z, d_row, w_row)


def _taps_kernel(f_ref, tv_ref, w1_ref, b1_ref, w2_ref, b2_ref, w3_ref, fr_ref, dl_ref, o_ref):
    hp = lax.Precision.HIGHEST
    fr = fr_ref[...]
    h = jnp.sin(fr * (jnp.dot(f_ref[...], w1_ref[...], precision=hp, preferred_element_type=F32) + b1_ref[...]))
    h = jnp.sin(fr * (jnp.dot(h, w2_ref[...], precision=hp, preferred_element_type=F32) + b2_ref[...]))
    filt = _dot(h.astype(BF16), w3_ref[...].astype(BF16))
    c = dl_ref.shape[1]
    win = jnp.exp(-tv_ref[:, 0:1] * dl_ref[...]) * tv_ref[:, 1:2]
    o_ref[0] = filt[:, :c] * win
    o_ref[1] = filt[:, c:] * win


def hyena_taps(length, w1, b1, w2, b2, w3, freq):
    hid = w1.shape[1]
    c = w3.shape[1] // (2 * HY_ORDER)
    n = 2 * length
    t = jnp.linspace(0.0, 1.0, length, dtype=F32)[:, None]
    phase = 2.0 * math.pi * jnp.arange(length, dtype=F32)[:, None] / length
    bands = jnp.linspace(1e-4, HY_BANDS - 1, HY_BANDS, dtype=F32)[None, :]
    feats = jnp.concatenate([t, jnp.cos(phase * bands), -jnp.sin(phase * bands)], axis=-1)
    nf = feats.shape[1]
    rev = jnp.concatenate([jnp.zeros((1,), jnp.int32), jnp.arange(length - 1, 0, -1, dtype=jnp.int32)])
    feats2 = jnp.concatenate([feats, feats[rev]], axis=0)
    feats2 = jnp.pad(feats2, ((0, 0), (0, LANES - nf)))
    valid = jnp.ones((n,), F32).at[length].set(0.0)
    tv = jnp.stack([jnp.concatenate([t[:, 0], t[rev, 0]]), valid], axis=1)
    max_decay = math.log(HY_DECAY_TARGET) / HY_SHORT_DECAY_PCT
    min_decay = math.log(HY_DECAY_TARGET) / HY_LONG_DECAY_PCT
    deltas = jnp.abs(jnp.linspace(min_decay, max_decay, c, dtype=F32))[None, :]
    w1p = jnp.pad(w1, ((0, LANES - nf), (0, 0)))
    w3r = w3.reshape(hid, HY_ORDER, 2, c).transpose(2, 0, 1, 3).reshape(2, hid, HY_ORDER * c)
    tr = _pick_tile(length, 1024)
    nhalf = length // tr
    return pl.pallas_call(
        _taps_kernel,
        grid=(n // tr,),
        in_specs=[pl.BlockSpec((tr, LANES), lambda i: (i, 0)),
                  pl.BlockSpec((tr, 2), lambda i: (i, 0)),
                  pl.BlockSpec((LANES, hid), lambda i: (0, 0)),
                  pl.BlockSpec((1, hid), lambda i: (0, 0)),
                  pl.BlockSpec((hid, hid), lambda i: (0, 0)),
                  pl.BlockSpec((1, hid), lambda i: (0, 0)),
                  pl.BlockSpec((None, hid, HY_ORDER * c), lambda i: (i // nhalf, 0, 0)),
                  pl.BlockSpec((1, hid), lambda i: (0, 0)),
                  pl.BlockSpec((1, c), lambda i: (0, 0))],
        out_specs=pl.BlockSpec((HY_ORDER, tr, c), lambda i: (0, i, 0)),
        out_shape=jax.ShapeDtypeStruct((HY_ORDER, n, c), F32),
        compiler_params=_cparams("parallel"),
        name="hyena_taps",
    )(feats2, tv, w1p, b1.reshape(1, hid), w2, b2.reshape(1, hid), w3r, freq.reshape(1, hid), deltas)


def _cs(num, den):
    ang = (2.0 * math.pi / den) * num.astype(F32)
    return jnp.cos(ang), jnp.sin(ang)


def fft_tables(n1, n2):
    n = n1 * n2
    h = n1 // 2
    k1 = jnp.arange(n1, dtype=jnp.int32)
    c1, s1 = _cs((k1[:, None] * k1[None, :]) % n1, n1)
    g_u = jnp.block([[c1[:, :h], s1[:, :h]], [-s1[:, :h], c1[:, :h]]])
    g_t = jnp.concatenate([c1, -s1], axis=0)
    g_i = jnp.block([[c1[:h, :], -s1[:h, :]], [s1[:h, :], c1[:h, :]]])
    s = jnp.arange(n2, dtype=jnp.int32)
    num = (n1 * (s[:, None] * s[None, :]))[None] + (k1[:, None, None] * s[None, None, :])
    cb, sb = _cs(num % n, n)
    m_f = jnp.concatenate([jnp.concatenate([cb, sb], axis=2),
                           jnp.concatenate([-sb, cb], axis=2)], axis=1)
    cbt, sbt = jnp.swapaxes(cb, 1, 2) / n, jnp.swapaxes(sb, 1, 2) / n
    m_i = jnp.concatenate([jnp.concatenate([cbt, -sbt], axis=2),
                           jnp.concatenate([sbt, cbt], axis=2)], axis=1)
    return dict(g_u=g_u.astype(BF16), g_t=g_t.astype(BF16), g_i=g_i.astype(BF16),
                m_f=m_f.astype(BF16), m_i=m_i.astype(BF16))


FFT_KGROUP = SUBLANES


def _rows_get(ref, start, n, stride):
    return ref.reshape(math.prod(ref.shape[:-1]), ref.shape[-1])[pl.ds(start, n, stride=stride), :]


def _rows_set(ref, start, n, stride, val):
    ref.reshape(math.prod(ref.shape[:-1]), ref.shape[-1])[pl.ds(start, n, stride=stride), :] = val


def _cpack(re, im):
    rb = lax.bitcast_convert_type(re.astype(BF16).astype(F32), jnp.uint32)
    ib = lax.bitcast_convert_type(im.astype(BF16).astype(F32), jnp.uint32)
    return (rb & jnp.uint32(0xFFFF0000)) | (ib >> 16)


def _cunpack(w):
    re = lax.bitcast_convert_type(w & jnp.uint32(0xFFFF0000), F32)
    im = lax.bitcast_convert_type(w << 16, F32)
    return re, im


FFT_LANE_BLOCKS = 2


def _fft_a_kernel(g_ref, *refs, s_major_in):
    *x_refs, o_ref = refs
    xa_refs, xb_refs = x_refs[:len(x_refs) // 2], x_refs[len(x_refs) // 2:]
    h = g_ref.shape[1] // 2
    ts, n1, _ = o_ref.shape
    g = g_ref[...]

    def rows(x_refs_, s):
        if s_major_in:
            return x_refs_[0][s]
        return jnp.concatenate([_rows_get(r, s, h, ts) for r in x_refs_], axis=1)

    for s in range(ts):
        x = jnp.concatenate([rows(xa_refs, s), rows(xb_refs, s)], axis=0).astype(BF16)
        r = _dot(g, x)
        o_ref[s] = _cpack(r[:n1], r[n1:])


def fft_stage_a(g, x4, sel_a, sel_b, col_off, c, s_major_in=False):
    n1 = g.shape[0] // 2
    h = n1 // 2
    n2 = x4.shape[1] if s_major_in else x4.shape[2]
    ts = SUBLANES
    nl = FFT_LANE_BLOCKS
    tc = nl * LANES
    if s_major_in:
        coff = col_off // tc
        specs = lambda sel: [pl.BlockSpec((None, ts, h, tc), lambda j, s: (sel[0], s, sel[1], j + coff))]
    else:
        coff = col_off // LANES
        specs = lambda sel: [pl.BlockSpec((None, h, ts, LANES),
                                          functools.partial(lambda j, s, l: (sel[0], sel[1], s, nl * j + l + coff), l=l))
                             for l in range(nl)]
    x_specs = specs(sel_a) + specs(sel_b)
    return pl.pallas_call(
        functools.partial(_fft_a_kernel, s_major_in=s_major_in),
        grid=(c // tc, n2 // ts),
        in_specs=[pl.BlockSpec((2 * n1, n1), lambda j, s: (0, 0))] + x_specs,
        out_specs=pl.BlockSpec((ts, n1, tc), lambda j, s: (s, 0, j)),
        out_shape=jax.ShapeDtypeStruct((n2, n1, c), jnp.uint32),
        compiler_params=_cparams("parallel", "parallel"),
        name="fft_stage_a",
    )(g, *([x4] * len(x_specs)))


def _fft_load_k(a_refs, j):
    n2, kb, _ = a_refs[0].shape
    re, im = _cunpack(jnp.concatenate([_rows_get(r, j, n2, kb) for r in a_refs], axis=1))
    return jnp.concatenate([re, im], axis=0).astype(BF16)


def _fft_bh_kernel(m_ref, *refs, groups):
    *a_refs, o_ref = refs
    n2, kb, _ = a_refs[0].shape
    nl = len(a_refs) // groups
    for g in range(groups):
        for j in range(kb):
            x = _dot(m_ref[g * kb + j], _fft_load_k(a_refs[g * nl:(g + 1) * nl], j))
            o_ref[g * kb + j] = _cpack(x[:n2], x[n2:])


def _fft_b_kernel(mf_ref, mi_ref, *refs, groups):
    *a_refs, h_ref, o_ref = refs
    n2, kb, _ = a_refs[0].shape
    nl = len(a_refs) // groups
    for g in range(groups):
        for j in range(kb):
            k = g * kb + j
            x = _dot(mf_ref[k], _fft_load_k(a_refs[g * nl:(g + 1) * nl], j))
            xr, xi = x[:n2], x[n2:]
            hr, hi = _cunpack(h_ref[k])
            y = jnp.concatenate([xr * hr - xi * hi, xr * hi + xi * hr], axis=0).astype(BF16)
            r = _dot(mi_ref[k], y)
            o_ref[k] = _cpack(r[:n2], r[n2:])


def fft_stage_b(m_f, m_i, a, hspec, groups=2):
    n2, n1, c = a.shape
    groups = min(groups, n1 // FFT_KGROUP)
    kb = FFT_KGROUP * groups
    nl = FFT_LANE_BLOCKS
    tc = nl * LANES
    ablks = [pl.BlockSpec((n2, FFT_KGROUP, LANES),
                          functools.partial(lambda k, j, g, l: (0, groups * k + g, nl * j + l), g=g, l=l))
             for g in range(groups) for l in range(nl)]
    kblk = pl.BlockSpec((kb, n2, tc), lambda k, j: (k, 0, j))
    mat = pl.BlockSpec((kb, 2 * n2, 2 * n2), lambda k, j: (k, 0, 0))
    if hspec is None:
        kern, in_specs, name = _fft_bh_kernel, [mat] + ablks, "fft_stage_b_spectrum"
        args = (m_f,) + (a,) * len(ablks)
    else:
        kern, in_specs, name = _fft_b_kernel, [mat, mat] + ablks + [kblk], "fft_stage_b"
        args = (m_f, m_i) + (a,) * len(ablks) + (hspec,)
    return pl.pallas_call(
        functools.partial(kern, groups=groups),
        grid=(n1 // kb, c // tc),
        in_specs=in_specs,
        out_specs=kblk,
        out_shape=jax.ShapeDtypeStruct((n1, n2, c), jnp.uint32),
        compiler_params=_cparams("parallel", "parallel"),
        name=name,
    )(*args)


def _fft_ai_kernel(g_ref, b_ref, u_ref, x_ref, skip_ref, o_ref, *, u_s_major, out_s_major):
    n1, ts, _ = b_ref.shape
    h = n1 // 2
    ga = g_ref[:, :n1]
    gb = g_ref[:, n1:]
    skip = skip_ref[...]
    for s in range(ts):
        br, bi_ = _cunpack(_rows_get(b_ref, s, n1, ts))
        y = _dot(ga, br.astype(BF16)) + _dot(gb, bi_.astype(BF16))
        for bi in range(2):
            start = bi * h * ts + s
            u = u_ref[bi, s] if u_s_major else _rows_get(u_ref, start, h, ts)
            val = _rows_get(x_ref, start, h, ts) * (y[bi * h:(bi + 1) * h] + u * skip)
            if out_s_major:
                o_ref[bi, s] = val
            else:
                _rows_set(o_ref, start, h, ts, val)


def fft_stage_a_inv(g_i, bm, u4, u_off, u_s_major, x4, x_off, skip_row, out_s_major, tc=LANES):
    n1, n2, c = bm.shape
    h = n1 // 2
    ts = SUBLANES
    uo, xo = u_off // tc, x_off // tc
    nat = lambda off: pl.BlockSpec((2, h, ts, tc), lambda j, s: (0, 0, s, j + off))
    smj = lambda off: pl.BlockSpec((2, ts, h, tc), lambda j, s: (0, s, 0, j + off))
    return pl.pallas_call(
        functools.partial(_fft_ai_kernel, u_s_major=u_s_major, out_s_major=out_s_major),
        grid=(c // tc, n2 // ts),
        in_specs=[pl.BlockSpec((n1, 2 * n1), lambda j, s: (0, 0)),
                  pl.BlockSpec((n1, ts, tc), lambda j, s: (0, s, j)),
                  smj(uo) if u_s_major else nat(uo),
                  nat(xo),
                  pl.BlockSpec((1, tc), lambda j, s: (0, j))],
        out_specs=smj(0) if out_s_major else nat(0),
        out_shape=jax.ShapeDtypeStruct((2, n2, h, c) if out_s_major else (2, h, n2, c), F32),
        compiler_params=_cparams("parallel", "parallel"),
        name="fft_stage_a_inv",
    )(g_i, bm, u4, x4, skip_row)


def hyena_long(u, taps, skip, tabs):
    b, length, c3 = u.shape
    assert b == 2, "the two batch rows are packed as one complex sequence"
    c = c3 // 3
    n2 = FFT_N2
    n1 = 2 * length // n2
    h = n1 // 2
    u4 = u.reshape(b, h, n2, c3)
    taps4 = taps.reshape(HY_ORDER * 2, h, n2, c)
    zin, zoff, z_s_major = u4, 0, False
    out = None
    for conv in range(HY_ORDER):
        last = conv == HY_ORDER - 1
        hspec = fft_stage_b(tabs["m_f"], None,
                            fft_stage_a(tabs["g_t"], taps4, (2 * conv, 0), (2 * conv + 1, 0), 0, c), None)
        a = fft_stage_a(tabs["g_u"], zin, (0, 0), (1, 0), zoff, c, s_major_in=z_s_major)
        bm = fft_stage_b(tabs["m_f"], tabs["m_i"], a, hspec)
        out = fft_stage_a_inv(tabs["g_i"], bm, zin, zoff, z_s_major, u4, (conv + 1) * c,
                              skip[conv].reshape(1, c), not last)
        zin, zoff, z_s_major = out, 0, True
    return out.reshape(b, length, c)


def _hyena_ctx_kernel(u_ref, x1_ref, x2_ref, taps_ref, fc_ref, fs_ref, skip_ref, o_ref):
    length = u_ref.shape[1]
    n = 2 * length
    fc = fc_ref[...]
    fs = fs_ref[...]
    fcl, fsl = fc[:, :length], fs[:, :length]
    fct, fst = fc[:length, :], fs[:length, :]
    zr = u_ref[0]
    zi = u_ref[1]
    gates = (x1_ref, x2_ref)
    for conv in range(HY_ORDER):
        tp = taps_ref[conv].astype(BF16)
        hr, hi = _dot(fc, tp), -_dot(fs, tp)
        zrb, zib = zr.astype(BF16), zi.astype(BF16)
        xr = _dot(fcl, zrb) + _dot(fsl, zib)
        xi = _dot(fcl, zib) - _dot(fsl, zrb)
        yr = (xr * hr - xi * hi).astype(BF16)
        yi = (xr * hi + xi * hr).astype(BF16)
        cr = (_dot(fct, yr) - _dot(fst, yi)) * (1.0 / n)
        ci = (_dot(fct, yi) + _dot(fst, yr)) * (1.0 / n)
        sk = skip_ref[conv:conv + 1, :]
        zr = gates[conv][0] * (cr + zr * sk)
        zi = gates[conv][1] * (ci + zi * sk)
    o_ref[0] = zr.astype(o_ref.dtype)
    o_ref[1] = zi.astype(o_ref.dtype)


def hyena_ctx(u, taps, skip, tc=256):
    b, length, c3 = u.shape
    assert b == 2
    c = c3 // 3
    n = 2 * length
    k = jnp.arange(n, dtype=jnp.int32)
    fc, fs = _cs((k[:, None] * k[None, :]) % n, n)
    nb = c // tc
    ub = lambda off: pl.BlockSpec((2, length, tc), lambda j: (0, 0, j + off * nb))
    mat = pl.BlockSpec((n, n), lambda j: (0, 0))
    return pl.pallas_call(
        _hyena_ctx_kernel,
        grid=(nb,),
        in_specs=[ub(0), ub(1), ub(2),
                  pl.BlockSpec((HY_ORDER, n, tc), lambda j: (0, 0, j)),
                  mat, mat,
                  pl.BlockSpec((HY_ORDER, tc), lambda j: (0, j))],
        out_specs=pl.BlockSpec((2, length, tc), lambda j: (0, 0, j)),
        out_shape=jax.ShapeDtypeStruct((b, length, c), BF16),
        compiler_params=_cparams("parallel"),
        name="hyena_ctx",
    )(u, u, u, taps, fc.astype(BF16), fs.astype(BF16), skip)


def rope_tables(length):
    rows = length // GRID_W
    row = jnp.repeat(jnp.arange(rows, dtype=F32), GRID_W)
    col = (jnp.arange(length) % GRID_W).astype(F32)
    inv = ROPE_THETA ** (-jnp.arange(0, ROPE_AXIS_DIM, 2, dtype=F32) / ROPE_AXIS_DIM)
    ang = jnp.stack([row[:, None] * inv, col[:, None] * inv], axis=1)
    cos, sin = jnp.cos(ang), jnp.sin(ang)
    zero = jnp.zeros_like(sin)
    per_map = lambda lo, hi: jnp.concatenate([lo, hi], axis=-1).reshape(length, 2 * ROPE_AXIS_DIM)
    reps = LANES // (2 * ROPE_AXIS_DIM)
    c_t = jnp.tile(per_map(cos, cos), (1, reps))
    sm_t = jnp.tile(per_map(-sin, zero), (1, reps))
    sp_t = jnp.tile(per_map(zero, sin), (1, reps))
    return c_t, sm_t, sp_t


def _rope_kernel(x_ref, c_ref, sm_ref, sp_ref, o_ref, *n_refs, scale):
    half = ROPE_AXIS_DIM // 2
    c, sm, sp = c_ref[...], sm_ref[...], sp_ref[...]
    for g in range(x_ref.shape[1] // LANES):
        ls = slice(g * LANES, (g + 1) * LANES)
        x = x_ref[:, ls].astype(F32)
        y = (x * c + pltpu.roll(x, LANES - half, 1) * sm + pltpu.roll(x, half, 1) * sp) * scale
        o_ref[:, ls] = y.astype(o_ref.dtype)
        if n_refs:
            n2 = jnp.max(jnp.sum(y * y, axis=1, keepdims=True), axis=0, keepdims=True)
            n_refs[0][g:g + 1, :] = jnp.broadcast_to(n2, (1, LANES))


def rope(x, tables, scale, with_norms=False, tr=512):
    b, t, d = x.shape
    tr = _pick_tile(t, tr)
    nh = d // LANES
    tab = pl.BlockSpec((tr, LANES), lambda bi, i: (i, 0))
    out_specs = [pl.BlockSpec((None, tr, d), lambda bi, i: (bi, i, 0))]
    out_shape = [jax.ShapeDtypeStruct((b, t, d), BF16)]
    if with_norms:
        out_specs.append(pl.BlockSpec((None, None, nh, LANES), lambda bi, i: (bi, i, 0, 0)))
        out_shape.append(jax.ShapeDtypeStruct((b, t // tr, nh, LANES), F32))
    outs = pl.pallas_call(
        functools.partial(_rope_kernel, scale=scale),
        grid=(b, t // tr),
        in_specs=[pl.BlockSpec((None, tr, d), lambda bi, i: (bi, i, 0)), tab, tab, tab],
        out_specs=out_specs,
        out_shape=out_shape,
        compiler_params=_cparams("parallel", "parallel"),
        name="rope",
    )(x, *tables)
    if with_norms:
        return outs[0], jnp.swapaxes(outs[1], 1, 2)
    return outs[0]


def _scale_cast_kernel(x_ref, o_ref, *, scale):
    o_ref[...] = (x_ref[...].astype(F32) * scale).astype(o_ref.dtype)


def scale_cast(x, scale, tr=256):
    b, t, d = x.shape
    tr = _pick_tile(t, tr)
    blk = pl.BlockSpec((None, tr, d), lambda bi, i: (bi, i, 0))
    return pl.pallas_call(
        functools.partial(_scale_cast_kernel, scale=scale),
        grid=(b, t // tr), in_specs=[blk], out_specs=blk,
        out_shape=jax.ShapeDtypeStruct((b, t, d), BF16),
        compiler_params=_cparams("parallel", "parallel"),
        name="scale_cast",
    )(x)


ATTN_GUARD_LOG2 = 80.0
ATTN_NORM_SLACK = 1.03


def _attn_kernel(*refs, lam_init, nk, with_ctx):
    if with_ctx:
        (q_ref, k_ref, v_ref, kc_ref, vc_ref, kn_ref, lam_ref, w_ref, o_ref,
         q2_ref, m_ref, c_ref, acc_ref, p_scr, flag_ref) = refs
    else:
        q_ref, k_ref, v_ref, lam_ref, w_ref, o_ref, q2_ref, m_ref, c_ref, acc_ref, p_scr, flag_ref = refs
    ki = pl.program_id(3)
    tq = q_ref.shape[0]

    def with_ones(v):
        return jnp.concatenate([v, jnp.ones_like(v)], axis=1)

    def exact_update(k, v, p_buf):
        s = _dot_nt(q2_ref[...], k)
        m_prev = m_ref[...]
        m_new = jnp.maximum(m_prev, jnp.max(s, axis=1, keepdims=True))
        alpha = jnp.exp2(m_prev - m_new)
        p_buf[...] = jnp.exp2(s - m_new).astype(BF16)
        acc_ref[...] = alpha * acc_ref[...] + _dot(p_buf[...], with_ones(v))
        m_ref[...] = m_new

    def fast_update(k, v, p_buf):
        s = _dot_nt(q2_ref[...], k)
        p_buf[...] = jnp.exp2(s - pltpu.repeat(c_ref[...], k.shape[0] // LANES, axis=1)).astype(BF16)
        acc_ref[...] += _dot(p_buf[...], with_ones(v))

    @pl.when(ki == 0)
    def _():
        q = q_ref[...]
        lane = lax.broadcasted_iota(jnp.int32, q.shape, 1)
        zero = jnp.zeros_like(q)
        q2_ref[:tq, :] = jnp.where(lane < DA_HEAD_DIM, q, zero)
        q2_ref[tq:, :] = jnp.where(lane >= DA_HEAD_DIM, q, zero)
        m_ref[...] = jnp.full(m_ref.shape, -jnp.inf, F32)
        acc_ref[...] = jnp.zeros(acc_ref.shape, F32)
        flag_ref[0] = 0
        if with_ctx:
            q2 = q2_ref[...]
            c = _dot_nt(q2, jnp.broadcast_to(kc_ref[0:1, :], (LANES, DA_V_DIM)))
            sq = q2.astype(F32) * q2.astype(F32)
            hi = sq.astype(BF16)
            lo = (sq - hi.astype(F32)).astype(BF16)
            ones = jnp.ones((DA_V_DIM, LANES), BF16)
            qn2 = _dot(hi, ones) + _dot(lo, ones)
            kn2 = jnp.max(kn_ref[...])
            gap = jnp.max(jnp.sqrt(qn2 * kn2) * ATTN_NORM_SLACK - c)
            c_ref[...] = c
            flag_ref[0] = (gap <= ATTN_GUARD_LOG2).astype(jnp.int32)

    fast = flag_ref[0] == 1
    if with_ctx:
        tc = kc_ref.shape[0]
        first = ki == 0

        @pl.when(jnp.logical_and(first, fast))
        def _():
            fast_update(kc_ref[...], vc_ref[...], p_scr.at[:, pl.ds(0, tc)])

        @pl.when(jnp.logical_and(first, jnp.logical_not(fast)))
        def _():
            exact_update(kc_ref[...], vc_ref[...], p_scr.at[:, pl.ds(0, tc)])

    @pl.when(fast)
    def _():
        fast_update(k_ref[...], v_ref[...], p_scr)

    @pl.when(jnp.logical_not(fast))
    def _():
        exact_update(k_ref[...], v_ref[...], p_scr)

    @pl.when(ki == nk - 1)
    def _():
        lv = lam_ref[...]
        e1 = jnp.exp(jnp.sum(lv[0:1, :] * lv[1:2, :], axis=1, keepdims=True))
        e2 = jnp.exp(jnp.sum(lv[2:3, :] * lv[3:4, :], axis=1, keepdims=True))
        lam = e1 - e2 + lam_init
        acc = acc_ref[...]
        o = acc[:, :DA_V_DIM] / acc[:, DA_V_DIM:]
        o = o[:tq] - lam * o[tq:]
        o_ref[...] = (_rms(o, w_ref[...]) * (1.0 - lam_init)).astype(o_ref.dtype)


def diff_attention(q, k, v, k_ctx, v_ctx, knorm, lam_vec, subln_w, lam_init, tq=1024, tk=2048):
    b, t, d = q.shape
    nh = d // DA_V_DIM
    tq = _pick_tile(t, tq)
    tk = _pick_tile(k.shape[1], tk)
    nk = k.shape[1] // tk
    with_ctx = k_ctx is not None
    assert not with_ctx or k_ctx.shape[1] <= tk
    qspec = pl.BlockSpec((None, tq, DA_V_DIM), lambda bi, h, i, j: (bi, i, h))
    kspec = pl.BlockSpec((None, tk, DA_V_DIM), lambda bi, h, i, j: (bi, j, h))
    in_specs = [qspec, kspec, kspec]
    args = [q, k, v]
    if with_ctx:
        cspec = pl.BlockSpec((None, k_ctx.shape[1], DA_V_DIM), lambda bi, h, i, j: (bi, 0, h))
        nspec = pl.BlockSpec((None, None) + knorm.shape[2:], lambda bi, h, i, j: (bi, h, 0, 0))
        in_specs += [cspec, cspec, nspec]
        args += [k_ctx, v_ctx, knorm]
    in_specs += [pl.BlockSpec((4, DA_HEAD_DIM), lambda bi, h, i, j: (0, 0)),
                 pl.BlockSpec((1, DA_V_DIM), lambda bi, h, i, j: (0, 0))]
    args += [lam_vec, subln_w.reshape(1, DA_V_DIM)]
    return pl.pallas_call(
        functools.partial(_attn_kernel, lam_init=lam_init, nk=nk, with_ctx=with_ctx),
        grid=(b, nh, t // tq, nk),
        in_specs=in_specs,
        out_specs=qspec,
        out_shape=jax.ShapeDtypeStruct((b, t, d), BF16),
        scratch_shapes=[pltpu.VMEM((2 * tq, DA_V_DIM), BF16),
                        pltpu.VMEM((2 * tq, 1), F32),
                        pltpu.VMEM((2 * tq, LANES), F32),
                        pltpu.VMEM((2 * tq, 2 * DA_V_DIM), F32),
                        pltpu.VMEM((2 * tq, tk), BF16),
                        pltpu.SMEM((1,), jnp.int32)],
        compiler_params=_cparams("parallel", "parallel", "parallel", "arbitrary"),
        name="diff_attention",
    )(*args)


def _merge_kernel(ya_ref, yb_ref, yc_ref, g_ref, wb_ref, wo_ref, x_ref, mod_ref, o_ref):
    d = x_ref.shape[1]
    ys = (ya_ref, yb_ref, yc_ref)
    s = None
    for k in range(N_BRANCH):
        t = jax.nn.sigmoid(g_ref[:, k * d:(k + 1) * d].astype(F32)) * _dot(ys[k][...].astype(BF16), wb_ref[k])
        s = t if s is None else s + t
    r = _dot(s.astype(BF16), wo_ref[...])
    o_ref[...] = x_ref[...] + mod_ref[2:3, :] * r


def merge(ya, yb, yc, gates, wb, wo, x, mods, tr=256):
    b, t, d = x.shape
    tr = _pick_tile(t, tr)
    per_batch = mods.shape[0] > 1
    blk = pl.BlockSpec((None, tr, d), lambda bi, i: (bi, i, 0))
    return pl.pallas_call(
        _merge_kernel,
        grid=(b, t // tr),
        in_specs=[blk, blk, blk,
                  pl.BlockSpec((None, tr, N_BRANCH * d), lambda bi, i: (bi, i, 0)),
                  pl.BlockSpec((N_BRANCH, d, d), lambda bi, i: (0, 0, 0)),
                  pl.BlockSpec((d, d), lambda bi, i: (0, 0)),
                  blk,
                  pl.BlockSpec((None, 8, d), (lambda bi, i: (bi, 0, 0)) if per_batch else (lambda bi, i: (0, 0, 0)))],
        out_specs=blk,
        out_shape=jax.ShapeDtypeStruct((b, t, d), F32),
        compiler_params=_cparams("parallel", "parallel"),
        name="merge",
    )(ya, yb, yc, gates, wb, wo, x, mods)


def _ffn_kernel(x_ref, g_ref, mod_ref, w13_ref, w2_ref, o_ref, *, nchunk):
    x = x_ref[...]
    h = (_rms(x, g_ref[...]) * (1.0 + mod_ref[4:5, :]) + mod_ref[3:4, :]).astype(BF16)
    f = w2_ref.shape[0]
    fc = f // nchunk
    acc = None
    for c in range(nchunk):
        a1 = _dot(h, w13_ref[:, c * fc:(c + 1) * fc])
        a3 = _dot(h, w13_ref[:, f + c * fc:f + (c + 1) * fc])
        t = _dot((_silu(a1) * a3).astype(BF16), w2_ref[c * fc:(c + 1) * fc, :])
        acc = t if acc is None else acc + t
    o_ref[...] = x + mod_ref[5:6, :] * acc


def ffn(x, g, mods, w13, w2, tr=256):
    b, t, d = x.shape
    f = w2.shape[0]
    tr = _pick_tile(t, tr)
    nchunk = 2 if (f // 2) % LANES == 0 else 1
    per_batch = mods.shape[0] > 1
    blk = pl.BlockSpec((None, tr, d), lambda bi, i: (bi, i, 0))
    return pl.pallas_call(
        functools.partial(_ffn_kernel, nchunk=nchunk),
        grid=(b, t // tr),
        in_specs=[blk,
                  pl.BlockSpec((1, d), lambda bi, i: (0, 0)),
                  pl.BlockSpec((None, 8, d), (lambda bi, i: (bi, 0, 0)) if per_batch else (lambda bi, i: (0, 0, 0))),
                  pl.BlockSpec((d, 2 * f), lambda bi, i: (0, 0)),
                  pl.BlockSpec((f, d), lambda bi, i: (0, 0))],
        out_specs=blk,
        out_shape=jax.ShapeDtypeStruct((b, t, d), F32),
        compiler_params=_cparams("parallel", "parallel"),
        name="ffn",
    )(x, g.reshape(1, d), mods, w13, w2)


def kernel(x, c, ctx, c_ctx, ada_w, ada_b, norm1_g, norm2_g, w_in, ssd_conv_w, ssd_conv_b,
           ssd_a_log, ssd_dt_bias, ssd_d, ssd_norm_w, hy_conv_w, hy_conv_b, hy_w1, hy_b1, hy_w2,
           hy_b2, hy_w3, hy_freq, hy_bias, da_lambda, da_subln_w, w_branch, w_out, ffn_w13,
           ffn_w2, final_g):
    bsz, length, d = x.shape
    ctx_len = ctx.shape[1]
    depth = w_in.shape[0]
    inner = SSD_HEADS * SSD_HEAD_DIM
    xbc_w = inner + 2 * SSD_GROUPS * SSD_STATE
    dt_w = 2 * SSD_HEADS
    hy_w = (HY_ORDER + 1) * d
    widths = (inner, xbc_w, dt_w, hy_w, d, d, d, N_BRANCH * d)
    offs = [0]
    for wd in widths:
        offs.append(offs[-1] + wd)

    cond = jnp.concatenate([c, c_ctx[None, :], jnp.zeros((8 - bsz - 1, d), F32)], axis=0)

    rope_tabs = rope_tables(length)
    tabs = fft_tables(2 * length // FFT_N2, FFT_N2)
    expand = (jnp.arange(LANES)[:, None] % SSD_HEADS
              == jnp.arange(inner)[None, :] // SSD_HEAD_DIM)
    expand_dir = [(expand & ((jnp.arange(LANES)[:, None] // SSD_HEADS) == dr)).astype(BF16) for dr in range(2)]
    qscale = DA_HEAD_DIM ** -0.5 * math.log2(math.e)

    x_l, x_c = x, ctx
    for layer in range(depth):
        last = layer == depth - 1
        lam_init = 0.8 - 0.6 * math.exp(-0.3 * layer)
        mod = matmul(cond, ada_w[layer].astype(BF16), ada_b[layer].reshape(1, -1), silu_in=True)
        mod = jnp.pad(mod.reshape(8, 6, d), ((0, 0), (0, 2), (0, 0)))
        mods_l, mods_c = mod[:bsz], mod[bsz:bsz + 1]

        w_l = w_in[layer].astype(BF16)
        w_parts = [w_l[:, offs[i]:offs[i + 1]] for i in range(len(widths))]
        w_parts[2] = jnp.pad(w_parts[2], ((0, 0), (0, LANES - dt_w)))
        a_neg = -jnp.exp(ssd_a_log[layer].astype(F32)).reshape(1, dt_w)
        aneg_row = jnp.pad(a_neg, ((0, 0), (0, LANES - dt_w)))
        bias_row = jnp.pad(ssd_dt_bias[layer].astype(F32).reshape(1, dt_w), ((0, 0), (0, LANES - dt_w)))
        d_row = jnp.repeat(ssd_d[layer].astype(F32), SSD_HEAD_DIM).reshape(1, inner)
        nw_row = ssd_norm_w[layer].astype(F32).reshape(1, inner)
        wb = w_branch[layer].astype(BF16)
        wo = w_out[layer].astype(BF16)
        w13 = ffn_w13[layer].astype(BF16)
        w2 = ffn_w2[layer].astype(BF16)

        def project(stream, mods):
            bb, tt, _ = stream.shape
            h = norm_mod(stream, norm1_g[layer], mods, 0).reshape(bb * tt, d)
            outs = [matmul(h, wp, out_dtype=(F32 if i == 2 else BF16)).reshape(bb, tt, -1)
                    for i, wp in enumerate(w_parts)]
            return outs

        z_c, xbc_c, dt_c, hy_c, q_c, k_c, v_c, g_c = project(x_c, mods_c)
        z_l, xbc_l, dt_l, hy_l, q_l, k_l, v_l, g_l = project(x_l, mods_l)

        xa_c = dwconv3(xbc_c, ssd_conv_w[layer], ssd_conv_b[layer], True)
        xa_l = dwconv3(xbc_l, ssd_conv_w[layer], ssd_conv_b[layer], True)
        h_zero = jnp.zeros((bsz, SSD_STATE, inner), F32)
        ys_c, ys_l = [], []
        for dr in range(2):
            y_c, st = ssd_scan(xa_c, dt_c, bias_row, aneg_row, expand_dir[dr], h_zero, dr)
            y_l, _ = ssd_scan(xa_l, dt_l, bias_row, aneg_row, expand_dir[dr], st, dr)
            ys_c.append(y_c)
            ys_l.append(y_l)
        ya_l = ssd_out(ys_l[0], ys_l[1], xa_l, z_l, d_row, nw_row)

        filt = (hy_w1[layer], hy_b1[layer], hy_w2[layer], hy_b2[layer], hy_w3[layer], hy_freq[layer])
        u_l = dwconv3(hy_l, hy_conv_w[layer], hy_conv_b[layer], False)
        yb_l = hyena_long(u_l, hyena_taps(length, *filt), hy_bias[layer], tabs)

        qr_l = rope(q_l, rope_tabs, qscale)
        kr_l, kn_l = rope(k_l, rope_tabs, 1.0, with_norms=True)
        kb_c = scale_cast(k_c, 1.0)
        yc_l = diff_attention(qr_l, kr_l, v_l, kb_c, v_c, kn_l, da_lambda[layer], da_subln_w[layer], lam_init)

        x_l_new = merge(ya_l, yb_l, yc_l, g_l, wb, wo, x_l, mods_l)
        x_l = ffn(x_l_new, norm2_g[layer], mods_l, w13, w2)

        if not last:
            ya_c = ssd_out(ys_c[0], ys_c[1], xa_c, z_c, d_row, nw_row)
            u_c = dwconv3(hy_c, hy_conv_w[layer], hy_conv_b[layer], False)
            yb_c = hyena_ctx(u_c, hyena_taps(ctx_len, *filt), hy_bias[layer])
            qb_c = scale_cast(q_c, qscale)
            yc_c = diff_attention(qb_c, kb_c, v_c, None, None, None, da_lambda[layer], da_subln_w[layer], lam_init)
            x_c_new = merge(ya_c, yb_c, yc_c, g_c, wb, wo, x_c, mods_c)
            x_c = ffn(x_c_new, norm2_g[layer], mods_c, w13, w2)
    return final_norm(x_l, final_g)
```

```python
import functools
import math

import jax
import jax.numpy as jnp
from jax import lax
from jax.experimental import pallas as pl
from jax.experimental.pallas import tpu as pltpu

F32 = jnp.float32
BF16 = jnp.bfloat16
EPS = 1e-6

SSD_HEADS = 16
SSD_HEAD_DIM = 64
SSD_GROUPS = 2
SSD_STATE = 128
SSD_CHUNK = 128
HY_ORDER = 2
HY_BANDS = 16
HY_SHORT_DECAY_PCT = 0.3
HY_LONG_DECAY_PCT = 1.5
HY_DECAY_TARGET = 1e-2
DA_HEADS = 8
DA_V_DIM = 128
DA_HEAD_DIM = 64
ROPE_AXIS_DIM = 32
ROPE_THETA = 10000.0
GRID_W = 64
N_BRANCH = 3

LANES = 128
SUBLANES = 8
FFT_N2 = 128
VMEM_LIMIT = 56 * 1024 * 1024


def _cparams(*sem):
    return pltpu.CompilerParams(dimension_semantics=sem, vmem_limit_bytes=VMEM_LIMIT)


def _dot(a, b):
    return jnp.dot(a, b, preferred_element_type=F32)


def _dot_nt(a, b):
    return lax.dot_general(a, b, (((1,), (1,)), ((), ())), preferred_element_type=F32)


def _split3(x):
    hi = x.astype(BF16)
    r1 = x - hi.astype(F32)
    mid = r1.astype(BF16)
    lo = (r1 - mid.astype(F32)).astype(BF16)
    return hi, mid, lo


def _silu(x):
    return x * jax.nn.sigmoid(x)


def _softplus(x):
    return jnp.maximum(x, 0.0) + jnp.log1p(jnp.exp(-jnp.abs(x)))


def _pick_tile(n, pref):
    t = min(pref, n)
    while n % t:
        t //= 2
    return t


def _matmul_kernel(a_ref, w_ref, b_ref, o_ref, *, silu_in):
    a = a_ref[...]
    if silu_in:
        a = _silu(a.astype(F32))
    r = _dot(a.astype(BF16), w_ref[...]) + b_ref[...]
    o_ref[...] = r.astype(o_ref.dtype)


def matmul(a, w, bias=None, out_dtype=F32, silu_in=False, tm=2048, tn=1024):
    m, k = a.shape
    n = w.shape[1]
    tm = _pick_tile(m, tm)
    tn = _pick_tile(n, tn)
    if bias is None:
        bias = jnp.zeros((1, n), F32)
    return pl.pallas_call(
        functools.partial(_matmul_kernel, silu_in=silu_in),
        grid=(m // tm, n // tn),
        in_specs=[pl.BlockSpec((tm, k), lambda i, j: (i, 0)),
                  pl.BlockSpec((k, tn), lambda i, j: (0, j)),
                  pl.BlockSpec((1, tn), lambda i, j: (0, j))],
        out_specs=pl.BlockSpec((tm, tn), lambda i, j: (i, j)),
        out_shape=jax.ShapeDtypeStruct((m, n), out_dtype),
        compiler_params=_cparams("parallel", "parallel"),
        name="matmul",
    )(a, w, bias)


def _rms(x, g):
    return x * lax.rsqrt(jnp.mean(x * x, axis=-1, keepdims=True) + EPS) * g


def _norm_mod_kernel(x_ref, g_ref, mod_ref, o_ref, *, row0):
    x = x_ref[...]
    y = _rms(x, g_ref[...])
    shift = mod_ref[row0:row0 + 1, :]
    scale = mod_ref[row0 + 1:row0 + 2, :]
    o_ref[...] = (y * (1.0 + scale) + shift).astype(o_ref.dtype)


def norm_mod(x, g, mods, row0, tr=512):
    b, t, d = x.shape
    tr = _pick_tile(t, tr)
    per_batch = mods.shape[0] > 1
    return pl.pallas_call(
        functools.partial(_norm_mod_kernel, row0=row0),
        grid=(b, t // tr),
        in_specs=[pl.BlockSpec((None, tr, d), lambda bi, i: (bi, i, 0)),
                  pl.BlockSpec((1, d), lambda bi, i: (0, 0)),
                  pl.BlockSpec((None, 8, d), (lambda bi, i: (bi, 0, 0)) if per_batch else (lambda bi, i: (0, 0, 0)))],
        out_specs=pl.BlockSpec((None, tr, d), lambda bi, i: (bi, i, 0)),
        out_shape=jax.ShapeDtypeStruct((b, t, d), BF16),
        compiler_params=_cparams("parallel", "parallel"),
        name="norm_mod",
    )(x, g.reshape(1, d), mods)


def _final_norm_kernel(x_ref, g_ref, o_ref):
    o_ref[...] = _rms(x_ref[...], g_ref[...])


def final_norm(x, g, tr=512):
    b, t, d = x.shape
    tr = _pick_tile(t, tr)
    return pl.pallas_call(
        _final_norm_kernel,
        grid=(b, t // tr),
        in_specs=[pl.BlockSpec((None, tr, d), lambda bi, i: (bi, i, 0)),
                  pl.BlockSpec((1, d), lambda bi, i: (0, 0))],
        out_specs=pl.BlockSpec((None, tr, d), lambda bi, i: (bi, i, 0)),
        out_shape=jax.ShapeDtypeStruct((b, t, d), F32),
        compiler_params=_cparams("parallel", "parallel"),
        name="final_norm",
    )(x, g.reshape(1, d))


def _dwconv_kernel(x_ref, xp_ref, xn_ref, w_ref, b_ref, o_ref, *, act, nblk):
    i = pl.program_id(1)
    x = x_ref[...].astype(F32)
    tr = x.shape[0]
    halo = xp_ref.shape[0]
    prev_row = jnp.where(i == 0, 0.0, xp_ref[halo - 1:halo, :].astype(F32))
    next_row = jnp.where(i == nblk - 1, 0.0, xn_ref[0:1, :].astype(F32))
    rows = lax.broadcasted_iota(jnp.int32, x.shape, 0)
    x_m1 = jnp.where(rows == 0, prev_row, pltpu.roll(x, 1, 0))
    x_p1 = jnp.where(rows == tr - 1, next_row, pltpu.roll(x, tr - 1, 0))
    y = b_ref[...] + x_m1 * w_ref[0:1, :] + x * w_ref[1:2, :] + x_p1 * w_ref[2:3, :]
    if act:
        y = _silu(y)
    o_ref[...] = y


def dwconv3(x, w, bias, act, tr=512, tc=1536):
    b, t, c = x.shape
    tr = _pick_tile(t, tr)
    tc = min(tc, c)
    assert c % tc == 0
    nblk = t // tr
    halo = SUBLANES * (4 // x.dtype.itemsize)
    rh = tr // halo
    lasth = t // halo - 1
    return pl.pallas_call(
        functools.partial(_dwconv_kernel, act=act, nblk=nblk),
        grid=(b, nblk, c // tc),
        in_specs=[pl.BlockSpec((None, tr, tc), lambda bi, i, j: (bi, i, j)),
                  pl.BlockSpec((None, halo, tc), lambda bi, i, j: (bi, jnp.maximum(i * rh - 1, 0), j)),
                  pl.BlockSpec((None, halo, tc), lambda bi, i, j: (bi, jnp.minimum((i + 1) * rh, lasth), j)),
                  pl.BlockSpec((3, tc), lambda bi, i, j: (0, j)),
                  pl.BlockSpec((1, tc), lambda bi, i, j: (0, j))],
        out_specs=pl.BlockSpec((None, tr, tc), lambda bi, i, j: (bi, i, j)),
        out_shape=jax.ShapeDtypeStruct((b, t, c), F32),
        compiler_params=_cparams("parallel", "parallel", "parallel"),
        name="dwconv3",
    )(x, x, x, w, bias.reshape(1, c))


SSD_STEP_CHUNKS = 4


def _ssd_kernel(x_ref, b_ref, c_ref, dt_ref, bias_ref, aneg_ref, e_ref, h0_ref,
                y_ref, hout_ref, st_ref, *, direction, nsteps):
    ci = pl.program_id(1)

    @pl.when(ci == 0)
    def _():
        st_ref[...] = h0_ref[...]

    q = SSD_CHUNK
    hd = SSD_HEAD_DIM
    gw = (SSD_HEADS // SSD_GROUPS) * hd
    row = lax.broadcasted_iota(jnp.int32, (q, q), 0)
    col = lax.broadcasted_iota(jnp.int32, (q, q), 1)
    mask = (col >= row) if direction else (col <= row)
    tri = mask.astype(BF16)
    lane = lax.broadcasted_iota(jnp.int32, (q, LANES), 1)
    first = lane < hd

    def state_free(rows):
        x = x_ref[rows, :]
        dtv = _softplus(dt_ref[rows, :] + bias_ref[...])
        adt = dtv * aneg_ref[...]
        acum = sum(_dot(tri, p) for p in _split3(adt))
        acum_t = sum(_dot_nt(p, tri) for p in _split3(adt.T))
        total = jnp.sum(adt, axis=0, keepdims=True)
        eac = jnp.exp(acum)
        dte = jnp.exp(total - acum)
        cd = jnp.broadcast_to(jnp.exp(total), (SUBLANES, LANES))
        stack = jnp.concatenate([dtv, eac, dte, cd], axis=0)
        ex = _dot(stack.astype(BF16), e_ref[...])
        dt_e, eac_e, dte_e, cd_e = ex[:q], ex[q:2 * q], ex[2 * q:3 * q], ex[3 * q:3 * q + 1]
        xdt = x * dt_e
        y_diag, c_gs, b_ts = [], [], []
        for g in range(SSD_GROUPS):
            c_g = c_ref[rows, g * SSD_STATE:(g + 1) * SSD_STATE].astype(BF16)
            b_g32 = b_ref[rows, g * SSD_STATE:(g + 1) * SSD_STATE]
            cb = _dot_nt(c_g, b_g32.astype(BF16))
            for jp in range(gw // LANES):
                h_a = g * (SSD_HEADS // SSD_GROUPS) + 2 * jp
                xpair = xdt[:, h_a * hd:h_a * hd + LANES]
                acc = None
                for k in range(2):
                    cix = direction * SSD_HEADS + h_a + k
                    seg = acum[:, cix:cix + 1] - acum_t[cix:cix + 1, :]
                    dec = jnp.where(mask, jnp.exp(jnp.where(mask, seg, 0.0)), 0.0)
                    w = (cb * dec).astype(BF16)
                    xk = jnp.where(first if k == 0 else jnp.logical_not(first), xpair, 0.0).astype(BF16)
                    d = _dot(w, xk)
                    acc = d if acc is None else acc + d
                y_diag.append(acc)
            c_gs.append(c_g)
            b_ts.append(b_g32.T.astype(BF16))
        xdte = (xdt * dte_e).astype(BF16)
        return y_diag, c_gs, b_ts, eac_e, cd_e, xdte

    nsub = x_ref.shape[0] // q
    order = list(range(nsub - 1, -1, -1) if direction else range(nsub))
    pre = {s: state_free(slice(s * q, (s + 1) * q)) for s in order}
    for s in order:
        y_diag, c_gs, b_ts, eac_e, cd_e, xdte = pre[s]
        for g in range(SSD_GROUPS):
            gs = slice(g * gw, (g + 1) * gw)
            s_g = st_ref[:, gs]
            y_off = _dot(c_gs[g], s_g.astype(BF16)) * eac_e[:, gs]
            for jp in range(gw // LANES):
                ls = slice(g * gw + jp * LANES, g * gw + (jp + 1) * LANES)
                y_ref[s * q:(s + 1) * q, ls] = y_diag[g * (gw // LANES) + jp] + y_off[:, jp * LANES:(jp + 1) * LANES]
            st_ref[:, gs] = s_g * cd_e[:, gs] + _dot(b_ts[g], xdte[:, gs])

    @pl.when(ci == nsteps - 1)
    def _():
        hout_ref[...] = st_ref[...]


def ssd_scan(xbc, dt_raw, dt_bias_row, aneg_row, expand, h0, direction):
    b, t, _ = xbc.shape
    q = SSD_CHUNK * min(SSD_STEP_CHUNKS, t // SSD_CHUNK)
    nsteps = t // q
    inner = SSD_HEADS * SSD_HEAD_DIM
    bn = SSD_GROUPS * SSD_STATE
    cmap = (lambda c: nsteps - 1 - c) if direction else (lambda c: c)
    y, hout = pl.pallas_call(
        functools.partial(_ssd_kernel, direction=direction, nsteps=nsteps),
        grid=(b, nsteps),
        in_specs=[pl.BlockSpec((None, q, inner), lambda bi, c: (bi, cmap(c), 0)),
                  pl.BlockSpec((None, q, bn), lambda bi, c: (bi, cmap(c), inner // bn)),
                  pl.BlockSpec((None, q, bn), lambda bi, c: (bi, cmap(c), inner // bn + 1)),
                  pl.BlockSpec((None, q, LANES), lambda bi, c: (bi, cmap(c), 0)),
                  pl.BlockSpec((1, LANES), lambda bi, c: (0, 0)),
                  pl.BlockSpec((1, LANES), lambda bi, c: (0, 0)),
                  pl.BlockSpec((LANES, inner), lambda bi, c: (0, 0)),
                  pl.BlockSpec((None, SSD_STATE, inner), lambda bi, c: (bi, 0, 0))],
        out_specs=[pl.BlockSpec((None, q, inner), lambda bi, c: (bi, cmap(c), 0)),
                   pl.BlockSpec((None, SSD_STATE, inner), lambda bi, c: (bi, 0, 0))],
        out_shape=[jax.ShapeDtypeStruct((b, t, inner), F32),
                   jax.ShapeDtypeStruct((b, SSD_STATE, inner), F32)],
        scratch_shapes=[pltpu.VMEM((SSD_STATE, inner), F32)],
        compiler_params=_cparams("parallel", "arbitrary"),
        name="ssd_scan_bwd" if direction else "ssd_scan_fwd",
    )(xbc, xbc, xbc, dt_raw, dt_bias_row, aneg_row, expand, h0)
    return y, hout


def _ssd_out_kernel(yf_ref, yb_ref, x_ref, z_ref, d_ref, w_ref, o_ref):
    y = yf_ref[...] + yb_ref[...] + d_ref[...] * x_ref[...]
    y = y * _silu(z_ref[...].astype(F32))
    gw = y.shape[1] // SSD_GROUPS
    for g in range(SSD_GROUPS):
        gs = slice(g * gw, (g + 1) * gw)
        yg = y[:, gs]
        yg = yg * lax.rsqrt(jnp.mean(yg * yg, axis=-1, keepdims=True) + EPS)
        o_ref[:, gs] = (yg * w_ref[:, gs]).astype(o_ref.dtype)


def ssd_out(yf, yb, xbc, z, d_row, w_row, tr=256):
    b, t, inner = yf.shape
    tr = _pick_tile(t, tr)
    blk = pl.BlockSpec((None, tr, inner), lambda bi, i: (bi, i, 0))
    vec = pl.BlockSpec((1, inner), lambda bi, i: (0, 0))
    return pl.pallas_call(
        _ssd_out_kernel,
        grid=(b, t // tr),
        in_specs=[blk, blk, blk, blk, vec, vec],
        out_specs=blk,
        out_shape=jax.ShapeDtypeStruct((b, t, inner), BF16),
        compiler_params=_cparams("parallel", "parallel"),
        name="ssd_out",
    )(yf, yb, xbc, z, d_row, w_row)


def _taps_kernel(f_ref, tv_ref, w1_ref, b1_ref, w2_ref, b2_ref, w3_ref, fr_ref, dl_ref, o_ref):
    hp = lax.Precision.HIGHEST
    fr = fr_ref[...]
    h = jnp.sin(fr * (jnp.dot(f_ref[...], w1_ref[...], precision=hp, preferred_element_type=F32) + b1_ref[...]))
    h = jnp.sin(fr * (jnp.dot(h, w2_ref[...], precision=hp, preferred_element_type=F32) + b2_ref[...]))
    filt = _dot(h.astype(BF16), w3_ref[...].astype(BF16))
    c = dl_ref.shape[1]
    win = jnp.exp(-tv_ref[:, 0:1] * dl_ref[...]) * tv_ref[:, 1:2]
    o_ref[0] = filt[:, :c] * win
    o_ref[1] = filt[:, c:] * win


def hyena_taps(length, w1, b1, w2, b2, w3, freq):
    hid = w1.shape[1]
    c = w3.shape[1] // (2 * HY_ORDER)
    n = 2 * length
    t = jnp.linspace(0.0, 1.0, length, dtype=F32)[:, None]
    phase = 2.0 * math.pi * jnp.arange(length, dtype=F32)[:, None] / length
    bands = jnp.linspace(1e-4, HY_BANDS - 1, HY_BANDS, dtype=F32)[None, :]
    feats = jnp.concatenate([t, jnp.cos(phase * bands), -jnp.sin(phase * bands)], axis=-1)
    nf = feats.shape[1]
    rev = jnp.concatenate([jnp.zeros((1,), jnp.int32), jnp.arange(length - 1, 0, -1, dtype=jnp.int32)])
    feats2 = jnp.concatenate([feats, feats[rev]], axis=0)
    feats2 = jnp.pad(feats2, ((0, 0), (0, LANES - nf)))
    valid = jnp.ones((n,), F32).at[length].set(0.0)
    tv = jnp.stack([jnp.concatenate([t[:, 0], t[rev, 0]]), valid], axis=1)
    max_decay = math.log(HY_DECAY_TARGET) / HY_SHORT_DECAY_PCT
    min_decay = math.log(HY_DECAY_TARGET) / HY_LONG_DECAY_PCT
    deltas = jnp.abs(jnp.linspace(min_decay, max_decay, c, dtype=F32))[None, :]
    w1p = jnp.pad(w1, ((0, LANES - nf), (0, 0)))
    w3r = w3.reshape(hid, HY_ORDER, 2, c).transpose(2, 0, 1, 3).reshape(2, hid, HY_ORDER * c)
    tr = _pick_tile(length, 1024)
    nhalf = length // tr
    return pl.pallas_call(
        _taps_kernel,
        grid=(n // tr,),
        in_specs=[pl.BlockSpec((tr, LANES), lambda i: (i, 0)),
                  pl.BlockSpec((tr, 2), lambda i: (i, 0)),
                  pl.BlockSpec((LANES, hid), lambda i: (0, 0)),
                  pl.BlockSpec((1, hid), lambda i: (0, 0)),
                  pl.BlockSpec((hid, hid), lambda i: (0, 0)),
                  pl.BlockSpec((1, hid), lambda i: (0, 0)),
                  pl.BlockSpec((None, hid, HY_ORDER * c), lambda i: (i // nhalf, 0, 0)),
                  pl.BlockSpec((1, hid), lambda i: (0, 0)),
                  pl.BlockSpec((1, c), lambda i: (0, 0))],
        out_specs=pl.BlockSpec((HY_ORDER, tr, c), lambda i: (0, i, 0)),
        out_shape=jax.ShapeDtypeStruct((HY_ORDER, n, c), F32),
        compiler_params=_cparams("parallel"),
        name="hyena_taps",
    )(feats2, tv, w1p, b1.reshape(1, hid), w2, b2.reshape(1, hid), w3r, freq.reshape(1, hid), deltas)


def _cs(num, den):
    ang = (2.0 * math.pi / den) * num.astype(F32)
    return jnp.cos(ang), jnp.sin(ang)


def fft_tables(n1, n2):
    n = n1 * n2
    h = n1 // 2
    k1 = jnp.arange(n1, dtype=jnp.int32)
    c1, s1 = _cs((k1[:, None] * k1[None, :]) % n1, n1)
    g_u = jnp.block([[c1[:, :h], s1[:, :h]], [-s1[:, :h], c1[:, :h]]])
    g_t = jnp.concatenate([c1, -s1], axis=0)
    g_i = jnp.block([[c1[:h, :], -s1[:h, :]], [s1[:h, :], c1[:h, :]]])
    s = jnp.arange(n2, dtype=jnp.int32)
    num = (n1 * (s[:, None] * s[None, :]))[None] + (k1[:, None, None] * s[None, None, :])
    cb, sb = _cs(num % n, n)
    m_f = jnp.concatenate([jnp.concatenate([cb, sb], axis=2),
                           jnp.concatenate([-sb, cb], axis=2)], axis=1)
    cbt, sbt = jnp.swapaxes(cb, 1, 2) / n, jnp.swapaxes(sb, 1, 2) / n
    m_i = jnp.concatenate([jnp.concatenate([cbt, -sbt], axis=2),
                           jnp.concatenate([sbt, cbt], axis=2)], axis=1)
    return dict(g_u=g_u.astype(BF16), g_t=g_t.astype(BF16), g_i=g_i.astype(BF16),
                m_f=m_f.astype(BF16), m_i=m_i.astype(BF16))


FFT_KGROUP = SUBLANES


def _rows_get(ref, start, n, stride):
    return ref.reshape(math.prod(ref.shape[:-1]), ref.shape[-1])[pl.ds(start, n, stride=stride), :]


def _rows_set(ref, start, n, stride, val):
    ref.reshape(math.prod(ref.shape[:-1]), ref.shape[-1])[pl.ds(start, n, stride=stride), :] = val


def _cpack(re, im):
    rb = lax.bitcast_convert_type(re.astype(BF16).astype(F32), jnp.uint32)
    ib = lax.bitcast_convert_type(im.astype(BF16).astype(F32), jnp.uint32)
    return (rb & jnp.uint32(0xFFFF0000)) | (ib >> 16)


def _cunpack(w):
    re = lax.bitcast_convert_type(w & jnp.uint32(0xFFFF0000), F32)
    im = lax.bitcast_convert_type(w << 16, F32)
    return re, im


FFT_LANE_BLOCKS = 2


def _fft_a_kernel(g_ref, *refs, s_major_in):
    *x_refs, o_ref = refs
    xa_refs, xb_refs = x_refs[:len(x_refs) // 2], x_refs[len(x_refs) // 2:]
    h = g_ref.shape[1] // 2
    ts, n1, _ = o_ref.shape
    g = g_ref[...]

    def rows(x_refs_, s):
        if s_major_in:
            return x_refs_[0][s]
        return jnp.concatenate([_rows_get(r, s, h, ts) for r in x_refs_], axis=1)

    for s in range(ts):
        x = jnp.concatenate([rows(xa_refs, s), rows(xb_refs, s)], axis=0).astype(BF16)
        r = _dot(g, x)
        o_ref[s] = _cpack(r[:n1], r[n1:])


def fft_stage_a(g, x4, sel_a, sel_b, col_off, c, s_major_in=False):
    n1 = g.shape[0] // 2
    h = n1 // 2
    n2 = x4.shape[1] if s_major_in else x4.shape[2]
    ts = SUBLANES
    nl = FFT_LANE_BLOCKS
    tc = nl * LANES
    if s_major_in:
        coff = col_off // tc
        specs = lambda sel: [pl.BlockSpec((None, ts, h, tc), lambda j, s: (sel[0], s, sel[1], j + coff))]
    else:
        coff = col_off // LANES
        specs = lambda sel: [pl.BlockSpec((None, h, ts, LANES),
                                          functools.partial(lambda j, s, l: (sel[0], sel[1], s, nl * j + l + coff), l=l))
                             for l in range(nl)]
    x_specs = specs(sel_a) + specs(sel_b)
    return pl.pallas_call(
        functools.partial(_fft_a_kernel, s_major_in=s_major_in),
        grid=(c // tc, n2 // ts),
        in_specs=[pl.BlockSpec((2 * n1, n1), lambda j, s: (0, 0))] + x_specs,
        out_specs=pl.BlockSpec((ts, n1, tc), lambda j, s: (s, 0, j)),
        out_shape=jax.ShapeDtypeStruct((n2, n1, c), jnp.uint32),
        compiler_params=_cparams("parallel", "parallel"),
        name="fft_stage_a",
    )(g, *([x4] * len(x_specs)))


def _fft_load_k(a_refs, j):
    n2, kb, _ = a_refs[0].shape
    re, im = _cunpack(jnp.concatenate([_rows_get(r, j, n2, kb) for r in a_refs], axis=1))
    return jnp.concatenate([re, im], axis=0).astype(BF16)


def _fft_bh_kernel(m_ref, *refs, groups):
    *a_refs, o_ref = refs
    n2, kb, _ = a_refs[0].shape
    nl = len(a_refs) // groups
    for g in range(groups):
        for j in range(kb):
            x = _dot(m_ref[g * kb + j], _fft_load_k(a_refs[g * nl:(g + 1) * nl], j))
            o_ref[g * kb + j] = _cpack(x[:n2], x[n2:])


def _fft_b_kernel(mf_ref, mi_ref, *refs, groups):
    *a_refs, h_ref, o_ref = refs
    n2, kb, _ = a_refs[0].shape
    nl = len(a_refs) // groups
    for g in range(groups):
        for j in range(kb):
            k = g * kb + j
            x = _dot(mf_ref[k], _fft_load_k(a_refs[g * nl:(g + 1) * nl], j))
            xr, xi = x[:n2], x[n2:]
            hr, hi = _cunpack(h_ref[k])
            y = jnp.concatenate([xr * hr - xi * hi, xr * hi + xi * hr], axis=0).astype(BF16)
            r = _dot(mi_ref[k], y)
            o_ref[k] = _cpack(r[:n2], r[n2:])


def fft_stage_b(m_f, m_i, a, hspec, groups=2):
    n2, n1, c = a.shape
    groups = min(groups, n1 // FFT_KGROUP)
    kb = FFT_KGROUP * groups
    nl = FFT_LANE_BLOCKS
    tc = nl * LANES
    ablks = [pl.BlockSpec((n2, FFT_KGROUP, LANES),
                          functools.partial(lambda k, j, g, l: (0, groups * k + g, nl * j + l), g=g, l=l))
             for g in range(groups) for l in range(nl)]
    kblk = pl.BlockSpec((kb, n2, tc), lambda k, j: (k, 0, j))
    mat = pl.BlockSpec((kb, 2 * n2, 2 * n2), lambda k, j: (k, 0, 0))
    if hspec is None:
        kern, in_specs, name = _fft_bh_kernel, [mat] + ablks, "fft_stage_b_spectrum"
        args = (m_f,) + (a,) * len(ablks)
    else:
        kern, in_specs, name = _fft_b_kernel, [mat, mat] + ablks + [kblk], "fft_stage_b"
        args = (m_f, m_i) + (a,) * len(ablks) + (hspec,)
    return pl.pallas_call(
        functools.partial(kern, groups=groups),
        grid=(n1 // kb, c // tc),
        in_specs=in_specs,
        out_specs=kblk,
        out_shape=jax.ShapeDtypeStruct((n1, n2, c), jnp.uint32),
        compiler_params=_cparams("parallel", "parallel"),
        name=name,
    )(*args)


def _fft_ai_kernel(g_ref, *refs, nl, u_s_major, out_s_major):
    nu = 1 if u_s_major else nl
    b_refs, u_refs, x_refs = refs[:nl], refs[nl:nl + nu], refs[nl + nu:2 * nl + nu]
    skip_ref, o_ref = refs[-2:]
    n1, ts, _ = b_refs[0].shape
    h = n1 // 2
    g = g_ref[...]
    skip = skip_ref[...]
    cat = lambda parts: parts[0] if len(parts) == 1 else jnp.concatenate(parts, axis=1)
    for s in range(ts):
        br, bi_ = _cunpack(cat([_rows_get(r, s, n1, ts) for r in b_refs]))
        y = _dot(g, jnp.concatenate([br, bi_], axis=0).astype(BF16))
        for bi in range(2):
            start = bi * h * ts + s
            u = u_refs[0][bi, s] if u_s_major else cat([_rows_get(r, start, h, ts) for r in u_refs])
            x = cat([_rows_get(r, start, h, ts) for r in x_refs])
            val = x * (y[bi * h:(bi + 1) * h] + u * skip)
            if out_s_major:
                o_ref[bi, s] = val
            else:
                _rows_set(o_ref, start, h, ts, val)


def fft_stage_a_inv(g_i, bm, u4, u_off, u_s_major, x4, x_off, skip_row, out_s_major):
    n1, n2, c = bm.shape
    h = n1 // 2
    ts = SUBLANES
    nl = FFT_LANE_BLOCKS if out_s_major else 1
    tc = nl * LANES
    lane_blocks = lambda shape, imap: [pl.BlockSpec(shape, functools.partial(imap, l=l)) for l in range(nl)]
    nat = lambda off: lane_blocks((2, h, ts, LANES), lambda j, s, l: (0, 0, s, nl * j + l + off // LANES))
    smj = lambda off: pl.BlockSpec((2, ts, h, tc), lambda j, s: (0, s, 0, j + off // tc))
    in_specs = ([pl.BlockSpec((n1, 2 * n1), lambda j, s: (0, 0))]
                + lane_blocks((n1, ts, LANES), lambda j, s, l: (0, s, nl * j + l))
                + ([smj(u_off)] if u_s_major else nat(u_off))
                + nat(x_off)
                + [pl.BlockSpec((1, tc), lambda j, s: (0, j))])
    args = [g_i] + [bm] * nl + [u4] * (1 if u_s_major else nl) + [x4] * nl + [skip_row]
    return pl.pallas_call(
        functools.partial(_fft_ai_kernel, nl=nl, u_s_major=u_s_major, out_s_major=out_s_major),
        grid=(c // tc, n2 // ts),
        in_specs=in_specs,
        out_specs=smj(0) if out_s_major else nat(0)[0],
        out_shape=jax.ShapeDtypeStruct((2, n2, h, c) if out_s_major else (2, h, n2, c), F32),
        compiler_params=_cparams("parallel", "parallel"),
        name="fft_stage_a_inv",
    )(*args)


def hyena_long(u, taps, skip, tabs):
    b, length, c3 = u.shape
    assert b == 2, "the two batch rows are packed as one complex sequence"
    c = c3 // 3
    n2 = FFT_N2
    n1 = 2 * length // n2
    h = n1 // 2
    u4 = u.reshape(b, h, n2, c3)
    taps4 = taps.reshape(HY_ORDER * 2, h, n2, c)
    zin, zoff, z_s_major = u4, 0, False
    out = None
    for conv in range(HY_ORDER):
        last = conv == HY_ORDER - 1
        hspec = fft_stage_b(tabs["m_f"], None,
                            fft_stage_a(tabs["g_t"], taps4, (2 * conv, 0), (2 * conv + 1, 0), 0, c), None)
        a = fft_stage_a(tabs["g_u"], zin, (0, 0), (1, 0), zoff, c, s_major_in=z_s_major)
        bm = fft_stage_b(tabs["m_f"], tabs["m_i"], a, hspec)
        out = fft_stage_a_inv(tabs["g_i"], bm, zin, zoff, z_s_major, u4, (conv + 1) * c,
                              skip[conv].reshape(1, c), not last)
        zin, zoff, z_s_major = out, 0, True
    return out.reshape(b, length, c)


def _hyena_ctx_kernel(u_ref, x1_ref, x2_ref, taps_ref, fc_ref, fs_ref, skip_ref, o_ref):
    length = u_ref.shape[1]
    n = 2 * length
    fc = fc_ref[...]
    fs = fs_ref[...]
    fcl, fsl = fc[:, :length], fs[:, :length]
    fct, fst = fc[:length, :], fs[:length, :]
    zr = u_ref[0]
    zi = u_ref[1]
    gates = (x1_ref, x2_ref)
    for conv in range(HY_ORDER):
        tp = taps_ref[conv].astype(BF16)
        hr, hi = _dot(fc, tp), -_dot(fs, tp)
        zrb, zib = zr.astype(BF16), zi.astype(BF16)
        xr = _dot(fcl, zrb) + _dot(fsl, zib)
        xi = _dot(fcl, zib) - _dot(fsl, zrb)
        yr = (xr * hr - xi * hi).astype(BF16)
        yi = (xr * hi + xi * hr).astype(BF16)
        cr = (_dot(fct, yr) - _dot(fst, yi)) * (1.0 / n)
        ci = (_dot(fct, yi) + _dot(fst, yr)) * (1.0 / n)
        sk = skip_ref[conv:conv + 1, :]
        zr = gates[conv][0] * (cr + zr * sk)
        zi = gates[conv][1] * (ci + zi * sk)
    o_ref[0] = zr.astype(o_ref.dtype)
    o_ref[1] = zi.astype(o_ref.dtype)


def hyena_ctx(u, taps, skip, tc=256):
    b, length, c3 = u.shape
    assert b == 2
    c = c3 // 3
    n = 2 * length
    k = jnp.arange(n, dtype=jnp.int32)
    fc, fs = _cs((k[:, None] * k[None, :]) % n, n)
    nb = c // tc
    ub = lambda off: pl.BlockSpec((2, length, tc), lambda j: (0, 0, j + off * nb))
    mat = pl.BlockSpec((n, n), lambda j: (0, 0))
    return pl.pallas_call(
        _hyena_ctx_kernel,
        grid=(nb,),
        in_specs=[ub(0), ub(1), ub(2),
                  pl.BlockSpec((HY_ORDER, n, tc), lambda j: (0, 0, j)),
                  mat, mat,
                  pl.BlockSpec((HY_ORDER, tc), lambda j: (0, j))],
        out_specs=pl.BlockSpec((2, length, tc), lambda j: (0, 0, j)),
        out_shape=jax.ShapeDtypeStruct((b, length, c), BF16),
        compiler_params=_cparams("parallel"),
        name="hyena_ctx",
    )(u, u, u, taps, fc.astype(BF16), fs.astype(BF16), skip)


def rope_tables(length):
    rows = length // GRID_W
    row = jnp.repeat(jnp.arange(rows, dtype=F32), GRID_W)
    col = (jnp.arange(length) % GRID_W).astype(F32)
    inv = ROPE_THETA ** (-jnp.arange(0, ROPE_AXIS_DIM, 2, dtype=F32) / ROPE_AXIS_DIM)
    ang = jnp.stack([row[:, None] * inv, col[:, None] * inv], axis=1)
    cos, sin = jnp.cos(ang), jnp.sin(ang)
    zero = jnp.zeros_like(sin)
    per_map = lambda lo, hi: jnp.concatenate([lo, hi], axis=-1).reshape(length, 2 * ROPE_AXIS_DIM)
    reps = LANES // (2 * ROPE_AXIS_DIM)
    c_t = jnp.tile(per_map(cos, cos), (1, reps))
    sm_t = jnp.tile(per_map(-sin, zero), (1, reps))
    sp_t = jnp.tile(per_map(zero, sin), (1, reps))
    return c_t, sm_t, sp_t


def _rope_kernel(x_ref, c_ref, sm_ref, sp_ref, o_ref, *n_refs, scale):
    half = ROPE_AXIS_DIM // 2
    c, sm, sp = c_ref[...], sm_ref[...], sp_ref[...]
    for g in range(x_ref.shape[1] // LANES):
        ls = slice(g * LANES, (g + 1) * LANES)
        x = x_ref[:, ls].astype(F32)
        y = (x * c + pltpu.roll(x, LANES - half, 1) * sm + pltpu.roll(x, half, 1) * sp) * scale
        o_ref[:, ls] = y.astype(o_ref.dtype)
        if n_refs:
            n2 = jnp.max(jnp.sum(y * y, axis=1, keepdims=True), axis=0, keepdims=True)
            n_refs[0][g:g + 1, :] = jnp.broadcast_to(n2, (1, LANES))


def rope(x, tables, scale, with_norms=False, tr=512):
    b, t, d = x.shape
    tr = _pick_tile(t, tr)
    nh = d // LANES
    tab = pl.BlockSpec((tr, LANES), lambda bi, i: (i, 0))
    out_specs = [pl.BlockSpec((None, tr, d), lambda bi, i: (bi, i, 0))]
    out_shape = [jax.ShapeDtypeStruct((b, t, d), BF16)]
    if with_norms:
        out_specs.append(pl.BlockSpec((None, None, nh, LANES), lambda bi, i: (bi, i, 0, 0)))
        out_shape.append(jax.ShapeDtypeStruct((b, t // tr, nh, LANES), F32))
    outs = pl.pallas_call(
        functools.partial(_rope_kernel, scale=scale),
        grid=(b, t // tr),
        in_specs=[pl.BlockSpec((None, tr, d), lambda bi, i: (bi, i, 0)), tab, tab, tab],
        out_specs=out_specs,
        out_shape=out_shape,
        compiler_params=_cparams("parallel", "parallel"),
        name="rope",
    )(x, *tables)
    if with_norms:
        return outs[0], jnp.swapaxes(outs[1], 1, 2)
    return outs[0]


def _scale_cast_kernel(x_ref, o_ref, *, scale):
    o_ref[...] = (x_ref[...].astype(F32) * scale).astype(o_ref.dtype)


def scale_cast(x, scale, tr=256):
    b, t, d = x.shape
    tr = _pick_tile(t, tr)
    blk = pl.BlockSpec((None, tr, d), lambda bi, i: (bi, i, 0))
    return pl.pallas_call(
        functools.partial(_scale_cast_kernel, scale=scale),
        grid=(b, t // tr), in_specs=[blk], out_specs=blk,
        out_shape=jax.ShapeDtypeStruct((b, t, d), BF16),
        compiler_params=_cparams("parallel", "parallel"),
        name="scale_cast",
    )(x)


ATTN_GUARD_LOG2 = 80.0
ATTN_NORM_SLACK = 1.03


def _attn_kernel(*refs, lam_init, nk, with_ctx):
    if with_ctx:
        (q_ref, k_ref, v_ref, kc_ref, vc_ref, kn_ref, lam_ref, w_ref, o_ref,
         q2_ref, m_ref, c_ref, acc_ref, p_scr, flag_ref) = refs
    else:
        q_ref, k_ref, v_ref, lam_ref, w_ref, o_ref, q2_ref, m_ref, c_ref, acc_ref, p_scr, flag_ref = refs
    ki = pl.program_id(3)
    tq = q_ref.shape[0]

    def with_ones(v):
        return jnp.concatenate([v, jnp.ones_like(v)], axis=1)

    def exact_update(k, v, p_buf):
        s = _dot_nt(q2_ref[...], k)
        m_prev = m_ref[...]
        m_new = jnp.maximum(m_prev, jnp.max(s, axis=1, keepdims=True))
        alpha = jnp.exp2(m_prev - m_new)
        p_buf[...] = jnp.exp2(s - m_new).astype(BF16)
        acc_ref[...] = alpha * acc_ref[...] + _dot(p_buf[...], with_ones(v))
        m_ref[...] = m_new

    def fast_update(k, v, p_buf):
        s = _dot_nt(q2_ref[...], k)
        p_buf[...] = jnp.exp2(s - pltpu.repeat(c_ref[...], k.shape[0] // LANES, axis=1)).astype(BF16)
        acc_ref[...] += _dot(p_buf[...], with_ones(v))

    @pl.when(ki == 0)
    def _():
        q = q_ref[...]
        lane = lax.broadcasted_iota(jnp.int32, q.shape, 1)
        zero = jnp.zeros_like(q)
        q2_ref[:tq, :] = jnp.where(lane < DA_HEAD_DIM, q, zero)
        q2_ref[tq:, :] = jnp.where(lane >= DA_HEAD_DIM, q, zero)
        m_ref[...] = jnp.full(m_ref.shape, -jnp.inf, F32)
        acc_ref[...] = jnp.zeros(acc_ref.shape, F32)
        flag_ref[0] = 0
        if with_ctx:
            q2 = q2_ref[...]
            sq = (q2.astype(F32) * q2.astype(F32)).astype(BF16)
            k0 = jnp.broadcast_to(kc_ref[0:1, :], (LANES, DA_V_DIM))
            zeros = jnp.zeros((LANES, DA_V_DIM), BF16)
            wt = jnp.concatenate([jnp.concatenate([k0, zeros], axis=1),
                                  jnp.concatenate([zeros, jnp.ones_like(zeros)], axis=1)], axis=0)
            res = _dot_nt(jnp.concatenate([q2, sq], axis=1), wt)
            c, qn2 = res[:, :LANES], res[:, LANES:]
            kn2 = jnp.max(kn_ref[...])
            gap = jnp.max(jnp.sqrt(qn2 * kn2) * ATTN_NORM_SLACK - c)
            c_ref[...] = c
            flag_ref[0] = (gap <= ATTN_GUARD_LOG2).astype(jnp.int32)

    fast = flag_ref[0] == 1
    if with_ctx:
        tc = kc_ref.shape[0]
        first = ki == 0

        @pl.when(jnp.logical_and(first, fast))
        def _():
            fast_update(kc_ref[...], vc_ref[...], p_scr.at[:, pl.ds(0, tc)])

        @pl.when(jnp.logical_and(first, jnp.logical_not(fast)))
        def _():
            exact_update(kc_ref[...], vc_ref[...], p_scr.at[:, pl.ds(0, tc)])

    @pl.when(fast)
    def _():
        fast_update(k_ref[...], v_ref[...], p_scr)

    @pl.when(jnp.logical_not(fast))
    def _():
        exact_update(k_ref[...], v_ref[...], p_scr)

    @pl.when(ki == nk - 1)
    def _():
        lv = lam_ref[...]
        e1 = jnp.exp(jnp.sum(lv[0:1, :] * lv[1:2, :], axis=1, keepdims=True))
        e2 = jnp.exp(jnp.sum(lv[2:3, :] * lv[3:4, :], axis=1, keepdims=True))
        lam = e1 - e2 + lam_init
        acc = acc_ref[...]
        o = acc[:, :DA_V_DIM] / acc[:, DA_V_DIM:]
        o = o[:tq] - lam * o[tq:]
        o_ref[...] = (_rms(o, w_ref[...]) * (1.0 - lam_init)).astype(o_ref.dtype)


def diff_attention(q, k, v, k_ctx, v_ctx, knorm, lam_vec, subln_w, lam_init, tq=1024, tk=2048):
    b, t, d = q.shape
    nh = d // DA_V_DIM
    tq = _pick_tile(t, tq)
    tk = _pick_tile(k.shape[1], tk)
    nk = k.shape[1] // tk
    with_ctx = k_ctx is not None
    assert not with_ctx or k_ctx.shape[1] <= tk
    qspec = pl.BlockSpec((None, tq, DA_V_DIM), lambda bi, h, i, j: (bi, i, h))
    kspec = pl.BlockSpec((None, tk, DA_V_DIM), lambda bi, h, i, j: (bi, j, h))
    in_specs = [qspec, kspec, kspec]
    args = [q, k, v]
    if with_ctx:
        cspec = pl.BlockSpec((None, k_ctx.shape[1], DA_V_DIM), lambda bi, h, i, j: (bi, 0, h))
        nspec = pl.BlockSpec((None, None) + knorm.shape[2:], lambda bi, h, i, j: (bi, h, 0, 0))
        in_specs += [cspec, cspec, nspec]
        args += [k_ctx, v_ctx, knorm]
    in_specs += [pl.BlockSpec((4, DA_HEAD_DIM), lambda bi, h, i, j: (0, 0)),
                 pl.BlockSpec((1, DA_V_DIM), lambda bi, h, i, j: (0, 0))]
    args += [lam_vec, subln_w.reshape(1, DA_V_DIM)]
    return pl.pallas_call(
        functools.partial(_attn_kernel, lam_init=lam_init, nk=nk, with_ctx=with_ctx),
        grid=(b, nh, t // tq, nk),
        in_specs=in_specs,
        out_specs=qspec,
        out_shape=jax.ShapeDtypeStruct((b, t, d), BF16),
        scratch_shapes=[pltpu.VMEM((2 * tq, DA_V_DIM), BF16),
                        pltpu.VMEM((2 * tq, 1), F32),
                        pltpu.VMEM((2 * tq, LANES), F32),
                        pltpu.VMEM((2 * tq, 2 * DA_V_DIM), F32),
                        pltpu.VMEM((2 * tq, tk), BF16),
                        pltpu.SMEM((1,), jnp.int32)],
        compiler_params=_cparams("parallel", "parallel", "parallel", "arbitrary"),
        name="diff_attention",
    )(*args)


def _merge_kernel(ya_ref, yb_ref, yc_ref, g_ref, wb_ref, wo_ref, x_ref, mod_ref, o_ref):
    d = x_ref.shape[1]
    ys = (ya_ref, yb_ref, yc_ref)
    s = None
    for k in range(N_BRANCH):
        t = jax.nn.sigmoid(g_ref[:, k * d:(k + 1) * d].astype(F32)) * _dot(ys[k][...].astype(BF16), wb_ref[k])
        s = t if s is None else s + t
    r = _dot(s.astype(BF16), wo_ref[...])
    o_ref[...] = x_ref[...] + mod_ref[2:3, :] * r


def merge(ya, yb, yc, gates, wb, wo, x, mods, tr=256):
    b, t, d = x.shape
    tr = _pick_tile(t, tr)
    per_batch = mods.shape[0] > 1
    blk = pl.BlockSpec((None, tr, d), lambda bi, i: (bi, i, 0))
    return pl.pallas_call(
        _merge_kernel,
        grid=(b, t // tr),
        in_specs=[blk, blk, blk,
                  pl.BlockSpec((None, tr, N_BRANCH * d), lambda bi, i: (bi, i, 0)),
                  pl.BlockSpec((N_BRANCH, d, d), lambda bi, i: (0, 0, 0)),
                  pl.BlockSpec((d, d), lambda bi, i: (0, 0)),
                  blk,
                  pl.BlockSpec((None, 8, d), (lambda bi, i: (bi, 0, 0)) if per_batch else (lambda bi, i: (0, 0, 0)))],
        out_specs=blk,
        out_shape=jax.ShapeDtypeStruct((b, t, d), F32),
        compiler_params=_cparams("parallel", "parallel"),
        name="merge",
    )(ya, yb, yc, gates, wb, wo, x, mods)


def _ffn_kernel(x_ref, g_ref, mod_ref, w13_ref, w2_ref, o_ref, *, nchunk):
    x = x_ref[...]
    h = (_rms(x, g_ref[...]) * (1.0 + mod_ref[4:5, :]) + mod_ref[3:4, :]).astype(BF16)
    f = w2_ref.shape[0]
    fc = f // nchunk
    acc = None
    for c in range(nchunk):
        a1 = _dot(h, w13_ref[:, c * fc:(c + 1) * fc])
        a3 = _dot(h, w13_ref[:, f + c * fc:f + (c + 1) * fc])
        t = _dot((_silu(a1) * a3).astype(BF16), w2_ref[c * fc:(c + 1) * fc, :])
        acc = t if acc is None else acc + t
    o_ref[...] = x + mod_ref[5:6, :] * acc


def ffn(x, g, mods, w13, w2, tr=256):
    b, t, d = x.shape
    f = w2.shape[0]
    tr = _pick_tile(t, tr)
    nchunk = 2 if (f // 2) % LANES == 0 else 1
    per_batch = mods.shape[0] > 1
    blk = pl.BlockSpec((None, tr, d), lambda bi, i: (bi, i, 0))
    return pl.pallas_call(
        functools.partial(_ffn_kernel, nchunk=nchunk),
        grid=(b, t // tr),
        in_specs=[blk,
                  pl.BlockSpec((1, d), lambda bi, i: (0, 0)),
                  pl.BlockSpec((None, 8, d), (lambda bi, i: (bi, 0, 0)) if per_batch else (lambda bi, i: (0, 0, 0))),
                  pl.BlockSpec((d, 2 * f), lambda bi, i: (0, 0)),
                  pl.BlockSpec((f, d), lambda bi, i: (0, 0))],
        out_specs=blk,
        out_shape=jax.ShapeDtypeStruct((b, t, d), F32),
        compiler_params=_cparams("parallel", "parallel"),
        name="ffn",
    )(x, g.reshape(1, d), mods, w13, w2)


def kernel(x, c, ctx, c_ctx, ada_w, ada_b, norm1_g, norm2_g, w_in, ssd_conv_w, ssd_conv_b,
           ssd_a_log, ssd_dt_bias, ssd_d, ssd_norm_w, hy_conv_w, hy_conv_b, hy_w1, hy_b1, hy_w2,
           hy_b2, hy_w3, hy_freq, hy_bias, da_lambda, da_subln_w, w_branch, w_out, ffn_w13,
           ffn_w2, final_g):
    bsz, length, d = x.shape
    ctx_len = ctx.shape[1]
    depth = w_in.shape[0]
    inner = SSD_HEADS * SSD_HEAD_DIM
    xbc_w = inner + 2 * SSD_GROUPS * SSD_STATE
    dt_w = 2 * SSD_HEADS
    hy_w = (HY_ORDER + 1) * d
    widths = (inner, xbc_w, dt_w, hy_w, d, d, d, N_BRANCH * d)
    offs = [0]
    for wd in widths:
        offs.append(offs[-1] + wd)

    cond = jnp.concatenate([c, c_ctx[None, :], jnp.zeros((8 - bsz - 1, d), F32)], axis=0)

    rope_tabs = rope_tables(length)
    tabs = fft_tables(2 * length // FFT_N2, FFT_N2)
    expand = (jnp.arange(LANES)[:, None] % SSD_HEADS
              == jnp.arange(inner)[None, :] // SSD_HEAD_DIM)
    expand_dir = [(expand & ((jnp.arange(LANES)[:, None] // SSD_HEADS) == dr)).astype(BF16) for dr in range(2)]
    qscale = DA_HEAD_DIM ** -0.5 * math.log2(math.e)

    x_l, x_c = x, ctx
    for layer in range(depth):
        last = layer == depth - 1
        lam_init = 0.8 - 0.6 * math.exp(-0.3 * layer)
        mod = matmul(cond, ada_w[layer].astype(BF16), ada_b[layer].reshape(1, -1), silu_in=True)
        mod = jnp.pad(mod.reshape(8, 6, d), ((0, 0), (0, 2), (0, 0)))
        mods_l, mods_c = mod[:bsz], mod[bsz:bsz + 1]

        w_l = w_in[layer].astype(BF16)
        w_parts = [w_l[:, offs[i]:offs[i + 1]] for i in range(len(widths))]
        w_parts[2] = jnp.pad(w_parts[2], ((0, 0), (0, LANES - dt_w)))
        a_neg = -jnp.exp(ssd_a_log[layer].astype(F32)).reshape(1, dt_w)
        aneg_row = jnp.pad(a_neg, ((0, 0), (0, LANES - dt_w)))
        bias_row = jnp.pad(ssd_dt_bias[layer].astype(F32).reshape(1, dt_w), ((0, 0), (0, LANES - dt_w)))
        d_row = jnp.repeat(ssd_d[layer].astype(F32), SSD_HEAD_DIM).reshape(1, inner)
        nw_row = ssd_norm_w[layer].astype(F32).reshape(1, inner)
        wb = w_branch[layer].astype(BF16)
        wo = w_out[layer].astype(BF16)
        w13 = ffn_w13[layer].astype(BF16)
        w2 = ffn_w2[layer].astype(BF16)

        def project(stream, mods):
            bb, tt, _ = stream.shape
            h = norm_mod(stream, norm1_g[layer], mods, 0).reshape(bb * tt, d)
            outs = [matmul(h, wp, out_dtype=(F32 if i == 2 else BF16)).reshape(bb, tt, -1)
                    for i, wp in enumerate(w_parts)]
            return outs

        z_c, xbc_c, dt_c, hy_c, q_c, k_c, v_c, g_c = project(x_c, mods_c)
        z_l, xbc_l, dt_l, hy_l, q_l, k_l, v_l, g_l = project(x_l, mods_l)

        xa_c = dwconv3(xbc_c, ssd_conv_w[layer], ssd_conv_b[layer], True)
        xa_l = dwconv3(xbc_l, ssd_conv_w[layer], ssd_conv_b[layer], True)
        h_zero = jnp.zeros((bsz, SSD_STATE, inner), F32)
        ys_c, ys_l = [], []
        for dr in range(2):
            y_c, st = ssd_scan(xa_c, dt_c, bias_row, aneg_row, expand_dir[dr], h_zero, dr)
            y_l, _ = ssd_scan(xa_l, dt_l, bias_row, aneg_row, expand_dir[dr], st, dr)
            ys_c.append(y_c)
            ys_l.append(y_l)
        ya_l = ssd_out(ys_l[0], ys_l[1], xa_l, z_l, d_row, nw_row)

        filt = (hy_w1[layer], hy_b1[layer], hy_w2[layer], hy_b2[layer], hy_w3[layer], hy_freq[layer])
        u_l = dwconv3(hy_l, hy_conv_w[layer], hy_conv_b[layer], False)
        yb_l = hyena_long(u_l, hyena_taps(length, *filt), hy_bias[layer], tabs)

        qr_l = rope(q_l, rope_tabs, qscale)
        kr_l, kn_l = rope(k_l, rope_tabs, 1.0, with_norms=True)
        kb_c = scale_cast(k_c, 1.0)
        yc_l = diff_attention(qr_l, kr_l, v_l, kb_c, v_c, kn_l, da_lambda[layer], da_subln_w[layer], lam_init)

        x_l_new = merge(ya_l, yb_l, yc_l, g_l, wb, wo, x_l, mods_l)
        x_l = ffn(x_l_new, norm2_g[layer], mods_l, w13, w2)

        if not last:
            ya_c = ssd_out(ys_c[0], ys_c[1], xa_c, z_c, d_row, nw_row)
            u_c = dwconv3(hy_c, hy_conv_w[layer], hy_conv_b[layer], False)
            yb_c = hyena_ctx(u_c, hyena_taps(ctx_len, *filt), hy_bias[layer])
            qb_c = scale_cast(q_c, qscale)
            yc_c = diff_attention(qb_c, kb_c, v_c, None, None, None, da_lambda[layer], da_subln_w[layer], lam_init)
            x_c_new = merge(ya_c, yb_c, yc_c, g_c, wb, wo, x_c, mods_c)
            x_c = ffn(x_c_new, norm2_g[layer], mods_c, w13, w2)
    return final_norm(x_l, final_g)
```

```python
import functools
import math

import jax
import jax.numpy as jnp
from jax import lax
from jax.experimental import pallas as pl
from jax.experimental.pallas import tpu as pltpu

F32 = jnp.float32
BF16 = jnp.bfloat16
EPS = 1e-6

SSD_HEADS = 16
SSD_HEAD_DIM = 64
SSD_GROUPS = 2
SSD_STATE = 128
SSD_CHUNK = 128
HY_ORDER = 2
HY_BANDS = 16
HY_SHORT_DECAY_PCT = 0.3
HY_LONG_DECAY_PCT = 1.5
HY_DECAY_TARGET = 1e-2
DA_HEADS = 8
DA_V_DIM = 128
DA_HEAD_DIM = 64
ROPE_AXIS_DIM = 32
ROPE_THETA = 10000.0
GRID_W = 64
N_BRANCH = 3

LANES = 128
SUBLANES = 8
FFT_N2 = 128
VMEM_LIMIT = 56 * 1024 * 1024


def _cparams(*sem):
    return pltpu.CompilerParams(dimension_semantics=sem, vmem_limit_bytes=VMEM_LIMIT)


def _dot(a, b):
    return jnp.dot(a, b, preferred_element_type=F32)


def _dot_nt(a, b):
    return lax.dot_general(a, b, (((1,), (1,)), ((), ())), preferred_element_type=F32)


def _split3(x):
    hi = x.astype(BF16)
    r1 = x - hi.astype(F32)
    mid = r1.astype(BF16)
    lo = (r1 - mid.astype(F32)).astype(BF16)
    return hi, mid, lo


def _silu(x):
    return x * jax.nn.sigmoid(x)


def _softplus(x):
    return jnp.maximum(x, 0.0) + jnp.log1p(jnp.exp(-jnp.abs(x)))


def _pick_tile(n, pref):
    t = min(pref, n)
    while n % t:
        t //= 2
    return t


def _matmul_kernel(a_ref, w_ref, b_ref, o_ref, *, silu_in):
    a = a_ref[...]
    if silu_in:
        a = _silu(a.astype(F32))
    r = _dot(a.astype(BF16), w_ref[...]) + b_ref[...]
    o_ref[...] = r.astype(o_ref.dtype)


def matmul(a, w, bias=None, out_dtype=F32, silu_in=False, tm=2048, tn=1024):
    m, k = a.shape
    n = w.shape[1]
    tm = _pick_tile(m, tm)
    tn = _pick_tile(n, tn)
    if bias is None:
        bias = jnp.zeros((1, n), F32)
    return pl.pallas_call(
        functools.partial(_matmul_kernel, silu_in=silu_in),
        grid=(m // tm, n // tn),
        in_specs=[pl.BlockSpec((tm, k), lambda i, j: (i, 0)),
                  pl.BlockSpec((k, tn), lambda i, j: (0, j)),
                  pl.BlockSpec((1, tn), lambda i, j: (0, j))],
        out_specs=pl.BlockSpec((tm, tn), lambda i, j: (i, j)),
        out_shape=jax.ShapeDtypeStruct((m, n), out_dtype),
        compiler_params=_cparams("parallel", "parallel"),
        name="matmul",
    )(a, w, bias)


def _rms(x, g):
    return x * lax.rsqrt(jnp.mean(x * x, axis=-1, keepdims=True) + EPS) * g


def _norm_mod_kernel(x_ref, g_ref, mod_ref, o_ref, *, row0):
    x = x_ref[...]
    y = _rms(x, g_ref[...])
    shift = mod_ref[row0:row0 + 1, :]
    scale = mod_ref[row0 + 1:row0 + 2, :]
    o_ref[...] = (y * (1.0 + scale) + shift).astype(o_ref.dtype)


def norm_mod(x, g, mods, row0, tr=512):
    b, t, d = x.shape
    tr = _pick_tile(t, tr)
    per_batch = mods.shape[0] > 1
    return pl.pallas_call(
        functools.partial(_norm_mod_kernel, row0=row0),
        grid=(b, t // tr),
        in_specs=[pl.BlockSpec((None, tr, d), lambda bi, i: (bi, i, 0)),
                  pl.BlockSpec((1, d), lambda bi, i: (0, 0)),
                  pl.BlockSpec((None, 8, d), (lambda bi, i: (bi, 0, 0)) if per_batch else (lambda bi, i: (0, 0, 0)))],
        out_specs=pl.BlockSpec((None, tr, d), lambda bi, i: (bi, i, 0)),
        out_shape=jax.ShapeDtypeStruct((b, t, d), BF16),
        compiler_params=_cparams("parallel", "parallel"),
        name="norm_mod",
    )(x, g.reshape(1, d), mods)


def _final_norm_kernel(x_ref, g_ref, o_ref):
    o_ref[...] = _rms(x_ref[...], g_ref[...])


def final_norm(x, g, tr=512):
    b, t, d = x.shape
    tr = _pick_tile(t, tr)
    return pl.pallas_call(
        _final_norm_kernel,
        grid=(b, t // tr),
        in_specs=[pl.BlockSpec((None, tr, d), lambda bi, i: (bi, i, 0)),
                  pl.BlockSpec((1, d), lambda bi, i: (0, 0))],
        out_specs=pl.BlockSpec((None, tr, d), lambda bi, i: (bi, i, 0)),
        out_shape=jax.ShapeDtypeStruct((b, t, d), F32),
        compiler_params=_cparams("parallel", "parallel"),
        name="final_norm",
    )(x, g.reshape(1, d))


def _dwconv_kernel(x_ref, xp_ref, xn_ref, w_ref, b_ref, o_ref, *, act, nblk):
    i = pl.program_id(1)
    x = x_ref[...].astype(F32)
    tr = x.shape[0]
    halo = xp_ref.shape[0]
    prev_row = jnp.where(i == 0, 0.0, xp_ref[halo - 1:halo, :].astype(F32))
    next_row = jnp.where(i == nblk - 1, 0.0, xn_ref[0:1, :].astype(F32))
    rows = lax.broadcasted_iota(jnp.int32, x.shape, 0)
    x_m1 = jnp.where(rows == 0, prev_row, pltpu.roll(x, 1, 0))
    x_p1 = jnp.where(rows == tr - 1, next_row, pltpu.roll(x, tr - 1, 0))
    y = b_ref[...] + x_m1 * w_ref[0:1, :] + x * w_ref[1:2, :] + x_p1 * w_ref[2:3, :]
    if act:
        y = _silu(y)
    o_ref[...] = y


def dwconv3(x, w, bias, act, tr=512, tc=1536):
    b, t, c = x.shape
    tr = _pick_tile(t, tr)
    tc = min(tc, c)
    assert c % tc == 0
    nblk = t // tr
    halo = SUBLANES * (4 // x.dtype.itemsize)
    rh = tr // halo
    lasth = t // halo - 1
    return pl.pallas_call(
        functools.partial(_dwconv_kernel, act=act, nblk=nblk),
        grid=(b, nblk, c // tc),
        in_specs=[pl.BlockSpec((None, tr, tc), lambda bi, i, j: (bi, i, j)),
                  pl.BlockSpec((None, halo, tc), lambda bi, i, j: (bi, jnp.maximum(i * rh - 1, 0), j)),
                  pl.BlockSpec((None, halo, tc), lambda bi, i, j: (bi, jnp.minimum((i + 1) * rh, lasth), j)),
                  pl.BlockSpec((3, tc), lambda bi, i, j: (0, j)),
                  pl.BlockSpec((1, tc), lambda bi, i, j: (0, j))],
        out_specs=pl.BlockSpec((None, tr, tc), lambda bi, i, j: (bi, i, j)),
        out_shape=jax.ShapeDtypeStruct((b, t, c), F32),
        compiler_params=_cparams("parallel", "parallel", "parallel"),
        name="dwconv3",
    )(x, x, x, w, bias.reshape(1, c))


SSD_STEP_CHUNKS = 4


def _ssd_kernel(x_ref, b_ref, c_ref, dt_ref, bias_ref, aneg_ref, e_ref, h0_ref,
                y_ref, hout_ref, st_ref, *, direction, nsteps):
    ci = pl.program_id(1)

    @pl.when(ci == 0)
    def _():
        st_ref[...] = h0_ref[...]

    q = SSD_CHUNK
    hd = SSD_HEAD_DIM
    gw = (SSD_HEADS // SSD_GROUPS) * hd
    row = lax.broadcasted_iota(jnp.int32, (q, q), 0)
    col = lax.broadcasted_iota(jnp.int32, (q, q), 1)
    mask = (col >= row) if direction else (col <= row)
    tri = mask.astype(BF16)
    lane = lax.broadcasted_iota(jnp.int32, (q, LANES), 1)
    first = lane < hd

    def state_free(rows):
        x = x_ref[rows, :]
        dtv = _softplus(dt_ref[rows, :] + bias_ref[...])
        adt = dtv * aneg_ref[...]
        acum = sum(_dot(tri, p) for p in _split3(adt))
        acum_t = sum(_dot_nt(p, tri) for p in _split3(adt.T))
        total = jnp.sum(adt, axis=0, keepdims=True)
        eac = jnp.exp(acum)
        dte = jnp.exp(total - acum)
        cd = jnp.broadcast_to(jnp.exp(total), (SUBLANES, LANES))
        stack = jnp.concatenate([dtv, eac, dte, cd], axis=0)
        ex = _dot(stack.astype(BF16), e_ref[...])
        dt_e, eac_e, dte_e, cd_e = ex[:q], ex[q:2 * q], ex[2 * q:3 * q], ex[3 * q:3 * q + 1]
        xdt = x * dt_e
        y_diag, c_gs, b_ts = [], [], []
        for g in range(SSD_GROUPS):
            c_g = c_ref[rows, g * SSD_STATE:(g + 1) * SSD_STATE].astype(BF16)
            b_g32 = b_ref[rows, g * SSD_STATE:(g + 1) * SSD_STATE]
            cb = _dot_nt(c_g, b_g32.astype(BF16))
            for jp in range(gw // LANES):
                h_a = g * (SSD_HEADS // SSD_GROUPS) + 2 * jp
                xpair = xdt[:, h_a * hd:h_a * hd + LANES]
                acc = None
                for k in range(2):
                    cix = direction * SSD_HEADS + h_a + k
                    seg = acum[:, cix:cix + 1] - acum_t[cix:cix + 1, :]
                    dec = jnp.where(mask, jnp.exp(jnp.where(mask, seg, 0.0)), 0.0)
                    w = (cb * dec).astype(BF16)
                    xk = jnp.where(first if k == 0 else jnp.logical_not(first), xpair, 0.0).astype(BF16)
                    d = _dot(w, xk)
                    acc = d if acc is None else acc + d
                y_diag.append(acc)
            c_gs.append(c_g)
            b_ts.append(b_g32.T.astype(BF16))
        xdte = (xdt * dte_e).astype(BF16)
        return y_diag, c_gs, b_ts, eac_e, cd_e, xdte

    nsub = x_ref.shape[0] // q
    order = list(range(nsub - 1, -1, -1) if direction else range(nsub))
    pre = {s: state_free(slice(s * q, (s + 1) * q)) for s in order}
    for s in order:
        y_diag, c_gs, b_ts, eac_e, cd_e, xdte = pre[s]
        for g in range(SSD_GROUPS):
            gs = slice(g * gw, (g + 1) * gw)
            s_g = st_ref[:, gs]
            y_off = _dot(c_gs[g], s_g.astype(BF16)) * eac_e[:, gs]
            for jp in range(gw // LANES):
                ls = slice(g * gw + jp * LANES, g * gw + (jp + 1) * LANES)
                y_ref[s * q:(s + 1) * q, ls] = y_diag[g * (gw // LANES) + jp] + y_off[:, jp * LANES:(jp + 1) * LANES]
            st_ref[:, gs] = s_g * cd_e[:, gs] + _dot(b_ts[g], xdte[:, gs])

    @pl.when(ci == nsteps - 1)
    def _():
        hout_ref[...] = st_ref[...]


def ssd_scan(xbc, dt_raw, dt_bias_row, aneg_row, expand, h0, direction):
    b, t, _ = xbc.shape
    q = SSD_CHUNK * min(SSD_STEP_CHUNKS, t // SSD_CHUNK)
    nsteps = t // q
    inner = SSD_HEADS * SSD_HEAD_DIM
    bn = SSD_GROUPS * SSD_STATE
    cmap = (lambda c: nsteps - 1 - c) if direction else (lambda c: c)
    y, hout = pl.pallas_call(
        functools.partial(_ssd_kernel, direction=direction, nsteps=nsteps),
        grid=(b, nsteps),
        in_specs=[pl.BlockSpec((None, q, inner), lambda bi, c: (bi, cmap(c), 0)),
                  pl.BlockSpec((None, q, bn), lambda bi, c: (bi, cmap(c), inner // bn)),
                  pl.BlockSpec((None, q, bn), lambda bi, c: (bi, cmap(c), inner // bn + 1)),
                  pl.BlockSpec((None, q, LANES), lambda bi, c: (bi, cmap(c), 0)),
                  pl.BlockSpec((1, LANES), lambda bi, c: (0, 0)),
                  pl.BlockSpec((1, LANES), lambda bi, c: (0, 0)),
                  pl.BlockSpec((LANES, inner), lambda bi, c: (0, 0)),
                  pl.BlockSpec((None, SSD_STATE, inner), lambda bi, c: (bi, 0, 0))],
        out_specs=[pl.BlockSpec((None, q, inner), lambda bi, c: (bi, cmap(c), 0)),
                   pl.BlockSpec((None, SSD_STATE, inner), lambda bi, c: (bi, 0, 0))],
        out_shape=[jax.ShapeDtypeStruct((b, t, inner), F32),
                   jax.ShapeDtypeStruct((b, SSD_STATE, inner), F32)],
        scratch_shapes=[pltpu.VMEM((SSD_STATE, inner), F32)],
        compiler_params=_cparams("parallel", "arbitrary"),
        name="ssd_scan_bwd" if direction else "ssd_scan_fwd",
    )(xbc, xbc, xbc, dt_raw, dt_bias_row, aneg_row, expand, h0)
    return y, hout


def _ssd_out_kernel(yf_ref, yb_ref, x_ref, z_ref, d_ref, w_ref, o_ref):
    y = yf_ref[...] + yb_ref[...] + d_ref[...] * x_ref[...]
    y = y * _silu(z_ref[...].astype(F32))
    gw = y.shape[1] // SSD_GROUPS
    for g in range(SSD_GROUPS):
        gs = slice(g * gw, (g + 1) * gw)
        yg = y[:, gs]
        yg = yg * lax.rsqrt(jnp.mean(yg * yg, axis=-1, keepdims=True) + EPS)
        o_ref[:, gs] = (yg * w_ref[:, gs]).astype(o_ref.dtype)


def ssd_out(yf, yb, xbc, z, d_row, w_row, tr=256):
    b, t, inner = yf.shape
    tr = _pick_tile(t, tr)
    blk = pl.BlockSpec((None, tr, inner), lambda bi, i: (bi, i, 0))
    vec = pl.BlockSpec((1, inner), lambda bi, i: (0, 0))
    return pl.pallas_call(
        _ssd_out_kernel,
        grid=(b, t // tr),
        in_specs=[blk, blk, blk, blk, vec, vec],
        out_specs=blk,
        out_shape=jax.ShapeDtypeStruct((b, t, inner), BF16),
        compiler_params=_cparams("parallel", "parallel"),
        name="ssd_out",
    )(yf, yb, xbc, z, d_row, w_row)


def _taps_kernel(f_ref, tv_ref, w1_ref, b1_ref, w2_ref, b2_ref, w3_ref, fr_ref, dl_ref, o_ref):
    hp = lax.Precision.HIGHEST
    fr = fr_ref[...]
    h = jnp.sin(fr * (jnp.dot(f_ref[...], w1_ref[...], precision=hp, preferred_element_type=F32) + b1_ref[...]))
    h = jnp.sin(fr * (jnp.dot(h, w2_ref[...], precision=hp, preferred_element_type=F32) + b2_ref[...]))
    filt = _dot(h.astype(BF16), w3_ref[...].astype(BF16))
    c = dl_ref.shape[1]
    win = jnp.exp(-tv_ref[:, 0:1] * dl_ref[...]) * tv_ref[:, 1:2]
    o_ref[0] = filt[:, :c] * win
    o_ref[1] = filt[:, c:] * win


def hyena_taps(length, w1, b1, w2, b2, w3, freq):
    hid = w1.shape[1]
    c = w3.shape[1] // (2 * HY_ORDER)
    n = 2 * length
    t = jnp.linspace(0.0, 1.0, length, dtype=F32)[:, None]
    phase = 2.0 * math.pi * jnp.arange(length, dtype=F32)[:, None] / length
    bands = jnp.linspace(1e-4, HY_BANDS - 1, HY_BANDS, dtype=F32)[None, :]
    feats = jnp.concatenate([t, jnp.cos(phase * bands), -jnp.sin(phase * bands)], axis=-1)
    nf = feats.shape[1]
    rev = jnp.concatenate([jnp.zeros((1,), jnp.int32), jnp.arange(length - 1, 0, -1, dtype=jnp.int32)])
    feats2 = jnp.concatenate([feats, feats[rev]], axis=0)
    feats2 = jnp.pad(feats2, ((0, 0), (0, LANES - nf)))
    valid = jnp.ones((n,), F32).at[length].set(0.0)
    tv = jnp.stack([jnp.concatenate([t[:, 0], t[rev, 0]]), valid], axis=1)
    max_decay = math.log(HY_DECAY_TARGET) / HY_SHORT_DECAY_PCT
    min_decay = math.log(HY_DECAY_TARGET) / HY_LONG_DECAY_PCT
    deltas = jnp.abs(jnp.linspace(min_decay, max_decay, c, dtype=F32))[None, :]
    w1p = jnp.pad(w1, ((0, LANES - nf), (0, 0)))
    w3r = w3.reshape(hid, HY_ORDER, 2, c).transpose(2, 0, 1, 3).reshape(2, hid, HY_ORDER * c)
    tr = _pick_tile(length, 1024)
    nhalf = length // tr
    return pl.pallas_call(
        _taps_kernel,
        grid=(n // tr,),
        in_specs=[pl.BlockSpec((tr, LANES), lambda i: (i, 0)),
                  pl.BlockSpec((tr, 2), lambda i: (i, 0)),
                  pl.BlockSpec((LANES, hid), lambda i: (0, 0)),
                  pl.BlockSpec((1, hid), lambda i: (0, 0)),
                  pl.BlockSpec((hid, hid), lambda i: (0, 0)),
                  pl.BlockSpec((1, hid), lambda i: (0, 0)),
                  pl.BlockSpec((None, hid, HY_ORDER * c), lambda i: (i // nhalf, 0, 0)),
                  pl.BlockSpec((1, hid), lambda i: (0, 0)),
                  pl.BlockSpec((1, c), lambda i: (0, 0))],
        out_specs=pl.BlockSpec((HY_ORDER, tr, c), lambda i: (0, i, 0)),
        out_shape=jax.ShapeDtypeStruct((HY_ORDER, n, c), F32),
        compiler_params=_cparams("parallel"),
        name="hyena_taps",
    )(feats2, tv, w1p, b1.reshape(1, hid), w2, b2.reshape(1, hid), w3r, freq.reshape(1, hid), deltas)


def _cs(num, den):
    ang = (2.0 * math.pi / den) * num.astype(F32)
    return jnp.cos(ang), jnp.sin(ang)


def fft_tables(n1, n2):
    n = n1 * n2
    h = n1 // 2
    k1 = jnp.arange(n1, dtype=jnp.int32)
    c1, s1 = _cs((k1[:, None] * k1[None, :]) % n1, n1)
    g_u = jnp.block([[c1[:, :h], s1[:, :h]], [-s1[:, :h], c1[:, :h]]])
    g_t = jnp.concatenate([c1, -s1], axis=0)
    g_i = jnp.block([[c1[:h, :], -s1[:h, :]], [s1[:h, :], c1[:h, :]]])
    s = jnp.arange(n2, dtype=jnp.int32)
    num = (n1 * (s[:, None] * s[None, :]))[None] + (k1[:, None, None] * s[None, None, :])
    cb, sb = _cs(num % n, n)
    m_f = jnp.concatenate([jnp.concatenate([cb, sb], axis=2),
                           jnp.concatenate([-sb, cb], axis=2)], axis=1)
    cbt, sbt = jnp.swapaxes(cb, 1, 2) / n, jnp.swapaxes(sb, 1, 2) / n
    m_i = jnp.concatenate([jnp.concatenate([cbt, -sbt], axis=2),
                           jnp.concatenate([sbt, cbt], axis=2)], axis=1)
    return dict(g_u=g_u.astype(BF16), g_t=g_t.astype(BF16), g_i=g_i.astype(BF16),
                m_f=m_f.astype(BF16), m_i=m_i.astype(BF16))


FFT_KGROUP = SUBLANES


def _rows_get(ref, start, n, stride):
    return ref.reshape(math.prod(ref.shape[:-1]), ref.shape[-1])[pl.ds(start, n, stride=stride), :]


def _rows_set(ref, start, n, stride, val):
    ref.reshape(math.prod(ref.shape[:-1]), ref.shape[-1])[pl.ds(start, n, stride=stride), :] = val


def _cpack(re, im):
    rb = lax.bitcast_convert_type(re.astype(BF16).astype(F32), jnp.uint32)
    ib = lax.bitcast_convert_type(im.astype(BF16).astype(F32), jnp.uint32)
    return (rb & jnp.uint32(0xFFFF0000)) | (ib >> 16)


def _cunpack(w):
    re = lax.bitcast_convert_type(w & jnp.uint32(0xFFFF0000), F32)
    im = lax.bitcast_convert_type(w << 16, F32)
    return re, im


FFT_LANE_BLOCKS = 2


def _fft_a_kernel(g_ref, *refs, s_major_in):
    *x_refs, o_ref = refs
    xa_refs, xb_refs = x_refs[:len(x_refs) // 2], x_refs[len(x_refs) // 2:]
    h = g_ref.shape[1] // 2
    ts, n1, _ = o_ref.shape
    g = g_ref[...]

    def rows(x_refs_, s):
        if s_major_in:
            return x_refs_[0][s]
        return jnp.concatenate([_rows_get(r, s, h, ts) for r in x_refs_], axis=1)

    for s in range(ts):
        x = jnp.concatenate([rows(xa_refs, s), rows(xb_refs, s)], axis=0).astype(BF16)
        r = _dot(g, x)
        o_ref[s] = _cpack(r[:n1], r[n1:])


def fft_stage_a(g, x4, sel_a, sel_b, col_off, c, s_major_in=False):
    n1 = g.shape[0] // 2
    h = n1 // 2
    n2 = x4.shape[1] if s_major_in else x4.shape[2]
    ts = SUBLANES
    nl = FFT_LANE_BLOCKS
    tc = nl * LANES
    if s_major_in:
        coff = col_off // tc
        specs = lambda sel: [pl.BlockSpec((None, ts, h, tc), lambda j, s: (sel[0], s, sel[1], j + coff))]
    else:
        coff = col_off // LANES
        specs = lambda sel: [pl.BlockSpec((None, h, ts, LANES),
                                          functools.partial(lambda j, s, l: (sel[0], sel[1], s, nl * j + l + coff), l=l))
                             for l in range(nl)]
    x_specs = specs(sel_a) + specs(sel_b)
    return pl.pallas_call(
        functools.partial(_fft_a_kernel, s_major_in=s_major_in),
        grid=(c // tc, n2 // ts),
        in_specs=[pl.BlockSpec((2 * n1, n1), lambda j, s: (0, 0))] + x_specs,
        out_specs=pl.BlockSpec((ts, n1, tc), lambda j, s: (s, 0, j)),
        out_shape=jax.ShapeDtypeStruct((n2, n1, c), jnp.uint32),
        compiler_params=_cparams("parallel", "parallel"),
        name="fft_stage_a",
    )(g, *([x4] * len(x_specs)))


def _fft_load_k(a_refs, j):
    n2, kb, _ = a_refs[0].shape
    re, im = _cunpack(jnp.concatenate([_rows_get(r, j, n2, kb) for r in a_refs], axis=1))
    return jnp.concatenate([re, im], axis=0).astype(BF16)


def _fft_bh_kernel(m_ref, *refs, groups):
    *a_refs, o_ref = refs
    n2, kb, _ = a_refs[0].shape
    nl = len(a_refs) // groups
    for g in range(groups):
        for j in range(kb):
            x = _dot(m_ref[g * kb + j], _fft_load_k(a_refs[g * nl:(g + 1) * nl], j))
            o_ref[g * kb + j] = _cpack(x[:n2], x[n2:])


def _fft_b_kernel(mf_ref, mi_ref, *refs, groups):
    *a_refs, h_ref, o_ref = refs
    n2, kb, _ = a_refs[0].shape
    nl = len(a_refs) // groups
    for g in range(groups):
        for j in range(kb):
            k = g * kb + j
            x = _dot(mf_ref[k], _fft_load_k(a_refs[g * nl:(g + 1) * nl], j))
            xr, xi = x[:n2], x[n2:]
            hr, hi = _cunpack(h_ref[k])
            y = jnp.concatenate([xr * hr - xi * hi, xr * hi + xi * hr], axis=0).astype(BF16)
            r = _dot(mi_ref[k], y)
            o_ref[k] = _cpack(r[:n2], r[n2:])


def fft_stage_b(m_f, m_i, a, hspec, groups=2):
    n2, n1, c = a.shape
    groups = min(groups, n1 // FFT_KGROUP)
    kb = FFT_KGROUP * groups
    nl = FFT_LANE_BLOCKS
    tc = nl * LANES
    ablks = [pl.BlockSpec((n2, FFT_KGROUP, LANES),
                          functools.partial(lambda k, j, g, l: (0, groups * k + g, nl * j + l), g=g, l=l))
             for g in range(groups) for l in range(nl)]
    kblk = pl.BlockSpec((kb, n2, tc), lambda k, j: (k, 0, j))
    mat = pl.BlockSpec((kb, 2 * n2, 2 * n2), lambda k, j: (k, 0, 0))
    if hspec is None:
        kern, in_specs, name = _fft_bh_kernel, [mat] + ablks, "fft_stage_b_spectrum"
        args = (m_f,) + (a,) * len(ablks)
    else:
        kern, in_specs, name = _fft_b_kernel, [mat, mat] + ablks + [kblk], "fft_stage_b"
        args = (m_f, m_i) + (a,) * len(ablks) + (hspec,)
    return pl.pallas_call(
        functools.partial(kern, groups=groups),
        grid=(n1 // kb, c // tc),
        in_specs=in_specs,
        out_specs=kblk,
        out_shape=jax.ShapeDtypeStruct((n1, n2, c), jnp.uint32),
        compiler_params=_cparams("parallel", "parallel"),
        name=name,
    )(*args)


def _fft_ai_kernel(g_ref, *refs, nl, u_s_major, out_s_major):
    nu = 1 if u_s_major else nl
    b_refs, u_refs, x_refs = refs[:nl], refs[nl:nl + nu], refs[nl + nu:2 * nl + nu]
    skip_ref, o_ref = refs[-2:]
    n1, ts, _ = b_refs[0].shape
    h = n1 // 2
    g = g_ref[...]
    skip = skip_ref[...]
    cat = lambda parts: parts[0] if len(parts) == 1 else jnp.concatenate(parts, axis=1)
    for s in range(ts):
        br, bi_ = _cunpack(cat([_rows_get(r, s, n1, ts) for r in b_refs]))
        y = _dot(g, jnp.concatenate([br, bi_], axis=0).astype(BF16))
        for bi in range(2):
            start = bi * h * ts + s
            u = u_refs[0][bi, s] if u_s_major else cat([_rows_get(r, start, h, ts) for r in u_refs])
            x = cat([_rows_get(r, start, h, ts) for r in x_refs])
            val = x * (y[bi * h:(bi + 1) * h] + u * skip)
            if out_s_major:
                o_ref[bi, s] = val
            else:
                _rows_set(o_ref, start, h, ts, val)


def fft_stage_a_inv(g_i, bm, u4, u_off, u_s_major, x4, x_off, skip_row, out_s_major):
    n1, n2, c = bm.shape
    h = n1 // 2
    ts = SUBLANES
    nl = FFT_LANE_BLOCKS if out_s_major else 1
    tc = nl * LANES
    lane_blocks = lambda shape, imap: [pl.BlockSpec(shape, functools.partial(imap, l=l)) for l in range(nl)]
    nat = lambda off: lane_blocks((2, h, ts, LANES), lambda j, s, l: (0, 0, s, nl * j + l + off // LANES))
    smj = lambda off: pl.BlockSpec((2, ts, h, tc), lambda j, s: (0, s, 0, j + off // tc))
    in_specs = ([pl.BlockSpec((n1, 2 * n1), lambda j, s: (0, 0))]
                + lane_blocks((n1, ts, LANES), lambda j, s, l: (0, s, nl * j + l))
                + ([smj(u_off)] if u_s_major else nat(u_off))
                + nat(x_off)
                + [pl.BlockSpec((1, tc), lambda j, s: (0, j))])
    args = [g_i] + [bm] * nl + [u4] * (1 if u_s_major else nl) + [x4] * nl + [skip_row]
    return pl.pallas_call(
        functools.partial(_fft_ai_kernel, nl=nl, u_s_major=u_s_major, out_s_major=out_s_major),
        grid=(c // tc, n2 // ts),
        in_specs=in_specs,
        out_specs=smj(0) if out_s_major else nat(0)[0],
        out_shape=jax.ShapeDtypeStruct((2, n2, h, c) if out_s_major else (2, h, n2, c), F32),
        compiler_params=_cparams("parallel", "parallel"),
        name="fft_stage_a_inv",
    )(*args)


def hyena_long(u, taps, skip, tabs):
    b, length, c3 = u.shape
    assert b == 2, "the two batch rows are packed as one complex sequence"
    c = c3 // 3
    n2 = FFT_N2
    n1 = 2 * length // n2
    h = n1 // 2
    u4 = u.reshape(b, h, n2, c3)
    taps4 = taps.reshape(HY_ORDER * 2, h, n2, c)
    zin, zoff, z_s_major = u4, 0, False
    out = None
    for conv in range(HY_ORDER):
        last = conv == HY_ORDER - 1
        hspec = fft_stage_b(tabs["m_f"], None,
                            fft_stage_a(tabs["g_t"], taps4, (2 * conv, 0), (2 * conv + 1, 0), 0, c), None)
        a = fft_stage_a(tabs["g_u"], zin, (0, 0), (1, 0), zoff, c, s_major_in=z_s_major)
        bm = fft_stage_b(tabs["m_f"], tabs["m_i"], a, hspec)
        out = fft_stage_a_inv(tabs["g_i"], bm, zin, zoff, z_s_major, u4, (conv + 1) * c,
                              skip[conv].reshape(1, c), not last)
        zin, zoff, z_s_major = out, 0, True
    return out.reshape(b, length, c)


def _hyena_ctx_kernel(u_ref, x1_ref, x2_ref, taps_ref, fc_ref, fs_ref, skip_ref, o_ref):
    length = u_ref.shape[1]
    n = 2 * length
    fc = fc_ref[...]
    fs = fs_ref[...]
    fcl, fsl = fc[:, :length], fs[:, :length]
    fct, fst = fc[:length, :], fs[:length, :]
    zr = u_ref[0]
    zi = u_ref[1]
    gates = (x1_ref, x2_ref)
    for conv in range(HY_ORDER):
        tp = taps_ref[conv].astype(BF16)
        hr, hi = _dot(fc, tp), -_dot(fs, tp)
        zrb, zib = zr.astype(BF16), zi.astype(BF16)
        xr = _dot(fcl, zrb) + _dot(fsl, zib)
        xi = _dot(fcl, zib) - _dot(fsl, zrb)
        yr = (xr * hr - xi * hi).astype(BF16)
        yi = (xr * hi + xi * hr).astype(BF16)
        cr = (_dot(fct, yr) - _dot(fst, yi)) * (1.0 / n)
        ci = (_dot(fct, yi) + _dot(fst, yr)) * (1.0 / n)
        sk = skip_ref[conv:conv + 1, :]
        zr = gates[conv][0] * (cr + zr * sk)
        zi = gates[conv][1] * (ci + zi * sk)
    o_ref[0] = zr.astype(o_ref.dtype)
    o_ref[1] = zi.astype(o_ref.dtype)


def hyena_ctx(u, taps, skip, tc=256):
    b, length, c3 = u.shape
    assert b == 2
    c = c3 // 3
    n = 2 * length
    k = jnp.arange(n, dtype=jnp.int32)
    fc, fs = _cs((k[:, None] * k[None, :]) % n, n)
    nb = c // tc
    ub = lambda off: pl.BlockSpec((2, length, tc), lambda j: (0, 0, j + off * nb))
    mat = pl.BlockSpec((n, n), lambda j: (0, 0))
    return pl.pallas_call(
        _hyena_ctx_kernel,
        grid=(nb,),
        in_specs=[ub(0), ub(1), ub(2),
                  pl.BlockSpec((HY_ORDER, n, tc), lambda j: (0, 0, j)),
                  mat, mat,
                  pl.BlockSpec((HY_ORDER, tc), lambda j: (0, j))],
        out_specs=pl.BlockSpec((2, length, tc), lambda j: (0, 0, j)),
        out_shape=jax.ShapeDtypeStruct((b, length, c), BF16),
        compiler_params=_cparams("parallel"),
        name="hyena_ctx",
    )(u, u, u, taps, fc.astype(BF16), fs.astype(BF16), skip)


def rope_tables(length):
    rows = length // GRID_W
    row = jnp.repeat(jnp.arange(rows, dtype=F32), GRID_W)
    col = (jnp.arange(length) % GRID_W).astype(F32)
    inv = ROPE_THETA ** (-jnp.arange(0, ROPE_AXIS_DIM, 2, dtype=F32) / ROPE_AXIS_DIM)
    ang = jnp.stack([row[:, None] * inv, col[:, None] * inv], axis=1)
    cos, sin = jnp.cos(ang), jnp.sin(ang)
    zero = jnp.zeros_like(sin)
    per_map = lambda lo, hi: jnp.concatenate([lo, hi], axis=-1).reshape(length, 2 * ROPE_AXIS_DIM)
    reps = LANES // (2 * ROPE_AXIS_DIM)
    c_t = jnp.tile(per_map(cos, cos), (1, reps))
    sm_t = jnp.tile(per_map(-sin, zero), (1, reps))
    sp_t = jnp.tile(per_map(zero, sin), (1, reps))
    return c_t, sm_t, sp_t


def _rope_kernel(x_ref, c_ref, sm_ref, sp_ref, o_ref, *n_refs, scale):
    half = ROPE_AXIS_DIM // 2
    c, sm, sp = c_ref[...], sm_ref[...], sp_ref[...]
    for g in range(x_ref.shape[1] // LANES):
        ls = slice(g * LANES, (g + 1) * LANES)
        x = x_ref[:, ls].astype(F32)
        y = (x * c + pltpu.roll(x, LANES - half, 1) * sm + pltpu.roll(x, half, 1) * sp) * scale
        o_ref[:, ls] = y.astype(o_ref.dtype)
        if n_refs:
            n2 = jnp.max(jnp.sum(y * y, axis=1, keepdims=True), axis=0, keepdims=True)
            n_refs[0][g:g + 1, :] = jnp.broadcast_to(n2, (1, LANES))


def rope(x, tables, scale, with_norms=False, tr=512):
    b, t, d = x.shape
    tr = _pick_tile(t, tr)
    nh = d // LANES
    tab = pl.BlockSpec((tr, LANES), lambda bi, i: (i, 0))
    out_specs = [pl.BlockSpec((None, tr, d), lambda bi, i: (bi, i, 0))]
    out_shape = [jax.ShapeDtypeStruct((b, t, d), BF16)]
    if with_norms:
        out_specs.append(pl.BlockSpec((None, None, nh, LANES), lambda bi, i: (bi, i, 0, 0)))
        out_shape.append(jax.ShapeDtypeStruct((b, t // tr, nh, LANES), F32))
    outs = pl.pallas_call(
        functools.partial(_rope_kernel, scale=scale),
        grid=(b, t // tr),
        in_specs=[pl.BlockSpec((None, tr, d), lambda bi, i: (bi, i, 0)), tab, tab, tab],
        out_specs=out_specs,
        out_shape=out_shape,
        compiler_params=_cparams("parallel", "parallel"),
        name="rope",
    )(x, *tables)
    if with_norms:
        return outs[0], jnp.swapaxes(outs[1], 1, 2)
    return outs[0]


def _scale_cast_kernel(x_ref, o_ref, *, scale):
    o_ref[...] = (x_ref[...].astype(F32) * scale).astype(o_ref.dtype)


def scale_cast(x, scale, tr=256):
    b, t, d = x.shape
    tr = _pick_tile(t, tr)
    blk = pl.BlockSpec((None, tr, d), lambda bi, i: (bi, i, 0))
    return pl.pallas_call(
        functools.partial(_scale_cast_kernel, scale=scale),
        grid=(b, t // tr), in_specs=[blk], out_specs=blk,
        out_shape=jax.ShapeDtypeStruct((b, t, d), BF16),
        compiler_params=_cparams("parallel", "parallel"),
        name="scale_cast",
    )(x)


ATTN_GUARD_LOG2 = 80.0
ATTN_NORM_SLACK = 1.03


def _attn_kernel(*refs, lam_init, nk, with_ctx):
    if with_ctx:
        (q_ref, k_ref, v_ref, kc_ref, vc_ref, kn_ref, lam_ref, w_ref, o_ref,
         q2_ref, m_ref, c_ref, acc_ref, p_scr, flag_ref) = refs
    else:
        q_ref, k_ref, v_ref, lam_ref, w_ref, o_ref, q2_ref, m_ref, c_ref, acc_ref, p_scr, flag_ref = refs
    ki = pl.program_id(3)
    tq = q_ref.shape[0]

    def with_ones(v):
        return jnp.concatenate([v, jnp.ones_like(v)], axis=1)

    def exact_update(k, v, p_buf):
        s = _dot_nt(q2_ref[...], k)
        m_prev = m_ref[...]
        m_new = jnp.maximum(m_prev, jnp.max(s, axis=1, keepdims=True))
        alpha = jnp.exp2(m_prev - m_new)
        p_buf[...] = jnp.exp2(s - m_new).astype(BF16)
        acc_ref[...] = alpha * acc_ref[...] + _dot(p_buf[...], with_ones(v))
        m_ref[...] = m_new

    def fast_update(k, v, p_buf):
        s = _dot_nt(q2_ref[...], k)
        p_buf[...] = jnp.exp2(s - jnp.tile(c_ref[...], (1, k.shape[0] // LANES))).astype(BF16)
        acc_ref[...] += _dot(p_buf[...], with_ones(v))

    @pl.when(ki == 0)
    def _():
        q = q_ref[...]
        lane = lax.broadcasted_iota(jnp.int32, q.shape, 1)
        zero = jnp.zeros_like(q)
        q2_ref[:tq, :] = jnp.where(lane < DA_HEAD_DIM, q, zero)
        q2_ref[tq:, :] = jnp.where(lane >= DA_HEAD_DIM, q, zero)
        m_ref[...] = jnp.full(m_ref.shape, -jnp.inf, F32)
        acc_ref[...] = jnp.zeros(acc_ref.shape, F32)
        flag_ref[0] = 0
        if with_ctx:
            q2 = q2_ref[...]
            sq = (q2.astype(F32) * q2.astype(F32)).astype(BF16)
            k0 = jnp.broadcast_to(kc_ref[0:1, :], (LANES, DA_V_DIM))
            zeros = jnp.zeros((LANES, DA_V_DIM), BF16)
            wt = jnp.concatenate([jnp.concatenate([k0, zeros], axis=1),
                                  jnp.concatenate([zeros, jnp.ones_like(zeros)], axis=1)], axis=0)
            res = _dot_nt(jnp.concatenate([q2, sq], axis=1), wt)
            c, qn2 = res[:, :LANES], res[:, LANES:]
            kn2 = jnp.max(kn_ref[...])
            gap = jnp.max(jnp.sqrt(qn2 * kn2) * ATTN_NORM_SLACK - c)
            c_ref[...] = c
            flag_ref[0] = (gap <= ATTN_GUARD_LOG2).astype(jnp.int32)

    fast = flag_ref[0] == 1
    if with_ctx:
        tc = kc_ref.shape[0]
        first = ki == 0

        @pl.when(jnp.logical_and(first, fast))
        def _():
            fast_update(kc_ref[...], vc_ref[...], p_scr.at[:, pl.ds(0, tc)])

        @pl.when(jnp.logical_and(first, jnp.logical_not(fast)))
        def _():
            exact_update(kc_ref[...], vc_ref[...], p_scr.at[:, pl.ds(0, tc)])

    @pl.when(fast)
    def _():
        fast_update(k_ref[...], v_ref[...], p_scr)

    @pl.when(jnp.logical_not(fast))
    def _():
        exact_update(k_ref[...], v_ref[...], p_scr)

    @pl.when(ki == nk - 1)
    def _():
        lv = lam_ref[...]
        e1 = jnp.exp(jnp.sum(lv[0:1, :] * lv[1:2, :], axis=1, keepdims=True))
        e2 = jnp.exp(jnp.sum(lv[2:3, :] * lv[3:4, :], axis=1, keepdims=True))
        lam = e1 - e2 + lam_init
        acc = acc_ref[...]
        o = acc[:, :DA_V_DIM] / acc[:, DA_V_DIM:]
        o = o[:tq] - lam * o[tq:]
        o_ref[...] = (_rms(o, w_ref[...]) * (1.0 - lam_init)).astype(o_ref.dtype)


def diff_attention(q, k, v, k_ctx, v_ctx, knorm, lam_vec, subln_w, lam_init, tq=1024, tk=2048):
    b, t, d = q.shape
    nh = d // DA_V_DIM
    tq = _pick_tile(t, tq)
    tk = _pick_tile(k.shape[1], tk)
    nk = k.shape[1] // tk
    with_ctx = k_ctx is not None
    assert not with_ctx or k_ctx.shape[1] <= tk
    qspec = pl.BlockSpec((None, tq, DA_V_DIM), lambda bi, h, i, j: (bi, i, h))
    kspec = pl.BlockSpec((None, tk, DA_V_DIM), lambda bi, h, i, j: (bi, j, h))
    in_specs = [qspec, kspec, kspec]
    args = [q, k, v]
    if with_ctx:
        cspec = pl.BlockSpec((None, k_ctx.shape[1], DA_V_DIM), lambda bi, h, i, j: (bi, 0, h))
        nspec = pl.BlockSpec((None, None) + knorm.shape[2:], lambda bi, h, i, j: (bi, h, 0, 0))
        in_specs += [cspec, cspec, nspec]
        args += [k_ctx, v_ctx, knorm]
    in_specs += [pl.BlockSpec((4, DA_HEAD_DIM), lambda bi, h, i, j: (0, 0)),
                 pl.BlockSpec((1, DA_V_DIM), lambda bi, h, i, j: (0, 0))]
    args += [lam_vec, subln_w.reshape(1, DA_V_DIM)]
    return pl.pallas_call(
        functools.partial(_attn_kernel, lam_init=lam_init, nk=nk, with_ctx=with_ctx),
        grid=(b, nh, t // tq, nk),
        in_specs=in_specs,
        out_specs=qspec,
        out_shape=jax.ShapeDtypeStruct((b, t, d), BF16),
        scratch_shapes=[pltpu.VMEM((2 * tq, DA_V_DIM), BF16),
                        pltpu.VMEM((2 * tq, 1), F32),
                        pltpu.VMEM((2 * tq, LANES), F32),
                        pltpu.VMEM((2 * tq, 2 * DA_V_DIM), F32),
                        pltpu.VMEM((2 * tq, tk), BF16),
                        pltpu.SMEM((1,), jnp.int32)],
        compiler_params=_cparams("parallel", "parallel", "parallel", "arbitrary"),
        name="diff_attention",
    )(*args)


def _merge_kernel(ya_ref, yb_ref, yc_ref, g_ref, wb_ref, wo_ref, x_ref, mod_ref, o_ref):
    d = x_ref.shape[1]
    ys = (ya_ref, yb_ref, yc_ref)
    s = None
    for k in range(N_BRANCH):
        t = jax.nn.sigmoid(g_ref[:, k * d:(k + 1) * d].astype(F32)) * _dot(ys[k][...].astype(BF16), wb_ref[k])
        s = t if s is None else s + t
    r = _dot(s.astype(BF16), wo_ref[...])
    o_ref[...] = x_ref[...] + mod_ref[2:3, :] * r


def merge(ya, yb, yc, gates, wb, wo, x, mods, tr=256):
    b, t, d = x.shape
    tr = _pick_tile(t, tr)
    per_batch = mods.shape[0] > 1
    blk = pl.BlockSpec((None, tr, d), lambda bi, i: (bi, i, 0))
    return pl.pallas_call(
        _merge_kernel,
        grid=(b, t // tr),
        in_specs=[blk, blk, blk,
                  pl.BlockSpec((None, tr, N_BRANCH * d), lambda bi, i: (bi, i, 0)),
                  pl.BlockSpec((N_BRANCH, d, d), lambda bi, i: (0, 0, 0)),
                  pl.BlockSpec((d, d), lambda bi, i: (0, 0)),
                  blk,
                  pl.BlockSpec((None, 8, d), (lambda bi, i: (bi, 0, 0)) if per_batch else (lambda bi, i: (0, 0, 0)))],
        out_specs=blk,
        out_shape=jax.ShapeDtypeStruct((b, t, d), F32),
        compiler_params=_cparams("parallel", "parallel"),
        name="merge",
    )(ya, yb, yc, gates, wb, wo, x, mods)


def _ffn_kernel(x_ref, g_ref, mod_ref, w13_ref, w2_ref, o_ref, *, nchunk):
    x = x_ref[...]
    h = (_rms(x, g_ref[...]) * (1.0 + mod_ref[4:5, :]) + mod_ref[3:4, :]).astype(BF16)
    f = w2_ref.shape[0]
    fc = f // nchunk
    acc = None
    for c in range(nchunk):
        a1 = _dot(h, w13_ref[:, c * fc:(c + 1) * fc])
        a3 = _dot(h, w13_ref[:, f + c * fc:f + (c + 1) * fc])
        t = _dot((_silu(a1) * a3).astype(BF16), w2_ref[c * fc:(c + 1) * fc, :])
        acc = t if acc is None else acc + t
    o_ref[...] = x + mod_ref[5:6, :] * acc


def ffn(x, g, mods, w13, w2, tr=256):
    b, t, d = x.shape
    f = w2.shape[0]
    tr = _pick_tile(t, tr)
    nchunk = 2 if (f // 2) % LANES == 0 else 1
    per_batch = mods.shape[0] > 1
    blk = pl.BlockSpec((None, tr, d), lambda bi, i: (bi, i, 0))
    return pl.pallas_call(
        functools.partial(_ffn_kernel, nchunk=nchunk),
        grid=(b, t // tr),
        in_specs=[blk,
                  pl.BlockSpec((1, d), lambda bi, i: (0, 0)),
                  pl.BlockSpec((None, 8, d), (lambda bi, i: (bi, 0, 0)) if per_batch else (lambda bi, i: (0, 0, 0))),
                  pl.BlockSpec((d, 2 * f), lambda bi, i: (0, 0)),
                  pl.BlockSpec((f, d), lambda bi, i: (0, 0))],
        out_specs=blk,
        out_shape=jax.ShapeDtypeStruct((b, t, d), F32),
        compiler_params=_cparams("parallel", "parallel"),
        name="ffn",
    )(x, g.reshape(1, d), mods, w13, w2)


def kernel(x, c, ctx, c_ctx, ada_w, ada_b, norm1_g, norm2_g, w_in, ssd_conv_w, ssd_conv_b,
           ssd_a_log, ssd_dt_bias, ssd_d, ssd_norm_w, hy_conv_w, hy_conv_b, hy_w1, hy_b1, hy_w2,
           hy_b2, hy_w3, hy_freq, hy_bias, da_lambda, da_subln_w, w_branch, w_out, ffn_w13,
           ffn_w2, final_g):
    bsz, length, d = x.shape
    ctx_len = ctx.shape[1]
    depth = w_in.shape[0]
    inner = SSD_HEADS * SSD_HEAD_DIM
    xbc_w = inner + 2 * SSD_GROUPS * SSD_STATE
    dt_w = 2 * SSD_HEADS
    hy_w = (HY_ORDER + 1) * d
    widths = (inner, xbc_w, dt_w, hy_w, d, d, d, N_BRANCH * d)
    offs = [0]
    for wd in widths:
        offs.append(offs[-1] + wd)

    cond = jnp.concatenate([c, c_ctx[None, :], jnp.zeros((8 - bsz - 1, d), F32)], axis=0)

    rope_tabs = rope_tables(length)
    tabs = fft_tables(2 * length // FFT_N2, FFT_N2)
    expand = (jnp.arange(LANES)[:, None] % SSD_HEADS
              == jnp.arange(inner)[None, :] // SSD_HEAD_DIM)
    expand_dir = [(expand & ((jnp.arange(LANES)[:, None] // SSD_HEADS) == dr)).astype(BF16) for dr in range(2)]
    qscale = DA_HEAD_DIM ** -0.5 * math.log2(math.e)

    x_l, x_c = x, ctx
    for layer in range(depth):
        last = layer == depth - 1
        lam_init = 0.8 - 0.6 * math.exp(-0.3 * layer)
        mod = matmul(cond, ada_w[layer].astype(BF16), ada_b[layer].reshape(1, -1), silu_in=True)
        mod = jnp.pad(mod.reshape(8, 6, d), ((0, 0), (0, 2), (0, 0)))
        mods_l, mods_c = mod[:bsz], mod[bsz:bsz + 1]

        w_l = w_in[layer].astype(BF16)
        w_parts = [w_l[:, offs[i]:offs[i + 1]] for i in range(len(widths))]
        w_parts[2] = jnp.pad(w_parts[2], ((0, 0), (0, LANES - dt_w)))
        a_neg = -jnp.exp(ssd_a_log[layer].astype(F32)).reshape(1, dt_w)
        aneg_row = jnp.pad(a_neg, ((0, 0), (0, LANES - dt_w)))
        bias_row = jnp.pad(ssd_dt_bias[layer].astype(F32).reshape(1, dt_w), ((0, 0), (0, LANES - dt_w)))
        d_row = jnp.repeat(ssd_d[layer].astype(F32), SSD_HEAD_DIM).reshape(1, inner)
        nw_row = ssd_norm_w[layer].astype(F32).reshape(1, inner)
        wb = w_branch[layer].astype(BF16)
        wo = w_out[layer].astype(BF16)
        w13 = ffn_w13[layer].astype(BF16)
        w2 = ffn_w2[layer].astype(BF16)

        def project(stream, mods):
            bb, tt, _ = stream.shape
            h = norm_mod(stream, norm1_g[layer], mods, 0).reshape(bb * tt, d)
            outs = [matmul(h, wp, out_dtype=(F32 if i == 2 else BF16)).reshape(bb, tt, -1)
                    for i, wp in enumerate(w_parts)]
            return outs

        z_c, xbc_c, dt_c, hy_c, q_c, k_c, v_c, g_c = project(x_c, mods_c)
        z_l, xbc_l, dt_l, hy_l, q_l, k_l, v_l, g_l = project(x_l, mods_l)

        xa_c = dwconv3(xbc_c, ssd_conv_w[layer], ssd_conv_b[layer], True)
        xa_l = dwconv3(xbc_l, ssd_conv_w[layer], ssd_conv_b[layer], True)
        h_zero = jnp.zeros((bsz, SSD_STATE, inner), F32)
        ys_c, ys_l = [], []
        for dr in range(2):
            y_c, st = ssd_scan(xa_c, dt_c, bias_row, aneg_row, expand_dir[dr], h_zero, dr)
            y_l, _ = ssd_scan(xa_l, dt_l, bias_row, aneg_row, expand_dir[dr], st, dr)
            ys_c.append(y_c)
            ys_l.append(y_l)
        ya_l = ssd_out(ys_l[0], ys_l[1], xa_l, z_l, d_row, nw_row)

        filt = (hy_w1[layer], hy_b1[layer], hy_w2[layer], hy_b2[layer], hy_w3[layer], hy_freq[layer])
        u_l = dwconv3(hy_l, hy_conv_w[layer], hy_conv_b[layer], False)
        yb_l = hyena_long(u_l, hyena_taps(length, *filt), hy_bias[layer], tabs)

        qr_l = rope(q_l, rope_tabs, qscale)
        kr_l, kn_l = rope(k_l, rope_tabs, 1.0, with_norms=True)
        kb_c = scale_cast(k_c, 1.0)
        yc_l = diff_attention(qr_l, kr_l, v_l, kb_c, v_c, kn_l, da_lambda[layer], da_subln_w[layer], lam_init)

        x_l_new = merge(ya_l, yb_l, yc_l, g_l, wb, wo, x_l, mods_l)
        x_l = ffn(x_l_new, norm2_g[layer], mods_l, w13, w2)

        if not last:
            ya_c = ssd_out(ys_c[0], ys_c[1], xa_c, z_c, d_row, nw_row)
            u_c = dwconv3(hy_c, hy_conv_w[layer], hy_conv_b[layer], False)
            yb_c = hyena_ctx(u_c, hyena_taps(ctx_len, *filt), hy_bias[layer])
            qb_c = scale_cast(q_c, qscale)
            yc_c = diff_attention(qb_c, kb_c, v_c, None, None, None, da_lambda[layer], da_subln_w[layer], lam_init)
            x_c_new = merge(ya_c, yb_c, yc_c, g_c, wb, wo, x_c, mods_c)
            x_c = ffn(x_c_new, norm2_g[layer], mods_c, w13, w2)
    return final_norm(x_l, final_g)
```

```python
import functools
import math

import jax
import jax.numpy as jnp
from jax import lax
from jax.experimental import pallas as pl
from jax.experimental.pallas import tpu as pltpu

F32 = jnp.float32
BF16 = jnp.bfloat16
EPS = 1e-6

SSD_HEADS = 16
SSD_HEAD_DIM = 64
SSD_GROUPS = 2
SSD_STATE = 128
SSD_CHUNK = 128
HY_ORDER = 2
HY_BANDS = 16
HY_SHORT_DECAY_PCT = 0.3
HY_LONG_DECAY_PCT = 1.5
HY_DECAY_TARGET = 1e-2
DA_HEADS = 8
DA_V_DIM = 128
DA_HEAD_DIM = 64
ROPE_AXIS_DIM = 32
ROPE_THETA = 10000.0
GRID_W = 64
N_BRANCH = 3

LANES = 128
SUBLANES = 8
FFT_N2 = 128
VMEM_LIMIT = 56 * 1024 * 1024


def _cparams(*sem):
    return pltpu.CompilerParams(dimension_semantics=sem, vmem_limit_bytes=VMEM_LIMIT)


def _dot(a, b):
    return jnp.dot(a, b, preferred_element_type=F32)


def _dot_nt(a, b):
    return lax.dot_general(a, b, (((1,), (1,)), ((), ())), preferred_element_type=F32)


def _split3(x):
    hi = x.astype(BF16)
    r1 = x - hi.astype(F32)
    mid = r1.astype(BF16)
    lo = (r1 - mid.astype(F32)).astype(BF16)
    return hi, mid, lo


def _silu(x):
    return x * jax.nn.sigmoid(x)


def _softplus(x):
    return jnp.maximum(x, 0.0) + jnp.log1p(jnp.exp(-jnp.abs(x)))


def _pick_tile(n, pref):
    t = min(pref, n)
    while n % t:
        t //= 2
    return t


def _matmul_kernel(a_ref, w_ref, b_ref, o_ref, *, silu_in):
    a = a_ref[...]
    if silu_in:
        a = _silu(a.astype(F32))
    r = _dot(a.astype(BF16), w_ref[...]) + b_ref[...]
    o_ref[...] = r.astype(o_ref.dtype)


def matmul(a, w, bias=None, out_dtype=F32, silu_in=False, tm=2048, tn=1024):
    m, k = a.shape
    n = w.shape[1]
    tm = _pick_tile(m, tm)
    tn = _pick_tile(n, tn)
    if bias is None:
        bias = jnp.zeros((1, n), F32)
    return pl.pallas_call(
        functools.partial(_matmul_kernel, silu_in=silu_in),
        grid=(m // tm, n // tn),
        in_specs=[pl.BlockSpec((tm, k), lambda i, j: (i, 0)),
                  pl.BlockSpec((k, tn), lambda i, j: (0, j)),
                  pl.BlockSpec((1, tn), lambda i, j: (0, j))],
        out_specs=pl.BlockSpec((tm, tn), lambda i, j: (i, j)),
        out_shape=jax.ShapeDtypeStruct((m, n), out_dtype),
        compiler_params=_cparams("parallel", "parallel"),
        name="matmul",
    )(a, w, bias)


def _rms(x, g):
    return x * lax.rsqrt(jnp.mean(x * x, axis=-1, keepdims=True) + EPS) * g


def _norm_mod_kernel(x_ref, g_ref, mod_ref, o_ref, *, row0):
    x = x_ref[...]
    y = _rms(x, g_ref[...])
    shift = mod_ref[row0:row0 + 1, :]
    scale = mod_ref[row0 + 1:row0 + 2, :]
    o_ref[...] = (y * (1.0 + scale) + shift).astype(o_ref.dtype)


def norm_mod(x, g, mods, row0, tr=512):
    b, t, d = x.shape
    tr = _pick_tile(t, tr)
    per_batch = mods.shape[0] > 1
    return pl.pallas_call(
        functools.partial(_norm_mod_kernel, row0=row0),
        grid=(b, t // tr),
        in_specs=[pl.BlockSpec((None, tr, d), lambda bi, i: (bi, i, 0)),
                  pl.BlockSpec((1, d), lambda bi, i: (0, 0)),
                  pl.BlockSpec((None, 8, d), (lambda bi, i: (bi, 0, 0)) if per_batch else (lambda bi, i: (0, 0, 0)))],
        out_specs=pl.BlockSpec((None, tr, d), lambda bi, i: (bi, i, 0)),
        out_shape=jax.ShapeDtypeStruct((b, t, d), BF16),
        compiler_params=_cparams("parallel", "parallel"),
        name="norm_mod",
    )(x, g.reshape(1, d), mods)


def _final_norm_kernel(x_ref, g_ref, o_ref):
    o_ref[...] = _rms(x_ref[...], g_ref[...])


def final_norm(x, g, tr=512):
    b, t, d = x.shape
    tr = _pick_tile(t, tr)
    return pl.pallas_call(
        _final_norm_kernel,
        grid=(b, t // tr),
        in_specs=[pl.BlockSpec((None, tr, d), lambda bi, i: (bi, i, 0)),
                  pl.BlockSpec((1, d), lambda bi, i: (0, 0))],
        out_specs=pl.BlockSpec((None, tr, d), lambda bi, i: (bi, i, 0)),
        out_shape=jax.ShapeDtypeStruct((b, t, d), F32),
        compiler_params=_cparams("parallel", "parallel"),
        name="final_norm",
    )(x, g.reshape(1, d))


def _dwconv_kernel(x_ref, xp_ref, xn_ref, w_ref, b_ref, o_ref, *, act, nblk):
    i = pl.program_id(1)
    x = x_ref[...].astype(F32)
    tr = x.shape[0]
    halo = xp_ref.shape[0]
    prev_row = jnp.where(i == 0, 0.0, xp_ref[halo - 1:halo, :].astype(F32))
    next_row = jnp.where(i == nblk - 1, 0.0, xn_ref[0:1, :].astype(F32))
    rows = lax.broadcasted_iota(jnp.int32, x.shape, 0)
    x_m1 = jnp.where(rows == 0, prev_row, pltpu.roll(x, 1, 0))
    x_p1 = jnp.where(rows == tr - 1, next_row, pltpu.roll(x, tr - 1, 0))
    y = b_ref[...] + x_m1 * w_ref[0:1, :] + x * w_ref[1:2, :] + x_p1 * w_ref[2:3, :]
    if act:
        y = _silu(y)
    o_ref[...] = y


def dwconv3(x, w, bias, act, tr=512, tc=1536):
    b, t, c = x.shape
    tr = _pick_tile(t, tr)
    tc = min(tc, c)
    assert c % tc == 0
    nblk = t // tr
    halo = SUBLANES * (4 // x.dtype.itemsize)
    rh = tr // halo
    lasth = t // halo - 1
    return pl.pallas_call(
        functools.partial(_dwconv_kernel, act=act, nblk=nblk),
        grid=(b, nblk, c // tc),
        in_specs=[pl.BlockSpec((None, tr, tc), lambda bi, i, j: (bi, i, j)),
                  pl.BlockSpec((None, halo, tc), lambda bi, i, j: (bi, jnp.maximum(i * rh - 1, 0), j)),
                  pl.BlockSpec((None, halo, tc), lambda bi, i, j: (bi, jnp.minimum((i + 1) * rh, lasth), j)),
                  pl.BlockSpec((3, tc), lambda bi, i, j: (0, j)),
                  pl.BlockSpec((1, tc), lambda bi, i, j: (0, j))],
        out_specs=pl.BlockSpec((None, tr, tc), lambda bi, i, j: (bi, i, j)),
        out_shape=jax.ShapeDtypeStruct((b, t, c), F32),
        compiler_params=_cparams("parallel", "parallel", "parallel"),
        name="dwconv3",
    )(x, x, x, w, bias.reshape(1, c))


SSD_STEP_CHUNKS = 4


def _ssd_kernel(x_ref, b_ref, c_ref, dt_ref, bias_ref, aneg_ref, e_ref, h0_ref,
                y_ref, hout_ref, st_ref, *, direction, nsteps):
    ci = pl.program_id(1)

    @pl.when(ci == 0)
    def _():
        st_ref[...] = h0_ref[...]

    q = SSD_CHUNK
    hd = SSD_HEAD_DIM
    gw = (SSD_HEADS // SSD_GROUPS) * hd
    row = lax.broadcasted_iota(jnp.int32, (q, q), 0)
    col = lax.broadcasted_iota(jnp.int32, (q, q), 1)
    mask = (col >= row) if direction else (col <= row)
    tri = mask.astype(BF16)
    lane = lax.broadcasted_iota(jnp.int32, (q, LANES), 1)
    first = lane < hd

    def state_free(rows):
        x = x_ref[rows, :]
        dtv = _softplus(dt_ref[rows, :] + bias_ref[...])
        adt = dtv * aneg_ref[...]
        acum = sum(_dot(tri, p) for p in _split3(adt))
        acum_t = sum(_dot_nt(p, tri) for p in _split3(adt.T))
        total = jnp.sum(adt, axis=0, keepdims=True)
        eac = jnp.exp(acum)
        dte = jnp.exp(total - acum)
        cd = jnp.broadcast_to(jnp.exp(total), (SUBLANES, LANES))
        stack = jnp.concatenate([dtv, eac, dte, cd], axis=0)
        ex = _dot(stack.astype(BF16), e_ref[...])
        dt_e, eac_e, dte_e, cd_e = ex[:q], ex[q:2 * q], ex[2 * q:3 * q], ex[3 * q:3 * q + 1]
        xdt = x * dt_e
        y_diag, c_gs, b_ts = [], [], []
        for g in range(SSD_GROUPS):
            c_g = c_ref[rows, g * SSD_STATE:(g + 1) * SSD_STATE].astype(BF16)
            b_g32 = b_ref[rows, g * SSD_STATE:(g + 1) * SSD_STATE]
            cb = _dot_nt(c_g, b_g32.astype(BF16))
            for jp in range(gw // LANES):
                h_a = g * (SSD_HEADS // SSD_GROUPS) + 2 * jp
                xpair = xdt[:, h_a * hd:h_a * hd + LANES]
                acc = None
                for k in range(2):
                    cix = direction * SSD_HEADS + h_a + k
                    seg = acum[:, cix:cix + 1] - acum_t[cix:cix + 1, :]
                    dec = jnp.where(mask, jnp.exp(jnp.where(mask, seg, 0.0)), 0.0)
                    w = (cb * dec).astype(BF16)
                    xk = jnp.where(first if k == 0 else jnp.logical_not(first), xpair, 0.0).astype(BF16)
                    d = _dot(w, xk)
                    acc = d if acc is None else acc + d
                y_diag.append(acc)
            c_gs.append(c_g)
            b_ts.append(b_g32.T.astype(BF16))
        xdte = (xdt * dte_e).astype(BF16)
        return y_diag, c_gs, b_ts, eac_e, cd_e, xdte

    nsub = x_ref.shape[0] // q
    order = list(range(nsub - 1, -1, -1) if direction else range(nsub))
    pre = {s: state_free(slice(s * q, (s + 1) * q)) for s in order}
    for s in order:
        y_diag, c_gs, b_ts, eac_e, cd_e, xdte = pre[s]
        for g in range(SSD_GROUPS):
            gs = slice(g * gw, (g + 1) * gw)
            s_g = st_ref[:, gs]
            y_off = _dot(c_gs[g], s_g.astype(BF16)) * eac_e[:, gs]
            for jp in range(gw // LANES):
                ls = slice(g * gw + jp * LANES, g * gw + (jp + 1) * LANES)
                y_ref[s * q:(s + 1) * q, ls] = y_diag[g * (gw // LANES) + jp] + y_off[:, jp * LANES:(jp + 1) * LANES]
            st_ref[:, gs] = s_g * cd_e[:, gs] + _dot(b_ts[g], xdte[:, gs])

    @pl.when(ci == nsteps - 1)
    def _():
        hout_ref[...] = st_ref[...]


def ssd_scan(xbc, dt_raw, dt_bias_row, aneg_row, expand, h0, direction):
    b, t, _ = xbc.shape
    q = SSD_CHUNK * min(SSD_STEP_CHUNKS, t // SSD_CHUNK)
    nsteps = t // q
    inner = SSD_HEADS * SSD_HEAD_DIM
    bn = SSD_GROUPS * SSD_STATE
    cmap = (lambda c: nsteps - 1 - c) if direction else (lambda c: c)
    y, hout = pl.pallas_call(
        functools.partial(_ssd_kernel, direction=direction, nsteps=nsteps),
        grid=(b, nsteps),
        in_specs=[pl.BlockSpec((None, q, inner), lambda bi, c: (bi, cmap(c), 0)),
                  pl.BlockSpec((None, q, bn), lambda bi, c: (bi, cmap(c), inner // bn)),
                  pl.BlockSpec((None, q, bn), lambda bi, c: (bi, cmap(c), inner // bn + 1)),
                  pl.BlockSpec((None, q, LANES), lambda bi, c: (bi, cmap(c), 0)),
                  pl.BlockSpec((1, LANES), lambda bi, c: (0, 0)),
                  pl.BlockSpec((1, LANES), lambda bi, c: (0, 0)),
                  pl.BlockSpec((LANES, inner), lambda bi, c: (0, 0)),
                  pl.BlockSpec((None, SSD_STATE, inner), lambda bi, c: (bi, 0, 0))],
        out_specs=[pl.BlockSpec((None, q, inner), lambda bi, c: (bi, cmap(c), 0)),
                   pl.BlockSpec((None, SSD_STATE, inner), lambda bi, c: (bi, 0, 0))],
        out_shape=[jax.ShapeDtypeStruct((b, t, inner), F32),
                   jax.ShapeDtypeStruct((b, SSD_STATE, inner), F32)],
        scratch_shapes=[pltpu.VMEM((SSD_STATE, inner), F32)],
        compiler_params=_cparams("parallel", "arbitrary"),
        name="ssd_scan_bwd" if direction else "ssd_scan_fwd",
    )(xbc, xbc, xbc, dt_raw, dt_bias_row, aneg_row, expand, h0)
    return y, hout


def _ssd_out_kernel(yf_ref, yb_ref, x_ref, z_ref, d_ref, w_ref, o_ref):
    y = yf_ref[...] + yb_ref[...] + d_ref[...] * x_ref[...]
    y = y * _silu(z_ref[...].astype(F32))
    gw = y.shape[1] // SSD_GROUPS
    for g in range(SSD_GROUPS):
        gs = slice(g * gw, (g + 1) * gw)
        yg = y[:, gs]
        yg = yg * lax.rsqrt(jnp.mean(yg * yg, axis=-1, keepdims=True) + EPS)
        o_ref[:, gs] = (yg * w_ref[:, gs]).astype(o_ref.dtype)


def ssd_out(yf, yb, xbc, z, d_row, w_row, tr=256):
    b, t, inner = yf.shape
    tr = _pick_tile(t, tr)
    blk = pl.BlockSpec((None, tr, inner), lambda bi, i: (bi, i, 0))
    vec = pl.BlockSpec((1, inner), lambda bi, i: (0, 0))
    return pl.pallas_call(
        _ssd_out_kernel,
        grid=(b, t // tr),
        in_specs=[blk, blk, blk, blk, vec, vec],
        out_specs=blk,
        out_shape=jax.ShapeDtypeStruct((b, t, inner), BF16),
        compiler_params=_cparams("parallel", "parallel"),
        name="ssd_out",
    )(yf, yb, xbc, z, d_row, w_row)


def _taps_kernel(f_ref, tv_ref, w1_ref, b1_ref, w2_ref, b2_ref, w3_ref, fr_ref, dl_ref, o_ref):
    hp = lax.Precision.HIGHEST
    fr = fr_ref[...]
    h = jnp.sin(fr * (jnp.dot(f_ref[...], w1_ref[...], precision=hp, preferred_element_type=F32) + b1_ref[...]))
    h = jnp.sin(fr * (jnp.dot(h, w2_ref[...], precision=hp, preferred_element_type=F32) + b2_ref[...]))
    filt = _dot(h.astype(BF16), w3_ref[...].astype(BF16))
    c = dl_ref.shape[1]
    win = jnp.exp(-tv_ref[:, 0:1] * dl_ref[...]) * tv_ref[:, 1:2]
    o_ref[0] = filt[:, :c] * win
    o_ref[1] = filt[:, c:] * win


def hyena_taps(length, w1, b1, w2, b2, w3, freq):
    hid = w1.shape[1]
    c = w3.shape[1] // (2 * HY_ORDER)
    n = 2 * length
    t = jnp.linspace(0.0, 1.0, length, dtype=F32)[:, None]
    phase = 2.0 * math.pi * jnp.arange(length, dtype=F32)[:, None] / length
    bands = jnp.linspace(1e-4, HY_BANDS - 1, HY_BANDS, dtype=F32)[None, :]
    feats = jnp.concatenate([t, jnp.cos(phase * bands), -jnp.sin(phase * bands)], axis=-1)
    nf = feats.shape[1]
    rev = jnp.concatenate([jnp.zeros((1,), jnp.int32), jnp.arange(length - 1, 0, -1, dtype=jnp.int32)])
    feats2 = jnp.concatenate([feats, feats[rev]], axis=0)
    feats2 = jnp.pad(feats2, ((0, 0), (0, LANES - nf)))
    valid = jnp.ones((n,), F32).at[length].set(0.0)
    tv = jnp.stack([jnp.concatenate([t[:, 0], t[rev, 0]]), valid], axis=1)
    max_decay = math.log(HY_DECAY_TARGET) / HY_SHORT_DECAY_PCT
    min_decay = math.log(HY_DECAY_TARGET) / HY_LONG_DECAY_PCT
    deltas = jnp.abs(jnp.linspace(min_decay, max_decay, c, dtype=F32))[None, :]
    w1p = jnp.pad(w1, ((0, LANES - nf), (0, 0)))
    w3r = w3.reshape(hid, HY_ORDER, 2, c).transpose(2, 0, 1, 3).reshape(2, hid, HY_ORDER * c)
    tr = _pick_tile(length, 1024)
    nhalf = length // tr
    return pl.pallas_call(
        _taps_kernel,
        grid=(n // tr,),
        in_specs=[pl.BlockSpec((tr, LANES), lambda i: (i, 0)),
                  pl.BlockSpec((tr, 2), lambda i: (i, 0)),
                  pl.BlockSpec((LANES, hid), lambda i: (0, 0)),
                  pl.BlockSpec((1, hid), lambda i: (0, 0)),
                  pl.BlockSpec((hid, hid), lambda i: (0, 0)),
                  pl.BlockSpec((1, hid), lambda i: (0, 0)),
                  pl.BlockSpec((None, hid, HY_ORDER * c), lambda i: (i // nhalf, 0, 0)),
                  pl.BlockSpec((1, hid), lambda i: (0, 0)),
                  pl.BlockSpec((1, c), lambda i: (0, 0))],
        out_specs=pl.BlockSpec((HY_ORDER, tr, c), lambda i: (0, i, 0)),
        out_shape=jax.ShapeDtypeStruct((HY_ORDER, n, c), F32),
        compiler_params=_cparams("parallel"),
        name="hyena_taps",
    )(feats2, tv, w1p, b1.reshape(1, hid), w2, b2.reshape(1, hid), w3r, freq.reshape(1, hid), deltas)


def _cs(num, den):
    ang = (2.0 * math.pi / den) * num.astype(F32)
    return jnp.cos(ang), jnp.sin(ang)


def fft_tables(n1, n2):
    n = n1 * n2
    h = n1 // 2
    k1 = jnp.arange(n1, dtype=jnp.int32)
    c1, s1 = _cs((k1[:, None] * k1[None, :]) % n1, n1)
    g_u = jnp.block([[c1[:, :h], s1[:, :h]], [-s1[:, :h], c1[:, :h]]])
    g_t = jnp.concatenate([c1, -s1], axis=0)
    g_i = jnp.block([[c1[:h, :], -s1[:h, :]], [s1[:h, :], c1[:h, :]]])
    s = jnp.arange(n2, dtype=jnp.int32)
    num = (n1 * (s[:, None] * s[None, :]))[None] + (k1[:, None, None] * s[None, None, :])
    cb, sb = _cs(num % n, n)
    m_f = jnp.concatenate([jnp.concatenate([cb, sb], axis=2),
                           jnp.concatenate([-sb, cb], axis=2)], axis=1)
    cbt, sbt = jnp.swapaxes(cb, 1, 2) / n, jnp.swapaxes(sb, 1, 2) / n
    m_i = jnp.concatenate([jnp.concatenate([cbt, -sbt], axis=2),
                           jnp.concatenate([sbt, cbt], axis=2)], axis=1)
    return dict(g_u=g_u.astype(BF16), g_t=g_t.astype(BF16), g_i=g_i.astype(BF16),
                m_f=m_f.astype(BF16), m_i=m_i.astype(BF16))


FFT_KGROUP = SUBLANES


def _rows_get(ref, start, n, stride):
    return ref.reshape(math.prod(ref.shape[:-1]), ref.shape[-1])[pl.ds(start, n, stride=stride), :]


def _rows_set(ref, start, n, stride, val):
    ref.reshape(math.prod(ref.shape[:-1]), ref.shape[-1])[pl.ds(start, n, stride=stride), :] = val


def _cpack(re, im):
    rb = lax.bitcast_convert_type(re.astype(BF16).astype(F32), jnp.uint32)
    ib = lax.bitcast_convert_type(im.astype(BF16).astype(F32), jnp.uint32)
    return (rb & jnp.uint32(0xFFFF0000)) | (ib >> 16)


def _cunpack(w):
    re = lax.bitcast_convert_type(w & jnp.uint32(0xFFFF0000), F32)
    im = lax.bitcast_convert_type(w << 16, F32)
    return re, im


FFT_LANE_BLOCKS = 2


def _fft_a_kernel(g_ref, *refs, s_major_in):
    *x_refs, o_ref = refs
    xa_refs, xb_refs = x_refs[:len(x_refs) // 2], x_refs[len(x_refs) // 2:]
    h = g_ref.shape[1] // 2
    ts, n1, _ = o_ref.shape
    g = g_ref[...]

    def rows(x_refs_, s):
        if s_major_in:
            return x_refs_[0][s]
        return jnp.concatenate([_rows_get(r, s, h, ts) for r in x_refs_], axis=1)

    for s in range(ts):
        x = jnp.concatenate([rows(xa_refs, s), rows(xb_refs, s)], axis=0).astype(BF16)
        r = _dot(g, x)
        o_ref[s] = _cpack(r[:n1], r[n1:])


def fft_stage_a(g, x4, sel_a, sel_b, col_off, c, s_major_in=False):
    n1 = g.shape[0] // 2
    h = n1 // 2
    n2 = x4.shape[1] if s_major_in else x4.shape[2]
    ts = SUBLANES
    nl = FFT_LANE_BLOCKS
    tc = nl * LANES
    if s_major_in:
        coff = col_off // tc
        specs = lambda sel: [pl.BlockSpec((None, ts, h, tc), lambda j, s: (sel[0], s, sel[1], j + coff))]
    else:
        coff = col_off // LANES
        specs = lambda sel: [pl.BlockSpec((None, h, ts, LANES),
                                          functools.partial(lambda j, s, l: (sel[0], sel[1], s, nl * j + l + coff), l=l))
                             for l in range(nl)]
    x_specs = specs(sel_a) + specs(sel_b)
    return pl.pallas_call(
        functools.partial(_fft_a_kernel, s_major_in=s_major_in),
        grid=(c // tc, n2 // ts),
        in_specs=[pl.BlockSpec((2 * n1, n1), lambda j, s: (0, 0))] + x_specs,
        out_specs=pl.BlockSpec((ts, n1, tc), lambda j, s: (s, 0, j)),
        out_shape=jax.ShapeDtypeStruct((n2, n1, c), jnp.uint32),
        compiler_params=_cparams("parallel", "parallel"),
        name="fft_stage_a",
    )(g, *([x4] * len(x_specs)))


def _fft_load_k(a_refs, j):
    n2, kb, _ = a_refs[0].shape
    re, im = _cunpack(jnp.concatenate([_rows_get(r, j, n2, kb) for r in a_refs], axis=1))
    return jnp.concatenate([re, im], axis=0).astype(BF16)


def _fft_bh_kernel(m_ref, *refs, groups):
    *a_refs, o_ref = refs
    n2, kb, _ = a_refs[0].shape
    nl = len(a_refs) // groups
    for g in range(groups):
        for j in range(kb):
            x = _dot(m_ref[g * kb + j], _fft_load_k(a_refs[g * nl:(g + 1) * nl], j))
            o_ref[g * kb + j] = _cpack(x[:n2], x[n2:])


def _fft_b_kernel(mf_ref, mi_ref, *refs, groups):
    *a_refs, h_ref, o_ref = refs
    n2, kb, _ = a_refs[0].shape
    nl = len(a_refs) // groups
    for g in range(groups):
        for j in range(kb):
            k = g * kb + j
            x = _dot(mf_ref[k], _fft_load_k(a_refs[g * nl:(g + 1) * nl], j))
            xr, xi = x[:n2], x[n2:]
            hr, hi = _cunpack(h_ref[k])
            y = jnp.concatenate([xr * hr - xi * hi, xr * hi + xi * hr], axis=0).astype(BF16)
            r = _dot(mi_ref[k], y)
            o_ref[k] = _cpack(r[:n2], r[n2:])


def fft_stage_b(m_f, m_i, a, hspec, groups=2):
    n2, n1, c = a.shape
    groups = min(groups, n1 // FFT_KGROUP)
    kb = FFT_KGROUP * groups
    nl = FFT_LANE_BLOCKS
    tc = nl * LANES
    ablks = [pl.BlockSpec((n2, FFT_KGROUP, LANES),
                          functools.partial(lambda k, j, g, l: (0, groups * k + g, nl * j + l), g=g, l=l))
             for g in range(groups) for l in range(nl)]
    kblk = pl.BlockSpec((kb, n2, tc), lambda k, j: (k, 0, j))
    mat = pl.BlockSpec((kb, 2 * n2, 2 * n2), lambda k, j: (k, 0, 0))
    if hspec is None:
        kern, in_specs, name = _fft_bh_kernel, [mat] + ablks, "fft_stage_b_spectrum"
        args = (m_f,) + (a,) * len(ablks)
    else:
        kern, in_specs, name = _fft_b_kernel, [mat, mat] + ablks + [kblk], "fft_stage_b"
        args = (m_f, m_i) + (a,) * len(ablks) + (hspec,)
    return pl.pallas_call(
        functools.partial(kern, groups=groups),
        grid=(n1 // kb, c // tc),
        in_specs=in_specs,
        out_specs=kblk,
        out_shape=jax.ShapeDtypeStruct((n1, n2, c), jnp.uint32),
        compiler_params=_cparams("parallel", "parallel"),
        name=name,
    )(*args)


def _fft_ai_kernel(g_ref, *refs, nl, u_s_major, out_s_major):
    nu = 1 if u_s_major else nl
    b_refs, u_refs, x_refs = refs[:nl], refs[nl:nl + nu], refs[nl + nu:2 * nl + nu]
    skip_ref, o_ref = refs[-2:]
    n1, ts, _ = b_refs[0].shape
    h = n1 // 2
    g = g_ref[...]
    skip = skip_ref[...]
    cat = lambda parts: parts[0] if len(parts) == 1 else jnp.concatenate(parts, axis=1)
    for s in range(ts):
        br, bi_ = _cunpack(cat([_rows_get(r, s, n1, ts) for r in b_refs]))
        y = _dot(g, jnp.concatenate([br, bi_], axis=0).astype(BF16))
        for bi in range(2):
            start = bi * h * ts + s
            u = u_refs[0][bi, s] if u_s_major else cat([_rows_get(r, start, h, ts) for r in u_refs])
            x = cat([_rows_get(r, start, h, ts) for r in x_refs])
            val = x * (y[bi * h:(bi + 1) * h] + u * skip)
            if out_s_major:
                o_ref[bi, s] = val
            else:
                _rows_set(o_ref, start, h, ts, val)


def fft_stage_a_inv(g_i, bm, u4, u_off, u_s_major, x4, x_off, skip_row, out_s_major):
    n1, n2, c = bm.shape
    h = n1 // 2
    ts = SUBLANES
    nl = FFT_LANE_BLOCKS if out_s_major else 1
    tc = nl * LANES
    lane_blocks = lambda shape, imap: [pl.BlockSpec(shape, functools.partial(imap, l=l)) for l in range(nl)]
    nat = lambda off: lane_blocks((2, h, ts, LANES), lambda j, s, l: (0, 0, s, nl * j + l + off // LANES))
    smj = lambda off: pl.BlockSpec((2, ts, h, tc), lambda j, s: (0, s, 0, j + off // tc))
    in_specs = ([pl.BlockSpec((n1, 2 * n1), lambda j, s: (0, 0))]
                + lane_blocks((n1, ts, LANES), lambda j, s, l: (0, s, nl * j + l))
                + ([smj(u_off)] if u_s_major else nat(u_off))
                + nat(x_off)
                + [pl.BlockSpec((1, tc), lambda j, s: (0, j))])
    args = [g_i] + [bm] * nl + [u4] * (1 if u_s_major else nl) + [x4] * nl + [skip_row]
    return pl.pallas_call(
        functools.partial(_fft_ai_kernel, nl=nl, u_s_major=u_s_major, out_s_major=out_s_major),
        grid=(c // tc, n2 // ts),
        in_specs=in_specs,
        out_specs=smj(0) if out_s_major else nat(0)[0],
        out_shape=jax.ShapeDtypeStruct((2, n2, h, c) if out_s_major else (2, h, n2, c), F32),
        compiler_params=_cparams("parallel", "parallel"),
        name="fft_stage_a_inv",
    )(*args)


def hyena_long(u, taps, skip, tabs):
    b, length, c3 = u.shape
    assert b == 2, "the two batch rows are packed as one complex sequence"
    c = c3 // 3
    n2 = FFT_N2
    n1 = 2 * length // n2
    h = n1 // 2
    u4 = u.reshape(b, h, n2, c3)
    taps4 = taps.reshape(HY_ORDER * 2, h, n2, c)
    zin, zoff, z_s_major = u4, 0, False
    out = None
    for conv in range(HY_ORDER):
        last = conv == HY_ORDER - 1
        hspec = fft_stage_b(tabs["m_f"], None,
                            fft_stage_a(tabs["g_t"], taps4, (2 * conv, 0), (2 * conv + 1, 0), 0, c), None)
        a = fft_stage_a(tabs["g_u"], zin, (0, 0), (1, 0), zoff, c, s_major_in=z_s_major)
        bm = fft_stage_b(tabs["m_f"], tabs["m_i"], a, hspec)
        out = fft_stage_a_inv(tabs["g_i"], bm, zin, zoff, z_s_major, u4, (conv + 1) * c,
                              skip[conv].reshape(1, c), not last)
        zin, zoff, z_s_major = out, 0, True
    return out.reshape(b, length, c)


def _hyena_ctx_kernel(u_ref, x1_ref, x2_ref, taps_ref, fc_ref, fs_ref, skip_ref, o_ref):
    length = u_ref.shape[1]
    n = 2 * length
    fc = fc_ref[...]
    fs = fs_ref[...]
    fcl, fsl = fc[:, :length], fs[:, :length]
    fct, fst = fc[:length, :], fs[:length, :]
    zr = u_ref[0]
    zi = u_ref[1]
    gates = (x1_ref, x2_ref)
    for conv in range(HY_ORDER):
        tp = taps_ref[conv].astype(BF16)
        hr, hi = _dot(fc, tp), -_dot(fs, tp)
        zrb, zib = zr.astype(BF16), zi.astype(BF16)
        xr = _dot(fcl, zrb) + _dot(fsl, zib)
        xi = _dot(fcl, zib) - _dot(fsl, zrb)
        yr = (xr * hr - xi * hi).astype(BF16)
        yi = (xr * hi + xi * hr).astype(BF16)
        cr = (_dot(fct, yr) - _dot(fst, yi)) * (1.0 / n)
        ci = (_dot(fct, yi) + _dot(fst, yr)) * (1.0 / n)
        sk = skip_ref[conv:conv + 1, :]
        zr = gates[conv][0] * (cr + zr * sk)
        zi = gates[conv][1] * (ci + zi * sk)
    o_ref[0] = zr.astype(o_ref.dtype)
    o_ref[1] = zi.astype(o_ref.dtype)


def hyena_ctx(u, taps, skip, tc=256):
    b, length, c3 = u.shape
    assert b == 2
    c = c3 // 3
    n = 2 * length
    k = jnp.arange(n, dtype=jnp.int32)
    fc, fs = _cs((k[:, None] * k[None, :]) % n, n)
    nb = c // tc
    ub = lambda off: pl.BlockSpec((2, length, tc), lambda j: (0, 0, j + off * nb))
    mat = pl.BlockSpec((n, n), lambda j: (0, 0))
    return pl.pallas_call(
        _hyena_ctx_kernel,
        grid=(nb,),
        in_specs=[ub(0), ub(1), ub(2),
                  pl.BlockSpec((HY_ORDER, n, tc), lambda j: (0, 0, j)),
                  mat, mat,
                  pl.BlockSpec((HY_ORDER, tc), lambda j: (0, j))],
        out_specs=pl.BlockSpec((2, length, tc), lambda j: (0, 0, j)),
        out_shape=jax.ShapeDtypeStruct((b, length, c), BF16),
        compiler_params=_cparams("parallel"),
        name="hyena_ctx",
    )(u, u, u, taps, fc.astype(BF16), fs.astype(BF16), skip)


def rope_tables(length):
    rows = length // GRID_W
    row = jnp.repeat(jnp.arange(rows, dtype=F32), GRID_W)
    col = (jnp.arange(length) % GRID_W).astype(F32)
    inv = ROPE_THETA ** (-jnp.arange(0, ROPE_AXIS_DIM, 2, dtype=F32) / ROPE_AXIS_DIM)
    ang = jnp.stack([row[:, None] * inv, col[:, None] * inv], axis=1)
    cos, sin = jnp.cos(ang), jnp.sin(ang)
    zero = jnp.zeros_like(sin)
    per_map = lambda lo, hi: jnp.concatenate([lo, hi], axis=-1).reshape(length, 2 * ROPE_AXIS_DIM)
    reps = LANES // (2 * ROPE_AXIS_DIM)
    c_t = jnp.tile(per_map(cos, cos), (1, reps))
    sm_t = jnp.tile(per_map(-sin, zero), (1, reps))
    sp_t = jnp.tile(per_map(zero, sin), (1, reps))
    return c_t, sm_t, sp_t


def _rope_kernel(x_ref, c_ref, sm_ref, sp_ref, o_ref, *n_refs, scale):
    half = ROPE_AXIS_DIM // 2
    c, sm, sp = c_ref[...], sm_ref[...], sp_ref[...]
    for g in range(x_ref.shape[1] // LANES):
        ls = slice(g * LANES, (g + 1) * LANES)
        x = x_ref[:, ls].astype(F32)
        y = (x * c + pltpu.roll(x, LANES - half, 1) * sm + pltpu.roll(x, half, 1) * sp) * scale
        o_ref[:, ls] = y.astype(o_ref.dtype)
        if n_refs:
            n2 = jnp.max(jnp.sum(y * y, axis=1, keepdims=True), axis=0, keepdims=True)
            n_refs[0][g:g + 1, :] = jnp.broadcast_to(n2, (1, LANES))


def rope(x, tables, scale, with_norms=False, tr=512):
    b, t, d = x.shape
    tr = _pick_tile(t, tr)
    nh = d // LANES
    tab = pl.BlockSpec((tr, LANES), lambda bi, i: (i, 0))
    out_specs = [pl.BlockSpec((None, tr, d), lambda bi, i: (bi, i, 0))]
    out_shape = [jax.ShapeDtypeStruct((b, t, d), BF16)]
    if with_norms:
        out_specs.append(pl.BlockSpec((None, None, nh, LANES), lambda bi, i: (bi, i, 0, 0)))
        out_shape.append(jax.ShapeDtypeStruct((b, t // tr, nh, LANES), F32))
    outs = pl.pallas_call(
        functools.partial(_rope_kernel, scale=scale),
        grid=(b, t // tr),
        in_specs=[pl.BlockSpec((None, tr, d), lambda bi, i: (bi, i, 0)), tab, tab, tab],
        out_specs=out_specs,
        out_shape=out_shape,
        compiler_params=_cparams("parallel", "parallel"),
        name="rope",
    )(x, *tables)
    if with_norms:
        return outs[0], jnp.swapaxes(outs[1], 1, 2)
    return outs[0]


def _scale_cast_kernel(x_ref, o_ref, *, scale):
    o_ref[...] = (x_ref[...].astype(F32) * scale).astype(o_ref.dtype)


def scale_cast(x, scale, tr=256):
    b, t, d = x.shape
    tr = _pick_tile(t, tr)
    blk = pl.BlockSpec((None, tr, d), lambda bi, i: (bi, i, 0))
    return pl.pallas_call(
        functools.partial(_scale_cast_kernel, scale=scale),
        grid=(b, t // tr), in_specs=[blk], out_specs=blk,
        out_shape=jax.ShapeDtypeStruct((b, t, d), BF16),
        compiler_params=_cparams("parallel", "parallel"),
        name="scale_cast",
    )(x)


ATTN_GUARD_LOG2 = 80.0
ATTN_NORM_SLACK = 1.03


def _attn_kernel(*refs, lam_init, nk, with_ctx):
    if with_ctx:
        (q_ref, k_ref, v_ref, kc_ref, vc_ref, kn_ref, lam_ref, w_ref, o_ref,
         q2_ref, m_ref, c_ref, acc_ref, p_scr, flag_ref) = refs
    else:
        q_ref, k_ref, v_ref, lam_ref, w_ref, o_ref, q2_ref, m_ref, c_ref, acc_ref, p_scr, flag_ref = refs
    ki = pl.program_id(3)
    tq = q_ref.shape[0]

    def with_ones(v):
        return jnp.concatenate([v, jnp.ones_like(v)], axis=1)

    def exact_update(k, v, p_buf):
        s = _dot_nt(q2_ref[...], k)
        m_prev = m_ref[...]
        m_new = jnp.maximum(m_prev, jnp.max(s, axis=1, keepdims=True))
        alpha = jnp.exp2(m_prev - m_new)
        p_buf[...] = jnp.exp2(s - m_new).astype(BF16)
        acc_ref[...] = alpha * acc_ref[...] + _dot(p_buf[...], with_ones(v))
        m_ref[...] = m_new

    def fast_update(k, v, p_buf):
        s = _dot_nt(q2_ref[...], k)
        p_buf[...] = jnp.exp2(s - jnp.tile(c_ref[...], (1, k.shape[0] // LANES))).astype(BF16)
        acc_ref[...] += _dot(p_buf[...], with_ones(v))

    @pl.when(ki == 0)
    def _():
        q = q_ref[...]
        lane = lax.broadcasted_iota(jnp.int32, q.shape, 1)
        zero = jnp.zeros_like(q)
        q2_ref[:tq, :] = jnp.where(lane < DA_HEAD_DIM, q, zero)
        q2_ref[tq:, :] = jnp.where(lane >= DA_HEAD_DIM, q, zero)
        m_ref[...] = jnp.full(m_ref.shape, -jnp.inf, F32)
        acc_ref[...] = jnp.zeros(acc_ref.shape, F32)
        flag_ref[0] = 0
        if with_ctx:
            q2 = q2_ref[...]
            sq = (q2.astype(F32) * q2.astype(F32)).astype(BF16)
            k0 = jnp.broadcast_to(kc_ref[0:1, :], (LANES, DA_V_DIM))
            zeros = jnp.zeros((LANES, DA_V_DIM), BF16)
            wt = jnp.concatenate([jnp.concatenate([k0, zeros], axis=1),
                                  jnp.concatenate([zeros, jnp.ones_like(zeros)], axis=1)], axis=0)
            res = _dot_nt(jnp.concatenate([q2, sq], axis=1), wt)
            c, qn2 = res[:, :LANES], res[:, LANES:]
            kn2 = jnp.max(kn_ref[...])
            gap = jnp.max(jnp.sqrt(qn2 * kn2) * ATTN_NORM_SLACK - c)
            c_ref[...] = c
            flag_ref[0] = (gap <= ATTN_GUARD_LOG2).astype(jnp.int32)

    fast = flag_ref[0] == 1
    if with_ctx:
        tc = kc_ref.shape[0]
        first = ki == 0

        @pl.when(jnp.logical_and(first, fast))
        def _():
            fast_update(kc_ref[...], vc_ref[...], p_scr.at[:, pl.ds(0, tc)])

        @pl.when(jnp.logical_and(first, jnp.logical_not(fast)))
        def _():
            exact_update(kc_ref[...], vc_ref[...], p_scr.at[:, pl.ds(0, tc)])

    @pl.when(fast)
    def _():
        fast_update(k_ref[...], v_ref[...], p_scr)

    @pl.when(jnp.logical_not(fast))
    def _():
        exact_update(k_ref[...], v_ref[...], p_scr)

    @pl.when(ki == nk - 1)
    def _():
        lv = lam_ref[...]
        e1 = jnp.exp(jnp.sum(lv[0:1, :] * lv[1:2, :], axis=1, keepdims=True))
        e2 = jnp.exp(jnp.sum(lv[2:3, :] * lv[3:4, :], axis=1, keepdims=True))
        lam = e1 - e2 + lam_init
        acc = acc_ref[...]
        o = acc[:, :DA_V_DIM] / acc[:, DA_V_DIM:]
        o = o[:tq] - lam * o[tq:]
        o_ref[...] = (_rms(o, w_ref[...]) * (1.0 - lam_init)).astype(o_ref.dtype)


def diff_attention(q, k, v, k_ctx, v_ctx, knorm, lam_vec, subln_w, lam_init, tq=1024, tk=2048):
    b, t, d = q.shape
    nh = d // DA_V_DIM
    tq = _pick_tile(t, tq)
    tk = _pick_tile(k.shape[1], tk)
    nk = k.shape[1] // tk
    with_ctx = k_ctx is not None
    assert not with_ctx or k_ctx.shape[1] <= tk
    qspec = pl.BlockSpec((None, tq, DA_V_DIM), lambda bi, h, i, j: (bi, i, h))
    kspec = pl.BlockSpec((None, tk, DA_V_DIM), lambda bi, h, i, j: (bi, j, h))
    in_specs = [qspec, kspec, kspec]
    args = [q, k, v]
    if with_ctx:
        cspec = pl.BlockSpec((None, k_ctx.shape[1], DA_V_DIM), lambda bi, h, i, j: (bi, 0, h))
        nspec = pl.BlockSpec((None, None) + knorm.shape[2:], lambda bi, h, i, j: (bi, h, 0, 0))
        in_specs += [cspec, cspec, nspec]
        args += [k_ctx, v_ctx, knorm]
    in_specs += [pl.BlockSpec((4, DA_HEAD_DIM), lambda bi, h, i, j: (0, 0)),
                 pl.BlockSpec((1, DA_V_DIM), lambda bi, h, i, j: (0, 0))]
    args += [lam_vec, subln_w.reshape(1, DA_V_DIM)]
    return pl.pallas_call(
        functools.partial(_attn_kernel, lam_init=lam_init, nk=nk, with_ctx=with_ctx),
        grid=(b, nh, t // tq, nk),
        in_specs=in_specs,
        out_specs=qspec,
        out_shape=jax.ShapeDtypeStruct((b, t, d), BF16),
        scratch_shapes=[pltpu.VMEM((2 * tq, DA_V_DIM), BF16),
                        pltpu.VMEM((2 * tq, 1), F32),
                        pltpu.VMEM((2 * tq, LANES), F32),
                        pltpu.VMEM((2 * tq, 2 * DA_V_DIM), F32),
                        pltpu.VMEM((2 * tq, tk), BF16),
                        pltpu.SMEM((1,), jnp.int32)],
        compiler_params=_cparams("parallel", "parallel", "parallel", "arbitrary"),
        name="diff_attention",
    )(*args)


def _merge_kernel(ya_ref, yb_ref, yc_ref, g_ref, wb_ref, wo_ref, x_ref, mod_ref, o_ref):
    d = x_ref.shape[1]
    ys = (ya_ref, yb_ref, yc_ref)
    s = None
    for k in range(N_BRANCH):
        t = jax.nn.sigmoid(g_ref[:, k * d:(k + 1) * d].astype(F32)) * _dot(ys[k][...].astype(BF16), wb_ref[k])
        s = t if s is None else s + t
    r = _dot(s.astype(BF16), wo_ref[...])
    o_ref[...] = x_ref[...] + mod_ref[2:3, :] * r


def merge(ya, yb, yc, gates, wb, wo, x, mods, tr=512):
    b, t, d = x.shape
    tr = _pick_tile(t, tr)
    per_batch = mods.shape[0] > 1
    blk = pl.BlockSpec((None, tr, d), lambda bi, i: (bi, i, 0))
    return pl.pallas_call(
        _merge_kernel,
        grid=(b, t // tr),
        in_specs=[blk, blk, blk,
                  pl.BlockSpec((None, tr, N_BRANCH * d), lambda bi, i: (bi, i, 0)),
                  pl.BlockSpec((N_BRANCH, d, d), lambda bi, i: (0, 0, 0), pipeline_mode=pl.Buffered(1)),
                  pl.BlockSpec((d, d), lambda bi, i: (0, 0), pipeline_mode=pl.Buffered(1)),
                  blk,
                  pl.BlockSpec((None, 8, d), (lambda bi, i: (bi, 0, 0)) if per_batch else (lambda bi, i: (0, 0, 0)))],
        out_specs=blk,
        out_shape=jax.ShapeDtypeStruct((b, t, d), F32),
        compiler_params=_cparams("parallel", "parallel"),
        name="merge",
    )(ya, yb, yc, gates, wb, wo, x, mods)


def _ffn_kernel(x_ref, g_ref, mod_ref, w13_ref, w2_ref, o_ref, *, nchunk):
    x = x_ref[...]
    h = (_rms(x, g_ref[...]) * (1.0 + mod_ref[4:5, :]) + mod_ref[3:4, :]).astype(BF16)
    f = w2_ref.shape[0]
    fc = f // nchunk
    acc = None
    for c in range(nchunk):
        a1 = _dot(h, w13_ref[:, c * fc:(c + 1) * fc])
        a3 = _dot(h, w13_ref[:, f + c * fc:f + (c + 1) * fc])
        t = _dot((_silu(a1) * a3).astype(BF16), w2_ref[c * fc:(c + 1) * fc, :])
        acc = t if acc is None else acc + t
    o_ref[...] = x + mod_ref[5:6, :] * acc


def ffn(x, g, mods, w13, w2, tr=512):
    b, t, d = x.shape
    f = w2.shape[0]
    tr = _pick_tile(t, tr)
    nchunk = 2 if (f // 2) % LANES == 0 else 1
    per_batch = mods.shape[0] > 1
    blk = pl.BlockSpec((None, tr, d), lambda bi, i: (bi, i, 0))
    return pl.pallas_call(
        functools.partial(_ffn_kernel, nchunk=nchunk),
        grid=(b, t // tr),
        in_specs=[blk,
                  pl.BlockSpec((1, d), lambda bi, i: (0, 0)),
                  pl.BlockSpec((None, 8, d), (lambda bi, i: (bi, 0, 0)) if per_batch else (lambda bi, i: (0, 0, 0))),
                  pl.BlockSpec((d, 2 * f), lambda bi, i: (0, 0), pipeline_mode=pl.Buffered(1)),
                  pl.BlockSpec((f, d), lambda bi, i: (0, 0), pipeline_mode=pl.Buffered(1))],
        out_specs=blk,
        out_shape=jax.ShapeDtypeStruct((b, t, d), F32),
        compiler_params=_cparams("parallel", "parallel"),
        name="ffn",
    )(x, g.reshape(1, d), mods, w13, w2)


def kernel(x, c, ctx, c_ctx, ada_w, ada_b, norm1_g, norm2_g, w_in, ssd_conv_w, ssd_conv_b,
           ssd_a_log, ssd_dt_bias, ssd_d, ssd_norm_w, hy_conv_w, hy_conv_b, hy_w1, hy_b1, hy_w2,
           hy_b2, hy_w3, hy_freq, hy_bias, da_lambda, da_subln_w, w_branch, w_out, ffn_w13,
           ffn_w2, final_g):
    bsz, length, d = x.shape
    ctx_len = ctx.shape[1]
    depth = w_in.shape[0]
    inner = SSD_HEADS * SSD_HEAD_DIM
    xbc_w = inner + 2 * SSD_GROUPS * SSD_STATE
    dt_w = 2 * SSD_HEADS
    hy_w = (HY_ORDER + 1) * d
    widths = (inner, xbc_w, dt_w, hy_w, d, d, d, N_BRANCH * d)
    offs = [0]
    for wd in widths:
        offs.append(offs[-1] + wd)

    cond = jnp.concatenate([c, c_ctx[None, :], jnp.zeros((8 - bsz - 1, d), F32)], axis=0)

    rope_tabs = rope_tables(length)
    tabs = fft_tables(2 * length // FFT_N2, FFT_N2)
    expand = (jnp.arange(LANES)[:, None] % SSD_HEADS
              == jnp.arange(inner)[None, :] // SSD_HEAD_DIM)
    expand_dir = [(expand & ((jnp.arange(LANES)[:, None] // SSD_HEADS) == dr)).astype(BF16) for dr in range(2)]
    qscale = DA_HEAD_DIM ** -0.5 * math.log2(math.e)

    x_l, x_c = x, ctx
    for layer in range(depth):
        last = layer == depth - 1
        lam_init = 0.8 - 0.6 * math.exp(-0.3 * layer)
        mod = matmul(cond, ada_w[layer].astype(BF16), ada_b[layer].reshape(1, -1), silu_in=True)
        mod = jnp.pad(mod.reshape(8, 6, d), ((0, 0), (0, 2), (0, 0)))
        mods_l, mods_c = mod[:bsz], mod[bsz:bsz + 1]

        w_l = w_in[layer].astype(BF16)
        w_parts = [w_l[:, offs[i]:offs[i + 1]] for i in range(len(widths))]
        w_parts[2] = jnp.pad(w_parts[2], ((0, 0), (0, LANES - dt_w)))
        a_neg = -jnp.exp(ssd_a_log[layer].astype(F32)).reshape(1, dt_w)
        aneg_row = jnp.pad(a_neg, ((0, 0), (0, LANES - dt_w)))
        bias_row = jnp.pad(ssd_dt_bias[layer].astype(F32).reshape(1, dt_w), ((0, 0), (0, LANES - dt_w)))
        d_row = jnp.repeat(ssd_d[layer].astype(F32), SSD_HEAD_DIM).reshape(1, inner)
        nw_row = ssd_norm_w[layer].astype(F32).reshape(1, inner)
        wb = w_branch[layer].astype(BF16)
        wo = w_out[layer].astype(BF16)
        w13 = ffn_w13[layer].astype(BF16)
        w2 = ffn_w2[layer].astype(BF16)

        def project(stream, mods):
            bb, tt, _ = stream.shape
            h = norm_mod(stream, norm1_g[layer], mods, 0).reshape(bb * tt, d)
            outs = [matmul(h, wp, out_dtype=(F32 if i == 2 else BF16)).reshape(bb, tt, -1)
                    for i, wp in enumerate(w_parts)]
            return outs

        z_c, xbc_c, dt_c, hy_c, q_c, k_c, v_c, g_c = project(x_c, mods_c)
        z_l, xbc_l, dt_l, hy_l, q_l, k_l, v_l, g_l = project(x_l, mods_l)

        xa_c = dwconv3(xbc_c, ssd_conv_w[layer], ssd_conv_b[layer], True)
        xa_l = dwconv3(xbc_l, ssd_conv_w[layer], ssd_conv_b[layer], True)
        h_zero = jnp.zeros((bsz, SSD_STATE, inner), F32)
        ys_c, ys_l = [], []
        for dr in range(2):
            y_c, st = ssd_scan(xa_c, dt_c, bias_row, aneg_row, expand_dir[dr], h_zero, dr)
            y_l, _ = ssd_scan(xa_l, dt_l, bias_row, aneg_row, expand_dir[dr], st, dr)
            ys_c.append(y_c)
            ys_l.append(y_l)
        ya_l = ssd_out(ys_l[0], ys_l[1], xa_l, z_l, d_row, nw_row)

        filt = (hy_w1[layer], hy_b1[layer], hy_w2[layer], hy_b2[layer], hy_w3[layer], hy_freq[layer])
        u_l = dwconv3(hy_l, hy_conv_w[layer], hy_conv_b[layer], False)
        yb_l = hyena_long(u_l, hyena_taps(length, *filt), hy_bias[layer], tabs)

        qr_l = rope(q_l, rope_tabs, qscale)
        kr_l, kn_l = rope(k_l, rope_tabs, 1.0, with_norms=True)
        kb_c = scale_cast(k_c, 1.0)
        yc_l = diff_attention(qr_l, kr_l, v_l, kb_c, v_c, kn_l, da_lambda[layer], da_subln_w[layer], lam_init)

        x_l_new = merge(ya_l, yb_l, yc_l, g_l, wb, wo, x_l, mods_l)
        x_l = ffn(x_l_new, norm2_g[layer], mods_l, w13, w2)

        if not last:
            ya_c = ssd_out(ys_c[0], ys_c[1], xa_c, z_c, d_row, nw_row)
            u_c = dwconv3(hy_c, hy_conv_w[layer], hy_conv_b[layer], False)
            yb_c = hyena_ctx(u_c, hyena_taps(ctx_len, *filt), hy_bias[layer])
            qb_c = scale_cast(q_c, qscale)
            yc_c = diff_attention(qb_c, kb_c, v_c, None, None, None, da_lambda[layer], da_subln_w[layer], lam_init)
            x_c_new = merge(ya_c, yb_c, yc_c, g_c, wb, wo, x_c, mods_c)
            x_c = ffn(x_c_new, norm2_g[layer], mods_c, w13, w2)
    return final_norm(x_l, final_g)
```
